```python
import math
import jax
import jax.numpy as jnp
from jax import lax
import numpy as np

D_MODEL = 2048
BATCH = 4
SEQ = 2048
DEPTH = 4
DEC_BATCH = 8
DEC_SEQ = 1
PAST_LEN = 16384
PAGE_SIZE = 128

N_A = DEPTH // 2
N_B = DEPTH - N_A
MIX_W = D_MODEL
MEM_TOKENS = 256
MEM_HEADS = 4
MEM_HEAD_DIM = 128
MEM_W = MEM_HEADS * MEM_HEAD_DIM
MAIN_W = MIX_W - MEM_W
RWKV_HEAD = 64
RWKV_HEADS = MAIN_W // RWKV_HEAD
DECAY_RANK = max(32, int(round(1.8 * D_MODEL ** 0.5 / 32)) * 32)
ICL_RANK = max(32, int(round(1.8 * D_MODEL ** 0.5 / 32)) * 32)
VRES_RANK = max(32, int(round(1.3 * D_MODEL ** 0.5 / 32)) * 32)
GATE_RANK = max(32, int(round(0.6 * D_MODEL ** 0.8 / 32)) * 32)
GN_EPS = 64e-5
NSA_HEAD_DIM = 128
NSA_Q_HEADS = MAIN_W // NSA_HEAD_DIM
NSA_KV_GROUPS = 4
NSA_HPG = NSA_Q_HEADS // NSA_KV_GROUPS
N_BRANCH = 3
GATE_W = N_BRANCH * NSA_Q_HEADS
CMP_BLOCK = 32
CMP_STRIDE = 16
CMP_HIDDEN = NSA_HEAD_DIM
SEL_BLOCK = 64
SEL_TOPK = 16
SEL_LOCAL = 2
WINDOW = 512
NSA_Q_BLOCK = 32
REL_BUCKETS = 32
REL_MAX_DIST = 128
D_FF = int(round(8 * D_MODEL / 3 / 128)) * 128
A_IN_W = 3 * MAIN_W + MEM_W
B_IN_W = MAIN_W + GATE_W + MEM_W
KV_W = 6 * NSA_KV_GROUPS * NSA_HEAD_DIM
NORM_EPS = 1e-6
MASK_NEG = -1e30
FORCE_SCORE = 1e9

kernel_name = 'yoco_rwkv7_nsa_macaron_step'


def rms_norm(x, g):
    xf = x.astype(jnp.float32)
    y = xf * lax.rsqrt(jnp.mean(xf * xf, axis=-1, keepdims=True) + NORM_EPS)
    return (y * g.astype(jnp.float32)).astype(x.dtype)


def swiglu(x, w_gu, w_d):
    gate, up = jnp.split(x @ w_gu, 2, axis=-1)
    return (jax.nn.silu(gate) * up) @ w_d


def masked_softmax(s, mask):
    s = jnp.where(mask, s, MASK_NEG)
    p = jnp.exp(s - jnp.max(s, axis=-1, keepdims=True)) * mask
    den = jnp.sum(p, axis=-1, keepdims=True)
    return p / jnp.where(den > 0, den, 1.0)


def rel_bucket(dist):
    n = jnp.maximum(dist, 0)
    max_exact = REL_BUCKETS // 2
    nf = jnp.maximum(n, 1).astype(jnp.float32)
    large = max_exact + (jnp.log(nf / max_exact) / math.log(REL_MAX_DIST / max_exact)
                         * (REL_BUCKETS - max_exact)).astype(jnp.int32)
    return jnp.where(n < max_exact, n, jnp.minimum(large, REL_BUCKETS - 1))


def memory_kv(mem, g, w):
    bx, m = mem.shape[:2]
    k, v = jnp.split(rms_norm(mem, g) @ w, 2, axis=-1)
    return (k.reshape(bx, m, MEM_HEADS, MEM_HEAD_DIM), v.reshape(bx, m, MEM_HEADS, MEM_HEAD_DIM))


def mem_attend(q, mk, mv):
    s = jnp.einsum('bthd,bmhd->bhtm', q, mk).astype(jnp.float32) * MEM_HEAD_DIM ** -0.5
    p = jax.nn.softmax(s, axis=-1).astype(mv.dtype)
    return jnp.einsum('bhtm,bmhd->bthd', p, mv)


def shared_kv(hn, w_kv):
    bx, t = hn.shape[:2]
    parts = jnp.split(hn @ w_kv, 6, axis=-1)
    return [p.reshape(bx, t, NSA_KV_GROUPS, NSA_HEAD_DIM) for p in parts]


def compress_blocks(rows, pe, w1, b1, w2):
    bx, length = rows.shape[:2]
    nc = length // CMP_STRIDE
    r = CMP_BLOCK // CMP_STRIDE
    chunks = rows.reshape(bx, nc, CMP_STRIDE, NSA_KV_GROUPS, NSA_HEAD_DIM)
    w1s = w1.reshape(r, CMP_STRIDE, NSA_HEAD_DIM, CMP_HIDDEN)
    part = jnp.einsum('bcpgd,spde->sbcge', chunks, w1s)
    nb = nc - r + 1
    hid = pe.reshape(-1) @ w1 + b1
    for s in range(r):
        hid = hid + part[s, :, s:s + nb]
    return jax.nn.gelu(hid) @ w2


def cmp_to_sel(nb, nsb):
    i = jnp.arange(nb)[:, None]
    j = jnp.arange(nsb)[None, :]
    start = i * CMP_STRIDE
    return ((start < (j + 1) * SEL_BLOCK) & (start + CMP_BLOCK > j * SEL_BLOCK)).astype(jnp.float32)


def nsa_core(q, gates, t_pos, kc, vc, gather_sel, nsb, kw, vw, win_pos, rel_bias):
    f32 = jnp.float32
    scale = NSA_HEAD_DIM ** -0.5
    tab = rel_bias.astype(f32).reshape(REL_BUCKETS, NSA_KV_GROUPS, NSA_HPG)
    bx, nq = q.shape[:2]
    nb = kc.shape[1]
    c_end = jnp.arange(nb) * CMP_STRIDE + (CMP_BLOCK - 1)
    d_c = t_pos[:, None] - c_end[None, :]
    s_c = (jnp.einsum('bqghd,bngd->bgqhn', q, kc).astype(f32) * scale
           + jnp.transpose(tab[rel_bucket(d_c)], (2, 0, 3, 1)))
    p_c = masked_softmax(s_c, (d_c >= 0)[None, None, :, None, :])
    o_c = jnp.einsum('bgqhn,bngd->bqghd', p_c.astype(vc.dtype), vc)
    imp = jnp.einsum('bgqhn,nj->bgqj', p_c, cmp_to_sel(nb, nsb))
    j = jnp.arange(nsb)[None, :]
    cur = (t_pos // SEL_BLOCK)[:, None]
    valid = j <= cur
    forced = valid & ((j == 0) | (j > cur - SEL_LOCAL))
    imp = jnp.where(forced, FORCE_SCORE, jnp.where(valid, imp, -FORCE_SCORE))
    n_sel = min(SEL_TOPK, nsb)
    _, idx = lax.top_k(imp, n_sel)
    ks, vs = gather_sel(idx)
    n_keys = n_sel * SEL_BLOCK
    ks = ks.reshape(bx, NSA_KV_GROUPS, nq, n_keys, NSA_HEAD_DIM)
    vs = vs.reshape(bx, NSA_KV_GROUPS, nq, n_keys, NSA_HEAD_DIM)
    k_pos = (idx[..., None] * SEL_BLOCK + jnp.arange(SEL_BLOCK)).reshape(bx, NSA_KV_GROUPS, nq, n_keys)
    d_s = t_pos[None, None, :, None] - k_pos
    g_idx = jnp.arange(NSA_KV_GROUPS)[None, :, None, None]
    s_s = (jnp.einsum('bqghd,bgqkd->bgqhk', q, ks).astype(f32) * scale
           + jnp.swapaxes(tab[rel_bucket(d_s), g_idx], -1, -2))
    p_s = masked_softmax(s_s, (d_s >= 0)[:, :, :, None, :])
    o_s = jnp.einsum('bgqhk,bgqkd->bqghd', p_s.astype(vs.dtype), vs)
    d_w = t_pos[:, None] - win_pos[None, :]
    m_w = (d_w >= 0) & (d_w < WINDOW) & (win_pos[None, :] >= 0)
    s_w = (jnp.einsum('bqghd,bkgd->bgqhk', q, kw).astype(f32) * scale
           + jnp.transpose(tab[rel_bucket(d_w)], (2, 0, 3, 1)))
    p_w = masked_softmax(s_w, m_w[None, None, :, None, :])
    o_w = jnp.einsum('bgqhk,bkgd->bqghd', p_w.astype(vw.dtype), vw)
    gate = jax.nn.sigmoid(gates.astype(f32))[..., None]
    o = gate[:, :, 0] * o_c + gate[:, :, 1] * o_s + gate[:, :, 2] * o_w
    return o.astype(q.dtype)


def rwkv7_mix(x, shift_prev, s0, v_first, P, l):
    bx, t, _ = x.shape
    H, N = RWKV_HEADS, RWKV_HEAD
    w_in = P['w_in_a'][l]
    mu = P['rw_mu'][l]
    x_prev = jnp.concatenate([shift_prev[:, None, :].astype(x.dtype), x[:, :-1]], axis=1)
    xx = x_prev - x
    xr, xw, xk, xv, xa, xg = (x + xx * mu[i] for i in range(6))
    r = xr @ w_in[:, :MAIN_W]
    k = xk @ w_in[:, MAIN_W:2 * MAIN_W]
    v = xv @ w_in[:, 2 * MAIN_W:3 * MAIN_W]
    logw = -jax.nn.softplus(-(P['rw_w0'][l] + jnp.tanh(xw @ P['rw_w1'][l]) @ P['rw_w2'][l])) - 0.5
    a = jax.nn.sigmoid(P['rw_a0'][l] + (xa @ P['rw_a1'][l]) @ P['rw_a2'][l])
    g = jax.nn.sigmoid(xg @ P['rw_g1'][l]) @ P['rw_g2'][l]
    if v_first is None:
        v_first = v
    else:
        v = v + (v_first - v) * jax.nn.sigmoid(
            P['rw_v0'][l - 1] + (xv @ P['rw_v1'][l - 1]) @ P['rw_v2'][l - 1])
    heads = lambda z: z.reshape(bx, t, H, N).astype(jnp.float32)
    kk = heads(k * P['rw_kk'][l])
    kk = kk / jnp.maximum(jnp.sqrt(jnp.sum(kk * kk, axis=-1, keepdims=True)), 1e-12)
    k = k * (1 + (a - 1) * P['rw_ka'][l])
    rf, kf, vf, af = heads(r), heads(k), heads(v), heads(a)
    decay = jnp.exp(-jnp.exp(heads(logw)))
    xs = tuple(jnp.moveaxis(z, 1, 0) for z in (rf, decay, kf, vf, -kk, kk * af))

    def step(S, inp):
        r_t, d_t, k_t, v_t, a_t, b_t = inp
        S = (S * d_t[:, :, None, :]
             + jnp.einsum('bhij,bhj->bhi', S, a_t)[..., None] * b_t[:, :, None, :]
             + v_t[..., None] * k_t[:, :, None, :])
        return S, jnp.einsum('bhij,bhj->bhi', S, r_t)

    S, y = lax.scan(step, s0.astype(jnp.float32), xs)
    y = jnp.moveaxis(y, 0, 1)
    mean = jnp.mean(y, axis=-1, keepdims=True)
    var = jnp.mean(jnp.square(y - mean), axis=-1, keepdims=True)
    y = ((y - mean) * lax.rsqrt(var + GN_EPS)).reshape(bx, t, MAIN_W) * P['rw_lnw'][l] + P['rw_lnb'][l]
    bonus = jnp.sum(rf * kf * P['rw_rk'][l], axis=-1, keepdims=True) * vf
    out = (y + bonus.reshape(bx, t, MAIN_W)) * g
    return out.astype(x.dtype), S.astype(s0.dtype), x[:, -1], v_first


def trunk(h, P, shift0, wkv0, mem_k, mem_v, make_side, attend):
    bx, t, _ = h.shape
    shifts, states = [], []
    v_first, ctx, side_state = None, None, None
    for l in range(DEPTH):
        n = P['norms'][l]
        h = h + 0.5 * rms_norm(swiglu(rms_norm(h, n[0]), P['ffn_gu'][l, 0], P['ffn_d'][l, 0]), n[1])
        u = rms_norm(h, n[2])
        if l < N_A:
            main, S, last, v_first = rwkv7_mix(u, shift0[l], wkv0[l], v_first, P, l)
            shifts.append(last)
            states.append(S)
            qm = u @ P['w_in_a'][l][:, 3 * MAIN_W:]
        else:
            proj = u @ P['w_in_b'][l - N_A]
            q = proj[..., :MAIN_W].reshape(bx, t, NSA_KV_GROUPS, NSA_HPG, NSA_HEAD_DIM)
            gates = proj[..., MAIN_W:MAIN_W + GATE_W].reshape(bx, t, N_BRANCH, NSA_KV_GROUPS, NSA_HPG)
            qm = proj[..., MAIN_W + GATE_W:]
            main = attend(q, gates, ctx, P['rel_bias']).reshape(bx, t, MAIN_W)
        mo = mem_attend(qm.reshape(bx, t, MEM_HEADS, MEM_HEAD_DIM), mem_k[l], mem_v[l]).reshape(bx, t, MEM_W)
        h = h + rms_norm(jnp.concatenate([main, mo], axis=-1) @ P['w_out'][l], n[3])
        h = h + 0.5 * rms_norm(swiglu(rms_norm(h, n[4]), P['ffn_gu'][l, 1], P['ffn_d'][l, 1]), n[5])
        if l == N_A - 1:
            ctx, side_state = make_side(rms_norm(h, P['kv_norm']))
    return h, jnp.stack(shifts), jnp.stack(states), side_state


def setup_inputs(seed: int = 0) -> dict:
    key = jax.random.key(seed)
    ks = iter(jax.random.split(key, 64))
    nrm = lambda shape, scale: jax.random.normal(next(ks), shape, jnp.float32) * scale
    unif = lambda shape, lo, hi: jax.random.uniform(next(ks), shape, jnp.float32, lo, hi)
    n_pages = PAST_LEN // PAGE_SIZE
    n_phys = (DEC_BATCH * n_pages * 5) // 4
    wb = min(WINDOW, PAST_LEN)
    G, dh = NSA_KV_GROUPS, NSA_HEAD_DIM
    H, N = RWKV_HEADS, RWKV_HEAD
    return {
        'x_prompt': nrm((BATCH, SEQ, D_MODEL), 1.0),
        'x_sample': nrm((DEC_BATCH, DEC_SEQ, D_MODEL), 1.0),
        'mem_prompt': nrm((BATCH, MEM_TOKENS, D_MODEL), 1.0),
        'state_wkv': nrm((N_A, DEC_BATCH, H, N, N), 0.3),
        'state_shift': nrm((N_A, DEC_BATCH, D_MODEL), 1.0),
        'cache_mem_k': nrm((DEPTH, DEC_BATCH, MEM_TOKENS, MEM_HEADS, MEM_HEAD_DIM), 1.0),
        'cache_mem_v': nrm((DEPTH, DEC_BATCH, MEM_TOKENS, MEM_HEADS, MEM_HEAD_DIM), 1.0),
        'cache_cmp_k': nrm((n_phys, PAGE_SIZE, G, dh), 1.0),
        'cache_cmp_v': nrm((n_phys, PAGE_SIZE, G, dh), 1.0),
        'cache_slc_k': nrm((n_phys, PAGE_SIZE, G, dh), 1.0),
        'cache_slc_v': nrm((n_phys, PAGE_SIZE, G, dh), 1.0),
        'cache_win_k': nrm((DEC_BATCH, wb, G, dh), 1.0),
        'cache_win_v': nrm((DEC_BATCH, wb, G, dh), 1.0),
        'page_table': jax.random.permutation(next(ks), n_phys)[:DEC_BATCH * n_pages]
                      .reshape(DEC_BATCH, n_pages).astype(jnp.int32),
        'norms': 1.0 + nrm((DEPTH, 6, D_MODEL), 0.02),
        'ffn_gu': nrm((DEPTH, 2, D_MODEL, 2 * D_FF), D_MODEL ** -0.5),
        'ffn_d': nrm((DEPTH, 2, D_FF, D_MODEL), D_FF ** -0.5),
        'w_in_a': nrm((N_A, D_MODEL, A_IN_W), D_MODEL ** -0.5),
        'w_in_b': nrm((N_B, D_MODEL, B_IN_W), D_MODEL ** -0.5),
        'w_out': nrm((DEPTH, MIX_W, D_MODEL), MIX_W ** -0.5),
        'mem_norm': 1.0 + nrm((DEPTH, D_MODEL), 0.02),
        'w_mem_kv': nrm((DEPTH, D_MODEL, 2 * MEM_W), D_MODEL ** -0.5),
        'kv_norm': 1.0 + nrm((D_MODEL,), 0.02),
        'w_kv': nrm((D_MODEL, KV_W), D_MODEL ** -0.5),
        'cmp_pe': nrm((2, CMP_BLOCK, dh), 0.1),
        'cmp_w1': nrm((2, CMP_BLOCK * dh, CMP_HIDDEN), (CMP_BLOCK * dh) ** -0.5),
        'cmp_b1': nrm((2, CMP_HIDDEN), 0.01),
        'cmp_w2': nrm((2, CMP_HIDDEN, dh), CMP_HIDDEN ** -0.5),
        'rel_bias': nrm((REL_BUCKETS, NSA_Q_HEADS), 0.1),
        'rw_mu': unif((N_A, 6, D_MODEL), 0.0, 1.0),
        'rw_w0': unif((N_A, MAIN_W), -5.0, -1.0),
        'rw_w1': nrm((N_A, D_MODEL, DECAY_RANK), D_MODEL ** -0.5),
        'rw_w2': nrm((N_A, DECAY_RANK, MAIN_W), DECAY_RANK ** -0.5),
        'rw_a0': nrm((N_A, MAIN_W), 0.1),
        'rw_a1': nrm((N_A, D_MODEL, ICL_RANK), D_MODEL ** -0.5),
        'rw_a2': nrm((N_A, ICL_RANK, MAIN_W), ICL_RANK ** -0.5),
        'rw_g1': nrm((N_A, D_MODEL, GATE_RANK), D_MODEL ** -0.5),
        'rw_g2': nrm((N_A, GATE_RANK, MAIN_W), GATE_RANK ** -0.5),
        'rw_v0': nrm((N_A - 1, MAIN_W), 0.1),
        'rw_v1': nrm((N_A - 1, D_MODEL, VRES_RANK), D_MODEL ** -0.5),
        'rw_v2': nrm((N_A - 1, VRES_RANK, MAIN_W), VRES_RANK ** -0.5),
        'rw_kk': 0.85 + nrm((N_A, MAIN_W), 0.02),
        'rw_ka': 1.0 + nrm((N_A, MAIN_W), 0.02),
        'rw_rk': nrm((N_A, H, N), 0.1),
        'rw_lnw': 1.0 + nrm((N_A, MAIN_W), 0.02),
        'rw_lnb': nrm((N_A, MAIN_W), 0.01),
    }


def reference(x_prompt, x_sample, mem_prompt, state_wkv, state_shift, cache_mem_k, cache_mem_v,
              cache_cmp_k, cache_cmp_v, cache_slc_k, cache_slc_v, cache_win_k, cache_win_v, page_table,
              norms, ffn_gu, ffn_d, w_in_a, w_in_b, w_out, mem_norm, w_mem_kv, kv_norm, w_kv,
              cmp_pe, cmp_w1, cmp_b1, cmp_w2, rel_bias,
              rw_mu, rw_w0, rw_w1, rw_w2, rw_a0, rw_a1, rw_a2, rw_g1, rw_g2, rw_v0, rw_v1, rw_v2,
              rw_kk, rw_ka, rw_rk, rw_lnw, rw_lnb):
    P = dict(norms=norms, ffn_gu=ffn_gu, ffn_d=ffn_d, w_in_a=w_in_a, w_in_b=w_in_b, w_out=w_out,
             kv_norm=kv_norm, rel_bias=rel_bias, rw_mu=rw_mu, rw_w0=rw_w0, rw_w1=rw_w1, rw_w2=rw_w2,
             rw_a0=rw_a0, rw_a1=rw_a1, rw_a2=rw_a2, rw_g1=rw_g1, rw_g2=rw_g2, rw_v0=rw_v0,
             rw_v1=rw_v1, rw_v2=rw_v2, rw_kk=rw_kk, rw_ka=rw_ka, rw_rk=rw_rk, rw_lnw=rw_lnw,
             rw_lnb=rw_lnb)
    G, dh = NSA_KV_GROUPS, NSA_HEAD_DIM
    g_idx = jnp.arange(G)[None, :, None, None]

    def compress(rows, c):
        return compress_blocks(rows, cmp_pe[c], cmp_w1[c], cmp_b1[c], cmp_w2[c])

    bp = x_prompt.shape[0]
    bp_idx = jnp.arange(bp)[:, None, None, None]
    mkv = [memory_kv(mem_prompt, mem_norm[l], w_mem_kv[l]) for l in range(DEPTH)]
    p_mem_k = jnp.stack([m[0] for m in mkv])
    p_mem_v = jnp.stack([m[1] for m in mkv])

    def prompt_side(hn):
        kc_r, vc_r, ks, vs, kw, vw = shared_kv(hn, w_kv)
        t = hn.shape[1]
        nsb = t // SEL_BLOCK
        pad = ((0, 0), (WINDOW, 0), (0, 0), (0, 0))
        wb = min(WINDOW, t)
        ctx = (compress(kc_r, 0), compress(vc_r, 1),
               ks.reshape(bp, nsb, SEL_BLOCK, G, dh), vs.reshape(bp, nsb, SEL_BLOCK, G, dh),
               jnp.pad(kw, pad), jnp.pad(vw, pad), nsb)
        return ctx, (kc_r, vc_r, ks, vs, kw[:, t - wb:], vw[:, t - wb:])

    def prompt_attend(q, gates, ctx, table):
        kc, vc, kb, vb, kw_pad, vw_pad, nsb = ctx
        t = q.shape[1]
        nqb = t // NSA_Q_BLOCK
        span = NSA_Q_BLOCK + WINDOW

        def gather(idx):
            return kb[bp_idx, idx, :, g_idx], vb[bp_idx, idx, :, g_idx]

        def block(args):
            qb, gb, i = args
            t0 = i * NSA_Q_BLOCK
            kw = lax.dynamic_slice_in_dim(kw_pad, t0, span, axis=1)
            vw = lax.dynamic_slice_in_dim(vw_pad, t0, span, axis=1)
            return nsa_core(qb, gb, t0 + jnp.arange(NSA_Q_BLOCK), kc, vc, gather, nsb,
                            kw, vw, t0 - WINDOW + jnp.arange(span), table)

        split = lambda z: jnp.moveaxis(z.reshape((bp, nqb, NSA_Q_BLOCK) + z.shape[2:]), 1, 0)
        out = lax.map(block, (split(q), split(gates), jnp.arange(nqb)))
        return jnp.moveaxis(out, 0, 1).reshape(q.shape)

    y_prompt, p_shift, p_wkv, p_side = trunk(
        x_prompt, P, jnp.zeros((N_A, bp, D_MODEL), x_prompt.dtype),
        jnp.zeros((N_A, bp, RWKV_HEADS, RWKV_HEAD, RWKV_HEAD), jnp.float32),
        p_mem_k, p_mem_v, prompt_side, prompt_attend)
    p_cmp_k, p_cmp_v, p_slc_k, p_slc_v, p_win_k, p_win_v = p_side

    bd, s_new = x_sample.shape[:2]
    bd_idx = jnp.arange(bd)[:, None, None, None]
    n_past_blk = PAST_LEN // SEL_BLOCK
    blk_per_page = PAGE_SIZE // SEL_BLOCK
    n_new_blk = -(-s_new // SEL_BLOCK)

    def sample_side(hn):
        kc_n, vc_n, ks_n, vs_n, kw_n, vw_n = shared_kv(hn, w_kv)

        def history(pool, new_rows):
            past = pool[page_table].reshape(bd, -1, G, dh)
            seq = jnp.concatenate([past.astype(new_rows.dtype), new_rows], axis=1)
            length = seq.shape[1]
            extra = -(-length // CMP_STRIDE) * CMP_STRIDE - length
            return jnp.pad(seq, ((0, 0), (0, extra), (0, 0), (0, 0)))

        kc = compress(history(cache_cmp_k, kc_n), 0)
        vc = compress(history(cache_cmp_v, vc_n), 1)
        pool_kb = cache_slc_k.reshape(-1, SEL_BLOCK, G, dh)
        pool_vb = cache_slc_v.reshape(-1, SEL_BLOCK, G, dh)
        pad_new = ((0, 0), (0, n_new_blk * SEL_BLOCK - s_new), (0, 0), (0, 0))
        new_kb = jnp.pad(ks_n, pad_new).reshape(bd, n_new_blk, SEL_BLOCK, G, dh)
        new_vb = jnp.pad(vs_n, pad_new).reshape(bd, n_new_blk, SEL_BLOCK, G, dh)

        def gather(idx):
            is_past = (idx < n_past_blk)[..., None, None]
            jp = jnp.minimum(idx, n_past_blk - 1)
            phys = page_table[bd_idx, jp // blk_per_page] * blk_per_page + jp % blk_per_page
            jn = jnp.clip(idx - n_past_blk, 0, n_new_blk - 1)
            return (jnp.where(is_past, pool_kb[phys, :, g_idx], new_kb[bd_idx, jn, :, g_idx]),
                    jnp.where(is_past, pool_vb[phys, :, g_idx], new_vb[bd_idx, jn, :, g_idx]))

        wb = cache_win_k.shape[1]
        kw = jnp.concatenate([cache_win_k.astype(kw_n.dtype), kw_n], axis=1)
        vw = jnp.concatenate([cache_win_v.astype(vw_n.dtype), vw_n], axis=1)
        win_pos = PAST_LEN - wb + jnp.arange(wb + s_new)
        ctx = (kc, vc, gather, n_past_blk + n_new_blk, kw, vw, win_pos)
        return ctx, (kc_n, vc_n, ks_n, vs_n, kw[:, s_new:], vw[:, s_new:])

    def sample_attend(q, gates, ctx, table):
        kc, vc, gather, nsb, kw, vw, win_pos = ctx
        t_pos = PAST_LEN + jnp.arange(q.shape[1])
        return nsa_core(q, gates, t_pos, kc, vc, gather, nsb, kw, vw, win_pos, table)

    y_sample, s_shift, s_wkv, s_side = trunk(
        x_sample, P, state_shift, state_wkv, cache_mem_k, cache_mem_v, sample_side, sample_attend)
    s_cmp_k, s_cmp_v, s_slc_k, s_slc_v, s_win_k, s_win_v = s_side

    return (y_prompt, y_sample, p_mem_k, p_mem_v, p_wkv, p_shift,
            p_cmp_k, p_cmp_v, p_slc_k, p_slc_v, p_win_k, p_win_v,
            s_wkv, s_shift, s_cmp_k, s_cmp_v, s_slc_k, s_slc_v, s_win_k, s_win_v)
```

```python
import functools
import math

import jax
import jax.numpy as jnp
from jax import lax
from jax.experimental import pallas as pl
from jax.experimental.pallas import tpu as pltpu

F32 = jnp.float32
BF16 = jnp.bfloat16
HI = lax.Precision.HIGHEST

D_MODEL = 2048
DEPTH = 4
N_A = 2
MEM_HEADS = 4
MEM_HEAD_DIM = 128
MEM_W = 512
MAIN_W = 1536
RWKV_HEAD = 64
RWKV_HEADS = 24
GN_EPS = 64e-5
NSA_HEAD_DIM = 128
NSA_Q_HEADS = 12
NSA_KV_GROUPS = 4
NSA_HPG = 3
GATE_W = 36
CMP_BLOCK = 32
CMP_STRIDE = 16
SEL_BLOCK = 64
SEL_TOPK = 16
SEL_LOCAL = 2
WINDOW = 512
REL_BUCKETS = 32
REL_MAX_DIST = 128
D_FF = 5504
NORM_EPS = 1e-6
MASK_NEG = -1e30
FORCE_SCORE = 1e9

LANE = 128
SUBLANE = 8
MXU_DIM = 256
VMEM_LIMIT = 48 * 1024 * 1024

D_FF_PAD = -(-D_FF // (2 * MXU_DIM)) * (2 * MXU_DIM)
HEAD_BLOCK = MXU_DIM // RWKV_HEAD
N_HEAD_BLOCKS = RWKV_HEADS // HEAD_BLOCK
LOWRANK_PAD = MXU_DIM
NSA_TQ = 128


def _params(*sem):
    return pltpu.CompilerParams(dimension_semantics=sem, vmem_limit_bytes=VMEM_LIMIT)


def _rms(x, g):
    return x * lax.rsqrt(jnp.mean(x * x, axis=-1, keepdims=True) + NORM_EPS) * g


def _sigmoid(x):
    return 1.0 / (1.0 + jnp.exp(-x))


def _softplus(x):
    return jnp.maximum(x, 0.0) + jnp.log(1.0 + jnp.exp(-jnp.abs(x)))


def _gelu_tanh(x):
    return 0.5 * x * (1.0 + jnp.tanh(math.sqrt(2.0 / math.pi) * (x + 0.044715 * x * x * x)))


def _dot(a, b, precision=None):
    return jnp.dot(a, b, preferred_element_type=F32, precision=precision)


def _dot_nt(a, b, precision=None):
    return lax.dot_general(a, b, (((1,), (1,)), ((), ())), preferred_element_type=F32,
                           precision=precision)


def _dot_tn(a, b, precision=None):
    return lax.dot_general(a, b, (((0,), (0,)), ((), ())), preferred_element_type=F32,
                           precision=precision)


def _masked_softmax(s, mask):
    s = jnp.where(mask, s, MASK_NEG)
    p = jnp.exp(s - jnp.max(s, axis=-1, keepdims=True)) * mask.astype(F32)
    den = jnp.sum(p, axis=-1, keepdims=True)
    return p / jnp.where(den > 0, den, 1.0)


def _row_tile(rows, target):
    t = min(rows, target)
    assert rows % t == 0, (rows, t)
    return t


def _norm_matmul_kernel(x_ref, g_ref, w_ref, o_ref, xn_ref):
    @pl.when(pl.program_id(1) == 0)
    def _():
        xn_ref[...] = _rms(x_ref[...], g_ref[...]).astype(BF16)

    o_ref[...] = _dot(xn_ref[...], w_ref[...]).astype(o_ref.dtype)


def norm_matmul(x, g, w, tn, out_dtype=F32, tm=512):
    rows, d = x.shape
    n = w.shape[1]
    tm = _row_tile(rows, tm)
    assert n % tn == 0
    return pl.pallas_call(
        _norm_matmul_kernel,
        grid=(rows // tm, n // tn),
        in_specs=[pl.BlockSpec((tm, d), lambda i, j: (i, 0)),
                  pl.BlockSpec((1, d), lambda i, j: (0, 0)),
                  pl.BlockSpec((d, tn), lambda i, j: (0, j))],
        out_specs=pl.BlockSpec((tm, tn), lambda i, j: (i, j)),
        out_shape=jax.ShapeDtypeStruct((rows, n), out_dtype),
        scratch_shapes=[pltpu.VMEM((tm, d), BF16)],
        compiler_params=_params("arbitrary", "arbitrary"),
        name="norm_matmul",
    )(x, g.reshape(1, d), w)


def _norm_swiglu_kernel(x_ref, g_ref, wg_ref, wu_ref, o_ref, xn_ref):
    @pl.when(pl.program_id(1) == 0)
    def _():
        xn_ref[...] = _rms(x_ref[...], g_ref[...]).astype(BF16)

    xn = xn_ref[...]
    gate = _dot(xn, wg_ref[...])
    up = _dot(xn, wu_ref[...])
    o_ref[...] = (gate * _sigmoid(gate) * up).astype(o_ref.dtype)


def norm_swiglu(x, g, wg, wu, tn=512, tm=512):
    rows, d = x.shape
    f = wg.shape[1]
    tm = _row_tile(rows, tm)
    assert f % tn == 0
    return pl.pallas_call(
        _norm_swiglu_kernel,
        grid=(rows // tm, f // tn),
        in_specs=[pl.BlockSpec((tm, d), lambda i, j: (i, 0)),
                  pl.BlockSpec((1, d), lambda i, j: (0, 0)),
                  pl.BlockSpec((d, tn), lambda i, j: (0, j)),
                  pl.BlockSpec((d, tn), lambda i, j: (0, j))],
        out_specs=pl.BlockSpec((tm, tn), lambda i, j: (i, j)),
        out_shape=jax.ShapeDtypeStruct((rows, f), BF16),
        scratch_shapes=[pltpu.VMEM((tm, d), BF16)],
        compiler_params=_params("arbitrary", "arbitrary"),
        name="norm_swiglu",
    )(x, g.reshape(1, d), wg, wu)


def _matmul_norm_res_kernel(a_ref, w_ref, g_ref, h_ref, o_ref, acc_ref, *, scale):
    k = pl.program_id(1)

    @pl.when(k == 0)
    def _():
        acc_ref[...] = jnp.zeros_like(acc_ref)

    acc_ref[...] += _dot(a_ref[...], w_ref[...])

    @pl.when(k == pl.num_programs(1) - 1)
    def _():
        o_ref[...] = h_ref[...] + scale * _rms(acc_ref[...], g_ref[...])


def matmul_norm_res(a, w, g, h, scale, tk=512, tm=512):
    rows, kdim = a.shape
    d = w.shape[1]
    tm = _row_tile(rows, tm)
    assert kdim % tk == 0
    return pl.pallas_call(
        functools.partial(_matmul_norm_res_kernel, scale=scale),
        grid=(rows // tm, kdim // tk),
        in_specs=[pl.BlockSpec((tm, tk), lambda i, k: (i, k)),
                  pl.BlockSpec((tk, d), lambda i, k: (k, 0)),
                  pl.BlockSpec((1, d), lambda i, k: (0, 0)),
                  pl.BlockSpec((tm, d), lambda i, k: (i, 0))],
        out_specs=pl.BlockSpec((tm, d), lambda i, k: (i, 0)),
        out_shape=jax.ShapeDtypeStruct((rows, d), F32),
        scratch_shapes=[pltpu.VMEM((tm, d), F32)],
        compiler_params=_params("arbitrary", "arbitrary"),
        name="matmul_norm_res",
    )(a, w, g.reshape(1, d), h)


RW_R, RW_W1, RW_K, RW_V, RW_V1, RW_A1, RW_G1, RW_QM = 0, 6, 7, 13, 19, 20, 21, 22
RW_NBLK = 24
RW_MIX_STARTS = ((RW_R, 0), (RW_W1, 1), (RW_K, 2), (RW_V, 3), (RW_A1, 4), (RW_G1, 5), (RW_QM, None))


def _rwkv_in_kernel(x_ref, sp_ref, g_ref, mu_ref, w_ref, o_ref, last_ref,
                    u_sc, xx_sc, xm_sc, carry_sc, *, tm, last_tile, last_row):
    ti = pl.program_id(1)
    j = pl.program_id(2)

    @pl.when(j == 0)
    def _():
        @pl.when(ti == 0)
        def _():
            carry_sc[...] = sp_ref[0]

        u = _rms(x_ref[0], g_ref[...])
        rows = lax.broadcasted_iota(jnp.int32, u.shape, 0)
        prev = jnp.where(rows == 0, carry_sc[...], pltpu.roll(u, 1, axis=0))
        u_sc[...] = u
        xx_sc[...] = prev - u
        carry_sc[...] = u[tm - 1:tm, :]

        @pl.when(ti == last_tile)
        def _():
            last_ref[0] = u[last_row:last_row + 1, :]

    for start, mix in RW_MIX_STARTS:
        @pl.when(j == start)
        def _(mix=mix):
            if mix is None:
                xm_sc[...] = u_sc[...].astype(BF16)
            else:
                xm_sc[...] = (u_sc[...] + xx_sc[...] * mu_ref[mix:mix + 1, :]).astype(BF16)

    o_ref[0] = _dot(xm_sc[...], w_ref[...])


def rwkv_in_proj(h, shift_prev, g, mu, w_cat, t_real, tm=512):
    b, t, d = h.shape
    tm = _row_tile(t, tm)
    tn = MXU_DIM
    last_tile, last_row = (t_real - 1) // tm, (t_real - 1) % tm
    mu8 = jnp.concatenate([mu, jnp.zeros((SUBLANE - mu.shape[0], d), F32)], axis=0)
    return pl.pallas_call(
        functools.partial(_rwkv_in_kernel, tm=tm, last_tile=last_tile, last_row=last_row),
        grid=(b, t // tm, RW_NBLK),
        in_specs=[pl.BlockSpec((1, tm, d), lambda bi, ti, j: (bi, ti, 0)),
                  pl.BlockSpec((1, 1, d), lambda bi, ti, j: (bi, 0, 0)),
                  pl.BlockSpec((1, d), lambda bi, ti, j: (0, 0)),
                  pl.BlockSpec((SUBLANE, d), lambda bi, ti, j: (0, 0)),
                  pl.BlockSpec((d, tn), lambda bi, ti, j: (0, j))],
        out_specs=[pl.BlockSpec((1, tm, tn), lambda bi, ti, j: (bi, ti, j)),
                   pl.BlockSpec((1, 1, d), lambda bi, ti, j: (bi, 0, 0))],
        out_shape=[jax.ShapeDtypeStruct((b, t, RW_NBLK * tn), F32),
                   jax.ShapeDtypeStruct((b, 1, d), F32)],
        scratch_shapes=[pltpu.VMEM((tm, d), F32), pltpu.VMEM((tm, d), F32),
                        pltpu.VMEM((tm, d), BF16), pltpu.VMEM((1, d), F32)],
        compiler_params=_params("arbitrary", "arbitrary", "arbitrary"),
        name="rwkv_in_proj",
    )(h, shift_prev.reshape(b, 1, d), g.reshape(1, d), mu8, w_cat)


def _rwkv_scan_kernel(*refs, chunk, n_chunks, has_vres, t_valid):
    if has_vres:
        (r_ref, k_ref, v_ref, tw_ref, ta_ref, tg_ref, tv_ref, vf_ref,
         w2_ref, a2_ref, g2_ref, v2_ref, vec_ref, s0_ref, y_ref, sout_ref, s_sc) = refs
    else:
        (r_ref, k_ref, v_ref, tw_ref, ta_ref, tg_ref,
         w2_ref, a2_ref, g2_ref, vec_ref, s0_ref, y_ref, sout_ref, s_sc) = refs
    ti = pl.program_id(2)
    c_len = chunk
    lanes = MXU_DIM
    rows4 = HEAD_BLOCK * c_len

    @pl.when(ti == 0)
    def _():
        s_sc[...] = s0_ref[0, 0]

    li = lax.broadcasted_iota(jnp.int32, (lanes, lanes), 0) // RWKV_HEAD
    lj = lax.broadcasted_iota(jnp.int32, (lanes, lanes), 1) // RWKV_HEAD
    seg = (li == lj).astype(F32)
    ci = lax.broadcasted_iota(jnp.int32, (c_len, c_len), 0)
    cj = lax.broadcasted_iota(jnp.int32, (c_len, c_len), 1)
    tri_c = (cj <= ci).astype(F32)
    lane_head = lax.broadcasted_iota(jnp.int32, (c_len, lanes), 1) // RWKV_HEAD
    ri = lax.broadcasted_iota(jnp.int32, (rows4, 2 * rows4), 0)
    rj = lax.broadcasted_iota(jnp.int32, (rows4, 2 * rows4), 1) & (rows4 - 1)
    strict = rj < ri
    incl = rj <= ri
    ei = lax.broadcasted_iota(jnp.int32, (rows4, rows4), 0)
    ej = lax.broadcasted_iota(jnp.int32, (rows4, rows4), 1)
    eye = (ei == ej).astype(F32)
    n_double = int(math.log2(c_len)) - 1

    def stack(x):
        return jnp.concatenate(
            [jnp.where(lane_head == hh, x, 0.0) for hh in range(HEAD_BLOCK)], axis=0)

    w0, a0, v0 = vec_ref[0:1, :], vec_ref[1:2, :], vec_ref[2:3, :]
    kkw, kaw = vec_ref[3:4, :], vec_ref[4:5, :]
    lnw, lnb, rk = vec_ref[5:6, :], vec_ref[6:7, :], vec_ref[7:8, :]

    def one_chunk(c, carry):
        sl = pl.ds(pl.multiple_of(c * c_len, c_len), c_len)
        r = r_ref[0, sl, :]
        k = k_ref[0, sl, :]
        v = v_ref[0, sl, :]
        logw = -_softplus(-(w0 + _dot(jnp.tanh(tw_ref[0, sl, :]), w2_ref[...]))) - 0.5
        dlog = -jnp.exp(logw)
        rate = _sigmoid(a0 + _dot(ta_ref[0, sl, :], a2_ref[...]))
        gate = _dot(_sigmoid(tg_ref[0, sl, :]), g2_ref[...])
        if has_vres:
            v = v + (vf_ref[0, sl, :] - v) * _sigmoid(v0 + _dot(tv_ref[0, sl, :], v2_ref[...]))
        kk = k * kkw
        kk = kk / jnp.maximum(jnp.sqrt(_dot(kk * kk, seg, HI)), 1e-12)
        k = k * (1.0 + (rate - 1.0) * kaw)
        if t_valid is not None:
            t_idx = ti * (n_chunks * c_len) + c * c_len + lax.broadcasted_iota(
                jnp.int32, (c_len, lanes), 0)
            live = t_idx < t_valid
            dlog = jnp.where(live, dlog, 0.0)
            kk = jnp.where(live, kk, 0.0)
            k = jnp.where(live, k, 0.0)
        cum = _dot(tri_c, dlog, HI)
        a_s = stack(-kk * jnp.exp(cum - dlog))
        r_s = stack(r * jnp.exp(cum))
        inv = jnp.exp(-cum)
        bk_s = jnp.concatenate([stack(kk * rate * inv), stack(k * inv)], axis=0)
        v_s = stack(v)
        g_end = jnp.exp(cum[c_len - 1:c_len, :])

        a_bk = jnp.where(strict, _dot_nt(a_s, bk_s), 0.0)
        r_bk = jnp.where(incl, _dot_nt(r_s, bk_s), 0.0)
        lmat = a_bk[:, :rows4]
        tinv = eye + lmat
        lpow = lmat
        for _ in range(n_double):
            lpow = _dot(lpow, lpow, HI)
            tinv = tinv + _dot(tinv, lpow, HI)
        s_old = s_sc[...]
        u = _dot(tinv, _dot_nt(a_s, s_old) + _dot(a_bk[:, rows4:], v_s), HI)
        uv = jnp.concatenate([u, v_s], axis=0)
        y_s = _dot_nt(r_s, s_old) + _dot(r_bk, uv)
        s_sc[...] = (s_old + _dot_tn(uv, bk_s)) * g_end
        y = y_s[0:c_len]
        for hh in range(1, HEAD_BLOCK):
            y = y + y_s[hh * c_len:(hh + 1) * c_len]

        mean = _dot(y, seg, HI) * (1.0 / RWKV_HEAD)
        yc = y - mean
        var = _dot(yc * yc, seg, HI) * (1.0 / RWKV_HEAD)
        yn = yc * lax.rsqrt(var + GN_EPS) * lnw + lnb
        bonus = _dot(r * k * rk, seg, HI) * v
        y_ref[0, sl, :] = ((yn + bonus) * gate).astype(y_ref.dtype)
        return carry

    lax.fori_loop(0, n_chunks, one_chunk, 0)

    @pl.when(ti == pl.num_programs(2) - 1)
    def _():
        sout_ref[0, 0] = s_sc[...]


def rwkv_scan(proj, v_first_proj, w2, a2, g2, v2, vec, s0_bd, t_real, chunk, tb):
    b, t, _ = proj.shape
    lanes = MXU_DIM
    tb = min(tb, t)
    assert t % tb == 0 and tb % chunk == 0
    has_vres = v_first_proj is not None
    t_valid = None if t_real == t else t_real

    def col(block0):
        return pl.BlockSpec((1, tb, lanes), lambda bi, hg, ti: (bi, ti, block0 + hg))

    def fixed(block):
        return pl.BlockSpec((1, tb, lanes), lambda bi, hg, ti: (bi, ti, block))

    def wcol():
        return pl.BlockSpec((LOWRANK_PAD, lanes), lambda bi, hg, ti: (0, hg))

    st = pl.BlockSpec((1, 1, lanes, lanes), lambda bi, hg, ti: (bi, hg, 0, 0))
    if has_vres:
        in_specs = [col(RW_R), col(RW_K), col(RW_V), fixed(RW_W1), fixed(RW_A1), fixed(RW_G1),
                    fixed(RW_V1), col(RW_V), wcol(), wcol(), wcol(), wcol()]
        args = [proj, proj, proj, proj, proj, proj, proj, v_first_proj, w2, a2, g2, v2]
    else:
        in_specs = [col(RW_R), col(RW_K), col(RW_V), fixed(RW_W1), fixed(RW_A1), fixed(RW_G1),
                    wcol(), wcol(), wcol()]
        args = [proj, proj, proj, proj, proj, proj, w2, a2, g2]
    in_specs += [pl.BlockSpec((SUBLANE, lanes), lambda bi, hg, ti: (0, hg)), st]
    args += [vec, s0_bd]
    return pl.pallas_call(
        functools.partial(_rwkv_scan_kernel, chunk=chunk, n_chunks=tb // chunk,
                          has_vres=has_vres, t_valid=t_valid),
        grid=(b, N_HEAD_BLOCKS, t // tb),
        in_specs=in_specs,
        out_specs=[pl.BlockSpec((1, tb, lanes), lambda bi, hg, ti: (bi, ti, hg)), st],
        out_shape=[jax.ShapeDtypeStruct((b, t, MAIN_W), BF16),
                   jax.ShapeDtypeStruct(s0_bd.shape, F32)],
        scratch_shapes=[pltpu.VMEM((lanes, lanes), F32)],
        compiler_params=_params("arbitrary", "arbitrary", "arbitrary"),
        name="rwkv_scan",
    )(*args)


def _to_block_diag(s):
    b = s.shape[0]
    s = s.reshape(b, N_HEAD_BLOCKS, HEAD_BLOCK, RWKV_HEAD, RWKV_HEAD)
    eye = jnp.eye(HEAD_BLOCK, dtype=s.dtype)
    bd = s[:, :, :, :, None, :] * eye[None, None, :, None, :, None]
    return bd.reshape(b, N_HEAD_BLOCKS, MXU_DIM, MXU_DIM)


def _from_block_diag(bd):
    b = bd.shape[0]
    x = bd.reshape(b, N_HEAD_BLOCKS, HEAD_BLOCK, RWKV_HEAD, HEAD_BLOCK, RWKV_HEAD)
    idx = jnp.arange(HEAD_BLOCK)
    x = x[:, :, idx, :, idx, :]
    return jnp.moveaxis(x, 0, 2).reshape(b, RWKV_HEADS, RWKV_HEAD, RWKV_HEAD)


def _mem_attn_kernel(q_ref, k_ref, v_ref, o_ref):
    scale = MEM_HEAD_DIM ** -0.5
    for hh in range(MEM_HEADS):
        sl = slice(hh * MEM_HEAD_DIM, (hh + 1) * MEM_HEAD_DIM)
        s = _dot_nt(q_ref[0, :, sl], k_ref[0, :, sl]) * scale
        p = jnp.exp(s - jnp.max(s, axis=-1, keepdims=True))
        p = p / jnp.sum(p, axis=-1, keepdims=True)
        o_ref[0, :, sl] = _dot(p, v_ref[0, :, sl]).astype(o_ref.dtype)


def mem_attend(qsrc, q_block, ksrc, k_block, vsrc, v_block, tq=512):
    b, t, _ = qsrc.shape
    m = ksrc.shape[1]
    tq = _row_tile(t, tq)
    return pl.pallas_call(
        _mem_attn_kernel,
        grid=(b, t // tq),
        in_specs=[pl.BlockSpec((1, tq, MEM_W), lambda bi, ti: (bi, ti, q_block)),
                  pl.BlockSpec((1, m, MEM_W), lambda bi, ti: (bi, 0, k_block)),
                  pl.BlockSpec((1, m, MEM_W), lambda bi, ti: (bi, 0, v_block))],
        out_specs=pl.BlockSpec((1, tq, MEM_W), lambda bi, ti: (bi, ti, 0)),
        out_shape=jax.ShapeDtypeStruct((b, t, MEM_W), BF16),
        compiler_params=_params("arbitrary", "arbitrary"),
        name="mem_attend",
    )(qsrc, ksrc, vsrc)


CMP_PAGE = 128
CMP_CHUNKS = CMP_PAGE // CMP_STRIDE
KV_LANES = NSA_KV_GROUPS * NSA_HEAD_DIM


def _chunk_proj_kernel(pt_ref, *refs, n_pages):
    del pt_ref
    page_refs = refs[:n_pages]
    w_ref, o0_ref, o1_ref, x_sc = refs[n_pages:]
    for kp in range(n_pages):
        for pos in range(CMP_STRIDE):
            x_sc[kp * CMP_CHUNKS:(kp + 1) * CMP_CHUNKS, pos * LANE:(pos + 1) * LANE] = (
                page_refs[kp][pl.ds(pos, CMP_CHUNKS, stride=CMP_STRIDE), :])
    res = _dot(x_sc[...].astype(BF16), w_ref[...])
    o0_ref[0] = res[:, :LANE]
    o1_ref[0] = res[:, LANE:]


def chunk_proj(rows2d, table, w_flat, n_pages, col_block=0):
    b, n_tab = table.shape
    assert n_tab % n_pages == 0

    def page_spec(kp):
        return pl.BlockSpec(
            (CMP_PAGE, LANE),
            lambda bi, pg, g, pt: (pt[bi, pg * n_pages + kp], col_block * NSA_KV_GROUPS + g))

    out_spec = pl.BlockSpec((1, n_pages * CMP_CHUNKS, LANE), lambda bi, pg, g, pt: (bi, pg, g))
    out_shape = jax.ShapeDtypeStruct((b, n_tab * CMP_CHUNKS, KV_LANES), F32)
    grid_spec = pltpu.PrefetchScalarGridSpec(
        num_scalar_prefetch=1,
        grid=(b, n_tab // n_pages, NSA_KV_GROUPS),
        in_specs=[page_spec(kp) for kp in range(n_pages)]
        + [pl.BlockSpec(w_flat.shape, lambda bi, pg, g, pt: (0, 0))],
        out_specs=[out_spec, out_spec],
        scratch_shapes=[pltpu.VMEM((n_pages * CMP_CHUNKS, CMP_STRIDE * LANE), F32)],
    )
    return pl.pallas_call(
        functools.partial(_chunk_proj_kernel, n_pages=n_pages),
        grid_spec=grid_spec,
        out_shape=[out_shape, out_shape],
        compiler_params=_params("arbitrary", "arbitrary", "arbitrary"),
        name="chunk_proj",
    )(table, *([rows2d] * n_pages), w_flat)


def _block_mlp_kernel(p0_ref, p1_ref, pe_ref, w1_ref, b1_ref, w2_ref, o_ref):
    const = _dot(pe_ref[...], w1_ref[...])[0:1, :] + b1_ref[...]
    w2 = w2_ref[...]
    for g in range(NSA_KV_GROUPS):
        sl = slice(g * LANE, (g + 1) * LANE)
        hid = const + p0_ref[0, :, sl] + p1_ref[0, :, sl]
        o_ref[0, :, sl] = _dot(_gelu_tanh(hid), w2)


def block_mlp(p0, p1, pe, w1, b1, w2, tb=1024):
    b, nb, _ = p0.shape
    tb = _row_tile(nb, tb)
    pe8 = jnp.concatenate([pe.reshape(1, -1), jnp.zeros((SUBLANE - 1, pe.size), F32)], axis=0)
    spec = pl.BlockSpec((1, tb, KV_LANES), lambda bi, i: (bi, i, 0))
    full = lambda a: pl.BlockSpec(a.shape, lambda bi, i: (0,) * a.ndim)
    b1r = b1.reshape(1, -1)
    return pl.pallas_call(
        _block_mlp_kernel,
        grid=(b, nb // tb),
        in_specs=[spec, spec, full(pe8), full(w1), full(b1r), full(w2)],
        out_specs=spec,
        out_shape=jax.ShapeDtypeStruct((b, nb, KV_LANES), F32),
        compiler_params=_params("arbitrary", "arbitrary"),
        name="block_mlp",
    )(p0, p1, pe8, w1, b1r, w2)


def _w1_flat(w1):
    r = CMP_BLOCK // CMP_STRIDE
    e = w1.shape[1]
    w = w1.reshape(r, CMP_STRIDE, NSA_HEAD_DIM, e)
    return jnp.transpose(w, (1, 2, 0, 3)).reshape(CMP_STRIDE * NSA_HEAD_DIM, r * e).astype(BF16)


def compress(parts, nb, pe, w1, b1, w2):
    return block_mlp(parts[0][:, :nb], parts[1][:, 1:nb + 1], pe, w1, b1, w2)


def _rel_bucket(dist):
    n = jnp.maximum(dist, 0)
    max_exact = REL_BUCKETS // 2
    nf = jnp.maximum(n, 1).astype(F32)
    large = max_exact + (jnp.log(nf / max_exact) / math.log(REL_MAX_DIST / max_exact)
                         * (REL_BUCKETS - max_exact)).astype(jnp.int32)
    return jnp.where(n < max_exact, n, jnp.minimum(large, REL_BUCKETS - 1))


def _bias_lookup(rel_bias, dist):
    tab = rel_bias.astype(F32).reshape(REL_BUCKETS, NSA_KV_GROUPS, NSA_HPG)
    out = tab[_rel_bucket(dist)]
    return jnp.moveaxis(out, (-2, -1), (0, 1))


def _cmp_to_sel(nb_pad, nb, nsb_pad, nsb):
    i = jnp.arange(nb_pad)[:, None]
    j = jnp.arange(nsb_pad)[None, :]
    start = i * CMP_STRIDE
    hit = (start < (j + 1) * SEL_BLOCK) & (start + CMP_BLOCK > j * SEL_BLOCK) & (i < nb) & (j < nsb)
    return hit.astype(F32)


def _nsa_prompt_kernel(q_ref, gt_ref, kc_ref, vc_ref, bc_ref, band_ref, ks_ref, vs_ref,
                       kw_ref, vw_ref, m_ref, o_ref, *, nb, nsb):
    tq = NSA_TQ
    rows = NSA_HPG * tq
    i = pl.program_id(2)
    scale = NSA_HEAD_DIM ** -0.5
    qb = q_ref[0]
    q3 = jnp.concatenate([qb[:, hh * LANE:(hh + 1) * LANE] for hh in range(NSA_HPG)],
                         axis=0).astype(BF16)
    row_q = lax.broadcasted_iota(jnp.int32, (rows, LANE), 0) & (tq - 1)
    lane = lax.broadcasted_iota(jnp.int32, (rows, LANE), 1)
    t_pos = i * tq + row_q

    s_c = _dot_nt(q3, kc_ref[0].astype(BF16)) * scale + bc_ref[0].reshape(rows, LANE)
    m_c = (t_pos - (lane * CMP_STRIDE + (CMP_BLOCK - 1)) >= 0) & (lane < nb)
    p_c = _masked_softmax(s_c, m_c)
    o_c = _dot(p_c, vc_ref[0])

    imp3 = _dot(p_c, m_ref[...], HI)
    imp = imp3[0:tq]
    for hh in range(1, NSA_HPG):
        imp = imp + imp3[hh * tq:(hh + 1) * tq]
    jb = lax.broadcasted_iota(jnp.int32, (tq, nsb), 1)
    cur = (i * tq + lax.broadcasted_iota(jnp.int32, (tq, nsb), 0)) // SEL_BLOCK
    valid = jb <= cur
    forced = valid & ((jb == 0) | (jb > cur - SEL_LOCAL))
    imp = jnp.where(forced, FORCE_SCORE, jnp.where(valid, imp, -FORCE_SCORE))
    rank = jnp.zeros((tq, nsb), jnp.int32)
    for jp in range(nsb):
        col = imp[:, jp:jp + 1]
        rank = rank + ((col > imp) | ((col == imp) & (jb > jp))).astype(jnp.int32)
    sel = (rank < min(SEL_TOPK, nsb)).astype(F32)
    sel3 = jnp.concatenate([sel] * NSA_HPG, axis=0).astype(BF16)
    blk_row = lax.broadcasted_iota(jnp.int32, (nsb, LANE), 0)
    blk_lane = lax.broadcasted_iota(jnp.int32, (nsb, LANE), 1) // SEL_BLOCK
    blocks_per_tile = tq // SEL_BLOCK

    def attend(k_ref, v_ref, lo, use_sel):
        def body(kt, carry):
            m_run, l_run, acc = carry
            ksl = pl.ds(pl.multiple_of(kt * tq, tq), tq)
            band = band_ref[0, jnp.minimum(i - kt, 2)].reshape(rows, LANE)
            s = _dot_nt(q3, k_ref[0, ksl, :].astype(BF16)) * scale + band
            dist = t_pos - (kt * tq + lane)
            if use_sel:
                expand = (blk_row == kt * blocks_per_tile + blk_lane).astype(BF16)
                mask = (_dot(sel3, expand) > 0.5) & (dist >= 0)
            else:
                mask = (dist >= 0) & (dist < WINDOW)
            s = jnp.where(mask, s, MASK_NEG)
            m_new = jnp.maximum(m_run, jnp.max(s, axis=-1, keepdims=True))
            alpha = jnp.exp(m_run - m_new)
            p = jnp.exp(s - m_new) * mask.astype(F32)
            l_new = alpha * l_run + jnp.sum(p, axis=-1, keepdims=True)
            acc = alpha * acc + _dot(p, v_ref[0, ksl, :])
            return m_new, l_new, acc

        init = (jnp.full((rows, 1), MASK_NEG, F32), jnp.zeros((rows, 1), F32),
                jnp.zeros((rows, LANE), F32))
        _, l_fin, acc = lax.fori_loop(lo, i + 1, body, init)
        return acc / jnp.where(l_fin > 0, l_fin, 1.0)

    o_s = attend(ks_ref, vs_ref, 0, True)
    o_w = attend(kw_ref, vw_ref, jnp.maximum(i - WINDOW // tq, 0), False)

    gt = _sigmoid(gt_ref[0, 0])
    gcol = lambda br: jnp.concatenate(
        [gt[:, br * NSA_HPG + hh:br * NSA_HPG + hh + 1] for hh in range(NSA_HPG)], axis=0)
    o = gcol(0) * o_c + gcol(1) * o_s + gcol(2) * o_w
    for hh in range(NSA_HPG):
        o_ref[0, :, hh * LANE:(hh + 1) * LANE] = o[hh * tq:(hh + 1) * tq].astype(o_ref.dtype)


def nsa_prompt(proj, gates_t, kc, vc, bias_c, band, side, sel_map, nb, nsb):
    b, t, _ = proj.shape
    tq = NSA_TQ
    qw = NSA_HPG * LANE
    kv = lambda off: pl.BlockSpec((1, t, LANE), lambda bi, g, i: (bi, 0, off * NSA_KV_GROUPS + g))
    return pl.pallas_call(
        functools.partial(_nsa_prompt_kernel, nb=nb, nsb=nsb),
        grid=(b, NSA_KV_GROUPS, t // tq),
        in_specs=[pl.BlockSpec((1, tq, qw), lambda bi, g, i: (bi, i, g)),
                  pl.BlockSpec((1, 1, tq, NSA_HPG * 3), lambda bi, g, i: (bi, g, i, 0)),
                  pl.BlockSpec((1, LANE, LANE), lambda bi, g, i: (bi, 0, g)),
                  pl.BlockSpec((1, LANE, LANE), lambda bi, g, i: (bi, 0, g)),
                  pl.BlockSpec((1, NSA_HPG, tq, LANE), lambda bi, g, i: (g, 0, i, 0)),
                  pl.BlockSpec((1, 3, NSA_HPG, tq, LANE), lambda bi, g, i: (g, 0, 0, 0, 0)),
                  kv(2), kv(3), kv(4), kv(5),
                  pl.BlockSpec(sel_map.shape, lambda bi, g, i: (0, 0))],
        out_specs=pl.BlockSpec((1, tq, qw), lambda bi, g, i: (bi, i, g)),
        out_shape=jax.ShapeDtypeStruct((b, t, MAIN_W), BF16),
        compiler_params=_params("arbitrary", "arbitrary", "arbitrary"),
        name="nsa_prompt",
    )(proj, gates_t, kc, vc, bias_c, band, side, side, side, side, sel_map)


def _nsa_dec_cmp_kernel(q_ref, kc_ref, vc_ref, bc_ref, m_ref, oc_ref, idx_ref, *, nb, nsb, t_pos):
    scale = NSA_HEAD_DIM ** -0.5
    nbp = kc_ref.shape[1]
    nsp = m_ref.shape[1]
    n_sel = min(SEL_TOPK, nsb)
    lane_b = lax.broadcasted_iota(jnp.int32, (SUBLANE, nbp), 1)
    m_c = (t_pos - (lane_b * CMP_STRIDE + (CMP_BLOCK - 1)) >= 0) & (lane_b < nb)
    row_s = lax.broadcasted_iota(jnp.int32, (SUBLANE, nsp), 0)
    jb = lax.broadcasted_iota(jnp.int32, (1, nsp), 1)
    cur = t_pos // SEL_BLOCK
    valid = jb <= cur
    forced = valid & ((jb == 0) | (jb > cur - SEL_LOCAL))
    out_lane = lax.broadcasted_iota(jnp.int32, (1, LANE), 1)
    idx_rows = []
    for g in range(NSA_KV_GROUPS):
        q3 = jnp.concatenate(
            [q_ref[0, 0:1, (g * NSA_HPG + hh) * LANE:(g * NSA_HPG + hh + 1) * LANE]
             for hh in range(NSA_HPG)] + [jnp.zeros((SUBLANE - NSA_HPG, LANE), F32)], axis=0)
        sl = slice(g * LANE, (g + 1) * LANE)
        s_c = _dot_nt(q3, kc_ref[0, :, sl]) * scale + bc_ref[g]
        p_c = _masked_softmax(s_c, m_c)
        oc_ref[0, g] = _dot(p_c, vc_ref[0, :, sl])
        imp8 = jnp.where(row_s < NSA_HPG, _dot(p_c, m_ref[...], HI), 0.0)
        imp = jnp.sum(imp8, axis=0, keepdims=True)
        imp = jnp.where(forced, FORCE_SCORE, jnp.where(valid, imp, -FORCE_SCORE))
        imp = jnp.where(jb < nsb, imp, -jnp.inf)
        jbf = jb.astype(F32)
        picks = jnp.zeros((1, LANE), F32)
        for kk in range(n_sel):
            best = jnp.max(imp, axis=-1, keepdims=True)
            arg = jnp.min(jnp.where(imp == best, jbf, float(nsp)), axis=-1, keepdims=True)
            picks = jnp.where(out_lane == kk, arg, picks)
            imp = jnp.where(jbf == arg, -jnp.inf, imp)
        idx_rows.append(picks.astype(jnp.int32))
    idx_rows.append(jnp.zeros((SUBLANE - NSA_KV_GROUPS, LANE), jnp.int32))
    idx_ref[0] = jnp.concatenate(idx_rows, axis=0)


def nsa_dec_cmp(proj, kc, vc, bias_c, sel_map, nb, nsb, t_pos):
    b, tp, _ = proj.shape
    nbp = kc.shape[1]
    return pl.pallas_call(
        functools.partial(_nsa_dec_cmp_kernel, nb=nb, nsb=nsb, t_pos=t_pos),
        grid=(b,),
        in_specs=[pl.BlockSpec((1, tp, MAIN_W), lambda bi: (bi, 0, 0)),
                  pl.BlockSpec((1, nbp, KV_LANES), lambda bi: (bi, 0, 0)),
                  pl.BlockSpec((1, nbp, KV_LANES), lambda bi: (bi, 0, 0)),
                  pl.BlockSpec(bias_c.shape, lambda bi: (0, 0, 0)),
                  pl.BlockSpec(sel_map.shape, lambda bi: (0, 0))],
        out_specs=[pl.BlockSpec((1, NSA_KV_GROUPS, SUBLANE, LANE), lambda bi: (bi, 0, 0, 0)),
                   pl.BlockSpec((1, SUBLANE, LANE), lambda bi: (bi, 0, 0))],
        out_shape=[jax.ShapeDtypeStruct((b, NSA_KV_GROUPS, SUBLANE, LANE), F32),
                   jax.ShapeDtypeStruct((b, SUBLANE, LANE), jnp.int32)],
        compiler_params=_params("arbitrary"),
        name="nsa_dec_cmp",
    )(proj, kc, vc, bias_c, sel_map)


def _nsa_dec_sel_kernel(phys_ref, isnew_ref, *refs, n_sel, n_win, t_pos, win_start):
    del phys_ref
    kb_refs = refs[:n_sel]
    vb_refs = refs[n_sel:2 * n_sel]
    (q_ref, new_ref, ds_ref, bs_ref, wk_ref, wv_ref, bw_ref, oc_ref, gt_ref, o_ref) = refs[2 * n_sel:]
    bi = pl.program_id(0)
    g = pl.program_id(1)
    scale = NSA_HEAD_DIM ** -0.5
    q3 = q_ref[0, 0]
    new_rows = new_ref[0, 0]
    pad_blk = jnp.zeros((SEL_BLOCK - 1, LANE), F32)
    new_k = jnp.concatenate([new_rows[0:1], pad_blk], axis=0)
    new_v = jnp.concatenate([new_rows[1:2], pad_blk], axis=0)
    ks, vs = [], []
    for kk in range(n_sel):
        fresh = isnew_ref[bi, g, kk] > 0
        ks.append(jnp.where(fresh, new_k, kb_refs[kk][0]))
        vs.append(jnp.where(fresh, new_v, vb_refs[kk][0]))
    ks = jnp.concatenate(ks, axis=0)
    vs = jnp.concatenate(vs, axis=0)
    s_s = _dot_nt(q3, ks) * scale + bs_ref[0, 0]
    p_s = _masked_softmax(s_s, ds_ref[0, 0] >= 0)
    o_s = _dot(p_s, vs)

    pad_w = jnp.zeros((SUBLANE - 1, LANE), F32)
    kw = jnp.concatenate([wk_ref[0], new_rows[2:3], pad_w], axis=0)
    vw = jnp.concatenate([wv_ref[0], new_rows[3:4], pad_w], axis=0)
    nw = kw.shape[0]
    pos = win_start + lax.broadcasted_iota(jnp.int32, (SUBLANE, nw), 1)
    d_w = t_pos - pos
    m_w = (d_w >= 0) & (d_w < WINDOW) & (pos >= 0) & (pos - win_start < n_win)
    s_w = _dot_nt(q3, kw) * scale + bw_ref[0]
    p_w = _masked_softmax(s_w, m_w)
    o_w = _dot(p_w, vw)

    gt = _sigmoid(gt_ref[0, 0])
    o_ref[0, 0] = gt[:, 0:1] * oc_ref[0, 0] + gt[:, 1:2] * o_s + gt[:, 2:3] * o_w


def nsa_dec_sel(phys, isnew, pool_k, pool_v, q8, new_rows, d_s, bias_s, win_k, win_v, bias_w,
                o_c, gates8, n_win, t_pos, win_start):
    b = q8.shape[0]
    n_sel = phys.shape[-1]
    n_keys = n_sel * SEL_BLOCK
    nw = win_k.shape[1]
    nwp = nw + SUBLANE

    def blk_spec(kk):
        return pl.BlockSpec((1, SEL_BLOCK, LANE), lambda bi, g, ph, nf: (ph[bi, g, kk], 0, g))

    per_bg = lambda *shape: pl.BlockSpec((1, 1) + shape, lambda bi, g, ph, nf: (bi, g) + (0,) * len(shape))
    grid_spec = pltpu.PrefetchScalarGridSpec(
        num_scalar_prefetch=2,
        grid=(b, NSA_KV_GROUPS),
        in_specs=[blk_spec(kk) for kk in range(n_sel)] + [blk_spec(kk) for kk in range(n_sel)]
        + [per_bg(SUBLANE, LANE), per_bg(SUBLANE, LANE), per_bg(1, n_keys), per_bg(SUBLANE, n_keys),
           pl.BlockSpec((1, nw, LANE), lambda bi, g, ph, nf: (bi, 0, g)),
           pl.BlockSpec((1, nw, LANE), lambda bi, g, ph, nf: (bi, 0, g)),
           pl.BlockSpec((1, SUBLANE, nwp), lambda bi, g, ph, nf: (g, 0, 0)),
           per_bg(SUBLANE, LANE), per_bg(SUBLANE, SUBLANE)],
        out_specs=per_bg(SUBLANE, LANE),
    )
    return pl.pallas_call(
        functools.partial(_nsa_dec_sel_kernel, n_sel=n_sel, n_win=nw + 1, t_pos=t_pos,
                          win_start=win_start),
        grid_spec=grid_spec,
        out_shape=jax.ShapeDtypeStruct((b, NSA_KV_GROUPS, SUBLANE, LANE), F32),
        compiler_params=_params("arbitrary", "arbitrary"),
        name="nsa_dec_sel",
    )(phys, isnew, *([pool_k] * n_sel), *([pool_v] * n_sel), q8, new_rows, d_s, bias_s,
      win_k, win_v, bias_w, o_c, gates8)


def _pad_cols(w, n):
    return jnp.pad(w, ((0, 0), (0, n - w.shape[1])))


def _pad_rows(w, n):
    return jnp.pad(w, ((0, n - w.shape[0]), (0, 0)))


def _prep_weights(P):
    W = {}
    W['wg'] = [[_pad_cols(P['ffn_gu'][l, i, :, :D_FF], D_FF_PAD).astype(BF16) for i in range(2)]
               for l in range(DEPTH)]
    W['wu'] = [[_pad_cols(P['ffn_gu'][l, i, :, D_FF:], D_FF_PAD).astype(BF16) for i in range(2)]
               for l in range(DEPTH)]
    W['wd'] = [[_pad_rows(P['ffn_d'][l, i], D_FF_PAD).astype(BF16) for i in range(2)]
               for l in range(DEPTH)]
    W['w_out'] = [P['w_out'][l].astype(BF16) for l in range(DEPTH)]
    W['rw_in'], W['rw_w2'], W['rw_a2'], W['rw_g2'], W['rw_v2'], W['rw_vec'] = [], [], [], [], [], []
    lp = LOWRANK_PAD
    for l in range(N_A):
        w_in = P['w_in_a'][l]
        v1 = P['rw_v1'][l - 1] if l > 0 else jnp.zeros((D_MODEL, lp), F32)
        W['rw_in'].append(jnp.concatenate([
            w_in[:, :MAIN_W], _pad_cols(P['rw_w1'][l], lp), w_in[:, MAIN_W:2 * MAIN_W],
            w_in[:, 2 * MAIN_W:3 * MAIN_W], _pad_cols(v1, lp), _pad_cols(P['rw_a1'][l], lp),
            _pad_cols(P['rw_g1'][l], lp), w_in[:, 3 * MAIN_W:]], axis=1).astype(BF16))
        W['rw_w2'].append(_pad_rows(P['rw_w2'][l], lp))
        W['rw_a2'].append(_pad_rows(P['rw_a2'][l], lp))
        W['rw_g2'].append(_pad_rows(P['rw_g2'][l], lp))
        W['rw_v2'].append(_pad_rows(P['rw_v2'][l - 1], lp) if l > 0 else None)
        v0 = P['rw_v0'][l - 1] if l > 0 else jnp.zeros((MAIN_W,), F32)
        W['rw_vec'].append(jnp.stack([P['rw_w0'][l], P['rw_a0'][l], v0, P['rw_kk'][l], P['rw_ka'][l],
                                      P['rw_lnw'][l], P['rw_lnb'][l], P['rw_rk'][l].reshape(-1)]))
    W['w_in_b'] = []
    for l in range(DEPTH - N_A):
        w = P['w_in_b'][l]
        W['w_in_b'].append(jnp.concatenate([
            w[:, :MAIN_W], w[:, MAIN_W + GATE_W:], _pad_cols(w[:, MAIN_W:MAIN_W + GATE_W], MXU_DIM)],
            axis=1).astype(BF16))
    W['w_kv'] = P['w_kv'].astype(BF16)
    W['w_mem_kv'] = [P['w_mem_kv'][l].astype(BF16) for l in range(DEPTH)]
    W['cmp_w1_flat'] = [_w1_flat(P['cmp_w1'][c]) for c in range(2)]
    return W


B_QM_BLOCK = MAIN_W // MEM_W
B_GATE_OFF = MAIN_W + MEM_W


def _ffn(h2, norms_l, first, W, l, i):
    hid = norm_swiglu(h2, norms_l[first], W['wg'][l][i], W['wu'][l][i])
    return matmul_norm_res(hid, W['wd'][l][i], norms_l[first + 1], h2, 0.5)


def _trunk(x, t_real, P, W, shift0, wkv0_bd, mem_src, make_side, attend, chunk, tb):
    b, t, d = x.shape
    h2 = x.reshape(b * t, d)
    shifts, states = [], []
    v_first_proj, ctx, side_state = None, None, None
    for l in range(DEPTH):
        n = P['norms'][l]
        h2 = _ffn(h2, n, 0, W, l, 0)
        if l < N_A:
            proj, last = rwkv_in_proj(h2.reshape(b, t, d), shift0[l], n[2], P['rw_mu'][l],
                                      W['rw_in'][l], t_real)
            main, s_bd = rwkv_scan(proj, v_first_proj if l > 0 else None, W['rw_w2'][l], W['rw_a2'][l],
                                   W['rw_g2'][l], W['rw_v2'][l], W['rw_vec'][l], wkv0_bd[l],
                                   t_real, chunk, tb)
            if l == 0:
                v_first_proj = proj
            shifts.append(last[:, 0])
            states.append(_from_block_diag(s_bd))
            q_src, q_block = proj, RW_QM * MXU_DIM // MEM_W
        else:
            proj = norm_matmul(h2, n[2], W['w_in_b'][l - N_A], tn=768).reshape(b, t, -1)
            main = attend(proj, ctx)
            q_src, q_block = proj, B_QM_BLOCK
        mk, kb, mv, vb = mem_src(l)
        mo = mem_attend(q_src, q_block, mk, kb, mv, vb)
        mix = jnp.concatenate([main, mo], axis=-1).reshape(b * t, d)
        h2 = matmul_norm_res(mix, W['w_out'][l], n[3], h2, 1.0)
        h2 = _ffn(h2, n, 4, W, l, 1)
        if l == N_A - 1:
            side = norm_matmul(h2, P['kv_norm'], W['w_kv'], tn=768).reshape(b, t, -1)
            ctx, side_state = make_side(side)
    return h2.reshape(b, t, d), jnp.stack(shifts), jnp.stack(states), side_state


def kernel(x_prompt, x_sample, mem_prompt, state_wkv, state_shift, cache_mem_k, cache_mem_v,
           cache_cmp_k, cache_cmp_v, cache_slc_k, cache_slc_v, cache_win_k, cache_win_v, page_table,
           norms, ffn_gu, ffn_d, w_in_a, w_in_b, w_out, mem_norm, w_mem_kv, kv_norm, w_kv,
           cmp_pe, cmp_w1, cmp_b1, cmp_w2, rel_bias,
           rw_mu, rw_w0, rw_w1, rw_w2, rw_a0, rw_a1, rw_a2, rw_g1, rw_g2, rw_v0, rw_v1, rw_v2,
           rw_kk, rw_ka, rw_rk, rw_lnw, rw_lnb):
    P = dict(norms=norms, ffn_gu=ffn_gu, ffn_d=ffn_d, w_in_a=w_in_a, w_in_b=w_in_b, w_out=w_out,
             kv_norm=kv_norm, w_kv=w_kv, w_mem_kv=w_mem_kv, cmp_w1=cmp_w1, rw_mu=rw_mu, rw_w0=rw_w0,
             rw_w1=rw_w1, rw_w2=rw_w2, rw_a0=rw_a0, rw_a1=rw_a1, rw_a2=rw_a2, rw_g1=rw_g1,
             rw_g2=rw_g2, rw_v0=rw_v0, rw_v1=rw_v1, rw_v2=rw_v2, rw_kk=rw_kk, rw_ka=rw_ka,
             rw_rk=rw_rk, rw_lnw=rw_lnw, rw_lnb=rw_lnb)
    W = _prep_weights(P)
    G, dh = NSA_KV_GROUPS, NSA_HEAD_DIM
    split_side = lambda side, bx, t: [side[:, :t, c * KV_LANES:(c + 1) * KV_LANES].reshape(bx, t, G, dh)
                                      for c in range(6)]

    def cmp_mlp(parts, nb, c):
        return compress(parts, nb, cmp_pe[c], cmp_w1[c], cmp_b1[c], cmp_w2[c])

    bp, tp, d = x_prompt.shape
    n_mem = mem_prompt.shape[1]
    mem2 = mem_prompt.reshape(bp * n_mem, d)
    p_mkv = [norm_matmul(mem2, mem_norm[l], W['w_mem_kv'][l], tn=512).reshape(bp, n_mem, 2 * MEM_W)
             for l in range(DEPTH)]
    p_mem_k = jnp.stack([m[..., :MEM_W].reshape(bp, n_mem, MEM_HEADS, MEM_HEAD_DIM) for m in p_mkv])
    p_mem_v = jnp.stack([m[..., MEM_W:].reshape(bp, n_mem, MEM_HEADS, MEM_HEAD_DIM) for m in p_mkv])

    nb_p = tp // CMP_STRIDE - 1
    nsb_p = tp // SEL_BLOCK
    nq_tiles = tp // NSA_TQ
    t_all = jnp.arange(tp)
    c_end = jnp.arange(LANE) * CMP_STRIDE + (CMP_BLOCK - 1)
    assert nb_p <= LANE
    bias_c_p = _bias_lookup(rel_bias, t_all[:, None] - c_end[None, :])
    ii = jnp.arange(NSA_TQ)
    band_p = jnp.stack([_bias_lookup(rel_bias, dd * NSA_TQ + ii[:, None] - ii[None, :])
                        for dd in range(3)], axis=1)
    assert 2 * NSA_TQ - (NSA_TQ - 1) >= REL_MAX_DIST
    sel_map_p = _cmp_to_sel(LANE, nb_p, nsb_p, nsb_p)
    ident = jnp.arange(bp * tp // CMP_PAGE, dtype=jnp.int32).reshape(bp, tp // CMP_PAGE)

    def prompt_side(side):
        rows = side.reshape(bp * tp, -1)
        kc = cmp_mlp(chunk_proj(rows, ident, W['cmp_w1_flat'][0], tp // CMP_PAGE, 0), nb_p, 0)
        vc = cmp_mlp(chunk_proj(rows, ident, W['cmp_w1_flat'][1], tp // CMP_PAGE, 1), nb_p, 1)
        padb = ((0, 0), (0, LANE - nb_p), (0, 0))
        wb = min(WINDOW, tp)
        kc_r, vc_r, ks, vs, kw, vw = split_side(side, bp, tp)
        return ((jnp.pad(kc, padb), jnp.pad(vc, padb), side),
                (kc_r, vc_r, ks, vs, kw[:, tp - wb:], vw[:, tp - wb:]))

    def prompt_attend(proj, ctx):
        kc, vc, side = ctx
        gates = proj[..., B_GATE_OFF:B_GATE_OFF + GATE_W].reshape(bp, tp, 3, G, NSA_HPG)
        gates_t = jnp.transpose(gates, (0, 3, 1, 2, 4)).reshape(bp, G, tp, 3 * NSA_HPG)
        return nsa_prompt(proj, gates_t, kc, vc, bias_c_p, band_p, side, sel_map_p, nb_p, nsb_p)

    zeros_shift = jnp.zeros((N_A, bp, d), F32)
    zeros_state = jnp.zeros((N_A, bp, N_HEAD_BLOCKS, MXU_DIM, MXU_DIM), F32)
    y_prompt, p_shift, p_wkv, p_side = _trunk(
        x_prompt, tp, P, W, zeros_shift, zeros_state,
        lambda l: (p_mkv[l], 0, p_mkv[l], 1), prompt_side, prompt_attend, chunk=64, tb=256)
    p_cmp_k, p_cmp_v, p_slc_k, p_slc_v, p_win_k, p_win_v = p_side

    bd, s_new, _ = x_sample.shape
    assert s_new == 1
    ts = SUBLANE
    xs = jnp.pad(x_sample, ((0, 0), (0, ts - s_new), (0, 0)))
    past_len = page_table.shape[1] * CMP_PAGE
    n_past_blk = past_len // SEL_BLOCK
    blk_per_page = CMP_PAGE // SEL_BLOCK
    nsb_s = n_past_blk + 1
    t_pos = past_len
    nc_s = -(-(past_len + s_new) // CMP_STRIDE)
    nb_s = nc_s - 1
    wb_s = cache_win_k.shape[1]
    win_start = past_len - wb_s
    mem_k2 = cache_mem_k.reshape(DEPTH, bd, n_mem, MEM_W)
    mem_v2 = cache_mem_v.reshape(DEPTH, bd, n_mem, MEM_W)
    nsp = -(-nsb_s // LANE) * LANE
    sel_map_s = _cmp_to_sel(nb_s, nb_s, nsp, nsb_s)
    c_end_s = jnp.arange(nb_s) * CMP_STRIDE + (CMP_BLOCK - 1)
    bias_c_s = _bias_lookup(rel_bias, t_pos - c_end_s)
    bias_c_s = jnp.pad(bias_c_s, ((0, 0), (0, SUBLANE - NSA_HPG), (0, 0)))
    nwp = wb_s + SUBLANE
    bias_w_s = _bias_lookup(rel_bias, t_pos - (win_start + jnp.arange(nwp)))
    bias_w_s = jnp.pad(bias_w_s, ((0, 0), (0, SUBLANE - NSA_HPG), (0, 0)))
    ident_s = jnp.arange(bd, dtype=jnp.int32).reshape(bd, 1)
    pages_per_step = math.gcd(32, page_table.shape[1])

    def sample_side(side):
        new = side[:, :s_new]
        parts = []
        for c, pool in enumerate((cache_cmp_k, cache_cmp_v)):
            past = chunk_proj(pool.reshape(-1, KV_LANES), page_table, W['cmp_w1_flat'][c], pages_per_step)
            fresh_page = jnp.pad(new[:, :, c * KV_LANES:(c + 1) * KV_LANES],
                                 ((0, 0), (0, CMP_PAGE - s_new), (0, 0))).reshape(bd * CMP_PAGE, KV_LANES)
            fresh = chunk_proj(fresh_page, ident_s, W['cmp_w1_flat'][c], 1)
            n_fresh = nc_s - past[0].shape[1]
            parts.append([jnp.concatenate([p, f[:, :n_fresh]], axis=1) for p, f in zip(past, fresh)])
        kc = cmp_mlp(parts[0], nb_s, 0)
        vc = cmp_mlp(parts[1], nb_s, 1)
        kc_n, vc_n, ks_n, vs_n, kw_n, vw_n = split_side(side, bd, s_new)
        s_win_k = jnp.concatenate([cache_win_k, kw_n], axis=1)[:, s_new:]
        s_win_v = jnp.concatenate([cache_win_v, vw_n], axis=1)[:, s_new:]
        return (kc, vc, new), (kc_n, vc_n, ks_n, vs_n, s_win_k, s_win_v)

    def sample_attend(proj, ctx):
        kc, vc, new = ctx
        o_c, idx8 = nsa_dec_cmp(proj, kc, vc, bias_c_s, sel_map_s, nb_s, nsb_s, t_pos)
        idx = idx8[:, :G, :SEL_TOPK]
        is_new = idx >= n_past_blk
        jp = jnp.minimum(idx, n_past_blk - 1)
        phys = (jnp.take_along_axis(page_table[:, None, :], jp // blk_per_page, axis=2) * blk_per_page
                + jp % blk_per_page)
        k_pos = (idx[..., None] * SEL_BLOCK + jnp.arange(SEL_BLOCK)).reshape(bd, G, 1, -1)
        d_s = t_pos - k_pos
        tab = rel_bias.astype(F32).reshape(REL_BUCKETS, G, NSA_HPG)
        bias_s = tab[_rel_bucket(d_s[:, :, 0]), jnp.arange(G)[None, :, None]]
        bias_s = jnp.pad(jnp.transpose(bias_s, (0, 1, 3, 2)),
                         ((0, 0), (0, 0), (0, SUBLANE - NSA_HPG), (0, 0)))
        q8 = jnp.pad(proj[:, 0, :MAIN_W].reshape(bd, G, NSA_HPG, dh),
                     ((0, 0), (0, 0), (0, SUBLANE - NSA_HPG), (0, 0)))
        new_rows = jnp.pad(jnp.transpose(new[:, 0, 2 * KV_LANES:].reshape(bd, 4, G, dh), (0, 2, 1, 3)),
                           ((0, 0), (0, 0), (0, SUBLANE - 4), (0, 0)))
        gates = proj[:, 0, B_GATE_OFF:B_GATE_OFF + GATE_W].reshape(bd, 3, G, NSA_HPG)
        gates8 = jnp.pad(jnp.transpose(gates, (0, 2, 3, 1)),
                         ((0, 0), (0, 0), (0, SUBLANE - NSA_HPG), (0, SUBLANE - 3)))
        o = nsa_dec_sel(phys.astype(jnp.int32), is_new.astype(jnp.int32),
                        cache_slc_k.reshape(-1, SEL_BLOCK, KV_LANES),
                        cache_slc_v.reshape(-1, SEL_BLOCK, KV_LANES),
                        q8, new_rows, d_s.astype(jnp.int32), bias_s,
                        cache_win_k.reshape(bd, wb_s, KV_LANES), cache_win_v.reshape(bd, wb_s, KV_LANES),
                        bias_w_s, o_c, gates8, wb_s, t_pos, win_start)
        main = o[:, :, :NSA_HPG].reshape(bd, 1, MAIN_W)
        return jnp.pad(main, ((0, 0), (0, ts - 1), (0, 0))).astype(BF16)

    y_s, s_shift, s_wkv, s_side = _trunk(
        xs, s_new, P, W, state_shift, jnp.stack([_to_block_diag(state_wkv[l]) for l in range(N_A)]),
        lambda l: (mem_k2[l], 0, mem_v2[l], 0), sample_side, sample_attend, chunk=SUBLANE, tb=SUBLANE)
    y_sample = y_s[:, :s_new]
    s_cmp_k, s_cmp_v, s_slc_k, s_slc_v, s_win_k, s_win_v = s_side

    return (y_prompt, y_sample, p_mem_k, p_mem_v, p_wkv, p_shift,
            p_cmp_k, p_cmp_v, p_slc_k, p_slc_v, p_win_k, p_win_v,
            s_wkv, s_shift, s_cmp_k, s_cmp_v, s_slc_k, s_slc_v, s_win_k, s_win_v)
```

```python
import functools
import math

import jax
import jax.numpy as jnp
from jax import lax
from jax.experimental import pallas as pl
from jax.experimental.pallas import tpu as pltpu

F32 = jnp.float32
BF16 = jnp.bfloat16
HI = lax.Precision.HIGHEST

D_MODEL = 2048
DEPTH = 4
N_A = 2
MEM_HEADS = 4
MEM_HEAD_DIM = 128
MEM_W = 512
MAIN_W = 1536
RWKV_HEAD = 64
RWKV_HEADS = 24
GN_EPS = 64e-5
NSA_HEAD_DIM = 128
NSA_Q_HEADS = 12
NSA_KV_GROUPS = 4
NSA_HPG = 3
GATE_W = 36
CMP_BLOCK = 32
CMP_STRIDE = 16
SEL_BLOCK = 64
SEL_TOPK = 16
SEL_LOCAL = 2
WINDOW = 512
REL_BUCKETS = 32
REL_MAX_DIST = 128
D_FF = 5504
NORM_EPS = 1e-6
MASK_NEG = -1e30
FORCE_SCORE = 1e9

LANE = 128
SUBLANE = 8
MXU_DIM = 256
VMEM_LIMIT = 48 * 1024 * 1024

D_FF_PAD = -(-D_FF // (2 * MXU_DIM)) * (2 * MXU_DIM)
HEAD_BLOCK = MXU_DIM // RWKV_HEAD
N_HEAD_BLOCKS = RWKV_HEADS // HEAD_BLOCK
LOWRANK_PAD = MXU_DIM
NSA_TQ = 128
NSA_KT = 256
NSA_BANDS = 4


def _params(*sem):
    return pltpu.CompilerParams(dimension_semantics=sem, vmem_limit_bytes=VMEM_LIMIT)


def _rms(x, g):
    return x * lax.rsqrt(jnp.mean(x * x, axis=-1, keepdims=True) + NORM_EPS) * g


def _sigmoid(x):
    return 1.0 / (1.0 + jnp.exp(-x))


def _softplus(x):
    return jnp.maximum(x, 0.0) + jnp.log(1.0 + jnp.exp(-jnp.abs(x)))


def _gelu_tanh(x):
    return 0.5 * x * (1.0 + jnp.tanh(math.sqrt(2.0 / math.pi) * (x + 0.044715 * x * x * x)))


def _dot(a, b, precision=None):
    return jnp.dot(a, b, preferred_element_type=F32, precision=precision)


def _dot_nt(a, b, precision=None):
    return lax.dot_general(a, b, (((1,), (1,)), ((), ())), preferred_element_type=F32,
                           precision=precision)


def _dot_tn(a, b, precision=None):
    return lax.dot_general(a, b, (((0,), (0,)), ((), ())), preferred_element_type=F32,
                           precision=precision)


def _masked_softmax(s, mask):
    s = jnp.where(mask, s, MASK_NEG)
    p = jnp.exp(s - jnp.max(s, axis=-1, keepdims=True)) * mask.astype(F32)
    den = jnp.sum(p, axis=-1, keepdims=True)
    return p / jnp.where(den > 0, den, 1.0)


def _row_tile(rows, target):
    t = min(rows, target)
    while rows % t:
        t -= SUBLANE
    return t


def _norm_matmul_kernel(x_ref, g_ref, w_ref, o_ref, xn_ref):
    @pl.when(pl.program_id(1) == 0)
    def _():
        xn_ref[...] = _rms(x_ref[...], g_ref[...]).astype(BF16)

    o_ref[...] = _dot(xn_ref[...], w_ref[...]).astype(o_ref.dtype)


def norm_matmul(x, g, w, tn, out_dtype=F32, tm=512):
    rows, d = x.shape
    n = w.shape[1]
    tm = _row_tile(rows, tm)
    assert n % tn == 0
    return pl.pallas_call(
        _norm_matmul_kernel,
        grid=(rows // tm, n // tn),
        in_specs=[pl.BlockSpec((tm, d), lambda i, j: (i, 0)),
                  pl.BlockSpec((1, d), lambda i, j: (0, 0)),
                  pl.BlockSpec((d, tn), lambda i, j: (0, j))],
        out_specs=pl.BlockSpec((tm, tn), lambda i, j: (i, j)),
        out_shape=jax.ShapeDtypeStruct((rows, n), out_dtype),
        scratch_shapes=[pltpu.VMEM((tm, d), BF16)],
        compiler_params=_params("arbitrary", "arbitrary"),
        name="norm_matmul",
    )(x, g.reshape(1, d), w)


def _norm_swiglu_kernel(x_ref, g_ref, wg_ref, wu_ref, o_ref, xn_ref):
    @pl.when(pl.program_id(1) == 0)
    def _():
        xn_ref[...] = _rms(x_ref[...], g_ref[...]).astype(BF16)

    xn = xn_ref[...]
    gate = _dot(xn, wg_ref[...])
    up = _dot(xn, wu_ref[...])
    o_ref[...] = (gate * _sigmoid(gate) * up).astype(o_ref.dtype)


def norm_swiglu(x, g, wg, wu, tn=512, tm=512):
    rows, d = x.shape
    f = wg.shape[1]
    tm = _row_tile(rows, tm)
    assert f % tn == 0
    return pl.pallas_call(
        _norm_swiglu_kernel,
        grid=(rows // tm, f // tn),
        in_specs=[pl.BlockSpec((tm, d), lambda i, j: (i, 0)),
                  pl.BlockSpec((1, d), lambda i, j: (0, 0)),
                  pl.BlockSpec((d, tn), lambda i, j: (0, j)),
                  pl.BlockSpec((d, tn), lambda i, j: (0, j))],
        out_specs=pl.BlockSpec((tm, tn), lambda i, j: (i, j)),
        out_shape=jax.ShapeDtypeStruct((rows, f), BF16),
        scratch_shapes=[pltpu.VMEM((tm, d), BF16)],
        compiler_params=_params("arbitrary", "arbitrary"),
        name="norm_swiglu",
    )(x, g.reshape(1, d), wg, wu)


def _matmul_norm_res_kernel(a_ref, w_ref, g_ref, h_ref, o_ref, acc_ref, *, scale):
    k = pl.program_id(1)

    @pl.when(k == 0)
    def _():
        acc_ref[...] = jnp.zeros_like(acc_ref)

    acc_ref[...] += _dot(a_ref[...], w_ref[...])

    @pl.when(k == pl.num_programs(1) - 1)
    def _():
        o_ref[...] = h_ref[...] + scale * _rms(acc_ref[...], g_ref[...])


def matmul_norm_res(a, w, g, h, scale, tk=512, tm=512):
    rows, kdim = a.shape
    d = w.shape[1]
    tm = _row_tile(rows, tm)
    assert kdim % tk == 0
    return pl.pallas_call(
        functools.partial(_matmul_norm_res_kernel, scale=scale),
        grid=(rows // tm, kdim // tk),
        in_specs=[pl.BlockSpec((tm, tk), lambda i, k: (i, k)),
                  pl.BlockSpec((tk, d), lambda i, k: (k, 0)),
                  pl.BlockSpec((1, d), lambda i, k: (0, 0)),
                  pl.BlockSpec((tm, d), lambda i, k: (i, 0))],
        out_specs=pl.BlockSpec((tm, d), lambda i, k: (i, 0)),
        out_shape=jax.ShapeDtypeStruct((rows, d), F32),
        scratch_shapes=[pltpu.VMEM((tm, d), F32)],
        compiler_params=_params("arbitrary", "arbitrary"),
        name="matmul_norm_res",
    )(a, w, g.reshape(1, d), h)


RW_R, RW_W1, RW_K, RW_V, RW_V1, RW_A1, RW_G1, RW_QM = 0, 6, 7, 13, 19, 20, 21, 22
RW_NBLK = 24
RW_MIX_STARTS = ((RW_R, 0), (RW_W1, 1), (RW_K, 2), (RW_V, 3), (RW_A1, 4), (RW_G1, 5), (RW_QM, None))


def _rwkv_in_kernel(x_ref, sp_ref, g_ref, mu_ref, w_ref, o_ref, last_ref,
                    u_sc, xx_sc, xm_sc, carry_sc, *, tm, last_tile, last_row):
    ti = pl.program_id(1)
    j = pl.program_id(2)

    @pl.when(j == 0)
    def _():
        @pl.when(ti == 0)
        def _():
            carry_sc[...] = sp_ref[0]

        u = _rms(x_ref[0], g_ref[...])
        rows = lax.broadcasted_iota(jnp.int32, u.shape, 0)
        prev = jnp.where(rows == 0, carry_sc[...], pltpu.roll(u, 1, axis=0))
        u_sc[...] = u
        xx_sc[...] = prev - u
        carry_sc[...] = u[tm - 1:tm, :]

        @pl.when(ti == last_tile)
        def _():
            last_ref[0] = u[last_row:last_row + 1, :]

    for start, mix in RW_MIX_STARTS:
        @pl.when(j == start)
        def _(mix=mix):
            if mix is None:
                xm_sc[...] = u_sc[...].astype(BF16)
            else:
                xm_sc[...] = (u_sc[...] + xx_sc[...] * mu_ref[mix:mix + 1, :]).astype(BF16)

    o_ref[0] = _dot(xm_sc[...], w_ref[...])


def rwkv_in_proj(h, shift_prev, g, mu, w_cat, t_real, tm=1024):
    b, t, d = h.shape
    tm = _row_tile(t, tm)
    tn = MXU_DIM
    last_tile, last_row = (t_real - 1) // tm, (t_real - 1) % tm
    mu8 = jnp.concatenate([mu, jnp.zeros((SUBLANE - mu.shape[0], d), F32)], axis=0)
    return pl.pallas_call(
        functools.partial(_rwkv_in_kernel, tm=tm, last_tile=last_tile, last_row=last_row),
        grid=(b, t // tm, RW_NBLK),
        in_specs=[pl.BlockSpec((1, tm, d), lambda bi, ti, j: (bi, ti, 0)),
                  pl.BlockSpec((1, 1, d), lambda bi, ti, j: (bi, 0, 0)),
                  pl.BlockSpec((1, d), lambda bi, ti, j: (0, 0)),
                  pl.BlockSpec((SUBLANE, d), lambda bi, ti, j: (0, 0)),
                  pl.BlockSpec((d, tn), lambda bi, ti, j: (0, j))],
        out_specs=[pl.BlockSpec((1, tm, tn), lambda bi, ti, j: (bi, ti, j)),
                   pl.BlockSpec((1, 1, d), lambda bi, ti, j: (bi, 0, 0))],
        out_shape=[jax.ShapeDtypeStruct((b, t, RW_NBLK * tn), F32),
                   jax.ShapeDtypeStruct((b, 1, d), F32)],
        scratch_shapes=[pltpu.VMEM((tm, d), F32), pltpu.VMEM((tm, d), F32),
                        pltpu.VMEM((tm, d), BF16), pltpu.VMEM((1, d), F32)],
        compiler_params=_params("arbitrary", "arbitrary", "arbitrary"),
        name="rwkv_in_proj",
    )(h, shift_prev.reshape(b, 1, d), g.reshape(1, d), mu8, w_cat)


def _rwkv_scan_kernel(*refs, chunk, n_chunks, has_vres, t_valid):
    if has_vres:
        (r_ref, k_ref, v_ref, tw_ref, ta_ref, tg_ref, tv_ref, vf_ref,
         w2_ref, a2_ref, g2_ref, v2_ref, vec_ref, s0_ref, y_ref, sout_ref, s_sc) = refs
    else:
        (r_ref, k_ref, v_ref, tw_ref, ta_ref, tg_ref,
         w2_ref, a2_ref, g2_ref, vec_ref, s0_ref, y_ref, sout_ref, s_sc) = refs
    ti = pl.program_id(2)
    c_len = chunk
    lanes = MXU_DIM
    rows4 = HEAD_BLOCK * c_len

    @pl.when(ti == 0)
    def _():
        s_sc[...] = s0_ref[0, 0]

    li = lax.broadcasted_iota(jnp.int32, (lanes, lanes), 0) // RWKV_HEAD
    lj = lax.broadcasted_iota(jnp.int32, (lanes, lanes), 1) // RWKV_HEAD
    seg = (li == lj).astype(BF16)
    ci = lax.broadcasted_iota(jnp.int32, (c_len, c_len), 0)
    cj = lax.broadcasted_iota(jnp.int32, (c_len, c_len), 1)
    tri_c = (cj <= ci).astype(BF16)

    def split2(x):
        bits = lax.bitcast_convert_type(x, jnp.uint32) & jnp.uint32(0xFFFF0000)
        hi = lax.bitcast_convert_type(bits, F32)
        return hi, x - hi

    def seg_sums(xs):
        parts = [p for x in xs for p in split2(x)]
        res = _dot(jnp.concatenate(parts, axis=0).astype(BF16), seg)
        return [res[(2 * n) * c_len:(2 * n + 1) * c_len] + res[(2 * n + 1) * c_len:(2 * n + 2) * c_len]
                for n in range(len(xs))]
    lane_head = lax.broadcasted_iota(jnp.int32, (c_len, lanes), 1) // RWKV_HEAD
    ri = lax.broadcasted_iota(jnp.int32, (rows4, 2 * rows4), 0)
    rj = lax.broadcasted_iota(jnp.int32, (rows4, 2 * rows4), 1) & (rows4 - 1)
    strict = rj < ri
    incl = rj <= ri
    ei = lax.broadcasted_iota(jnp.int32, (rows4, rows4), 0)
    ej = lax.broadcasted_iota(jnp.int32, (rows4, rows4), 1)
    eye = (ei == ej).astype(F32)
    n_double = int(math.log2(c_len)) - 1

    def stack(x):
        return jnp.concatenate(
            [jnp.where(lane_head == hh, x, 0.0) for hh in range(HEAD_BLOCK)], axis=0)

    w0, a0, v0 = vec_ref[0:1, :], vec_ref[1:2, :], vec_ref[2:3, :]
    kkw, kaw = vec_ref[3:4, :], vec_ref[4:5, :]
    lnw, lnb, rk = vec_ref[5:6, :], vec_ref[6:7, :], vec_ref[7:8, :]

    def one_chunk(c, carry):
        sl = pl.ds(pl.multiple_of(c * c_len, c_len), c_len)
        r = r_ref[0, sl, :]
        k = k_ref[0, sl, :]
        v = v_ref[0, sl, :]
        logw = -_softplus(-(w0 + _dot(jnp.tanh(tw_ref[0, sl, :]), w2_ref[...]))) - 0.5
        dlog = -jnp.exp(logw)
        rate = _sigmoid(a0 + _dot(ta_ref[0, sl, :], a2_ref[...]))
        gate = _dot(_sigmoid(tg_ref[0, sl, :]), g2_ref[...])
        if has_vres:
            v = v + (vf_ref[0, sl, :] - v) * _sigmoid(v0 + _dot(tv_ref[0, sl, :], v2_ref[...]))
        kk = k * kkw
        k = k * (1.0 + (rate - 1.0) * kaw)
        kk_sq, rk_sum = seg_sums([kk * kk, r * k * rk])
        kk = kk / jnp.maximum(jnp.sqrt(kk_sq), 1e-12)
        if t_valid is not None:
            t_idx = ti * (n_chunks * c_len) + c * c_len + lax.broadcasted_iota(
                jnp.int32, (c_len, lanes), 0)
            live = t_idx < t_valid
            dlog = jnp.where(live, dlog, 0.0)
            kk = jnp.where(live, kk, 0.0)
            k_live = jnp.where(live, k, 0.0)
        else:
            k_live = k
        d_hi, d_lo = split2(dlog)
        cum2 = _dot(tri_c, jnp.concatenate([d_hi, d_lo], axis=1).astype(BF16))
        cum = cum2[:, :lanes] + cum2[:, lanes:]
        inv = jnp.exp(-cum)
        ar = jnp.concatenate([stack(-kk * jnp.exp(cum - dlog)), stack(r * jnp.exp(cum))],
                             axis=0).astype(BF16)
        bk_s = jnp.concatenate([stack(kk * rate * inv), stack(k_live * inv)],
                               axis=0).astype(BF16)
        v_s = stack(v)
        g_end = jnp.exp(cum[c_len - 1:c_len, :])
        s_old = s_sc[...]

        big = _dot_nt(ar, jnp.concatenate([bk_s, s_old.astype(BF16)], axis=0))
        a_bk = jnp.where(strict, big[:rows4, :2 * rows4], 0.0)
        r_bk = jnp.where(incl, big[rows4:, :2 * rows4], 0.0)
        lmat = a_bk[:, :rows4]
        tinv = eye + lmat
        lpow = lmat.astype(BF16)
        lpow = _dot(lpow, lpow).astype(BF16)
        for _ in range(n_double - 1):
            both = _dot(jnp.concatenate([lpow, tinv.astype(BF16)], axis=0), lpow)
            tinv = tinv + both[rows4:]
            lpow = both[:rows4].astype(BF16)
        tinv = tinv + _dot(tinv.astype(BF16), lpow)
        u = _dot(tinv, big[:rows4, 2 * rows4:] + _dot(a_bk[:, rows4:], v_s))
        uv = jnp.concatenate([u, v_s], axis=0).astype(BF16)
        y_s = big[rows4:, 2 * rows4:] + _dot(r_bk.astype(BF16), uv)
        s_sc[...] = (s_old + _dot_tn(uv, bk_s)) * g_end
        y = y_s[0:c_len]
        for hh in range(1, HEAD_BLOCK):
            y = y + y_s[hh * c_len:(hh + 1) * c_len]

        mean = seg_sums([y])[0] * (1.0 / RWKV_HEAD)
        yc = y - mean
        var = seg_sums([yc * yc])[0] * (1.0 / RWKV_HEAD)
        yn = yc * lax.rsqrt(var + GN_EPS) * lnw + lnb
        y_ref[0, sl, :] = ((yn + rk_sum * v) * gate).astype(y_ref.dtype)
        return carry

    lax.fori_loop(0, n_chunks, one_chunk, 0)

    @pl.when(ti == pl.num_programs(2) - 1)
    def _():
        sout_ref[0, 0] = s_sc[...]


def rwkv_scan(proj, v_first_proj, w2, a2, g2, v2, vec, s0_bd, t_real, chunk, tb):
    b, t, _ = proj.shape
    lanes = MXU_DIM
    tb = min(tb, t)
    assert t % tb == 0 and tb % chunk == 0
    has_vres = v_first_proj is not None
    t_valid = None if t_real == t else t_real

    def col(block0):
        return pl.BlockSpec((1, tb, lanes), lambda bi, hg, ti: (bi, ti, block0 + hg))

    def fixed(block):
        return pl.BlockSpec((1, tb, lanes), lambda bi, hg, ti: (bi, ti, block))

    def wcol():
        return pl.BlockSpec((LOWRANK_PAD, lanes), lambda bi, hg, ti: (0, hg))

    st = pl.BlockSpec((1, 1, lanes, lanes), lambda bi, hg, ti: (bi, hg, 0, 0))
    if has_vres:
        in_specs = [col(RW_R), col(RW_K), col(RW_V), fixed(RW_W1), fixed(RW_A1), fixed(RW_G1),
                    fixed(RW_V1), col(RW_V), wcol(), wcol(), wcol(), wcol()]
        args = [proj, proj, proj, proj, proj, proj, proj, v_first_proj, w2, a2, g2, v2]
    else:
        in_specs = [col(RW_R), col(RW_K), col(RW_V), fixed(RW_W1), fixed(RW_A1), fixed(RW_G1),
                    wcol(), wcol(), wcol()]
        args = [proj, proj, proj, proj, proj, proj, w2, a2, g2]
    in_specs += [pl.BlockSpec((SUBLANE, lanes), lambda bi, hg, ti: (0, hg)), st]
    args += [vec, s0_bd]
    return pl.pallas_call(
        functools.partial(_rwkv_scan_kernel, chunk=chunk, n_chunks=tb // chunk,
                          has_vres=has_vres, t_valid=t_valid),
        grid=(b, N_HEAD_BLOCKS, t // tb),
        in_specs=in_specs,
        out_specs=[pl.BlockSpec((1, tb, lanes), lambda bi, hg, ti: (bi, ti, hg)), st],
        out_shape=[jax.ShapeDtypeStruct((b, t, MAIN_W), BF16),
                   jax.ShapeDtypeStruct(s0_bd.shape, F32)],
        scratch_shapes=[pltpu.VMEM((lanes, lanes), F32)],
        compiler_params=_params("arbitrary", "arbitrary", "arbitrary"),
        name="rwkv_scan",
    )(*args)


def _to_block_diag(s):
    b = s.shape[0]
    s = s.reshape(b, N_HEAD_BLOCKS, HEAD_BLOCK, RWKV_HEAD, RWKV_HEAD)
    eye = jnp.eye(HEAD_BLOCK, dtype=s.dtype)
    bd = s[:, :, :, :, None, :] * eye[None, None, :, None, :, None]
    return bd.reshape(b, N_HEAD_BLOCKS, MXU_DIM, MXU_DIM)


def _from_block_diag(bd):
    b = bd.shape[0]
    n = RWKV_HEAD
    x = jnp.stack([bd[:, :, hh * n:(hh + 1) * n, hh * n:(hh + 1) * n] for hh in range(HEAD_BLOCK)],
                  axis=2)
    return x.reshape(b, RWKV_HEADS, RWKV_HEAD, RWKV_HEAD)


def _mem_attn_kernel(q_ref, k_ref, v_ref, o_ref):
    scale = MEM_HEAD_DIM ** -0.5
    for hh in range(MEM_HEADS):
        sl = slice(hh * MEM_HEAD_DIM, (hh + 1) * MEM_HEAD_DIM)
        s = _dot_nt(q_ref[0, :, sl], k_ref[0, :, sl]) * scale
        p = jnp.exp(s - jnp.max(s, axis=-1, keepdims=True))
        p = p / jnp.sum(p, axis=-1, keepdims=True)
        o_ref[0, :, sl] = _dot(p, v_ref[0, :, sl]).astype(o_ref.dtype)


def mem_attend(qsrc, q_block, ksrc, k_block, vsrc, v_block, tq=512):
    b, t, _ = qsrc.shape
    m = ksrc.shape[1]
    tq = _row_tile(t, tq)
    return pl.pallas_call(
        _mem_attn_kernel,
        grid=(b, t // tq),
        in_specs=[pl.BlockSpec((1, tq, MEM_W), lambda bi, ti: (bi, ti, q_block)),
                  pl.BlockSpec((1, m, MEM_W), lambda bi, ti: (bi, 0, k_block)),
                  pl.BlockSpec((1, m, MEM_W), lambda bi, ti: (bi, 0, v_block))],
        out_specs=pl.BlockSpec((1, tq, MEM_W), lambda bi, ti: (bi, ti, 0)),
        out_shape=jax.ShapeDtypeStruct((b, t, MEM_W), BF16),
        compiler_params=_params("arbitrary", "arbitrary"),
        name="mem_attend",
    )(qsrc, ksrc, vsrc)


CMP_PAGE = 128
CMP_CHUNKS = CMP_PAGE // CMP_STRIDE
KV_LANES = NSA_KV_GROUPS * NSA_HEAD_DIM


def _chunk_proj_kernel(pt_ref, *refs, n_pages):
    del pt_ref
    page_refs = refs[:n_pages]
    w_ref, o0_ref, o1_ref, x_sc = refs[n_pages:]
    for kp in range(n_pages):
        for pos in range(CMP_STRIDE):
            x_sc[kp * CMP_CHUNKS:(kp + 1) * CMP_CHUNKS, pos * LANE:(pos + 1) * LANE] = (
                page_refs[kp][pl.ds(pos, CMP_CHUNKS, stride=CMP_STRIDE), :])
    res = _dot(x_sc[...].astype(BF16), w_ref[...])
    o0_ref[0] = res[:, :LANE]
    o1_ref[0] = res[:, LANE:]


def chunk_proj(rows2d, table, w_flat, n_pages, col_block=0):
    b, n_tab = table.shape
    assert n_tab % n_pages == 0

    def page_spec(kp):
        return pl.BlockSpec(
            (CMP_PAGE, LANE),
            lambda bi, pg, g, pt: (pt[bi, pg * n_pages + kp], col_block * NSA_KV_GROUPS + g))

    out_spec = pl.BlockSpec((1, n_pages * CMP_CHUNKS, LANE), lambda bi, pg, g, pt: (bi, pg, g))
    out_shape = jax.ShapeDtypeStruct((b, n_tab * CMP_CHUNKS, KV_LANES), F32)
    grid_spec = pltpu.PrefetchScalarGridSpec(
        num_scalar_prefetch=1,
        grid=(b, n_tab // n_pages, NSA_KV_GROUPS),
        in_specs=[page_spec(kp) for kp in range(n_pages)]
        + [pl.BlockSpec(w_flat.shape, lambda bi, pg, g, pt: (0, 0))],
        out_specs=[out_spec, out_spec],
        scratch_shapes=[pltpu.VMEM((n_pages * CMP_CHUNKS, CMP_STRIDE * LANE), F32)],
    )
    return pl.pallas_call(
        functools.partial(_chunk_proj_kernel, n_pages=n_pages),
        grid_spec=grid_spec,
        out_shape=[out_shape, out_shape],
        compiler_params=_params("arbitrary", "arbitrary", "arbitrary"),
        name="chunk_proj",
    )(table, *([rows2d] * n_pages), w_flat)


def _chunk_proj_pool_kernel(pt_ref, *refs, n_pages):
    del pt_ref
    page_refs = refs[:n_pages]
    w_ref, o0_ref, o1_ref, x_sc = refs[n_pages:]
    for g in range(NSA_KV_GROUPS):
        for kp in range(n_pages):
            for pos in range(CMP_STRIDE):
                x_sc[kp * CMP_CHUNKS:(kp + 1) * CMP_CHUNKS, pos * LANE:(pos + 1) * LANE] = (
                    page_refs[kp][0, pl.ds(pos, CMP_CHUNKS, stride=CMP_STRIDE), g, :])
        res = _dot(x_sc[...].astype(BF16), w_ref[...])
        o0_ref[0, :, g * LANE:(g + 1) * LANE] = res[:, :LANE]
        o1_ref[0, :, g * LANE:(g + 1) * LANE] = res[:, LANE:]


def chunk_proj_pool(pool, table, w_flat, n_pages):
    b, n_tab = table.shape
    assert n_tab % n_pages == 0

    def page_spec(kp):
        return pl.BlockSpec((1, CMP_PAGE, NSA_KV_GROUPS, LANE),
                            lambda bi, pg, pt: (pt[bi, pg * n_pages + kp], 0, 0, 0))

    out_spec = pl.BlockSpec((1, n_pages * CMP_CHUNKS, KV_LANES), lambda bi, pg, pt: (bi, pg, 0))
    out_shape = jax.ShapeDtypeStruct((b, n_tab * CMP_CHUNKS, KV_LANES), F32)
    grid_spec = pltpu.PrefetchScalarGridSpec(
        num_scalar_prefetch=1,
        grid=(b, n_tab // n_pages),
        in_specs=[page_spec(kp) for kp in range(n_pages)]
        + [pl.BlockSpec(w_flat.shape, lambda bi, pg, pt: (0, 0))],
        out_specs=[out_spec, out_spec],
        scratch_shapes=[pltpu.VMEM((n_pages * CMP_CHUNKS, CMP_STRIDE * LANE), F32)],
    )
    return pl.pallas_call(
        functools.partial(_chunk_proj_pool_kernel, n_pages=n_pages),
        grid_spec=grid_spec,
        out_shape=[out_shape, out_shape],
        compiler_params=_params("arbitrary", "arbitrary"),
        name="chunk_proj_pool",
    )(table, *([pool] * n_pages), w_flat)


def _block_mlp_kernel(p0_ref, p1_ref, pe_ref, w1_ref, b1_ref, w2_ref, o_ref):
    const = _dot(pe_ref[...], w1_ref[...])[0:1, :] + b1_ref[...]
    w2 = w2_ref[...]
    for g in range(NSA_KV_GROUPS):
        sl = slice(g * LANE, (g + 1) * LANE)
        hid = const + p0_ref[0, :, sl] + p1_ref[0, :, sl]
        o_ref[0, :, sl] = _dot(_gelu_tanh(hid), w2)


def block_mlp(p0, p1, pe, w1, b1, w2, tb=1024):
    b, nb, _ = p0.shape
    tb = _row_tile(nb, tb)
    pe8 = jnp.concatenate([pe.reshape(1, -1), jnp.zeros((SUBLANE - 1, pe.size), F32)], axis=0)
    spec = pl.BlockSpec((1, tb, KV_LANES), lambda bi, i: (bi, i, 0))
    full = lambda a: pl.BlockSpec(a.shape, lambda bi, i: (0,) * a.ndim)
    b1r = b1.reshape(1, -1)
    return pl.pallas_call(
        _block_mlp_kernel,
        grid=(b, nb // tb),
        in_specs=[spec, spec, full(pe8), full(w1), full(b1r), full(w2)],
        out_specs=spec,
        out_shape=jax.ShapeDtypeStruct((b, nb, KV_LANES), F32),
        compiler_params=_params("arbitrary", "arbitrary"),
        name="block_mlp",
    )(p0, p1, pe8, w1, b1r, w2)


def _w1_flat(w1):
    r = CMP_BLOCK // CMP_STRIDE
    e = w1.shape[1]
    w = w1.reshape(r, CMP_STRIDE, NSA_HEAD_DIM, e)
    return jnp.transpose(w, (1, 2, 0, 3)).reshape(CMP_STRIDE * NSA_HEAD_DIM, r * e).astype(BF16)


def compress(parts, nb, pe, w1, b1, w2):
    return block_mlp(parts[0][:, :nb], parts[1][:, 1:nb + 1], pe, w1, b1, w2)


def _rel_bucket(dist):
    n = jnp.maximum(dist, 0)
    max_exact = REL_BUCKETS // 2
    nf = jnp.maximum(n, 1).astype(F32)
    large = max_exact + (jnp.log(nf / max_exact) / math.log(REL_MAX_DIST / max_exact)
                         * (REL_BUCKETS - max_exact)).astype(jnp.int32)
    return jnp.where(n < max_exact, n, jnp.minimum(large, REL_BUCKETS - 1))


def _bias_lookup(rel_bias, dist):
    tab = rel_bias.astype(F32).reshape(REL_BUCKETS, NSA_KV_GROUPS, NSA_HPG)
    onehot = jax.nn.one_hot(_rel_bucket(dist), REL_BUCKETS, dtype=F32)
    out = jnp.dot(onehot, tab.reshape(REL_BUCKETS, -1), precision=HI)
    out = out.reshape(dist.shape + (NSA_KV_GROUPS, NSA_HPG))
    return jnp.moveaxis(out, (-2, -1), (0, 1))


def _cmp_to_sel(nb_pad, nb, nsb_pad, nsb):
    i = jnp.arange(nb_pad)[:, None]
    j = jnp.arange(nsb_pad)[None, :]
    start = i * CMP_STRIDE
    hit = (start < (j + 1) * SEL_BLOCK) & (start + CMP_BLOCK > j * SEL_BLOCK) & (i < nb) & (j < nsb)
    return hit.astype(F32)


def _nsa_prompt_kernel(q_ref, gt_ref, kc_ref, vc_ref, bc_ref, band_ref, ks_ref, vs_ref,
                       kw_ref, vw_ref, m_ref, o_ref, *, nb, nsb):
    tq = NSA_TQ
    rows = NSA_HPG * tq
    i = pl.program_id(2)
    scale = NSA_HEAD_DIM ** -0.5
    qb = q_ref[0]
    q3 = jnp.concatenate([qb[:, hh * LANE:(hh + 1) * LANE] for hh in range(NSA_HPG)],
                         axis=0).astype(BF16)
    row_q = lax.broadcasted_iota(jnp.int32, (rows, LANE), 0) & (tq - 1)
    lane = lax.broadcasted_iota(jnp.int32, (rows, LANE), 1)
    t_pos = i * tq + row_q

    s_c = _dot_nt(q3, kc_ref[0].astype(BF16)) * scale + bc_ref[0].reshape(rows, LANE)
    m_c = (t_pos - (lane * CMP_STRIDE + (CMP_BLOCK - 1)) >= 0) & (lane < nb)
    p_c = _masked_softmax(s_c, m_c)
    o_c = _dot(p_c, vc_ref[0])

    imp3 = _dot_nt(m_ref[...], p_c, HI)
    imp = imp3[:, 0:tq]
    for hh in range(1, NSA_HPG):
        imp = imp + imp3[:, hh * tq:(hh + 1) * tq]
    jb = lax.broadcasted_iota(jnp.int32, (nsb, tq), 0)
    cur = (i * tq + lax.broadcasted_iota(jnp.int32, (nsb, tq), 1)) // SEL_BLOCK
    valid = jb <= cur
    forced = valid & ((jb == 0) | (jb > cur - SEL_LOCAL))
    imp = jnp.where(forced, FORCE_SCORE, jnp.where(valid, imp, -FORCE_SCORE))
    rank = jnp.zeros((nsb, tq), jnp.int32)
    for jp in range(nsb):
        other = imp[jp:jp + 1, :]
        rank = rank + ((other > imp) | ((other == imp) & (jb > jp))).astype(jnp.int32)
    sel = (rank < min(SEL_TOPK, nsb)).astype(F32).T
    sel3 = jnp.concatenate([sel] * NSA_HPG, axis=0).astype(BF16)
    kt_w = NSA_KT
    blk_row = lax.broadcasted_iota(jnp.int32, (nsb, kt_w), 0)
    blk_lane = lax.broadcasted_iota(jnp.int32, (nsb, kt_w), 1) // SEL_BLOCK
    key_lane = lax.broadcasted_iota(jnp.int32, (rows, kt_w), 1)
    t_pos_k = i * tq + (lax.broadcasted_iota(jnp.int32, (rows, kt_w), 0) & (tq - 1))
    tiles_per_kt = kt_w // tq

    def attend(k_ref, v_ref, lo, use_sel):
        def body(kt, carry):
            m_run, l_run, acc = carry
            ksl = pl.ds(pl.multiple_of(kt * kt_w, kt_w), kt_w)
            band = band_ref[0, jnp.clip(i - kt * tiles_per_kt, 0, NSA_BANDS - 1)].reshape(rows, kt_w)
            s = _dot_nt(q3, k_ref[0, ksl, :].astype(BF16)) * scale + band
            dist = t_pos_k - (kt * kt_w + key_lane)
            if use_sel:
                expand = (blk_row == kt * (kt_w // SEL_BLOCK) + blk_lane).astype(BF16)
                mask = (_dot(sel3, expand) > 0.5) & (dist >= 0)
            else:
                mask = (dist >= 0) & (dist < WINDOW)
            s = jnp.where(mask, s, MASK_NEG)
            m_new = jnp.maximum(m_run, jnp.max(s, axis=-1, keepdims=True))
            alpha = jnp.exp(m_run - m_new)
            p = jnp.where(mask, jnp.exp(s - m_new), 0.0)
            l_new = alpha * l_run + jnp.sum(p, axis=-1, keepdims=True)
            acc = alpha * acc + _dot(p, v_ref[0, ksl, :])
            return m_new, l_new, acc

        init = (jnp.full((rows, 1), MASK_NEG, F32), jnp.zeros((rows, 1), F32),
                jnp.zeros((rows, LANE), F32))
        _, l_fin, acc = lax.fori_loop(lo, i // tiles_per_kt + 1, body, init)
        return acc / jnp.where(l_fin > 0, l_fin, 1.0)

    o_s = attend(ks_ref, vs_ref, 0, True)
    o_w = attend(kw_ref, vw_ref, jnp.maximum(i - WINDOW // tq, 0) // tiles_per_kt, False)

    gt = _sigmoid(gt_ref[0, 0])
    gcol = lambda br: jnp.concatenate(
        [gt[:, br * NSA_HPG + hh:br * NSA_HPG + hh + 1] for hh in range(NSA_HPG)], axis=0)
    o = gcol(0) * o_c + gcol(1) * o_s + gcol(2) * o_w
    for hh in range(NSA_HPG):
        o_ref[0, :, hh * LANE:(hh + 1) * LANE] = o[hh * tq:(hh + 1) * tq].astype(o_ref.dtype)


def nsa_prompt(proj, gates_t, kc, vc, bias_c, band, side, sel_map, nb, nsb):
    b, t, _ = proj.shape
    tq = NSA_TQ
    qw = NSA_HPG * LANE
    kv = lambda off: pl.BlockSpec((1, t, LANE), lambda bi, g, i: (bi, 0, off * NSA_KV_GROUPS + g))
    return pl.pallas_call(
        functools.partial(_nsa_prompt_kernel, nb=nb, nsb=nsb),
        grid=(b, NSA_KV_GROUPS, t // tq),
        in_specs=[pl.BlockSpec((1, tq, qw), lambda bi, g, i: (bi, i, g)),
                  pl.BlockSpec((1, 1, tq, NSA_HPG * 3), lambda bi, g, i: (bi, g, i, 0)),
                  pl.BlockSpec((1, LANE, LANE), lambda bi, g, i: (bi, 0, g)),
                  pl.BlockSpec((1, LANE, LANE), lambda bi, g, i: (bi, 0, g)),
                  pl.BlockSpec((1, NSA_HPG, tq, LANE), lambda bi, g, i: (g, 0, i, 0)),
                  pl.BlockSpec((1, NSA_BANDS, NSA_HPG, tq, NSA_KT), lambda bi, g, i: (g, 0, 0, 0, 0)),
                  kv(2), kv(3), kv(4), kv(5),
                  pl.BlockSpec(sel_map.shape, lambda bi, g, i: (0, 0))],
        out_specs=pl.BlockSpec((1, tq, qw), lambda bi, g, i: (bi, i, g)),
        out_shape=jax.ShapeDtypeStruct((b, t, MAIN_W), BF16),
        compiler_params=_params("arbitrary", "arbitrary", "arbitrary"),
        name="nsa_prompt",
    )(proj, gates_t, kc, vc, bias_c, band, side, side, side, side, sel_map)


def _nsa_dec_cmp_kernel(q_ref, kc_ref, vc_ref, bc_ref, m_ref, oc_ref, idx_ref, *, nb, nsb, t_pos):
    scale = NSA_HEAD_DIM ** -0.5
    nbp = kc_ref.shape[1]
    nsp = m_ref.shape[1]
    n_sel = min(SEL_TOPK, nsb)
    lane_b = lax.broadcasted_iota(jnp.int32, (SUBLANE, nbp), 1)
    m_c = (t_pos - (lane_b * CMP_STRIDE + (CMP_BLOCK - 1)) >= 0) & (lane_b < nb)
    row_s = lax.broadcasted_iota(jnp.int32, (SUBLANE, nsp), 0)
    jb = lax.broadcasted_iota(jnp.int32, (1, nsp), 1)
    cur = t_pos // SEL_BLOCK
    valid = jb <= cur
    forced = valid & ((jb == 0) | (jb > cur - SEL_LOCAL))
    out_lane = lax.broadcasted_iota(jnp.int32, (1, LANE), 1)
    idx_rows = []
    for g in range(NSA_KV_GROUPS):
        q3 = jnp.concatenate(
            [q_ref[0, 0:1, (g * NSA_HPG + hh) * LANE:(g * NSA_HPG + hh + 1) * LANE]
             for hh in range(NSA_HPG)] + [jnp.zeros((SUBLANE - NSA_HPG, LANE), F32)], axis=0)
        sl = slice(g * LANE, (g + 1) * LANE)
        s_c = _dot_nt(q3, kc_ref[0, :, sl]) * scale + bc_ref[g]
        p_c = _masked_softmax(s_c, m_c)
        oc_ref[0, g] = _dot(p_c, vc_ref[0, :, sl])
        imp8 = jnp.where(row_s < NSA_HPG, _dot(p_c, m_ref[...], HI), 0.0)
        imp = jnp.sum(imp8, axis=0, keepdims=True)
        imp = jnp.where(forced, FORCE_SCORE, jnp.where(valid, imp, -FORCE_SCORE))
        imp = jnp.where(jb < nsb, imp, -jnp.inf)
        jbf = jb.astype(F32)
        picks = jnp.zeros((1, LANE), F32)
        for kk in range(n_sel):
            best = jnp.max(imp, axis=-1, keepdims=True)
            arg = jnp.min(jnp.where(imp == best, jbf, float(nsp)), axis=-1, keepdims=True)
            picks = jnp.where(out_lane == kk, arg, picks)
            imp = jnp.where(jbf == arg, -jnp.inf, imp)
        idx_rows.append(picks.astype(jnp.int32))
    idx_rows.append(jnp.zeros((SUBLANE - NSA_KV_GROUPS, LANE), jnp.int32))
    idx_ref[0] = jnp.concatenate(idx_rows, axis=0)


def nsa_dec_cmp(proj, kc, vc, bias_c, sel_map, nb, nsb, t_pos):
    b, tp, _ = proj.shape
    nbp = kc.shape[1]
    return pl.pallas_call(
        functools.partial(_nsa_dec_cmp_kernel, nb=nb, nsb=nsb, t_pos=t_pos),
        grid=(b,),
        in_specs=[pl.BlockSpec((1, tp, MAIN_W), lambda bi: (bi, 0, 0)),
                  pl.BlockSpec((1, nbp, KV_LANES), lambda bi: (bi, 0, 0)),
                  pl.BlockSpec((1, nbp, KV_LANES), lambda bi: (bi, 0, 0)),
                  pl.BlockSpec(bias_c.shape, lambda bi: (0, 0, 0)),
                  pl.BlockSpec(sel_map.shape, lambda bi: (0, 0))],
        out_specs=[pl.BlockSpec((1, NSA_KV_GROUPS, SUBLANE, LANE), lambda bi: (bi, 0, 0, 0)),
                   pl.BlockSpec((1, SUBLANE, LANE), lambda bi: (bi, 0, 0))],
        out_shape=[jax.ShapeDtypeStruct((b, NSA_KV_GROUPS, SUBLANE, LANE), F32),
                   jax.ShapeDtypeStruct((b, SUBLANE, LANE), jnp.int32)],
        compiler_params=_params("arbitrary"),
        name="nsa_dec_cmp",
    )(proj, kc, vc, bias_c, sel_map)


def _nsa_dec_sel_kernel(phys_ref, isnew_ref, *refs, n_sel, n_win, t_pos, win_start):
    del phys_ref
    kb_refs = refs[:n_sel]
    vb_refs = refs[n_sel:2 * n_sel]
    (q_ref, new_ref, ds_ref, bs_ref, wk_ref, wv_ref, bw_ref, oc_ref, gt_ref, o_ref) = refs[2 * n_sel:]
    bi = pl.program_id(0)
    g = pl.program_id(1)
    scale = NSA_HEAD_DIM ** -0.5
    q3 = q_ref[0, 0]
    new_rows = new_ref[0, 0]
    pad_blk = jnp.zeros((SEL_BLOCK - 1, LANE), F32)
    new_k = jnp.concatenate([new_rows[0:1], pad_blk], axis=0)
    new_v = jnp.concatenate([new_rows[1:2], pad_blk], axis=0)
    ks, vs = [], []
    for kk in range(n_sel):
        fresh = isnew_ref[bi, g, kk] > 0
        ks.append(jnp.where(fresh, new_k, kb_refs[kk][0, :, g, :]))
        vs.append(jnp.where(fresh, new_v, vb_refs[kk][0, :, g, :]))
    ks = jnp.concatenate(ks, axis=0)
    vs = jnp.concatenate(vs, axis=0)
    s_s = _dot_nt(q3, ks) * scale + bs_ref[0, 0]
    p_s = _masked_softmax(s_s, ds_ref[0, 0] >= 0)
    o_s = _dot(p_s, vs)

    pad_w = jnp.zeros((SUBLANE - 1, LANE), F32)
    kw = jnp.concatenate([wk_ref[0, :, g, :], new_rows[2:3], pad_w], axis=0)
    vw = jnp.concatenate([wv_ref[0, :, g, :], new_rows[3:4], pad_w], axis=0)
    nw = kw.shape[0]
    pos = win_start + lax.broadcasted_iota(jnp.int32, (SUBLANE, nw), 1)
    d_w = t_pos - pos
    m_w = (d_w >= 0) & (d_w < WINDOW) & (pos >= 0) & (pos - win_start < n_win)
    s_w = _dot_nt(q3, kw) * scale + bw_ref[0]
    p_w = _masked_softmax(s_w, m_w)
    o_w = _dot(p_w, vw)

    gt = _sigmoid(gt_ref[0, 0])
    o_ref[0, 0] = gt[:, 0:1] * oc_ref[0, 0] + gt[:, 1:2] * o_s + gt[:, 2:3] * o_w


def nsa_dec_sel(phys, isnew, pool_k, pool_v, q8, new_rows, d_s, bias_s, win_k, win_v, bias_w,
                o_c, gates8, n_win, t_pos, win_start):
    b = q8.shape[0]
    n_sel = phys.shape[-1]
    n_keys = n_sel * SEL_BLOCK
    nw = win_k.shape[1]
    nwp = nw + SUBLANE

    def blk_spec(kk):
        return pl.BlockSpec((1, SEL_BLOCK, NSA_KV_GROUPS, LANE),
                            lambda bi, g, ph, nf: (ph[bi, g, kk], 0, 0, 0))

    per_bg = lambda *shape: pl.BlockSpec((1, 1) + shape, lambda bi, g, ph, nf: (bi, g) + (0,) * len(shape))
    grid_spec = pltpu.PrefetchScalarGridSpec(
        num_scalar_prefetch=2,
        grid=(b, NSA_KV_GROUPS),
        in_specs=[blk_spec(kk) for kk in range(n_sel)] + [blk_spec(kk) for kk in range(n_sel)]
        + [per_bg(SUBLANE, LANE), per_bg(SUBLANE, LANE), per_bg(1, n_keys), per_bg(SUBLANE, n_keys),
           pl.BlockSpec((1, nw, NSA_KV_GROUPS, LANE), lambda bi, g, ph, nf: (bi, 0, 0, 0)),
           pl.BlockSpec((1, nw, NSA_KV_GROUPS, LANE), lambda bi, g, ph, nf: (bi, 0, 0, 0)),
           pl.BlockSpec((1, SUBLANE, nwp), lambda bi, g, ph, nf: (g, 0, 0)),
           per_bg(SUBLANE, LANE), per_bg(SUBLANE, SUBLANE)],
        out_specs=per_bg(SUBLANE, LANE),
    )
    return pl.pallas_call(
        functools.partial(_nsa_dec_sel_kernel, n_sel=n_sel, n_win=nw + 1, t_pos=t_pos,
                          win_start=win_start),
        grid_spec=grid_spec,
        out_shape=jax.ShapeDtypeStruct((b, NSA_KV_GROUPS, SUBLANE, LANE), F32),
        compiler_params=_params("arbitrary", "arbitrary"),
        name="nsa_dec_sel",
    )(phys, isnew, *([pool_k] * n_sel), *([pool_v] * n_sel), q8, new_rows, d_s, bias_s,
      win_k, win_v, bias_w, o_c, gates8)


def _pad_cols(w, n):
    return jnp.pad(w, ((0, 0), (0, n - w.shape[1])))


def _pad_rows(w, n):
    return jnp.pad(w, ((0, n - w.shape[0]), (0, 0)))


def _prep_weights(P):
    W = {}
    W['wg'] = [[_pad_cols(P['ffn_gu'][l, i, :, :D_FF], D_FF_PAD).astype(BF16) for i in range(2)]
               for l in range(DEPTH)]
    W['wu'] = [[_pad_cols(P['ffn_gu'][l, i, :, D_FF:], D_FF_PAD).astype(BF16) for i in range(2)]
               for l in range(DEPTH)]
    W['wd'] = [[_pad_rows(P['ffn_d'][l, i], D_FF_PAD).astype(BF16) for i in range(2)]
               for l in range(DEPTH)]
    W['w_out'] = [P['w_out'][l].astype(BF16) for l in range(DEPTH)]
    W['rw_in'], W['rw_w2'], W['rw_a2'], W['rw_g2'], W['rw_v2'], W['rw_vec'] = [], [], [], [], [], []
    lp = LOWRANK_PAD
    for l in range(N_A):
        w_in = P['w_in_a'][l]
        v1 = P['rw_v1'][l - 1] if l > 0 else jnp.zeros((D_MODEL, lp), F32)
        W['rw_in'].append(jnp.concatenate([
            w_in[:, :MAIN_W], _pad_cols(P['rw_w1'][l], lp), w_in[:, MAIN_W:2 * MAIN_W],
            w_in[:, 2 * MAIN_W:3 * MAIN_W], _pad_cols(v1, lp), _pad_cols(P['rw_a1'][l], lp),
            _pad_cols(P['rw_g1'][l], lp), w_in[:, 3 * MAIN_W:]], axis=1).astype(BF16))
        W['rw_w2'].append(_pad_rows(P['rw_w2'][l], lp))
        W['rw_a2'].append(_pad_rows(P['rw_a2'][l], lp))
        W['rw_g2'].append(_pad_rows(P['rw_g2'][l], lp))
        W['rw_v2'].append(_pad_rows(P['rw_v2'][l - 1], lp) if l > 0 else None)
        v0 = P['rw_v0'][l - 1] if l > 0 else jnp.zeros((MAIN_W,), F32)
        W['rw_vec'].append(jnp.stack([P['rw_w0'][l], P['rw_a0'][l], v0, P['rw_kk'][l], P['rw_ka'][l],
                                      P['rw_lnw'][l], P['rw_lnb'][l], P['rw_rk'][l].reshape(-1)]))
    W['w_in_b'] = []
    for l in range(DEPTH - N_A):
        w = P['w_in_b'][l]
        W['w_in_b'].append(jnp.concatenate([
            w[:, :MAIN_W], w[:, MAIN_W + GATE_W:], _pad_cols(w[:, MAIN_W:MAIN_W + GATE_W], MXU_DIM)],
            axis=1).astype(BF16))
    W['w_kv'] = P['w_kv'].astype(BF16)
    W['w_mem_kv'] = [P['w_mem_kv'][l].astype(BF16) for l in range(DEPTH)]
    W['cmp_w1_flat'] = [_w1_flat(P['cmp_w1'][c]) for c in range(2)]
    return W


B_QM_BLOCK = MAIN_W // MEM_W
B_GATE_OFF = MAIN_W + MEM_W


def _ffn(h2, norms_l, first, W, l, i):
    hid = norm_swiglu(h2, norms_l[first], W['wg'][l][i], W['wu'][l][i], tm=1024)
    return matmul_norm_res(hid, W['wd'][l][i], norms_l[first + 1], h2, 0.5, tk=D_FF_PAD // 4)


def _trunk(x, t_real, P, W, shift0, wkv0_bd, mem_src, make_side, attend, chunk, tb):
    b, t, d = x.shape
    h2 = x.reshape(b * t, d)
    shifts, states = [], []
    v_first_proj, ctx, side_state = None, None, None
    for l in range(DEPTH):
        n = P['norms'][l]
        h2 = _ffn(h2, n, 0, W, l, 0)
        if l < N_A:
            proj, last = rwkv_in_proj(h2.reshape(b, t, d), shift0[l], n[2], P['rw_mu'][l],
                                      W['rw_in'][l], t_real)
            main, s_bd = rwkv_scan(proj, v_first_proj if l > 0 else None, W['rw_w2'][l], W['rw_a2'][l],
                                   W['rw_g2'][l], W['rw_v2'][l], W['rw_vec'][l], wkv0_bd[l],
                                   t_real, chunk, tb)
            if l == 0:
                v_first_proj = proj
            shifts.append(last[:, 0])
            states.append(_from_block_diag(s_bd))
            q_src, q_block = proj, RW_QM * MXU_DIM // MEM_W
        else:
            proj = norm_matmul(h2, n[2], W['w_in_b'][l - N_A], tn=768).reshape(b, t, -1)
            main = attend(proj, ctx)
            q_src, q_block = proj, B_QM_BLOCK
        mk, kb, mv, vb = mem_src(l)
        mo = mem_attend(q_src, q_block, mk, kb, mv, vb)
        mix = jnp.concatenate([main, mo], axis=-1).reshape(b * t, d)
        h2 = matmul_norm_res(mix, W['w_out'][l], n[3], h2, 1.0, tk=1024)
        h2 = _ffn(h2, n, 4, W, l, 1)
        if l == N_A - 1:
            side = norm_matmul(h2, P['kv_norm'], W['w_kv'], tn=768).reshape(b, t, -1)
            ctx, side_state = make_side(side)
    return h2.reshape(b, t, d), jnp.stack(shifts), jnp.stack(states), side_state


def kernel(x_prompt, x_sample, mem_prompt, state_wkv, state_shift, cache_mem_k, cache_mem_v,
           cache_cmp_k, cache_cmp_v, cache_slc_k, cache_slc_v, cache_win_k, cache_win_v, page_table,
           norms, ffn_gu, ffn_d, w_in_a, w_in_b, w_out, mem_norm, w_mem_kv, kv_norm, w_kv,
           cmp_pe, cmp_w1, cmp_b1, cmp_w2, rel_bias,
           rw_mu, rw_w0, rw_w1, rw_w2, rw_a0, rw_a1, rw_a2, rw_g1, rw_g2, rw_v0, rw_v1, rw_v2,
           rw_kk, rw_ka, rw_rk, rw_lnw, rw_lnb):
    P = dict(norms=norms, ffn_gu=ffn_gu, ffn_d=ffn_d, w_in_a=w_in_a, w_in_b=w_in_b, w_out=w_out,
             kv_norm=kv_norm, w_kv=w_kv, w_mem_kv=w_mem_kv, cmp_w1=cmp_w1, rw_mu=rw_mu, rw_w0=rw_w0,
             rw_w1=rw_w1, rw_w2=rw_w2, rw_a0=rw_a0, rw_a1=rw_a1, rw_a2=rw_a2, rw_g1=rw_g1,
             rw_g2=rw_g2, rw_v0=rw_v0, rw_v1=rw_v1, rw_v2=rw_v2, rw_kk=rw_kk, rw_ka=rw_ka,
             rw_rk=rw_rk, rw_lnw=rw_lnw, rw_lnb=rw_lnb)
    W = _prep_weights(P)
    G, dh = NSA_KV_GROUPS, NSA_HEAD_DIM
    split_side = lambda side, bx, t: [side[:, :t, c * KV_LANES:(c + 1) * KV_LANES].reshape(bx, t, G, dh)
                                      for c in range(6)]

    def cmp_mlp(parts, nb, c):
        return compress(parts, nb, cmp_pe[c], cmp_w1[c], cmp_b1[c], cmp_w2[c])

    bp, tp, d = x_prompt.shape
    n_mem = mem_prompt.shape[1]
    mem2 = mem_prompt.reshape(bp * n_mem, d)
    p_mkv = [norm_matmul(mem2, mem_norm[l], W['w_mem_kv'][l], tn=512).reshape(bp, n_mem, 2 * MEM_W)
             for l in range(DEPTH)]
    p_mem_k = jnp.stack([m[..., :MEM_W].reshape(bp, n_mem, MEM_HEADS, MEM_HEAD_DIM) for m in p_mkv])
    p_mem_v = jnp.stack([m[..., MEM_W:].reshape(bp, n_mem, MEM_HEADS, MEM_HEAD_DIM) for m in p_mkv])

    nb_p = tp // CMP_STRIDE - 1
    nsb_p = tp // SEL_BLOCK
    nq_tiles = tp // NSA_TQ
    t_all = jnp.arange(tp)
    c_end = jnp.arange(LANE) * CMP_STRIDE + (CMP_BLOCK - 1)
    assert nb_p <= LANE
    bias_c_p = _bias_lookup(rel_bias, t_all[:, None] - c_end[None, :])
    ii = jnp.arange(NSA_TQ)
    cc = jnp.arange(NSA_KT)
    band_p = jnp.stack([_bias_lookup(rel_bias, dd * NSA_TQ + ii[:, None] - cc[None, :])
                        for dd in range(NSA_BANDS)], axis=1)
    assert (NSA_BANDS - 1) * NSA_TQ - (NSA_KT - 1) >= REL_MAX_DIST and tp % NSA_KT == 0
    sel_map_p = _cmp_to_sel(LANE, nb_p, nsb_p, nsb_p).T
    ident = jnp.arange(bp * tp // CMP_PAGE, dtype=jnp.int32).reshape(bp, tp // CMP_PAGE)

    def prompt_side(side):
        rows = side.reshape(bp * tp, -1)
        kc = cmp_mlp(chunk_proj(rows, ident, W['cmp_w1_flat'][0], tp // CMP_PAGE, 0), nb_p, 0)
        vc = cmp_mlp(chunk_proj(rows, ident, W['cmp_w1_flat'][1], tp // CMP_PAGE, 1), nb_p, 1)
        padb = ((0, 0), (0, LANE - nb_p), (0, 0))
        wb = min(WINDOW, tp)
        kc_r, vc_r, ks, vs, kw, vw = split_side(side, bp, tp)
        return ((jnp.pad(kc, padb), jnp.pad(vc, padb), side),
                (kc_r, vc_r, ks, vs, kw[:, tp - wb:], vw[:, tp - wb:]))

    def prompt_attend(proj, ctx):
        kc, vc, side = ctx
        gates = proj[..., B_GATE_OFF:B_GATE_OFF + GATE_W].reshape(bp, tp, 3, G, NSA_HPG)
        gates_t = jnp.transpose(gates, (0, 3, 1, 2, 4)).reshape(bp, G, tp, 3 * NSA_HPG)
        return nsa_prompt(proj, gates_t, kc, vc, bias_c_p, band_p, side, sel_map_p, nb_p, nsb_p)

    zeros_shift = jnp.zeros((N_A, bp, d), F32)
    zeros_state = jnp.zeros((N_A, bp, N_HEAD_BLOCKS, MXU_DIM, MXU_DIM), F32)
    y_prompt, p_shift, p_wkv, p_side = _trunk(
        x_prompt, tp, P, W, zeros_shift, zeros_state,
        lambda l: (p_mkv[l], 0, p_mkv[l], 1), prompt_side, prompt_attend, chunk=64, tb=256)
    p_cmp_k, p_cmp_v, p_slc_k, p_slc_v, p_win_k, p_win_v = p_side

    bd, s_new, _ = x_sample.shape
    assert s_new == 1
    ts = SUBLANE
    xs = jnp.pad(x_sample, ((0, 0), (0, ts - s_new), (0, 0)))
    past_len = page_table.shape[1] * CMP_PAGE
    n_past_blk = past_len // SEL_BLOCK
    blk_per_page = CMP_PAGE // SEL_BLOCK
    nsb_s = n_past_blk + 1
    t_pos = past_len
    nc_s = -(-(past_len + s_new) // CMP_STRIDE)
    nb_s = nc_s - 1
    wb_s = cache_win_k.shape[1]
    win_start = past_len - wb_s
    mem_k2 = cache_mem_k.reshape(DEPTH, bd, n_mem, MEM_W)
    mem_v2 = cache_mem_v.reshape(DEPTH, bd, n_mem, MEM_W)
    nsp = -(-nsb_s // LANE) * LANE
    sel_map_s = _cmp_to_sel(nb_s, nb_s, nsp, nsb_s)
    c_end_s = jnp.arange(nb_s) * CMP_STRIDE + (CMP_BLOCK - 1)
    bias_c_s = _bias_lookup(rel_bias, t_pos - c_end_s)
    bias_c_s = jnp.pad(bias_c_s, ((0, 0), (0, SUBLANE - NSA_HPG), (0, 0)))
    nwp = wb_s + SUBLANE
    bias_w_s = _bias_lookup(rel_bias, t_pos - (win_start + jnp.arange(nwp)))
    bias_w_s = jnp.pad(bias_w_s, ((0, 0), (0, SUBLANE - NSA_HPG), (0, 0)))
    ident_s = jnp.arange(bd, dtype=jnp.int32).reshape(bd, 1)
    pages_per_step = math.gcd(16, page_table.shape[1])

    def sample_side(side):
        new = side[:, :s_new]
        parts = []
        for c, pool in enumerate((cache_cmp_k, cache_cmp_v)):
            past = chunk_proj_pool(pool, page_table, W['cmp_w1_flat'][c], pages_per_step)
            fresh_page = jnp.pad(new[:, :, c * KV_LANES:(c + 1) * KV_LANES],
                                 ((0, 0), (0, CMP_PAGE - s_new), (0, 0))).reshape(bd, CMP_PAGE, G, dh)
            fresh = chunk_proj_pool(fresh_page, ident_s, W['cmp_w1_flat'][c], 1)
            n_fresh = nc_s - past[0].shape[1]
            parts.append([jnp.concatenate([p, f[:, :n_fresh]], axis=1) for p, f in zip(past, fresh)])
        kc = cmp_mlp(parts[0], nb_s, 0)
        vc = cmp_mlp(parts[1], nb_s, 1)
        kc_n, vc_n, ks_n, vs_n, kw_n, vw_n = split_side(side, bd, s_new)
        s_win_k = jnp.concatenate([cache_win_k, kw_n], axis=1)[:, s_new:]
        s_win_v = jnp.concatenate([cache_win_v, vw_n], axis=1)[:, s_new:]
        return (kc, vc, new), (kc_n, vc_n, ks_n, vs_n, s_win_k, s_win_v)

    def sample_attend(proj, ctx):
        kc, vc, new = ctx
        o_c, idx8 = nsa_dec_cmp(proj, kc, vc, bias_c_s, sel_map_s, nb_s, nsb_s, t_pos)
        idx = idx8[:, :G, :SEL_TOPK]
        is_new = idx >= n_past_blk
        jp = jnp.minimum(idx, n_past_blk - 1)
        phys = (jnp.take_along_axis(page_table[:, None, :], jp // blk_per_page, axis=2) * blk_per_page
                + jp % blk_per_page)
        k_pos = (idx[..., None] * SEL_BLOCK + jnp.arange(SEL_BLOCK)).reshape(bd, G, 1, -1)
        d_s = t_pos - k_pos
        tab = rel_bias.astype(F32).reshape(REL_BUCKETS, G, NSA_HPG)
        onehot = jax.nn.one_hot(_rel_bucket(d_s[:, :, 0]), REL_BUCKETS, dtype=F32)
        bias_s = jnp.einsum('bgkn,ngh->bghk', onehot, tab, precision=HI)
        bias_s = jnp.pad(bias_s, ((0, 0), (0, 0), (0, SUBLANE - NSA_HPG), (0, 0)))
        q8 = jnp.pad(proj[:, 0, :MAIN_W].reshape(bd, G, NSA_HPG, dh),
                     ((0, 0), (0, 0), (0, SUBLANE - NSA_HPG), (0, 0)))
        new_rows = jnp.pad(jnp.transpose(new[:, 0, 2 * KV_LANES:].reshape(bd, 4, G, dh), (0, 2, 1, 3)),
                           ((0, 0), (0, 0), (0, SUBLANE - 4), (0, 0)))
        gates = proj[:, 0, B_GATE_OFF:B_GATE_OFF + GATE_W].reshape(bd, 3, G, NSA_HPG)
        gates8 = jnp.pad(jnp.transpose(gates, (0, 2, 3, 1)),
                         ((0, 0), (0, 0), (0, SUBLANE - NSA_HPG), (0, SUBLANE - 3)))
        o = nsa_dec_sel(phys.astype(jnp.int32), is_new.astype(jnp.int32),
                        cache_slc_k.reshape(-1, SEL_BLOCK, G, dh),
                        cache_slc_v.reshape(-1, SEL_BLOCK, G, dh),
                        q8, new_rows, d_s.astype(jnp.int32), bias_s,
                        cache_win_k, cache_win_v,
                        bias_w_s, o_c, gates8, wb_s, t_pos, win_start)
        main = o[:, :, :NSA_HPG].reshape(bd, 1, MAIN_W)
        return jnp.pad(main, ((0, 0), (0, ts - 1), (0, 0))).astype(BF16)

    y_s, s_shift, s_wkv, s_side = _trunk(
        xs, s_new, P, W, state_shift, jnp.stack([_to_block_diag(state_wkv[l]) for l in range(N_A)]),
        lambda l: (mem_k2[l], 0, mem_v2[l], 0), sample_side, sample_attend, chunk=SUBLANE, tb=SUBLANE)
    y_sample = y_s[:, :s_new]
    s_cmp_k, s_cmp_v, s_slc_k, s_slc_v, s_win_k, s_win_v = s_side

    return (y_prompt, y_sample, p_mem_k, p_mem_v, p_wkv, p_shift,
            p_cmp_k, p_cmp_v, p_slc_k, p_slc_v, p_win_k, p_win_v,
            s_wkv, s_shift, s_cmp_k, s_cmp_v, s_slc_k, s_slc_v, s_win_k, s_win_v)
```

```python
import functools
import math

import jax
import jax.numpy as jnp
from jax import lax
from jax.experimental import pallas as pl
from jax.experimental.pallas import tpu as pltpu

F32 = jnp.float32
BF16 = jnp.bfloat16
HI = lax.Precision.HIGHEST

D_MODEL = 2048
DEPTH = 4
N_A = 2
MEM_HEADS = 4
MEM_HEAD_DIM = 128
MEM_W = 512
MAIN_W = 1536
RWKV_HEAD = 64
RWKV_HEADS = 24
GN_EPS = 64e-5
NSA_HEAD_DIM = 128
NSA_Q_HEADS = 12
NSA_KV_GROUPS = 4
NSA_HPG = 3
GATE_W = 36
CMP_BLOCK = 32
CMP_STRIDE = 16
SEL_BLOCK = 64
SEL_TOPK = 16
SEL_LOCAL = 2
WINDOW = 512
REL_BUCKETS = 32
REL_MAX_DIST = 128
D_FF = 5504
NORM_EPS = 1e-6
MASK_NEG = -1e30
FORCE_SCORE = 1e9

LANE = 128
SUBLANE = 8
MXU_DIM = 256
VMEM_LIMIT = 48 * 1024 * 1024

FFN_DOWN_TK = 11 * LANE
HEAD_BLOCK = MXU_DIM // RWKV_HEAD
N_HEAD_BLOCKS = RWKV_HEADS // HEAD_BLOCK
LOWRANK_PAD = MXU_DIM
NSA_TQ = 128
NSA_KT = 256
NSA_BANDS = 4


def _params(*sem):
    return pltpu.CompilerParams(dimension_semantics=sem, vmem_limit_bytes=VMEM_LIMIT)


def _rms(x, g):
    return x * lax.rsqrt(jnp.mean(x * x, axis=-1, keepdims=True) + NORM_EPS) * g


def _sigmoid(x):
    return 1.0 / (1.0 + jnp.exp(-x))


def _softplus(x):
    return jnp.maximum(x, 0.0) + jnp.log(1.0 + jnp.exp(-jnp.abs(x)))


def _gelu_tanh(x):
    return 0.5 * x * (1.0 + jnp.tanh(math.sqrt(2.0 / math.pi) * (x + 0.044715 * x * x * x)))


def _dot(a, b, precision=None):
    return jnp.dot(a, b, preferred_element_type=F32, precision=precision)


def _dot_nt(a, b, precision=None):
    return lax.dot_general(a, b, (((1,), (1,)), ((), ())), preferred_element_type=F32,
                           precision=precision)


def _dot_tn(a, b, precision=None):
    return lax.dot_general(a, b, (((0,), (0,)), ((), ())), preferred_element_type=F32,
                           precision=precision)


def _masked_softmax(s, mask):
    s = jnp.where(mask, s, MASK_NEG)
    p = jnp.exp(s - jnp.max(s, axis=-1, keepdims=True)) * mask.astype(F32)
    den = jnp.sum(p, axis=-1, keepdims=True)
    return p / jnp.where(den > 0, den, 1.0)


def _row_tile(rows, target):
    t = min(rows, target)
    while rows % t:
        t -= SUBLANE
    return t


def _norm_matmul_kernel(x_ref, g_ref, w_ref, o_ref, xn_ref):
    @pl.when(pl.program_id(1) == 0)
    def _():
        xn_ref[...] = _rms(x_ref[...], g_ref[...]).astype(BF16)

    o_ref[...] = _dot(xn_ref[...], w_ref[...]).astype(o_ref.dtype)


def norm_matmul(x, g, w, tn, out_dtype=F32, tm=512):
    rows, d = x.shape
    n = w.shape[1]
    tm = _row_tile(rows, tm)
    assert n % tn == 0
    return pl.pallas_call(
        _norm_matmul_kernel,
        grid=(rows // tm, n // tn),
        in_specs=[pl.BlockSpec((tm, d), lambda i, j: (i, 0)),
                  pl.BlockSpec((1, d), lambda i, j: (0, 0)),
                  pl.BlockSpec((d, tn), lambda i, j: (0, j))],
        out_specs=pl.BlockSpec((tm, tn), lambda i, j: (i, j)),
        out_shape=jax.ShapeDtypeStruct((rows, n), out_dtype),
        scratch_shapes=[pltpu.VMEM((tm, d), BF16)],
        compiler_params=_params("arbitrary", "arbitrary"),
        name="norm_matmul",
    )(x, g.reshape(1, d), w)


def _norm_swiglu_kernel(x_ref, g_ref, wg_ref, wu_ref, o_ref, xn_ref, *, tail):
    j = pl.program_id(1)
    last = pl.num_programs(1) - 1

    @pl.when(j == 0)
    def _():
        xn_ref[...] = _rms(x_ref[...], g_ref[...]).astype(BF16)

    xn = xn_ref[...]
    gate = _dot(xn, wg_ref[0, 0])
    up = _dot(xn, wu_ref[0, 0])
    tn = gate.shape[1]
    if tail == tn:
        o_ref[...] = (gate * _sigmoid(gate) * up).astype(o_ref.dtype)
    else:
        @pl.when(j != last)
        def _():
            o_ref[...] = (gate * _sigmoid(gate) * up).astype(o_ref.dtype)

        @pl.when(j == last)
        def _():
            up_s = jnp.concatenate([up[:, tn - tail:], jnp.zeros((up.shape[0], tn - tail), F32)], axis=1)
            col = lax.broadcasted_iota(jnp.int32, gate.shape, 1)
            o_ref[...] = jnp.where(col < tail, gate * _sigmoid(gate) * up_s, 0.0).astype(o_ref.dtype)


def norm_swiglu(x, g, w_gu, l, i, tn=512, tm=512):
    rows, d = x.shape
    f = w_gu.shape[-1] // 2
    assert f % LANE == 0 and tn % LANE == 0
    tm = _row_tile(rows, tm)
    n_tiles = -(-f // tn)
    tail = f - (n_tiles - 1) * tn
    el = pl.Element

    def up_col(j):
        return pl.multiple_of(jnp.minimum(f + j * tn, 2 * f - tn), LANE)

    return pl.pallas_call(
        functools.partial(_norm_swiglu_kernel, tail=tail),
        grid=(rows // tm, n_tiles),
        in_specs=[pl.BlockSpec((tm, d), lambda r, j: (r, 0)),
                  pl.BlockSpec((1, d), lambda r, j: (0, 0)),
                  pl.BlockSpec((el(1), el(1), el(d), el(tn)), lambda r, j: (l, i, 0, j * tn)),
                  pl.BlockSpec((el(1), el(1), el(d), el(tn)), lambda r, j: (l, i, 0, up_col(j)))],
        out_specs=pl.BlockSpec((tm, tn), lambda r, j: (r, j)),
        out_shape=jax.ShapeDtypeStruct((rows, n_tiles * tn), BF16),
        scratch_shapes=[pltpu.VMEM((tm, d), BF16)],
        compiler_params=_params("arbitrary", "arbitrary"),
        name="norm_swiglu",
    )(x, g.reshape(1, d), w_gu, w_gu)


def _matmul_norm_res_kernel(a_ref, w_ref, g_ref, h_ref, o_ref, acc_ref, *, scale, overlap):
    k = pl.program_id(1)
    last = pl.num_programs(1) - 1

    @pl.when(k == 0)
    def _():
        acc_ref[...] = jnp.zeros_like(acc_ref)

    a = a_ref[...]
    if overlap:
        col = lax.broadcasted_iota(jnp.int32, a.shape, 1)
        a = jnp.where(col < jnp.where(k == last, overlap, 0), jnp.zeros_like(a), a)
    acc_ref[...] += _dot(a, w_ref[0, 0])

    @pl.when(k == last)
    def _():
        o_ref[...] = h_ref[...] + scale * _rms(acc_ref[...], g_ref[...])


def matmul_norm_res(a, w, l, i, g, h, scale, tk=512, tm=512):
    rows = a.shape[0]
    kdim, d = w.shape[-2:]
    assert kdim % LANE == 0 and tk % LANE == 0 and a.shape[1] >= kdim
    tm = _row_tile(rows, tm)
    n_k = -(-kdim // tk)
    overlap = n_k * tk - kdim
    el = pl.Element

    def k_off(k):
        return pl.multiple_of(jnp.minimum(k * tk, kdim - tk), LANE)

    return pl.pallas_call(
        functools.partial(_matmul_norm_res_kernel, scale=scale, overlap=overlap),
        grid=(rows // tm, n_k),
        in_specs=[pl.BlockSpec((el(tm), el(tk)), lambda r, k: (r * tm, k_off(k))),
                  pl.BlockSpec((el(1), el(1), el(tk), el(d)), lambda r, k: (l, i, k_off(k), 0)),
                  pl.BlockSpec((1, d), lambda r, k: (0, 0)),
                  pl.BlockSpec((tm, d), lambda r, k: (r, 0))],
        out_specs=pl.BlockSpec((tm, d), lambda r, k: (r, 0)),
        out_shape=jax.ShapeDtypeStruct((rows, d), F32),
        scratch_shapes=[pltpu.VMEM((tm, d), F32)],
        compiler_params=_params("arbitrary", "arbitrary"),
        name="matmul_norm_res",
    )(a, w, g.reshape(1, d), h)


RW_R, RW_K, RW_V, RW_QM, RW_W1, RW_V1, RW_A1, RW_G1 = 0, 6, 12, 18, 20, 21, 22, 23
RW_NBLK = 24
RW_MIX_STARTS = ((RW_R, 0), (RW_K, 2), (RW_V, 3), (RW_QM, None), (RW_W1, 1), (RW_V1, 3), (RW_A1, 4),
                 (RW_G1, 5))


def _rwkv_in_kernel(x_ref, sp_ref, g_ref, mu_ref, w_ref, o_ref, last_ref,
                    u_sc, xx_sc, xm_sc, carry_sc, *, tm, t_seq, n_seq, last_tile, last_row):
    ti = pl.program_id(1)
    j = pl.program_id(2)

    @pl.when(j == 0)
    def _():
        @pl.when(ti == 0)
        def _():
            carry_sc[...] = sp_ref[0]

        u = _rms(x_ref[0], g_ref[...])
        rows = lax.broadcasted_iota(jnp.int32, u.shape, 0)
        prev = pltpu.roll(u, 1, axis=0)
        for s in range(n_seq):
            prev = jnp.where(rows == s * t_seq, carry_sc[s:s + 1, :], prev)
        u_sc[...] = u
        xx_sc[...] = prev - u
        if n_seq == 1:
            carry_sc[...] = u[tm - 1:tm, :]

        @pl.when(ti == last_tile)
        def _():
            for s in range(n_seq):
                last_ref[0, s:s + 1, :] = u[s * t_seq + last_row:s * t_seq + last_row + 1, :]

    for start, mix in RW_MIX_STARTS:
        @pl.when(j == start)
        def _(mix=mix):
            if mix is None:
                xm_sc[...] = u_sc[...].astype(BF16)
            else:
                xm_sc[...] = (u_sc[...] + xx_sc[...] * mu_ref[mix:mix + 1, :]).astype(BF16)

    o_ref[0] = _dot(xm_sc[...], w_ref[...])


def rwkv_in_proj(h, shift_prev, g, mu, w_cat, t_real, tm=1024):
    b, t, d = h.shape
    n_seq = math.gcd(b, max(1, tm // t))
    bg, tg = b // n_seq, t * n_seq
    tm = _row_tile(tg, tm)
    assert n_seq == 1 or tm == tg
    tn = MXU_DIM
    last_tile, last_row = ((t_real - 1) // tm, (t_real - 1) % tm) if n_seq == 1 else (0, t_real - 1)
    mu8 = jnp.concatenate([mu, jnp.zeros((SUBLANE - mu.shape[0], d), F32)], axis=0)
    proj, last = pl.pallas_call(
        functools.partial(_rwkv_in_kernel, tm=tm, t_seq=t, n_seq=n_seq, last_tile=last_tile,
                          last_row=last_row),
        grid=(bg, tg // tm, RW_NBLK),
        in_specs=[pl.BlockSpec((1, tm, d), lambda bi, ti, j: (bi, ti, 0)),
                  pl.BlockSpec((1, n_seq, d), lambda bi, ti, j: (bi, 0, 0)),
                  pl.BlockSpec((1, d), lambda bi, ti, j: (0, 0)),
                  pl.BlockSpec((SUBLANE, d), lambda bi, ti, j: (0, 0)),
                  pl.BlockSpec((d, tn), lambda bi, ti, j: (0, j))],
        out_specs=[pl.BlockSpec((1, tm, tn), lambda bi, ti, j: (bi, ti, j)),
                   pl.BlockSpec((1, n_seq, d), lambda bi, ti, j: (bi, 0, 0))],
        out_shape=[jax.ShapeDtypeStruct((bg, tg, RW_NBLK * tn), F32),
                   jax.ShapeDtypeStruct((bg, n_seq, d), F32)],
        scratch_shapes=[pltpu.VMEM((tm, d), F32), pltpu.VMEM((tm, d), F32),
                        pltpu.VMEM((tm, d), BF16), pltpu.VMEM((n_seq, d), F32)],
        compiler_params=_params("arbitrary", "arbitrary", "arbitrary"),
        name="rwkv_in_proj",
    )(h.reshape(bg, tg, d), shift_prev.reshape(bg, n_seq, d), g.reshape(1, d), mu8, w_cat)
    return proj.reshape(b, t, RW_NBLK * tn), last.reshape(b, d)


def _rwkv_scan_kernel(*refs, chunk, n_chunks, has_vres, t_valid, n_hb):
    if has_vres:
        (r_ref, k_ref, v_ref, tw_ref, ta_ref, tg_ref, tv_ref, vf_ref,
         w2_ref, a2_ref, g2_ref, v2_ref, vec_ref, s0_ref, y_ref, sout_ref, s_sc) = refs
    else:
        (r_ref, k_ref, v_ref, tw_ref, ta_ref, tg_ref,
         w2_ref, a2_ref, g2_ref, vec_ref, s0_ref, y_ref, sout_ref, s_sc) = refs
    ti = pl.program_id(2)
    c_len = chunk
    lanes = MXU_DIM
    rows4 = HEAD_BLOCK * c_len

    @pl.when(ti == 0)
    def _():
        s_sc[...] = s0_ref[0]

    li = lax.broadcasted_iota(jnp.int32, (lanes, lanes), 0) // RWKV_HEAD
    lj = lax.broadcasted_iota(jnp.int32, (lanes, lanes), 1) // RWKV_HEAD
    seg = (li == lj).astype(BF16)
    ci = lax.broadcasted_iota(jnp.int32, (c_len, c_len), 0)
    cj = lax.broadcasted_iota(jnp.int32, (c_len, c_len), 1)
    tri_c = (cj <= ci).astype(BF16)

    def split2(x):
        bits = lax.bitcast_convert_type(x, jnp.uint32) & jnp.uint32(0xFFFF0000)
        hi = lax.bitcast_convert_type(bits, F32)
        return hi, x - hi

    def seg_sums(xs):
        parts = [p for x in xs for p in split2(x)]
        res = _dot(jnp.concatenate(parts, axis=0).astype(BF16), seg)
        return [res[(2 * n) * c_len:(2 * n + 1) * c_len] + res[(2 * n + 1) * c_len:(2 * n + 2) * c_len]
                for n in range(len(xs))]

    lane_head = lax.broadcasted_iota(jnp.int32, (c_len, lanes), 1) // RWKV_HEAD
    ri = lax.broadcasted_iota(jnp.int32, (rows4, 2 * rows4), 0)
    rj = lax.broadcasted_iota(jnp.int32, (rows4, 2 * rows4), 1) & (rows4 - 1)
    strict = rj < ri
    incl = rj <= ri
    ei = lax.broadcasted_iota(jnp.int32, (rows4, rows4), 0)
    ej = lax.broadcasted_iota(jnp.int32, (rows4, rows4), 1)
    eye = (ei == ej).astype(F32)
    n_double = int(math.log2(c_len)) - 1

    def stack(x):
        return jnp.concatenate(
            [jnp.where(lane_head == hh, x, 0.0) for hh in range(HEAD_BLOCK)], axis=0)

    def one_chunk(c, carry):
        sl = pl.ds(pl.multiple_of(c * c_len, c_len), c_len)
        tw_act = jnp.tanh(tw_ref[0, sl, :])
        ta_act = ta_ref[0, sl, :]
        tg_act = _sigmoid(tg_ref[0, sl, :])
        tv_act = tv_ref[0, sl, :] if has_vres else None
        if t_valid is not None:
            t_idx = ti * (n_chunks * c_len) + c * c_len + lax.broadcasted_iota(
                jnp.int32, (c_len, lanes), 0)
            live = t_idx < t_valid
        for hb in range(n_hb):
            hl = slice(hb * lanes, (hb + 1) * lanes)
            w0, a0, v0 = vec_ref[0:1, hl], vec_ref[1:2, hl], vec_ref[2:3, hl]
            kkw, kaw = vec_ref[3:4, hl], vec_ref[4:5, hl]
            lnw, lnb, rk = vec_ref[5:6, hl], vec_ref[6:7, hl], vec_ref[7:8, hl]
            r = r_ref[0, sl, hl]
            k = k_ref[0, sl, hl]
            v = v_ref[0, sl, hl]
            logw = -_softplus(-(w0 + _dot(tw_act, w2_ref[:, hl]))) - 0.5
            dlog = -jnp.exp(logw)
            rate = _sigmoid(a0 + _dot(ta_act, a2_ref[:, hl]))
            gate = _dot(tg_act, g2_ref[:, hl])
            if has_vres:
                v = v + (vf_ref[0, sl, hl] - v) * _sigmoid(v0 + _dot(tv_act, v2_ref[:, hl]))
            kk = k * kkw
            k = k * (1.0 + (rate - 1.0) * kaw)
            kk_sq, rk_sum = seg_sums([kk * kk, r * k * rk])
            kk = kk / jnp.maximum(jnp.sqrt(kk_sq), 1e-12)
            if t_valid is not None:
                dlog = jnp.where(live, dlog, 0.0)
                kk = jnp.where(live, kk, 0.0)
                k_live = jnp.where(live, k, 0.0)
            else:
                k_live = k
            d_hi, d_lo = split2(dlog)
            cum2 = _dot(tri_c, jnp.concatenate([d_hi, d_lo], axis=1).astype(BF16))
            cum = cum2[:, :lanes] + cum2[:, lanes:]
            inv = jnp.exp(-cum)
            ar = jnp.concatenate([stack(-kk * jnp.exp(cum - dlog)), stack(r * jnp.exp(cum))],
                                 axis=0).astype(BF16)
            bk_s = jnp.concatenate([stack(kk * rate * inv), stack(k_live * inv)],
                                   axis=0).astype(BF16)
            v_s = stack(v)
            g_end = jnp.exp(cum[c_len - 1:c_len, :])
            s_old = s_sc[hb]

            big = _dot_nt(ar, jnp.concatenate([bk_s, s_old.astype(BF16)], axis=0))
            a_bk = jnp.where(strict, big[:rows4, :2 * rows4], 0.0)
            r_bk = jnp.where(incl, big[rows4:, :2 * rows4], 0.0)
            lmat = a_bk[:, :rows4]
            tinv = eye + lmat
            lpow = lmat.astype(BF16)
            lpow = _dot(lpow, lpow).astype(BF16)
            for _ in range(n_double - 1):
                both = _dot(jnp.concatenate([lpow, tinv.astype(BF16)], axis=0), lpow)
                tinv = tinv + both[rows4:]
                lpow = both[:rows4].astype(BF16)
            tinv = tinv + _dot(tinv.astype(BF16), lpow)
            u = _dot(tinv, big[:rows4, 2 * rows4:] + _dot(a_bk[:, rows4:], v_s))
            uv = jnp.concatenate([u, v_s], axis=0).astype(BF16)
            y_s = big[rows4:, 2 * rows4:] + _dot(r_bk.astype(BF16), uv)
            s_sc[hb] = (s_old + _dot_tn(uv, bk_s)) * g_end
            y = y_s[0:c_len]
            for hh in range(1, HEAD_BLOCK):
                y = y + y_s[hh * c_len:(hh + 1) * c_len]

            mean = seg_sums([y])[0] * (1.0 / RWKV_HEAD)
            yc = y - mean
            var = seg_sums([yc * yc])[0] * (1.0 / RWKV_HEAD)
            yn = yc * lax.rsqrt(var + GN_EPS) * lnw + lnb
            y_ref[0, sl, hl] = ((yn + rk_sum * v) * gate).astype(y_ref.dtype)
        return carry

    lax.fori_loop(0, n_chunks, one_chunk, 0)

    @pl.when(ti == pl.num_programs(2) - 1)
    def _():
        sout_ref[0] = s_sc[...]


def rwkv_scan(proj, v_first_proj, w2, a2, g2, v2, vec, s0_bd, t_real, chunk, tb, n_hb=2):
    b, t, _ = proj.shape
    lanes = MXU_DIM
    tb = min(tb, t)
    assert t % tb == 0 and tb % chunk == 0 and N_HEAD_BLOCKS % n_hb == 0
    has_vres = v_first_proj is not None
    t_valid = None if t_real == t else t_real

    wide = n_hb * lanes

    def col(block0):
        assert block0 % n_hb == 0
        return pl.BlockSpec((1, tb, wide), lambda bi, hg, ti: (bi, ti, block0 // n_hb + hg))

    def fixed(block):
        return pl.BlockSpec((1, tb, lanes), lambda bi, hg, ti: (bi, ti, block))

    def wcol():
        return pl.BlockSpec((LOWRANK_PAD, wide), lambda bi, hg, ti: (0, hg))

    st = pl.BlockSpec((1, n_hb, lanes, lanes), lambda bi, hg, ti: (bi, hg, 0, 0))
    if has_vres:
        in_specs = [col(RW_R), col(RW_K), col(RW_V), fixed(RW_W1), fixed(RW_A1), fixed(RW_G1),
                    fixed(RW_V1), col(RW_V), wcol(), wcol(), wcol(), wcol()]
        args = [proj, proj, proj, proj, proj, proj, proj, v_first_proj, w2, a2, g2, v2]
    else:
        in_specs = [col(RW_R), col(RW_K), col(RW_V), fixed(RW_W1), fixed(RW_A1), fixed(RW_G1),
                    wcol(), wcol(), wcol()]
        args = [proj, proj, proj, proj, proj, proj, w2, a2, g2]
    in_specs += [pl.BlockSpec((SUBLANE, wide), lambda bi, hg, ti: (0, hg)), st]
    args += [vec, s0_bd]
    return pl.pallas_call(
        functools.partial(_rwkv_scan_kernel, chunk=chunk, n_chunks=tb // chunk,
                          has_vres=has_vres, t_valid=t_valid, n_hb=n_hb),
        grid=(b, N_HEAD_BLOCKS // n_hb, t // tb),
        in_specs=in_specs,
        out_specs=[pl.BlockSpec((1, tb, wide), lambda bi, hg, ti: (bi, ti, hg)), st],
        out_shape=[jax.ShapeDtypeStruct((b, t, MAIN_W), BF16),
                   jax.ShapeDtypeStruct(s0_bd.shape, F32)],
        scratch_shapes=[pltpu.VMEM((n_hb, lanes, lanes), F32)],
        compiler_params=_params("arbitrary", "arbitrary", "arbitrary"),
        name="rwkv_scan",
    )(*args)


def _to_block_diag(s):
    b = s.shape[0]
    s = s.reshape(b, N_HEAD_BLOCKS, HEAD_BLOCK, RWKV_HEAD, RWKV_HEAD)
    eye = jnp.eye(HEAD_BLOCK, dtype=s.dtype)
    bd = s[:, :, :, :, None, :] * eye[None, None, :, None, :, None]
    return bd.reshape(b, N_HEAD_BLOCKS, MXU_DIM, MXU_DIM)


def _from_block_diag(bd):
    b = bd.shape[0]
    n = RWKV_HEAD
    x = jnp.stack([bd[:, :, hh * n:(hh + 1) * n, hh * n:(hh + 1) * n] for hh in range(HEAD_BLOCK)],
                  axis=2)
    return x.reshape(b, RWKV_HEADS, RWKV_HEAD, RWKV_HEAD)


def _mem_attn_kernel(q_ref, k_ref, v_ref, o_ref):
    scale = MEM_HEAD_DIM ** -0.5
    for hh in range(MEM_HEADS):
        sl = slice(hh * MEM_HEAD_DIM, (hh + 1) * MEM_HEAD_DIM)
        s = _dot_nt(q_ref[0, :, sl], k_ref[0, :, sl]) * scale
        p = jnp.exp(s - jnp.max(s, axis=-1, keepdims=True))
        p = p / jnp.sum(p, axis=-1, keepdims=True)
        o_ref[0, :, sl] = _dot(p, v_ref[0, :, sl]).astype(o_ref.dtype)


def mem_attend(qsrc, q_block, ksrc, k_block, vsrc, v_block, tq=512):
    b, t, _ = qsrc.shape
    m = ksrc.shape[1]
    tq = _row_tile(t, tq)
    return pl.pallas_call(
        _mem_attn_kernel,
        grid=(b, t // tq),
        in_specs=[pl.BlockSpec((1, tq, MEM_W), lambda bi, ti: (bi, ti, q_block)),
                  pl.BlockSpec((1, m, MEM_W), lambda bi, ti: (bi, 0, k_block)),
                  pl.BlockSpec((1, m, MEM_W), lambda bi, ti: (bi, 0, v_block))],
        out_specs=pl.BlockSpec((1, tq, MEM_W), lambda bi, ti: (bi, ti, 0)),
        out_shape=jax.ShapeDtypeStruct((b, t, MEM_W), BF16),
        compiler_params=_params("arbitrary", "arbitrary"),
        name="mem_attend",
    )(qsrc, ksrc, vsrc)


CMP_PAGE = 128
CMP_CHUNKS = CMP_PAGE // CMP_STRIDE
KV_LANES = NSA_KV_GROUPS * NSA_HEAD_DIM


def _chunk_proj_kernel(pt_ref, *refs, n_pages):
    del pt_ref
    page_refs = refs[:n_pages]
    w_ref, o0_ref, o1_ref, x_sc = refs[n_pages:]
    for kp in range(n_pages):
        for pos in range(CMP_STRIDE):
            x_sc[kp * CMP_CHUNKS:(kp + 1) * CMP_CHUNKS, pos * LANE:(pos + 1) * LANE] = (
                page_refs[kp][pl.ds(pos, CMP_CHUNKS, stride=CMP_STRIDE), :])
    res = _dot(x_sc[...].astype(BF16), w_ref[...])
    o0_ref[0] = res[:, :LANE]
    o1_ref[0] = res[:, LANE:]


def chunk_proj(rows2d, table, w_flat, n_pages, col_block=0):
    b, n_tab = table.shape
    assert n_tab % n_pages == 0

    def page_spec(kp):
        return pl.BlockSpec(
            (CMP_PAGE, LANE),
            lambda bi, pg, g, pt: (pt[bi, pg * n_pages + kp], col_block * NSA_KV_GROUPS + g))

    out_spec = pl.BlockSpec((1, n_pages * CMP_CHUNKS, LANE), lambda bi, pg, g, pt: (bi, pg, g))
    out_shape = jax.ShapeDtypeStruct((b, n_tab * CMP_CHUNKS, KV_LANES), F32)
    grid_spec = pltpu.PrefetchScalarGridSpec(
        num_scalar_prefetch=1,
        grid=(b, n_tab // n_pages, NSA_KV_GROUPS),
        in_specs=[page_spec(kp) for kp in range(n_pages)]
        + [pl.BlockSpec(w_flat.shape, lambda bi, pg, g, pt: (0, 0))],
        out_specs=[out_spec, out_spec],
        scratch_shapes=[pltpu.VMEM((n_pages * CMP_CHUNKS, CMP_STRIDE * LANE), F32)],
    )
    return pl.pallas_call(
        functools.partial(_chunk_proj_kernel, n_pages=n_pages),
        grid_spec=grid_spec,
        out_shape=[out_shape, out_shape],
        compiler_params=_params("arbitrary", "arbitrary", "arbitrary"),
        name="chunk_proj",
    )(table, *([rows2d] * n_pages), w_flat)


def _chunk_proj_pool_kernel(pt_ref, *refs, n_pages):
    del pt_ref
    page_refs = refs[:n_pages]
    w_ref, o0_ref, o1_ref, x_sc = refs[n_pages:]
    for g in range(NSA_KV_GROUPS):
        for kp in range(n_pages):
            for pos in range(CMP_STRIDE):
                x_sc[kp * CMP_CHUNKS:(kp + 1) * CMP_CHUNKS, pos * LANE:(pos + 1) * LANE] = (
                    page_refs[kp][0, pl.ds(pos, CMP_CHUNKS, stride=CMP_STRIDE), g, :])
        res = _dot(x_sc[...].astype(BF16), w_ref[...])
        o0_ref[0, :, g * LANE:(g + 1) * LANE] = res[:, :LANE]
        o1_ref[0, :, g * LANE:(g + 1) * LANE] = res[:, LANE:]


def chunk_proj_pool(pool, table, w_flat, n_pages):
    b, n_tab = table.shape
    assert n_tab % n_pages == 0

    def page_spec(kp):
        return pl.BlockSpec((1, CMP_PAGE, NSA_KV_GROUPS, LANE),
                            lambda bi, pg, pt: (pt[bi, pg * n_pages + kp], 0, 0, 0))

    out_spec = pl.BlockSpec((1, n_pages * CMP_CHUNKS, KV_LANES), lambda bi, pg, pt: (bi, pg, 0))
    out_shape = jax.ShapeDtypeStruct((b, n_tab * CMP_CHUNKS, KV_LANES), F32)
    grid_spec = pltpu.PrefetchScalarGridSpec(
        num_scalar_prefetch=1,
        grid=(b, n_tab // n_pages),
        in_specs=[page_spec(kp) for kp in range(n_pages)]
        + [pl.BlockSpec(w_flat.shape, lambda bi, pg, pt: (0, 0))],
        out_specs=[out_spec, out_spec],
        scratch_shapes=[pltpu.VMEM((n_pages * CMP_CHUNKS, CMP_STRIDE * LANE), F32)],
    )
    return pl.pallas_call(
        functools.partial(_chunk_proj_pool_kernel, n_pages=n_pages),
        grid_spec=grid_spec,
        out_shape=[out_shape, out_shape],
        compiler_params=_params("arbitrary", "arbitrary"),
        name="chunk_proj_pool",
    )(table, *([pool] * n_pages), w_flat)


def _block_mlp_kernel(p0_ref, p1_ref, pe_ref, w1_ref, b1_ref, w2_ref, o_ref):
    const = _dot(pe_ref[...], w1_ref[...])[0:1, :] + b1_ref[...]
    w2 = w2_ref[...]
    for g in range(NSA_KV_GROUPS):
        sl = slice(g * LANE, (g + 1) * LANE)
        hid = const + p0_ref[0, :, sl] + p1_ref[0, :, sl]
        o_ref[0, :, sl] = _dot(_gelu_tanh(hid), w2)


def block_mlp(p0, p1, pe, w1, b1, w2, tb=1024):
    b, nb, _ = p0.shape
    tb = _row_tile(nb, tb)
    pe8 = jnp.concatenate([pe.reshape(1, -1), jnp.zeros((SUBLANE - 1, pe.size), F32)], axis=0)
    spec = pl.BlockSpec((1, tb, KV_LANES), lambda bi, i: (bi, i, 0))
    full = lambda a: pl.BlockSpec(a.shape, lambda bi, i: (0,) * a.ndim)
    b1r = b1.reshape(1, -1)
    return pl.pallas_call(
        _block_mlp_kernel,
        grid=(b, nb // tb),
        in_specs=[spec, spec, full(pe8), full(w1), full(b1r), full(w2)],
        out_specs=spec,
        out_shape=jax.ShapeDtypeStruct((b, nb, KV_LANES), F32),
        compiler_params=_params("arbitrary", "arbitrary"),
        name="block_mlp",
    )(p0, p1, pe8, w1, b1r, w2)


def _w1_flat(w1):
    r = CMP_BLOCK // CMP_STRIDE
    e = w1.shape[1]
    w = w1.reshape(r, CMP_STRIDE, NSA_HEAD_DIM, e)
    return jnp.transpose(w, (1, 2, 0, 3)).reshape(CMP_STRIDE * NSA_HEAD_DIM, r * e).astype(BF16)


def compress(parts, nb, pe, w1, b1, w2):
    return block_mlp(parts[0][:, :nb], parts[1][:, 1:nb + 1], pe, w1, b1, w2)


def _rel_bucket(dist):
    n = jnp.maximum(dist, 0)
    max_exact = REL_BUCKETS // 2
    nf = jnp.maximum(n, 1).astype(F32)
    large = max_exact + (jnp.log(nf / max_exact) / math.log(REL_MAX_DIST / max_exact)
                         * (REL_BUCKETS - max_exact)).astype(jnp.int32)
    return jnp.where(n < max_exact, n, jnp.minimum(large, REL_BUCKETS - 1))


def _bias_lookup(rel_bias, dist):
    tab = rel_bias.astype(F32).reshape(REL_BUCKETS, NSA_KV_GROUPS, NSA_HPG)
    onehot = jax.nn.one_hot(_rel_bucket(dist), REL_BUCKETS, dtype=F32)
    out = jnp.dot(onehot, tab.reshape(REL_BUCKETS, -1), precision=HI)
    out = out.reshape(dist.shape + (NSA_KV_GROUPS, NSA_HPG))
    return jnp.moveaxis(out, (-2, -1), (0, 1))


def _cmp_to_sel(nb_pad, nb, nsb_pad, nsb):
    i = jnp.arange(nb_pad)[:, None]
    j = jnp.arange(nsb_pad)[None, :]
    start = i * CMP_STRIDE
    hit = (start < (j + 1) * SEL_BLOCK) & (start + CMP_BLOCK > j * SEL_BLOCK) & (i < nb) & (j < nsb)
    return hit.astype(F32)


def _nsa_prompt_kernel(q_ref, gt_ref, kc_ref, vc_ref, bc_ref, band_ref, ks_ref, vs_ref,
                       kw_ref, vw_ref, m_ref, o_ref, s_sc, *, nb, nsb):
    tq = NSA_TQ
    rows = NSA_HPG * tq
    i = pl.program_id(2)
    scale = NSA_HEAD_DIM ** -0.5
    qb = q_ref[0]
    q3 = jnp.concatenate([qb[:, hh * LANE:(hh + 1) * LANE] for hh in range(NSA_HPG)],
                         axis=0).astype(BF16)
    row_q = lax.broadcasted_iota(jnp.int32, (rows, LANE), 0) & (tq - 1)
    lane = lax.broadcasted_iota(jnp.int32, (rows, LANE), 1)
    t_pos = i * tq + row_q

    s_c = _dot_nt(q3, kc_ref[0].astype(BF16)) * scale + bc_ref[0].reshape(rows, LANE)
    m_c = (t_pos - (lane * CMP_STRIDE + (CMP_BLOCK - 1)) >= 0) & (lane < nb)
    p_c = _masked_softmax(s_c, m_c)
    o_c = _dot(p_c, vc_ref[0])

    imp3 = _dot_nt(m_ref[...], p_c, HI)
    imp = imp3[:, 0:tq]
    for hh in range(1, NSA_HPG):
        imp = imp + imp3[:, hh * tq:(hh + 1) * tq]
    jb = lax.broadcasted_iota(jnp.int32, (nsb, tq), 0)
    cur = (i * tq + lax.broadcasted_iota(jnp.int32, (nsb, tq), 1)) // SEL_BLOCK
    valid = jb <= cur
    forced = valid & ((jb == 0) | (jb > cur - SEL_LOCAL))
    imp = jnp.where(forced, FORCE_SCORE, jnp.where(valid, imp, -FORCE_SCORE))
    rank = jnp.zeros((nsb, tq), jnp.int32)
    for jp in range(nsb):
        other = imp[jp:jp + 1, :]
        rank = rank + ((other > imp) | ((other == imp) & (jb > jp))).astype(jnp.int32)
    sel = (rank < min(SEL_TOPK, nsb)).astype(F32).T
    sel3 = jnp.concatenate([sel] * NSA_HPG, axis=0).astype(BF16)
    kt_w = NSA_KT
    blk_row = lax.broadcasted_iota(jnp.int32, (nsb, kt_w), 0)
    blk_lane = lax.broadcasted_iota(jnp.int32, (nsb, kt_w), 1) // SEL_BLOCK
    key_lane = lax.broadcasted_iota(jnp.int32, (rows, kt_w), 1)
    t_pos_k = i * tq + (lax.broadcasted_iota(jnp.int32, (rows, kt_w), 0) & (tq - 1))
    tiles_per_kt = kt_w // tq

    def attend(k_ref, v_ref, lo, use_sel):
        hi = i // tiles_per_kt + 1

        def scores(kt, m_run):
            ksl = pl.ds(pl.multiple_of(kt * kt_w, kt_w), kt_w)
            band = band_ref[0, jnp.clip(i - kt * tiles_per_kt, 0, NSA_BANDS - 1)].reshape(rows, kt_w)
            s = _dot_nt(q3, k_ref[0, ksl, :].astype(BF16)) * scale + band
            dist = t_pos_k - (kt * kt_w + key_lane)
            if use_sel:
                expand = (blk_row == kt * (kt_w // SEL_BLOCK) + blk_lane).astype(BF16)
                mask = (_dot(sel3, expand) > 0.5) & (dist >= 0)
            else:
                mask = (dist >= 0) & (dist < WINDOW)
            s = jnp.where(mask, s, MASK_NEG)
            s_sc[kt] = s
            return jnp.maximum(m_run, jnp.max(s, axis=-1, keepdims=True))

        m_fin = lax.fori_loop(lo, hi, scores, jnp.full((rows, 1), MASK_NEG, F32))

        def accum(kt, carry):
            l_run, acc = carry
            ksl = pl.ds(pl.multiple_of(kt * kt_w, kt_w), kt_w)
            p = jnp.exp(s_sc[kt] - m_fin)
            return (l_run + jnp.sum(p, axis=-1, keepdims=True), acc + _dot(p, v_ref[0, ksl, :]))

        l_fin, acc = lax.fori_loop(lo, hi, accum, (jnp.zeros((rows, 1), F32),
                                                   jnp.zeros((rows, LANE), F32)))
        return jnp.where(m_fin > MASK_NEG, acc / jnp.where(l_fin > 0, l_fin, 1.0), 0.0)

    o_s = attend(ks_ref, vs_ref, 0, True)
    o_w = attend(kw_ref, vw_ref, jnp.maximum(i - WINDOW // tq, 0) // tiles_per_kt, False)

    gt = _sigmoid(gt_ref[0, 0])
    gcol = lambda br: jnp.concatenate(
        [gt[:, br * NSA_HPG + hh:br * NSA_HPG + hh + 1] for hh in range(NSA_HPG)], axis=0)
    o = gcol(0) * o_c + gcol(1) * o_s + gcol(2) * o_w
    for hh in range(NSA_HPG):
        o_ref[0, :, hh * LANE:(hh + 1) * LANE] = o[hh * tq:(hh + 1) * tq].astype(o_ref.dtype)


def nsa_prompt(proj, gates_t, kc, vc, bias_c, band, side, sel_map, nb, nsb):
    b, t, _ = proj.shape
    tq = NSA_TQ
    qw = NSA_HPG * LANE
    kv = lambda off: pl.BlockSpec((1, t, LANE), lambda bi, g, i: (bi, 0, off * NSA_KV_GROUPS + g))
    return pl.pallas_call(
        functools.partial(_nsa_prompt_kernel, nb=nb, nsb=nsb),
        grid=(b, NSA_KV_GROUPS, t // tq),
        in_specs=[pl.BlockSpec((1, tq, qw), lambda bi, g, i: (bi, i, g)),
                  pl.BlockSpec((1, 1, tq, NSA_HPG * 3), lambda bi, g, i: (bi, g, i, 0)),
                  pl.BlockSpec((1, LANE, LANE), lambda bi, g, i: (bi, 0, g)),
                  pl.BlockSpec((1, LANE, LANE), lambda bi, g, i: (bi, 0, g)),
                  pl.BlockSpec((1, NSA_HPG, tq, LANE), lambda bi, g, i: (g, 0, i, 0)),
                  pl.BlockSpec((1, NSA_BANDS, NSA_HPG, tq, NSA_KT), lambda bi, g, i: (g, 0, 0, 0, 0)),
                  kv(2), kv(3), kv(4), kv(5),
                  pl.BlockSpec(sel_map.shape, lambda bi, g, i: (0, 0))],
        out_specs=pl.BlockSpec((1, tq, qw), lambda bi, g, i: (bi, i, g)),
        out_shape=jax.ShapeDtypeStruct((b, t, MAIN_W), BF16),
        scratch_shapes=[pltpu.VMEM((t // NSA_KT, NSA_HPG * tq, NSA_KT), F32)],
        compiler_params=_params("arbitrary", "arbitrary", "arbitrary"),
        name="nsa_prompt",
    )(proj, gates_t, kc, vc, bias_c, band, side, side, side, side, sel_map)


def _nsa_dec_cmp_kernel(q_ref, kc_ref, vc_ref, bc_ref, m_ref, oc_ref, idx_ref, *, nb, nsb, t_pos):
    scale = NSA_HEAD_DIM ** -0.5
    nbp = kc_ref.shape[1]
    nsp = m_ref.shape[1]
    n_sel = min(SEL_TOPK, nsb)
    lane_b = lax.broadcasted_iota(jnp.int32, (SUBLANE, nbp), 1)
    m_c = (t_pos - (lane_b * CMP_STRIDE + (CMP_BLOCK - 1)) >= 0) & (lane_b < nb)
    row_s = lax.broadcasted_iota(jnp.int32, (SUBLANE, nsp), 0)
    jb = lax.broadcasted_iota(jnp.int32, (1, nsp), 1)
    cur = t_pos // SEL_BLOCK
    valid = jb <= cur
    forced = valid & ((jb == 0) | (jb > cur - SEL_LOCAL))
    out_lane = lax.broadcasted_iota(jnp.int32, (1, LANE), 1)
    idx_rows = []
    for g in range(NSA_KV_GROUPS):
        q3 = jnp.concatenate(
            [q_ref[0, 0:1, (g * NSA_HPG + hh) * LANE:(g * NSA_HPG + hh + 1) * LANE]
             for hh in range(NSA_HPG)] + [jnp.zeros((SUBLANE - NSA_HPG, LANE), F32)], axis=0)
        sl = slice(g * LANE, (g + 1) * LANE)
        s_c = _dot_nt(q3, kc_ref[0, :, sl]) * scale + bc_ref[g]
        p_c = _masked_softmax(s_c, m_c)
        oc_ref[0, g] = _dot(p_c, vc_ref[0, :, sl])
        imp8 = jnp.where(row_s < NSA_HPG, _dot(p_c, m_ref[...], HI), 0.0)
        imp = jnp.sum(imp8, axis=0, keepdims=True)
        imp = jnp.where(forced, FORCE_SCORE, jnp.where(valid, imp, -FORCE_SCORE))
        imp = jnp.where(jb < nsb, imp, -jnp.inf)
        jbf = jb.astype(F32)
        picks = jnp.zeros((1, LANE), F32)
        for kk in range(n_sel):
            best = jnp.max(imp, axis=-1, keepdims=True)
            arg = jnp.min(jnp.where(imp == best, jbf, float(nsp)), axis=-1, keepdims=True)
            picks = jnp.where(out_lane == kk, arg, picks)
            imp = jnp.where(jbf == arg, -jnp.inf, imp)
        idx_rows.append(picks.astype(jnp.int32))
    idx_rows.append(jnp.zeros((SUBLANE - NSA_KV_GROUPS, LANE), jnp.int32))
    idx_ref[0] = jnp.concatenate(idx_rows, axis=0)


def nsa_dec_cmp(proj, kc, vc, bias_c, sel_map, nb, nsb, t_pos):
    b, tp, _ = proj.shape
    nbp = kc.shape[1]
    return pl.pallas_call(
        functools.partial(_nsa_dec_cmp_kernel, nb=nb, nsb=nsb, t_pos=t_pos),
        grid=(b,),
        in_specs=[pl.BlockSpec((1, tp, MAIN_W), lambda bi: (bi, 0, 0)),
                  pl.BlockSpec((1, nbp, KV_LANES), lambda bi: (bi, 0, 0)),
                  pl.BlockSpec((1, nbp, KV_LANES), lambda bi: (bi, 0, 0)),
                  pl.BlockSpec(bias_c.shape, lambda bi: (0, 0, 0)),
                  pl.BlockSpec(sel_map.shape, lambda bi: (0, 0))],
        out_specs=[pl.BlockSpec((1, NSA_KV_GROUPS, SUBLANE, LANE), lambda bi: (bi, 0, 0, 0)),
                   pl.BlockSpec((1, SUBLANE, LANE), lambda bi: (bi, 0, 0))],
        out_shape=[jax.ShapeDtypeStruct((b, NSA_KV_GROUPS, SUBLANE, LANE), F32),
                   jax.ShapeDtypeStruct((b, SUBLANE, LANE), jnp.int32)],
        compiler_params=_params("arbitrary"),
        name="nsa_dec_cmp",
    )(proj, kc, vc, bias_c, sel_map)


def _nsa_dec_sel_kernel(phys_ref, isnew_ref, *refs, n_sel, n_win, t_pos, win_start):
    del phys_ref
    kb_refs = refs[:n_sel]
    vb_refs = refs[n_sel:2 * n_sel]
    (q_ref, new_ref, ds_ref, bs_ref, wk_ref, wv_ref, bw_ref, oc_ref, gt_ref, o_ref) = refs[2 * n_sel:]
    bi = pl.program_id(0)
    g = pl.program_id(1)
    scale = NSA_HEAD_DIM ** -0.5
    q3 = q_ref[0, 0]
    new_rows = new_ref[0, 0]
    pad_blk = jnp.zeros((SEL_BLOCK - 1, LANE), F32)
    new_k = jnp.concatenate([new_rows[0:1], pad_blk], axis=0)
    new_v = jnp.concatenate([new_rows[1:2], pad_blk], axis=0)
    ks, vs = [], []
    for kk in range(n_sel):
        fresh = isnew_ref[bi, g, kk] > 0
        ks.append(jnp.where(fresh, new_k, kb_refs[kk][0, :, g, :]))
        vs.append(jnp.where(fresh, new_v, vb_refs[kk][0, :, g, :]))
    ks = jnp.concatenate(ks, axis=0)
    vs = jnp.concatenate(vs, axis=0)
    s_s = _dot_nt(q3, ks) * scale + bs_ref[0, 0]
    p_s = _masked_softmax(s_s, ds_ref[0, 0] >= 0)
    o_s = _dot(p_s, vs)

    pad_w = jnp.zeros((SUBLANE - 1, LANE), F32)
    kw = jnp.concatenate([wk_ref[0, :, g, :], new_rows[2:3], pad_w], axis=0)
    vw = jnp.concatenate([wv_ref[0, :, g, :], new_rows[3:4], pad_w], axis=0)
    nw = kw.shape[0]
    pos = win_start + lax.broadcasted_iota(jnp.int32, (SUBLANE, nw), 1)
    d_w = t_pos - pos
    m_w = (d_w >= 0) & (d_w < WINDOW) & (pos >= 0) & (pos - win_start < n_win)
    s_w = _dot_nt(q3, kw) * scale + bw_ref[0]
    p_w = _masked_softmax(s_w, m_w)
    o_w = _dot(p_w, vw)

    gt = _sigmoid(gt_ref[0, 0])
    o_ref[0, 0] = gt[:, 0:1] * oc_ref[0, 0] + gt[:, 1:2] * o_s + gt[:, 2:3] * o_w


def nsa_dec_sel(phys, isnew, pool_k, pool_v, q8, new_rows, d_s, bias_s, win_k, win_v, bias_w,
                o_c, gates8, n_win, t_pos, win_start):
    b = q8.shape[0]
    n_sel = phys.shape[-1]
    n_keys = n_sel * SEL_BLOCK
    nw = win_k.shape[1]
    nwp = nw + SUBLANE

    def blk_spec(kk):
        return pl.BlockSpec((1, SEL_BLOCK, NSA_KV_GROUPS, LANE),
                            lambda bi, g, ph, nf: (ph[bi, g, kk], 0, 0, 0))

    per_bg = lambda *shape: pl.BlockSpec((1, 1) + shape, lambda bi, g, ph, nf: (bi, g) + (0,) * len(shape))
    grid_spec = pltpu.PrefetchScalarGridSpec(
        num_scalar_prefetch=2,
        grid=(b, NSA_KV_GROUPS),
        in_specs=[blk_spec(kk) for kk in range(n_sel)] + [blk_spec(kk) for kk in range(n_sel)]
        + [per_bg(SUBLANE, LANE), per_bg(SUBLANE, LANE), per_bg(1, n_keys), per_bg(SUBLANE, n_keys),
           pl.BlockSpec((1, nw, NSA_KV_GROUPS, LANE), lambda bi, g, ph, nf: (bi, 0, 0, 0)),
           pl.BlockSpec((1, nw, NSA_KV_GROUPS, LANE), lambda bi, g, ph, nf: (bi, 0, 0, 0)),
           pl.BlockSpec((1, SUBLANE, nwp), lambda bi, g, ph, nf: (g, 0, 0)),
           per_bg(SUBLANE, LANE), per_bg(SUBLANE, SUBLANE)],
        out_specs=per_bg(SUBLANE, LANE),
    )
    return pl.pallas_call(
        functools.partial(_nsa_dec_sel_kernel, n_sel=n_sel, n_win=nw + 1, t_pos=t_pos,
                          win_start=win_start),
        grid_spec=grid_spec,
        out_shape=jax.ShapeDtypeStruct((b, NSA_KV_GROUPS, SUBLANE, LANE), F32),
        compiler_params=_params("arbitrary", "arbitrary"),
        name="nsa_dec_sel",
    )(phys, isnew, *([pool_k] * n_sel), *([pool_v] * n_sel), q8, new_rows, d_s, bias_s,
      win_k, win_v, bias_w, o_c, gates8)


def _pad_cols(w, n):
    return jnp.pad(w, ((0, 0), (0, n - w.shape[1])))


def _pad_rows(w, n):
    return jnp.pad(w, ((0, n - w.shape[0]), (0, 0)))


def _prep_weights(P):
    W = {}
    W['w_gu'] = P['ffn_gu'].astype(BF16)
    W['w_d'] = P['ffn_d'].astype(BF16)
    W['w_out'] = P['w_out'].astype(BF16)[None]
    W['rw_in'], W['rw_w2'], W['rw_a2'], W['rw_g2'], W['rw_v2'], W['rw_vec'] = [], [], [], [], [], []
    lp = LOWRANK_PAD
    for l in range(N_A):
        w_in = P['w_in_a'][l]
        v1 = P['rw_v1'][l - 1] if l > 0 else jnp.zeros((D_MODEL, lp), F32)
        W['rw_in'].append(jnp.concatenate([
            w_in, _pad_cols(P['rw_w1'][l], lp), _pad_cols(v1, lp), _pad_cols(P['rw_a1'][l], lp),
            _pad_cols(P['rw_g1'][l], lp)], axis=1).astype(BF16))
        W['rw_w2'].append(_pad_rows(P['rw_w2'][l], lp))
        W['rw_a2'].append(_pad_rows(P['rw_a2'][l], lp))
        W['rw_g2'].append(_pad_rows(P['rw_g2'][l], lp))
        W['rw_v2'].append(_pad_rows(P['rw_v2'][l - 1], lp) if l > 0 else None)
        v0 = P['rw_v0'][l - 1] if l > 0 else jnp.zeros((MAIN_W,), F32)
        W['rw_vec'].append(jnp.stack([P['rw_w0'][l], P['rw_a0'][l], v0, P['rw_kk'][l], P['rw_ka'][l],
                                      P['rw_lnw'][l], P['rw_lnb'][l], P['rw_rk'][l].reshape(-1)]))
    W['w_in_b'] = []
    for l in range(DEPTH - N_A):
        w = P['w_in_b'][l]
        W['w_in_b'].append(jnp.concatenate([
            w[:, :MAIN_W], w[:, MAIN_W + GATE_W:], _pad_cols(w[:, MAIN_W:MAIN_W + GATE_W], MXU_DIM)],
            axis=1).astype(BF16))
    W['w_kv'] = P['w_kv'].astype(BF16)
    W['w_mem_kv'] = [P['w_mem_kv'][l].astype(BF16) for l in range(DEPTH)]
    W['cmp_w1_flat'] = [_w1_flat(P['cmp_w1'][c]) for c in range(2)]
    return W


B_QM_BLOCK = MAIN_W // MEM_W
B_GATE_OFF = MAIN_W + MEM_W


def _ffn(h2, norms_l, first, W, l, i):
    hid = norm_swiglu(h2, norms_l[first], W['w_gu'], l, i, tm=1024)
    return matmul_norm_res(hid, W['w_d'], l, i, norms_l[first + 1], h2, 0.5, tk=FFN_DOWN_TK)


def _trunk(x, t_real, P, W, shift0, wkv0_bd, mem_src, make_side, attend, chunk, tb):
    b, t, d = x.shape
    h2 = x.reshape(b * t, d)
    shifts, states = [], []
    v_first_proj, ctx, side_state = None, None, None
    for l in range(DEPTH):
        n = P['norms'][l]
        h2 = _ffn(h2, n, 0, W, l, 0)
        if l < N_A:
            proj, last = rwkv_in_proj(h2.reshape(b, t, d), shift0[l], n[2], P['rw_mu'][l],
                                      W['rw_in'][l], t_real)
            main, s_bd = rwkv_scan(proj, v_first_proj if l > 0 else None, W['rw_w2'][l], W['rw_a2'][l],
                                   W['rw_g2'][l], W['rw_v2'][l], W['rw_vec'][l], wkv0_bd[l],
                                   t_real, chunk, tb)
            if l == 0:
                v_first_proj = proj
            shifts.append(last)
            states.append(_from_block_diag(s_bd))
            q_src, q_block = proj, RW_QM * MXU_DIM // MEM_W
        else:
            proj = norm_matmul(h2, n[2], W['w_in_b'][l - N_A], tn=768).reshape(b, t, -1)
            main = attend(proj, ctx)
            q_src, q_block = proj, B_QM_BLOCK
        mk, kb, mv, vb = mem_src(l)
        mo = mem_attend(q_src, q_block, mk, kb, mv, vb)
        mix = jnp.concatenate([main, mo], axis=-1).reshape(b * t, d)
        h2 = matmul_norm_res(mix, W['w_out'], 0, l, n[3], h2, 1.0, tk=1024)
        h2 = _ffn(h2, n, 4, W, l, 1)
        if l == N_A - 1:
            side = norm_matmul(h2, P['kv_norm'], W['w_kv'], tn=768).reshape(b, t, -1)
            ctx, side_state = make_side(side)
    return h2.reshape(b, t, d), jnp.stack(shifts), jnp.stack(states), side_state


def kernel(x_prompt, x_sample, mem_prompt, state_wkv, state_shift, cache_mem_k, cache_mem_v,
           cache_cmp_k, cache_cmp_v, cache_slc_k, cache_slc_v, cache_win_k, cache_win_v, page_table,
           norms, ffn_gu, ffn_d, w_in_a, w_in_b, w_out, mem_norm, w_mem_kv, kv_norm, w_kv,
           cmp_pe, cmp_w1, cmp_b1, cmp_w2, rel_bias,
           rw_mu, rw_w0, rw_w1, rw_w2, rw_a0, rw_a1, rw_a2, rw_g1, rw_g2, rw_v0, rw_v1, rw_v2,
           rw_kk, rw_ka, rw_rk, rw_lnw, rw_lnb):
    P = dict(norms=norms, ffn_gu=ffn_gu, ffn_d=ffn_d, w_in_a=w_in_a, w_in_b=w_in_b, w_out=w_out,
             kv_norm=kv_norm, w_kv=w_kv, w_mem_kv=w_mem_kv, cmp_w1=cmp_w1, rw_mu=rw_mu, rw_w0=rw_w0,
             rw_w1=rw_w1, rw_w2=rw_w2, rw_a0=rw_a0, rw_a1=rw_a1, rw_a2=rw_a2, rw_g1=rw_g1,
             rw_g2=rw_g2, rw_v0=rw_v0, rw_v1=rw_v1, rw_v2=rw_v2, rw_kk=rw_kk, rw_ka=rw_ka,
             rw_rk=rw_rk, rw_lnw=rw_lnw, rw_lnb=rw_lnb)
    W = _prep_weights(P)
    G, dh = NSA_KV_GROUPS, NSA_HEAD_DIM
    split_side = lambda side, bx, t: [side[:, :t, c * KV_LANES:(c + 1) * KV_LANES].reshape(bx, t, G, dh)
                                      for c in range(6)]

    def cmp_mlp(parts, nb, c):
        return compress(parts, nb, cmp_pe[c], cmp_w1[c], cmp_b1[c], cmp_w2[c])

    bp, tp, d = x_prompt.shape
    n_mem = mem_prompt.shape[1]
    mem2 = mem_prompt.reshape(bp * n_mem, d)
    p_mkv = [norm_matmul(mem2, mem_norm[l], W['w_mem_kv'][l], tn=512).reshape(bp, n_mem, 2 * MEM_W)
             for l in range(DEPTH)]
    p_mem_k = jnp.stack([m[..., :MEM_W].reshape(bp, n_mem, MEM_HEADS, MEM_HEAD_DIM) for m in p_mkv])
    p_mem_v = jnp.stack([m[..., MEM_W:].reshape(bp, n_mem, MEM_HEADS, MEM_HEAD_DIM) for m in p_mkv])

    nb_p = tp // CMP_STRIDE - 1
    nsb_p = tp // SEL_BLOCK
    nq_tiles = tp // NSA_TQ
    t_all = jnp.arange(tp)
    c_end = jnp.arange(LANE) * CMP_STRIDE + (CMP_BLOCK - 1)
    assert nb_p <= LANE
    bias_c_p = _bias_lookup(rel_bias, t_all[:, None] - c_end[None, :])
    ii = jnp.arange(NSA_TQ)
    cc = jnp.arange(NSA_KT)
    band_p = jnp.stack([_bias_lookup(rel_bias, dd * NSA_TQ + ii[:, None] - cc[None, :])
                        for dd in range(NSA_BANDS)], axis=1)
    assert (NSA_BANDS - 1) * NSA_TQ - (NSA_KT - 1) >= REL_MAX_DIST and tp % NSA_KT == 0
    sel_map_p = _cmp_to_sel(LANE, nb_p, nsb_p, nsb_p).T
    ident = jnp.arange(bp * tp // CMP_PAGE, dtype=jnp.int32).reshape(bp, tp // CMP_PAGE)

    def prompt_side(side):
        rows = side.reshape(bp * tp, -1)
        kc = cmp_mlp(chunk_proj(rows, ident, W['cmp_w1_flat'][0], tp // CMP_PAGE, 0), nb_p, 0)
        vc = cmp_mlp(chunk_proj(rows, ident, W['cmp_w1_flat'][1], tp // CMP_PAGE, 1), nb_p, 1)
        padb = ((0, 0), (0, LANE - nb_p), (0, 0))
        wb = min(WINDOW, tp)
        kc_r, vc_r, ks, vs, kw, vw = split_side(side, bp, tp)
        return ((jnp.pad(kc, padb), jnp.pad(vc, padb), side),
                (kc_r, vc_r, ks, vs, kw[:, tp - wb:], vw[:, tp - wb:]))

    def prompt_attend(proj, ctx):
        kc, vc, side = ctx
        gates = proj[..., B_GATE_OFF:B_GATE_OFF + GATE_W].reshape(bp, tp, 3, G, NSA_HPG)
        gates_t = jnp.transpose(gates, (0, 3, 1, 2, 4)).reshape(bp, G, tp, 3 * NSA_HPG)
        return nsa_prompt(proj, gates_t, kc, vc, bias_c_p, band_p, side, sel_map_p, nb_p, nsb_p)

    zeros_shift = jnp.zeros((N_A, bp, d), F32)
    zeros_state = jnp.zeros((N_A, bp, N_HEAD_BLOCKS, MXU_DIM, MXU_DIM), F32)
    y_prompt, p_shift, p_wkv, p_side = _trunk(
        x_prompt, tp, P, W, zeros_shift, zeros_state,
        lambda l: (p_mkv[l], 0, p_mkv[l], 1), prompt_side, prompt_attend, chunk=64, tb=256)
    p_cmp_k, p_cmp_v, p_slc_k, p_slc_v, p_win_k, p_win_v = p_side

    bd, s_new, _ = x_sample.shape
    assert s_new == 1
    ts = SUBLANE
    xs = jnp.pad(x_sample, ((0, 0), (0, ts - s_new), (0, 0)))
    past_len = page_table.shape[1] * CMP_PAGE
    n_past_blk = past_len // SEL_BLOCK
    blk_per_page = CMP_PAGE // SEL_BLOCK
    nsb_s = n_past_blk + 1
    t_pos = past_len
    nc_s = -(-(past_len + s_new) // CMP_STRIDE)
    nb_s = nc_s - 1
    wb_s = cache_win_k.shape[1]
    win_start = past_len - wb_s
    mem_k2 = cache_mem_k.reshape(DEPTH, bd, n_mem, MEM_W)
    mem_v2 = cache_mem_v.reshape(DEPTH, bd, n_mem, MEM_W)
    nsp = -(-nsb_s // LANE) * LANE
    sel_map_s = _cmp_to_sel(nb_s, nb_s, nsp, nsb_s)
    c_end_s = jnp.arange(nb_s) * CMP_STRIDE + (CMP_BLOCK - 1)
    bias_c_s = _bias_lookup(rel_bias, t_pos - c_end_s)
    bias_c_s = jnp.pad(bias_c_s, ((0, 0), (0, SUBLANE - NSA_HPG), (0, 0)))
    nwp = wb_s + SUBLANE
    bias_w_s = _bias_lookup(rel_bias, t_pos - (win_start + jnp.arange(nwp)))
    bias_w_s = jnp.pad(bias_w_s, ((0, 0), (0, SUBLANE - NSA_HPG), (0, 0)))
    ident_s = jnp.arange(bd, dtype=jnp.int32).reshape(bd, 1)
    pages_per_step = math.gcd(16, page_table.shape[1])

    def sample_side(side):
        new = side[:, :s_new]
        parts = []
        for c, pool in enumerate((cache_cmp_k, cache_cmp_v)):
            past = chunk_proj_pool(pool, page_table, W['cmp_w1_flat'][c], pages_per_step)
            fresh_page = jnp.pad(new[:, :, c * KV_LANES:(c + 1) * KV_LANES],
                                 ((0, 0), (0, CMP_PAGE - s_new), (0, 0))).reshape(bd, CMP_PAGE, G, dh)
            fresh = chunk_proj_pool(fresh_page, ident_s, W['cmp_w1_flat'][c], 1)
            n_fresh = nc_s - past[0].shape[1]
            parts.append([jnp.concatenate([p, f[:, :n_fresh]], axis=1) for p, f in zip(past, fresh)])
        kc = cmp_mlp(parts[0], nb_s, 0)
        vc = cmp_mlp(parts[1], nb_s, 1)
        kc_n, vc_n, ks_n, vs_n, kw_n, vw_n = split_side(side, bd, s_new)
        s_win_k = jnp.concatenate([cache_win_k, kw_n], axis=1)[:, s_new:]
        s_win_v = jnp.concatenate([cache_win_v, vw_n], axis=1)[:, s_new:]
        return (kc, vc, new), (kc_n, vc_n, ks_n, vs_n, s_win_k, s_win_v)

    def sample_attend(proj, ctx):
        kc, vc, new = ctx
        o_c, idx8 = nsa_dec_cmp(proj, kc, vc, bias_c_s, sel_map_s, nb_s, nsb_s, t_pos)
        idx = idx8[:, :G, :SEL_TOPK]
        is_new = idx >= n_past_blk
        jp = jnp.minimum(idx, n_past_blk - 1)
        phys = (jnp.take_along_axis(page_table[:, None, :], jp // blk_per_page, axis=2) * blk_per_page
                + jp % blk_per_page)
        k_pos = (idx[..., None] * SEL_BLOCK + jnp.arange(SEL_BLOCK)).reshape(bd, G, 1, -1)
        d_s = t_pos - k_pos
        tab = rel_bias.astype(F32).reshape(REL_BUCKETS, G, NSA_HPG)
        onehot = jax.nn.one_hot(_rel_bucket(d_s[:, :, 0]), REL_BUCKETS, dtype=F32)
        bias_s = jnp.einsum('bgkn,ngh->bghk', onehot, tab, precision=HI)
        bias_s = jnp.pad(bias_s, ((0, 0), (0, 0), (0, SUBLANE - NSA_HPG), (0, 0)))
        q8 = jnp.pad(proj[:, 0, :MAIN_W].reshape(bd, G, NSA_HPG, dh),
                     ((0, 0), (0, 0), (0, SUBLANE - NSA_HPG), (0, 0)))
        new_rows = jnp.pad(jnp.transpose(new[:, 0, 2 * KV_LANES:].reshape(bd, 4, G, dh), (0, 2, 1, 3)),
                           ((0, 0), (0, 0), (0, SUBLANE - 4), (0, 0)))
        gates = proj[:, 0, B_GATE_OFF:B_GATE_OFF + GATE_W].reshape(bd, 3, G, NSA_HPG)
        gates8 = jnp.pad(jnp.transpose(gates, (0, 2, 3, 1)),
                         ((0, 0), (0, 0), (0, SUBLANE - NSA_HPG), (0, SUBLANE - 3)))
        o = nsa_dec_sel(phys.astype(jnp.int32), is_new.astype(jnp.int32),
                        cache_slc_k.reshape(-1, SEL_BLOCK, G, dh),
                        cache_slc_v.reshape(-1, SEL_BLOCK, G, dh),
                        q8, new_rows, d_s.astype(jnp.int32), bias_s,
                        cache_win_k, cache_win_v,
                        bias_w_s, o_c, gates8, wb_s, t_pos, win_start)
        main = o[:, :, :NSA_HPG].reshape(bd, 1, MAIN_W)
        return jnp.pad(main, ((0, 0), (0, ts - 1), (0, 0))).astype(BF16)

    y_s, s_shift, s_wkv, s_side = _trunk(
        xs, s_new, P, W, state_shift, jnp.stack([_to_block_diag(state_wkv[l]) for l in range(N_A)]),
        lambda l: (mem_k2[l], 0, mem_v2[l], 0), sample_side, sample_attend, chunk=SUBLANE, tb=SUBLANE)
    y_sample = y_s[:, :s_new]
    s_cmp_k, s_cmp_v, s_slc_k, s_slc_v, s_win_k, s_win_v = s_side

    return (y_prompt, y_sample, p_mem_k, p_mem_v, p_wkv, p_shift,
            p_cmp_k, p_cmp_v, p_slc_k, p_slc_v, p_win_k, p_win_v,
            s_wkv, s_shift, s_cmp_k, s_cmp_v, s_slc_k, s_slc_v, s_win_k, s_win_v)
```

```python
import functools
import math

import jax
import jax.numpy as jnp
from jax import lax
from jax.experimental import pallas as pl
from jax.experimental.pallas import tpu as pltpu

F32 = jnp.float32
BF16 = jnp.bfloat16
HI = lax.Precision.HIGHEST

D_MODEL = 2048
DEPTH = 4
N_A = 2
MEM_HEADS = 4
MEM_HEAD_DIM = 128
MEM_W = 512
MAIN_W = 1536
RWKV_HEAD = 64
RWKV_HEADS = 24
GN_EPS = 64e-5
NSA_HEAD_DIM = 128
NSA_Q_HEADS = 12
NSA_KV_GROUPS = 4
NSA_HPG = 3
GATE_W = 36
CMP_BLOCK = 32
CMP_STRIDE = 16
SEL_BLOCK = 64
SEL_TOPK = 16
SEL_LOCAL = 2
WINDOW = 512
REL_BUCKETS = 32
REL_MAX_DIST = 128
D_FF = 5504
NORM_EPS = 1e-6
MASK_NEG = -1e30
FORCE_SCORE = 1e9

LANE = 128
SUBLANE = 8
MXU_DIM = 256
VMEM_LIMIT = 48 * 1024 * 1024

FFN_DOWN_TK = 11 * LANE
HEAD_BLOCK = MXU_DIM // RWKV_HEAD
N_HEAD_BLOCKS = RWKV_HEADS // HEAD_BLOCK
LOWRANK_PAD = MXU_DIM
NSA_TQ = 256
NSA_KT = 256
NSA_BANDS = 3


def _params(*sem):
    return pltpu.CompilerParams(dimension_semantics=sem, vmem_limit_bytes=VMEM_LIMIT)


def _rms(x, g):
    return x * lax.rsqrt(jnp.mean(x * x, axis=-1, keepdims=True) + NORM_EPS) * g


def _sigmoid(x):
    return 1.0 / (1.0 + jnp.exp(-x))


def _softplus(x):
    return jnp.maximum(x, 0.0) + jnp.log(1.0 + jnp.exp(-jnp.abs(x)))


def _gelu_tanh(x):
    return 0.5 * x * (1.0 + jnp.tanh(math.sqrt(2.0 / math.pi) * (x + 0.044715 * x * x * x)))


def _dot(a, b, precision=None):
    return jnp.dot(a, b, preferred_element_type=F32, precision=precision)


def _dot_nt(a, b, precision=None):
    return lax.dot_general(a, b, (((1,), (1,)), ((), ())), preferred_element_type=F32,
                           precision=precision)


def _dot_tn(a, b, precision=None):
    return lax.dot_general(a, b, (((0,), (0,)), ((), ())), preferred_element_type=F32,
                           precision=precision)


def _masked_softmax(s, mask):
    s = jnp.where(mask, s, MASK_NEG)
    p = jnp.exp(s - jnp.max(s, axis=-1, keepdims=True)) * mask.astype(F32)
    den = jnp.sum(p, axis=-1, keepdims=True)
    return p / jnp.where(den > 0, den, 1.0)


def _row_tile(rows, target):
    t = min(rows, target)
    while rows % t:
        t -= SUBLANE
    return t


def _norm_matmul_kernel(x_ref, g_ref, w_ref, o_ref, xn_ref):
    @pl.when(pl.program_id(1) == 0)
    def _():
        xn_ref[...] = _rms(x_ref[...], g_ref[...]).astype(BF16)

    o_ref[...] = _dot(xn_ref[...], w_ref[...]).astype(o_ref.dtype)


def norm_matmul(x, g, w, tn, out_dtype=F32, tm=512):
    rows, d = x.shape
    n = w.shape[1]
    tm = _row_tile(rows, tm)
    assert n % tn == 0
    return pl.pallas_call(
        _norm_matmul_kernel,
        grid=(rows // tm, n // tn),
        in_specs=[pl.BlockSpec((tm, d), lambda i, j: (i, 0)),
                  pl.BlockSpec((1, d), lambda i, j: (0, 0)),
                  pl.BlockSpec((d, tn), lambda i, j: (0, j))],
        out_specs=pl.BlockSpec((tm, tn), lambda i, j: (i, j)),
        out_shape=jax.ShapeDtypeStruct((rows, n), out_dtype),
        scratch_shapes=[pltpu.VMEM((tm, d), BF16)],
        compiler_params=_params("arbitrary", "arbitrary"),
        name="norm_matmul",
    )(x, g.reshape(1, d), w)


def _norm_swiglu_kernel(x_ref, g_ref, wg_ref, wu_ref, o_ref, xn_ref, *, tail):
    j = pl.program_id(1)
    last = pl.num_programs(1) - 1

    @pl.when(j == 0)
    def _():
        xn_ref[...] = _rms(x_ref[...], g_ref[...]).astype(BF16)

    xn = xn_ref[...]
    tn = o_ref.shape[1]
    sub = min(tn, MXU_DIM)

    def tile(up_shift):
        c0 = 0
        while c0 < tn - up_shift:
            w = min(sub, tn - up_shift - c0)
            gate = _dot(xn, wg_ref[0, 0, :, c0:c0 + w])
            up = _dot(xn, wu_ref[0, 0, :, c0 + up_shift:c0 + up_shift + w])
            o_ref[:, c0:c0 + w] = (gate * _sigmoid(gate) * up).astype(o_ref.dtype)
            c0 += w
        if up_shift:
            o_ref[:, c0:] = jnp.zeros((o_ref.shape[0], up_shift), o_ref.dtype)

    if tail == tn:
        tile(0)
    else:
        pl.when(j != last)(lambda: tile(0))
        pl.when(j == last)(lambda: tile(tn - tail))


def norm_swiglu(x, g, w_gu, l, i, tn=512, tm=512):
    rows, d = x.shape
    f = w_gu.shape[-1] // 2
    assert f % LANE == 0 and tn % LANE == 0
    tm = _row_tile(rows, tm)
    n_tiles = -(-f // tn)
    tail = f - (n_tiles - 1) * tn
    el = pl.Element

    def up_col(j):
        return pl.multiple_of(jnp.minimum(f + j * tn, 2 * f - tn), LANE)

    return pl.pallas_call(
        functools.partial(_norm_swiglu_kernel, tail=tail),
        grid=(rows // tm, n_tiles),
        in_specs=[pl.BlockSpec((tm, d), lambda r, j: (r, 0)),
                  pl.BlockSpec((1, d), lambda r, j: (0, 0)),
                  pl.BlockSpec((el(1), el(1), el(d), el(tn)), lambda r, j: (l, i, 0, j * tn)),
                  pl.BlockSpec((el(1), el(1), el(d), el(tn)), lambda r, j: (l, i, 0, up_col(j)))],
        out_specs=pl.BlockSpec((tm, tn), lambda r, j: (r, j)),
        out_shape=jax.ShapeDtypeStruct((rows, n_tiles * tn), BF16),
        scratch_shapes=[pltpu.VMEM((tm, d), BF16)],
        compiler_params=_params("arbitrary", "arbitrary"),
        name="norm_swiglu",
    )(x, g.reshape(1, d), w_gu, w_gu)


def _matmul_norm_res_kernel(a_ref, w_ref, g_ref, h_ref, o_ref, acc_ref, *, scale, overlap):
    k = pl.program_id(1)
    last = pl.num_programs(1) - 1

    @pl.when(k == 0)
    def _():
        acc_ref[...] = jnp.zeros_like(acc_ref)

    a = a_ref[...]
    if overlap:
        col = lax.broadcasted_iota(jnp.int32, a.shape, 1)
        a = jnp.where(col < jnp.where(k == last, overlap, 0), jnp.zeros_like(a), a)
    n_out = acc_ref.shape[1]
    sub = min(n_out, 2 * MXU_DIM)
    for c0 in range(0, n_out, sub):
        acc_ref[:, c0:c0 + sub] += _dot(a, w_ref[0, 0, :, c0:c0 + sub])

    @pl.when(k == last)
    def _():
        o_ref[...] = h_ref[...] + scale * _rms(acc_ref[...], g_ref[...])


def matmul_norm_res(a, w, l, i, g, h, scale, tk=512, tm=512):
    rows = a.shape[0]
    kdim, d = w.shape[-2:]
    assert kdim % LANE == 0 and tk % LANE == 0 and a.shape[1] >= kdim
    tm = _row_tile(rows, tm)
    n_k = -(-kdim // tk)
    overlap = n_k * tk - kdim
    el = pl.Element

    def k_off(k):
        return pl.multiple_of(jnp.minimum(k * tk, kdim - tk), LANE)

    return pl.pallas_call(
        functools.partial(_matmul_norm_res_kernel, scale=scale, overlap=overlap),
        grid=(rows // tm, n_k),
        in_specs=[pl.BlockSpec((el(tm), el(tk)), lambda r, k: (r * tm, k_off(k))),
                  pl.BlockSpec((el(1), el(1), el(tk), el(d)), lambda r, k: (l, i, k_off(k), 0)),
                  pl.BlockSpec((1, d), lambda r, k: (0, 0)),
                  pl.BlockSpec((tm, d), lambda r, k: (r, 0))],
        out_specs=pl.BlockSpec((tm, d), lambda r, k: (r, 0)),
        out_shape=jax.ShapeDtypeStruct((rows, d), F32),
        scratch_shapes=[pltpu.VMEM((tm, d), F32)],
        compiler_params=_params("arbitrary", "arbitrary"),
        name="matmul_norm_res",
    )(a, w, g.reshape(1, d), h)


RW_R, RW_K, RW_V, RW_QM, RW_W1, RW_V1, RW_A1, RW_G1 = 0, 6, 12, 18, 20, 21, 22, 23
RW_NBLK = 24
RW_MIX_STARTS = ((RW_R, 0), (RW_K, 2), (RW_V, 3), (RW_QM, None), (RW_W1, 1), (RW_V1, 3), (RW_A1, 4),
                 (RW_G1, 5))


def _rwkv_in_kernel(x_ref, sp_ref, g_ref, mu_ref, w_ref, o_ref, last_ref,
                    u_sc, xx_sc, xm_sc, carry_sc, *, tm, t_seq, n_seq, last_tile, last_row):
    ti = pl.program_id(1)
    j = pl.program_id(2)

    @pl.when(j == 0)
    def _():
        @pl.when(ti == 0)
        def _():
            carry_sc[...] = sp_ref[0]

        u = _rms(x_ref[0], g_ref[...])
        rows = lax.broadcasted_iota(jnp.int32, u.shape, 0)
        prev = pltpu.roll(u, 1, axis=0)
        for s in range(n_seq):
            prev = jnp.where(rows == s * t_seq, carry_sc[s:s + 1, :], prev)
        u_sc[...] = u
        xx_sc[...] = prev - u
        if n_seq == 1:
            carry_sc[...] = u[tm - 1:tm, :]

        @pl.when(ti == last_tile)
        def _():
            for s in range(n_seq):
                last_ref[0, s:s + 1, :] = u[s * t_seq + last_row:s * t_seq + last_row + 1, :]

    for start, mix in RW_MIX_STARTS:
        @pl.when(j == start)
        def _(mix=mix):
            if mix is None:
                xm_sc[...] = u_sc[...].astype(BF16)
            else:
                xm_sc[...] = (u_sc[...] + xx_sc[...] * mu_ref[mix:mix + 1, :]).astype(BF16)

    o_ref[0] = _dot(xm_sc[...], w_ref[...])


def rwkv_in_proj(h, shift_prev, g, mu, w_cat, t_real, tm=1024):
    b, t, d = h.shape
    n_seq = math.gcd(b, max(1, tm // t))
    bg, tg = b // n_seq, t * n_seq
    tm = _row_tile(tg, tm)
    assert n_seq == 1 or tm == tg
    tn = MXU_DIM
    last_tile, last_row = ((t_real - 1) // tm, (t_real - 1) % tm) if n_seq == 1 else (0, t_real - 1)
    mu8 = jnp.concatenate([mu, jnp.zeros((SUBLANE - mu.shape[0], d), F32)], axis=0)
    proj, last = pl.pallas_call(
        functools.partial(_rwkv_in_kernel, tm=tm, t_seq=t, n_seq=n_seq, last_tile=last_tile,
                          last_row=last_row),
        grid=(bg, tg // tm, RW_NBLK),
        in_specs=[pl.BlockSpec((1, tm, d), lambda bi, ti, j: (bi, ti, 0)),
                  pl.BlockSpec((1, n_seq, d), lambda bi, ti, j: (bi, 0, 0)),
                  pl.BlockSpec((1, d), lambda bi, ti, j: (0, 0)),
                  pl.BlockSpec((SUBLANE, d), lambda bi, ti, j: (0, 0)),
                  pl.BlockSpec((d, tn), lambda bi, ti, j: (0, j))],
        out_specs=[pl.BlockSpec((1, tm, tn), lambda bi, ti, j: (bi, ti, j)),
                   pl.BlockSpec((1, n_seq, d), lambda bi, ti, j: (bi, 0, 0))],
        out_shape=[jax.ShapeDtypeStruct((bg, tg, RW_NBLK * tn), F32),
                   jax.ShapeDtypeStruct((bg, n_seq, d), F32)],
        scratch_shapes=[pltpu.VMEM((tm, d), F32), pltpu.VMEM((tm, d), F32),
                        pltpu.VMEM((tm, d), BF16), pltpu.VMEM((n_seq, d), F32)],
        compiler_params=_params("arbitrary", "arbitrary", "arbitrary"),
        name="rwkv_in_proj",
    )(h.reshape(bg, tg, d), shift_prev.reshape(bg, n_seq, d), g.reshape(1, d), mu8, w_cat)
    return proj.reshape(b, t, RW_NBLK * tn), last.reshape(b, d)


def _rwkv_scan_kernel(*refs, chunk, n_chunks, has_vres, t_valid, n_hb):
    if has_vres:
        (r_ref, k_ref, v_ref, tw_ref, ta_ref, tg_ref, tv_ref, vf_ref,
         w2_ref, a2_ref, g2_ref, v2_ref, vec_ref, s0_ref, y_ref, sout_ref, s_sc) = refs
    else:
        (r_ref, k_ref, v_ref, tw_ref, ta_ref, tg_ref,
         w2_ref, a2_ref, g2_ref, vec_ref, s0_ref, y_ref, sout_ref, s_sc) = refs
    ti = pl.program_id(2)
    c_len = chunk
    lanes = MXU_DIM
    rows4 = HEAD_BLOCK * c_len

    @pl.when(ti == 0)
    def _():
        s_sc[...] = s0_ref[0]

    li = lax.broadcasted_iota(jnp.int32, (lanes, lanes), 0) // RWKV_HEAD
    lj = lax.broadcasted_iota(jnp.int32, (lanes, lanes), 1) // RWKV_HEAD
    seg = (li == lj).astype(BF16)
    ci = lax.broadcasted_iota(jnp.int32, (c_len, c_len), 0)
    cj = lax.broadcasted_iota(jnp.int32, (c_len, c_len), 1)
    tri_c = (cj <= ci).astype(BF16)

    def split2(x):
        bits = lax.bitcast_convert_type(x, jnp.uint32) & jnp.uint32(0xFFFF0000)
        hi = lax.bitcast_convert_type(bits, F32)
        return hi, x - hi

    def seg_sums(xs):
        parts = [p for x in xs for p in split2(x)]
        res = _dot(jnp.concatenate(parts, axis=0).astype(BF16), seg)
        return [res[(2 * n) * c_len:(2 * n + 1) * c_len] + res[(2 * n + 1) * c_len:(2 * n + 2) * c_len]
                for n in range(len(xs))]

    lane_head = lax.broadcasted_iota(jnp.int32, (c_len, lanes), 1) // RWKV_HEAD
    ri = lax.broadcasted_iota(jnp.int32, (rows4, 2 * rows4), 0)
    rj = lax.broadcasted_iota(jnp.int32, (rows4, 2 * rows4), 1) & (rows4 - 1)
    strict = rj < ri
    incl = rj <= ri
    ei = lax.broadcasted_iota(jnp.int32, (rows4, rows4), 0)
    ej = lax.broadcasted_iota(jnp.int32, (rows4, rows4), 1)
    eye = (ei == ej).astype(F32)
    n_double = int(math.log2(c_len)) - 1

    def stack(x):
        return jnp.concatenate(
            [jnp.where(lane_head == hh, x, 0.0) for hh in range(HEAD_BLOCK)], axis=0)

    def one_chunk(c, carry):
        sl = pl.ds(pl.multiple_of(c * c_len, c_len), c_len)
        tw_act = jnp.tanh(tw_ref[0, sl, :])
        ta_act = ta_ref[0, sl, :]
        tg_act = _sigmoid(tg_ref[0, sl, :])
        tv_act = tv_ref[0, sl, :] if has_vres else None
        if t_valid is not None:
            t_idx = ti * (n_chunks * c_len) + c * c_len + lax.broadcasted_iota(
                jnp.int32, (c_len, lanes), 0)
            live = t_idx < t_valid
        st = [dict() for _ in range(n_hb)]

        def prep(hb):
            e = st[hb]
            hl = slice(hb * lanes, (hb + 1) * lanes)
            w0, a0, v0 = vec_ref[0:1, hl], vec_ref[1:2, hl], vec_ref[2:3, hl]
            kkw, kaw, rk = vec_ref[3:4, hl], vec_ref[4:5, hl], vec_ref[7:8, hl]
            r = r_ref[0, sl, hl]
            k = k_ref[0, sl, hl]
            v = v_ref[0, sl, hl]
            logw = -_softplus(-(w0 + _dot(tw_act, w2_ref[:, hl]))) - 0.5
            dlog = -jnp.exp(logw)
            rate = _sigmoid(a0 + _dot(ta_act, a2_ref[:, hl]))
            e['gate'] = _dot(tg_act, g2_ref[:, hl])
            if has_vres:
                v = v + (vf_ref[0, sl, hl] - v) * _sigmoid(v0 + _dot(tv_act, v2_ref[:, hl]))
            kk = k * kkw
            k = k * (1.0 + (rate - 1.0) * kaw)
            kk_sq, rk_sum = seg_sums([kk * kk, r * k * rk])
            kk = kk / jnp.maximum(jnp.sqrt(kk_sq), 1e-12)
            if t_valid is not None:
                dlog = jnp.where(live, dlog, 0.0)
                kk = jnp.where(live, kk, 0.0)
                k_live = jnp.where(live, k, 0.0)
            else:
                k_live = k
            d_hi, d_lo = split2(dlog)
            cum2 = _dot(tri_c, jnp.concatenate([d_hi, d_lo], axis=1).astype(BF16))
            cum = cum2[:, :lanes] + cum2[:, lanes:]
            inv = jnp.exp(-cum)
            e['ar'] = jnp.concatenate([stack(-kk * jnp.exp(cum - dlog)), stack(r * jnp.exp(cum))],
                                      axis=0).astype(BF16)
            e['bk'] = jnp.concatenate([stack(kk * rate * inv), stack(k_live * inv)],
                                      axis=0).astype(BF16)
            e['v_s'] = stack(v)
            e['g_end'] = jnp.exp(cum[c_len - 1:c_len, :])
            e['bonus'] = rk_sum * v

        def products(hb):
            e = st[hb]
            e['s_old'] = s_sc[hb]
            big = _dot_nt(e['ar'], jnp.concatenate([e['bk'], e['s_old'].astype(BF16)], axis=0))
            a_bk = jnp.where(strict, big[:rows4, :2 * rows4], 0.0)
            e['r_bk'] = jnp.where(incl, big[rows4:, :2 * rows4], 0.0)
            e['a_s0'], e['r_s0'] = big[:rows4, 2 * rows4:], big[rows4:, 2 * rows4:]
            e['a_k'] = a_bk[:, rows4:]
            lmat = a_bk[:, :rows4]
            e['tinv'] = eye + lmat
            e['lpow'] = lmat.astype(BF16)

        def square(hb):
            e = st[hb]
            e['lpow'] = _dot(e['lpow'], e['lpow']).astype(BF16)

        def double(hb):
            e = st[hb]
            both = _dot(jnp.concatenate([e['lpow'], e['tinv'].astype(BF16)], axis=0), e['lpow'])
            e['tinv'] = e['tinv'] + both[rows4:]
            e['lpow'] = both[:rows4].astype(BF16)

        def solve(hb):
            e = st[hb]
            tinv = e['tinv'] + _dot(e['tinv'].astype(BF16), e['lpow'])
            u = _dot(tinv, e['a_s0'] + _dot(e['a_k'], e['v_s']))
            e['uv'] = jnp.concatenate([u, e['v_s']], axis=0).astype(BF16)

        def outputs(hb):
            e = st[hb]
            y_s = e['r_s0'] + _dot(e['r_bk'].astype(BF16), e['uv'])
            s_sc[hb] = (e['s_old'] + _dot_tn(e['uv'], e['bk'])) * e['g_end']
            y = y_s[0:c_len]
            for hh in range(1, HEAD_BLOCK):
                y = y + y_s[hh * c_len:(hh + 1) * c_len]
            e['y'] = y

        def group_norm(hb):
            e = st[hb]
            hl = slice(hb * lanes, (hb + 1) * lanes)
            y = e['y']
            mean = seg_sums([y])[0] * (1.0 / RWKV_HEAD)
            yc = y - mean
            var = seg_sums([yc * yc])[0] * (1.0 / RWKV_HEAD)
            yn = yc * lax.rsqrt(var + GN_EPS) * vec_ref[5:6, hl] + vec_ref[6:7, hl]
            y_ref[0, sl, hl] = ((yn + e['bonus']) * e['gate']).astype(y_ref.dtype)

        for stage in [prep, products, square] + [double] * (n_double - 1) + [solve, outputs, group_norm]:
            for hb in range(n_hb):
                stage(hb)
        return carry

    lax.fori_loop(0, n_chunks, one_chunk, 0)

    @pl.when(ti == pl.num_programs(2) - 1)
    def _():
        sout_ref[0] = s_sc[...]


def rwkv_scan(proj, v_first_proj, w2, a2, g2, v2, vec, s0_bd, t_real, chunk, tb, n_hb=3):
    b, t, _ = proj.shape
    lanes = MXU_DIM
    tb = min(tb, t)
    assert t % tb == 0 and tb % chunk == 0 and N_HEAD_BLOCKS % n_hb == 0
    has_vres = v_first_proj is not None
    t_valid = None if t_real == t else t_real

    wide = n_hb * lanes

    def col(block0):
        assert block0 % n_hb == 0
        return pl.BlockSpec((1, tb, wide), lambda bi, hg, ti: (bi, ti, block0 // n_hb + hg))

    def fixed(block):
        return pl.BlockSpec((1, tb, lanes), lambda bi, hg, ti: (bi, ti, block))

    def wcol():
        return pl.BlockSpec((LOWRANK_PAD, wide), lambda bi, hg, ti: (0, hg))

    st = pl.BlockSpec((1, n_hb, lanes, lanes), lambda bi, hg, ti: (bi, hg, 0, 0))
    if has_vres:
        in_specs = [col(RW_R), col(RW_K), col(RW_V), fixed(RW_W1), fixed(RW_A1), fixed(RW_G1),
                    fixed(RW_V1), col(RW_V), wcol(), wcol(), wcol(), wcol()]
        args = [proj, proj, proj, proj, proj, proj, proj, v_first_proj, w2, a2, g2, v2]
    else:
        in_specs = [col(RW_R), col(RW_K), col(RW_V), fixed(RW_W1), fixed(RW_A1), fixed(RW_G1),
                    wcol(), wcol(), wcol()]
        args = [proj, proj, proj, proj, proj, proj, w2, a2, g2]
    in_specs += [pl.BlockSpec((SUBLANE, wide), lambda bi, hg, ti: (0, hg)), st]
    args += [vec, s0_bd]
    return pl.pallas_call(
        functools.partial(_rwkv_scan_kernel, chunk=chunk, n_chunks=tb // chunk,
                          has_vres=has_vres, t_valid=t_valid, n_hb=n_hb),
        grid=(b, N_HEAD_BLOCKS // n_hb, t // tb),
        in_specs=in_specs,
        out_specs=[pl.BlockSpec((1, tb, wide), lambda bi, hg, ti: (bi, ti, hg)), st],
        out_shape=[jax.ShapeDtypeStruct((b, t, MAIN_W), BF16),
                   jax.ShapeDtypeStruct(s0_bd.shape, F32)],
        scratch_shapes=[pltpu.VMEM((n_hb, lanes, lanes), F32)],
        compiler_params=_params("arbitrary", "arbitrary", "arbitrary"),
        name="rwkv_scan",
    )(*args)


def _to_block_diag(s):
    b = s.shape[0]
    s = s.reshape(b, N_HEAD_BLOCKS, HEAD_BLOCK, RWKV_HEAD, RWKV_HEAD)
    eye = jnp.eye(HEAD_BLOCK, dtype=s.dtype)
    bd = s[:, :, :, :, None, :] * eye[None, None, :, None, :, None]
    return bd.reshape(b, N_HEAD_BLOCKS, MXU_DIM, MXU_DIM)


def _from_block_diag(bd):
    b = bd.shape[0]
    n = RWKV_HEAD
    x = jnp.stack([bd[:, :, hh * n:(hh + 1) * n, hh * n:(hh + 1) * n] for hh in range(HEAD_BLOCK)],
                  axis=2)
    return x.reshape(b, RWKV_HEADS, RWKV_HEAD, RWKV_HEAD)


def _mem_attn_kernel(q_ref, k_ref, v_ref, o_ref):
    scale = MEM_HEAD_DIM ** -0.5
    for hh in range(MEM_HEADS):
        sl = slice(hh * MEM_HEAD_DIM, (hh + 1) * MEM_HEAD_DIM)
        s = _dot_nt(q_ref[0, :, sl], k_ref[0, :, sl]) * scale
        p = jnp.exp(s - jnp.max(s, axis=-1, keepdims=True))
        p = p / jnp.sum(p, axis=-1, keepdims=True)
        o_ref[0, :, sl] = _dot(p, v_ref[0, :, sl]).astype(o_ref.dtype)


def mem_attend(qsrc, q_block, ksrc, k_block, vsrc, v_block, tq=512):
    b, t, _ = qsrc.shape
    m = ksrc.shape[1]
    tq = _row_tile(t, tq)
    return pl.pallas_call(
        _mem_attn_kernel,
        grid=(b, t // tq),
        in_specs=[pl.BlockSpec((1, tq, MEM_W), lambda bi, ti: (bi, ti, q_block)),
                  pl.BlockSpec((1, m, MEM_W), lambda bi, ti: (bi, 0, k_block)),
                  pl.BlockSpec((1, m, MEM_W), lambda bi, ti: (bi, 0, v_block))],
        out_specs=pl.BlockSpec((1, tq, MEM_W), lambda bi, ti: (bi, ti, 0)),
        out_shape=jax.ShapeDtypeStruct((b, t, MEM_W), BF16),
        compiler_params=_params("arbitrary", "arbitrary"),
        name="mem_attend",
    )(qsrc, ksrc, vsrc)


CMP_PAGE = 128
CMP_CHUNKS = CMP_PAGE // CMP_STRIDE
KV_LANES = NSA_KV_GROUPS * NSA_HEAD_DIM


def _chunk_proj_kernel(pt_ref, *refs, n_pages):
    del pt_ref
    page_refs = refs[:n_pages]
    w_ref, o0_ref, o1_ref, x_sc = refs[n_pages:]
    for kp in range(n_pages):
        for pos in range(CMP_STRIDE):
            x_sc[kp * CMP_CHUNKS:(kp + 1) * CMP_CHUNKS, pos * LANE:(pos + 1) * LANE] = (
                page_refs[kp][pl.ds(pos, CMP_CHUNKS, stride=CMP_STRIDE), :])
    res = _dot(x_sc[...].astype(BF16), w_ref[...])
    o0_ref[0] = res[:, :LANE]
    o1_ref[0] = res[:, LANE:]


def chunk_proj(rows2d, table, w_flat, n_pages, col_block=0):
    b, n_tab = table.shape
    assert n_tab % n_pages == 0

    def page_spec(kp):
        return pl.BlockSpec(
            (CMP_PAGE, LANE),
            lambda bi, pg, g, pt: (pt[bi, pg * n_pages + kp], col_block * NSA_KV_GROUPS + g))

    out_spec = pl.BlockSpec((1, n_pages * CMP_CHUNKS, LANE), lambda bi, pg, g, pt: (bi, pg, g))
    out_shape = jax.ShapeDtypeStruct((b, n_tab * CMP_CHUNKS, KV_LANES), F32)
    grid_spec = pltpu.PrefetchScalarGridSpec(
        num_scalar_prefetch=1,
        grid=(b, n_tab // n_pages, NSA_KV_GROUPS),
        in_specs=[page_spec(kp) for kp in range(n_pages)]
        + [pl.BlockSpec(w_flat.shape, lambda bi, pg, g, pt: (0, 0))],
        out_specs=[out_spec, out_spec],
        scratch_shapes=[pltpu.VMEM((n_pages * CMP_CHUNKS, CMP_STRIDE * LANE), F32)],
    )
    return pl.pallas_call(
        functools.partial(_chunk_proj_kernel, n_pages=n_pages),
        grid_spec=grid_spec,
        out_shape=[out_shape, out_shape],
        compiler_params=_params("arbitrary", "arbitrary", "arbitrary"),
        name="chunk_proj",
    )(table, *([rows2d] * n_pages), w_flat)


def _chunk_proj_pool_kernel(pt_ref, *refs, n_pages):
    del pt_ref
    page_refs = refs[:n_pages]
    w_ref, o0_ref, o1_ref, x_sc = refs[n_pages:]
    for g in range(NSA_KV_GROUPS):
        for kp in range(n_pages):
            for pos in range(CMP_STRIDE):
                x_sc[kp * CMP_CHUNKS:(kp + 1) * CMP_CHUNKS, pos * LANE:(pos + 1) * LANE] = (
                    page_refs[kp][0, pl.ds(pos, CMP_CHUNKS, stride=CMP_STRIDE), g, :])
        res = _dot(x_sc[...].astype(BF16), w_ref[...])
        o0_ref[0, :, g * LANE:(g + 1) * LANE] = res[:, :LANE]
        o1_ref[0, :, g * LANE:(g + 1) * LANE] = res[:, LANE:]


def chunk_proj_pool(pool, table, w_flat, n_pages):
    b, n_tab = table.shape
    assert n_tab % n_pages == 0

    def page_spec(kp):
        return pl.BlockSpec((1, CMP_PAGE, NSA_KV_GROUPS, LANE),
                            lambda bi, pg, pt: (pt[bi, pg * n_pages + kp], 0, 0, 0))

    out_spec = pl.BlockSpec((1, n_pages * CMP_CHUNKS, KV_LANES), lambda bi, pg, pt: (bi, pg, 0))
    out_shape = jax.ShapeDtypeStruct((b, n_tab * CMP_CHUNKS, KV_LANES), F32)
    grid_spec = pltpu.PrefetchScalarGridSpec(
        num_scalar_prefetch=1,
        grid=(b, n_tab // n_pages),
        in_specs=[page_spec(kp) for kp in range(n_pages)]
        + [pl.BlockSpec(w_flat.shape, lambda bi, pg, pt: (0, 0))],
        out_specs=[out_spec, out_spec],
        scratch_shapes=[pltpu.VMEM((n_pages * CMP_CHUNKS, CMP_STRIDE * LANE), F32)],
    )
    return pl.pallas_call(
        functools.partial(_chunk_proj_pool_kernel, n_pages=n_pages),
        grid_spec=grid_spec,
        out_shape=[out_shape, out_shape],
        compiler_params=_params("arbitrary", "arbitrary"),
        name="chunk_proj_pool",
    )(table, *([pool] * n_pages), w_flat)


def _block_mlp_kernel(p0_ref, p1_ref, pe_ref, w1_ref, b1_ref, w2_ref, o_ref):
    const = _dot(pe_ref[...], w1_ref[...])[0:1, :] + b1_ref[...]
    w2 = w2_ref[...]
    for g in range(NSA_KV_GROUPS):
        sl = slice(g * LANE, (g + 1) * LANE)
        hid = const + p0_ref[0, :, sl] + p1_ref[0, :, sl]
        o_ref[0, :, sl] = _dot(_gelu_tanh(hid), w2)


def block_mlp(p0, p1, pe, w1, b1, w2, tb=1024):
    b, nb, _ = p0.shape
    tb = _row_tile(nb, tb)
    pe8 = jnp.concatenate([pe.reshape(1, -1), jnp.zeros((SUBLANE - 1, pe.size), F32)], axis=0)
    spec = pl.BlockSpec((1, tb, KV_LANES), lambda bi, i: (bi, i, 0))
    full = lambda a: pl.BlockSpec(a.shape, lambda bi, i: (0,) * a.ndim)
    b1r = b1.reshape(1, -1)
    return pl.pallas_call(
        _block_mlp_kernel,
        grid=(b, nb // tb),
        in_specs=[spec, spec, full(pe8), full(w1), full(b1r), full(w2)],
        out_specs=spec,
        out_shape=jax.ShapeDtypeStruct((b, nb, KV_LANES), F32),
        compiler_params=_params("arbitrary", "arbitrary"),
        name="block_mlp",
    )(p0, p1, pe8, w1, b1r, w2)


def _w1_flat(w1):
    r = CMP_BLOCK // CMP_STRIDE
    e = w1.shape[1]
    w = w1.reshape(r, CMP_STRIDE, NSA_HEAD_DIM, e)
    return jnp.transpose(w, (1, 2, 0, 3)).reshape(CMP_STRIDE * NSA_HEAD_DIM, r * e).astype(BF16)


def compress(parts, nb, pe, w1, b1, w2):
    return block_mlp(parts[0][:, :nb], parts[1][:, 1:nb + 1], pe, w1, b1, w2)


def _rel_bucket(dist):
    n = jnp.maximum(dist, 0)
    max_exact = REL_BUCKETS // 2
    nf = jnp.maximum(n, 1).astype(F32)
    large = max_exact + (jnp.log(nf / max_exact) / math.log(REL_MAX_DIST / max_exact)
                         * (REL_BUCKETS - max_exact)).astype(jnp.int32)
    return jnp.where(n < max_exact, n, jnp.minimum(large, REL_BUCKETS - 1))


def _bias_lookup(rel_bias, dist):
    tab = rel_bias.astype(F32).reshape(REL_BUCKETS, NSA_KV_GROUPS, NSA_HPG)
    onehot = jax.nn.one_hot(_rel_bucket(dist), REL_BUCKETS, dtype=F32)
    out = jnp.dot(onehot, tab.reshape(REL_BUCKETS, -1), precision=HI)
    out = out.reshape(dist.shape + (NSA_KV_GROUPS, NSA_HPG))
    return jnp.moveaxis(out, (-2, -1), (0, 1))


def _cmp_to_sel(nb_pad, nb, nsb_pad, nsb):
    i = jnp.arange(nb_pad)[:, None]
    j = jnp.arange(nsb_pad)[None, :]
    start = i * CMP_STRIDE
    hit = (start < (j + 1) * SEL_BLOCK) & (start + CMP_BLOCK > j * SEL_BLOCK) & (i < nb) & (j < nsb)
    return hit.astype(F32)


def _nsa_prompt_kernel(q_ref, gt_ref, kc_ref, vc_ref, bc_ref, band_ref, ks_ref, vs_ref,
                       kw_ref, vw_ref, m_ref, o_ref, s_sc, *, nb, nsb):
    tq = NSA_TQ
    rows = NSA_HPG * tq
    i = pl.program_id(2)
    scale = NSA_HEAD_DIM ** -0.5
    qb = q_ref[0]
    q3 = jnp.concatenate([qb[:, hh * LANE:(hh + 1) * LANE] for hh in range(NSA_HPG)],
                         axis=0).astype(BF16)
    row_q = lax.broadcasted_iota(jnp.int32, (rows, LANE), 0) & (tq - 1)
    lane = lax.broadcasted_iota(jnp.int32, (rows, LANE), 1)
    t_pos = i * tq + row_q
    kt_w = NSA_KT
    tiles_per_kt = kt_w // tq
    key_lane = lax.broadcasted_iota(jnp.int32, (rows, kt_w), 1)
    t_pos_k = i * tq + (lax.broadcasted_iota(jnp.int32, (rows, kt_w), 0) & (tq - 1))

    def masked_scores(k_ref, kt, mask_of):
        ksl = pl.ds(pl.multiple_of(kt * kt_w, kt_w), kt_w)
        band = band_ref[0, jnp.clip(i - kt * tiles_per_kt, 0, NSA_BANDS - 1)].reshape(rows, kt_w)
        s = _dot_nt(q3, k_ref[0, ksl, :].astype(BF16)) * scale + band
        return jnp.where(mask_of(t_pos_k - (kt * kt_w + key_lane)), s, MASK_NEG), ksl

    n_win_tiles = WINDOW // kt_w + 1
    win_tiles = []
    m_w = jnp.full((rows, 1), MASK_NEG, F32)
    for jw in range(n_win_tiles):
        kt_raw = i // tiles_per_kt - (n_win_tiles - 1) + jw
        reach = jnp.where(kt_raw >= 0, WINDOW, 0)
        s_w, ksl_w = masked_scores(kw_ref, jnp.maximum(kt_raw, 0),
                                   lambda dist, reach=reach: (dist >= 0) & (dist < reach))
        win_tiles.append((s_w, ksl_w))
        m_w = jnp.maximum(m_w, jnp.max(s_w, axis=-1, keepdims=True))

    s_c = _dot_nt(q3, kc_ref[0].astype(BF16)) * scale + bc_ref[0].reshape(rows, LANE)
    m_c = (t_pos - (lane * CMP_STRIDE + (CMP_BLOCK - 1)) >= 0) & (lane < nb)
    p_c = _masked_softmax(s_c, m_c)
    o_c = _dot(p_c, vc_ref[0])

    imp3 = _dot_nt(m_ref[...], p_c, HI)
    imp = imp3[:, 0:tq]
    for hh in range(1, NSA_HPG):
        imp = imp + imp3[:, hh * tq:(hh + 1) * tq]
    jb = lax.broadcasted_iota(jnp.int32, (nsb, tq), 0)
    cur = (i * tq + lax.broadcasted_iota(jnp.int32, (nsb, tq), 1)) // SEL_BLOCK
    valid = jb <= cur
    forced = valid & ((jb == 0) | (jb > cur - SEL_LOCAL))
    imp = jnp.where(forced, FORCE_SCORE, jnp.where(valid, imp, -FORCE_SCORE))
    rank = jnp.zeros((nsb, tq), jnp.int32)
    for jp in range(nsb):
        other = imp[jp:jp + 1, :]
        rank = rank + ((other > imp) | ((other == imp) & (jb > jp))).astype(jnp.int32)
    sel = (rank < min(SEL_TOPK, nsb)).astype(F32).T
    sel3 = jnp.concatenate([sel] * NSA_HPG, axis=0).astype(BF16)
    blk_row = lax.broadcasted_iota(jnp.int32, (nsb, kt_w), 0)
    blk_lane = lax.broadcasted_iota(jnp.int32, (nsb, kt_w), 1) // SEL_BLOCK

    def finish(m_fin, l_fin, acc):
        return jnp.where(m_fin > MASK_NEG, acc / jnp.where(l_fin > 0, l_fin, 1.0), 0.0)

    l_w = jnp.zeros((rows, 1), F32)
    acc_w = jnp.zeros((rows, LANE), F32)
    for s_w, ksl_w in win_tiles:
        p_w = jnp.exp(s_w - m_w)
        l_w = l_w + jnp.sum(p_w, axis=-1, keepdims=True)
        acc_w = acc_w + _dot(p_w, vw_ref[0, ksl_w, :])
    o_w = finish(m_w, l_w, acc_w)

    n_kt = i // tiles_per_kt + 1

    def scores(kt, m_run):
        expand = (blk_row == kt * (kt_w // SEL_BLOCK) + blk_lane).astype(BF16)
        chosen = _dot(sel3, expand) > 0.5
        s, _ = masked_scores(ks_ref, kt, lambda dist: chosen & (dist >= 0))
        s_sc[kt] = s
        return jnp.maximum(m_run, jnp.max(s, axis=-1, keepdims=True))

    m_s = lax.fori_loop(0, n_kt, scores, jnp.full((rows, 1), MASK_NEG, F32))

    def accum(kt, carry):
        l_run, acc = carry
        ksl = pl.ds(pl.multiple_of(kt * kt_w, kt_w), kt_w)
        p = jnp.exp(s_sc[kt] - m_s)
        return (l_run + jnp.sum(p, axis=-1, keepdims=True), acc + _dot(p, vs_ref[0, ksl, :]))

    l_s, acc_s = lax.fori_loop(0, n_kt, accum, (jnp.zeros((rows, 1), F32),
                                                jnp.zeros((rows, LANE), F32)))
    o_s = finish(m_s, l_s, acc_s)

    gt = _sigmoid(gt_ref[0, 0])
    gcol = lambda br: jnp.concatenate(
        [gt[:, br * NSA_HPG + hh:br * NSA_HPG + hh + 1] for hh in range(NSA_HPG)], axis=0)
    o = gcol(0) * o_c + gcol(1) * o_s + gcol(2) * o_w
    for hh in range(NSA_HPG):
        o_ref[0, :, hh * LANE:(hh + 1) * LANE] = o[hh * tq:(hh + 1) * tq].astype(o_ref.dtype)


def nsa_prompt(proj, gates_t, kc, vc, bias_c, band, side, sel_map, nb, nsb):
    b, t, _ = proj.shape
    tq = NSA_TQ
    qw = NSA_HPG * LANE
    kv = lambda off: pl.BlockSpec((1, t, LANE), lambda bi, g, i: (bi, 0, off * NSA_KV_GROUPS + g))
    return pl.pallas_call(
        functools.partial(_nsa_prompt_kernel, nb=nb, nsb=nsb),
        grid=(b, NSA_KV_GROUPS, t // tq),
        in_specs=[pl.BlockSpec((1, tq, qw), lambda bi, g, i: (bi, i, g)),
                  pl.BlockSpec((1, 1, tq, NSA_HPG * 3), lambda bi, g, i: (bi, g, i, 0)),
                  pl.BlockSpec((1, LANE, LANE), lambda bi, g, i: (bi, 0, g)),
                  pl.BlockSpec((1, LANE, LANE), lambda bi, g, i: (bi, 0, g)),
                  pl.BlockSpec((1, NSA_HPG, tq, LANE), lambda bi, g, i: (g, 0, i, 0)),
                  pl.BlockSpec((1, NSA_BANDS, NSA_HPG, tq, NSA_KT), lambda bi, g, i: (g, 0, 0, 0, 0)),
                  kv(2), kv(3), kv(4), kv(5),
                  pl.BlockSpec(sel_map.shape, lambda bi, g, i: (0, 0))],
        out_specs=pl.BlockSpec((1, tq, qw), lambda bi, g, i: (bi, i, g)),
        out_shape=jax.ShapeDtypeStruct((b, t, MAIN_W), BF16),
        scratch_shapes=[pltpu.VMEM((t // NSA_KT, NSA_HPG * tq, NSA_KT), F32)],
        compiler_params=_params("arbitrary", "arbitrary", "arbitrary"),
        name="nsa_prompt",
    )(proj, gates_t, kc, vc, bias_c, band, side, side, side, side, sel_map)


def _nsa_dec_cmp_kernel(q_ref, kc_ref, vc_ref, bc_ref, m_ref, oc_ref, idx_ref, *, nb, nsb, t_pos):
    scale = NSA_HEAD_DIM ** -0.5
    nbp = kc_ref.shape[1]
    nsp = m_ref.shape[1]
    n_sel = min(SEL_TOPK, nsb)
    lane_b = lax.broadcasted_iota(jnp.int32, (SUBLANE, nbp), 1)
    m_c = (t_pos - (lane_b * CMP_STRIDE + (CMP_BLOCK - 1)) >= 0) & (lane_b < nb)
    row_s = lax.broadcasted_iota(jnp.int32, (SUBLANE, nsp), 0)
    jb = lax.broadcasted_iota(jnp.int32, (1, nsp), 1)
    cur = t_pos // SEL_BLOCK
    valid = jb <= cur
    forced = valid & ((jb == 0) | (jb > cur - SEL_LOCAL))
    out_lane = lax.broadcasted_iota(jnp.int32, (1, LANE), 1)
    idx_rows = []
    for g in range(NSA_KV_GROUPS):
        q3 = jnp.concatenate(
            [q_ref[0, 0:1, (g * NSA_HPG + hh) * LANE:(g * NSA_HPG + hh + 1) * LANE]
             for hh in range(NSA_HPG)] + [jnp.zeros((SUBLANE - NSA_HPG, LANE), F32)], axis=0)
        sl = slice(g * LANE, (g + 1) * LANE)
        s_c = _dot_nt(q3, kc_ref[0, :, sl]) * scale + bc_ref[g]
        p_c = _masked_softmax(s_c, m_c)
        oc_ref[0, g] = _dot(p_c, vc_ref[0, :, sl])
        imp8 = jnp.where(row_s < NSA_HPG, _dot(p_c, m_ref[...], HI), 0.0)
        imp = jnp.sum(imp8, axis=0, keepdims=True)
        imp = jnp.where(forced, FORCE_SCORE, jnp.where(valid, imp, -FORCE_SCORE))
        imp = jnp.where(jb < nsb, imp, -jnp.inf)
        jbf = jb.astype(F32)
        picks = jnp.zeros((1, LANE), F32)
        for kk in range(n_sel):
            best = jnp.max(imp, axis=-1, keepdims=True)
            arg = jnp.min(jnp.where(imp == best, jbf, float(nsp)), axis=-1, keepdims=True)
            picks = jnp.where(out_lane == kk, arg, picks)
            imp = jnp.where(jbf == arg, -jnp.inf, imp)
        idx_rows.append(picks.astype(jnp.int32))
    idx_rows.append(jnp.zeros((SUBLANE - NSA_KV_GROUPS, LANE), jnp.int32))
    idx_ref[0] = jnp.concatenate(idx_rows, axis=0)


def nsa_dec_cmp(proj, kc, vc, bias_c, sel_map, nb, nsb, t_pos):
    b, tp, _ = proj.shape
    nbp = kc.shape[1]
    return pl.pallas_call(
        functools.partial(_nsa_dec_cmp_kernel, nb=nb, nsb=nsb, t_pos=t_pos),
        grid=(b,),
        in_specs=[pl.BlockSpec((1, tp, MAIN_W), lambda bi: (bi, 0, 0)),
                  pl.BlockSpec((1, nbp, KV_LANES), lambda bi: (bi, 0, 0)),
                  pl.BlockSpec((1, nbp, KV_LANES), lambda bi: (bi, 0, 0)),
                  pl.BlockSpec(bias_c.shape, lambda bi: (0, 0, 0)),
                  pl.BlockSpec(sel_map.shape, lambda bi: (0, 0))],
        out_specs=[pl.BlockSpec((1, NSA_KV_GROUPS, SUBLANE, LANE), lambda bi: (bi, 0, 0, 0)),
                   pl.BlockSpec((1, SUBLANE, LANE), lambda bi: (bi, 0, 0))],
        out_shape=[jax.ShapeDtypeStruct((b, NSA_KV_GROUPS, SUBLANE, LANE), F32),
                   jax.ShapeDtypeStruct((b, SUBLANE, LANE), jnp.int32)],
        compiler_params=_params("arbitrary"),
        name="nsa_dec_cmp",
    )(proj, kc, vc, bias_c, sel_map)


def _nsa_dec_sel_kernel(phys_ref, isnew_ref, *refs, n_sel, n_win, t_pos, win_start):
    del phys_ref
    kb_refs = refs[:n_sel]
    vb_refs = refs[n_sel:2 * n_sel]
    (q_ref, new_ref, ds_ref, bs_ref, wk_ref, wv_ref, bw_ref, oc_ref, gt_ref, o_ref) = refs[2 * n_sel:]
    bi = pl.program_id(0)
    g = pl.program_id(1)
    scale = NSA_HEAD_DIM ** -0.5
    q3 = q_ref[0, 0]
    new_rows = new_ref[0, 0]
    pad_blk = jnp.zeros((SEL_BLOCK - 1, LANE), F32)
    new_k = jnp.concatenate([new_rows[0:1], pad_blk], axis=0)
    new_v = jnp.concatenate([new_rows[1:2], pad_blk], axis=0)
    ks, vs = [], []
    for kk in range(n_sel):
        fresh = isnew_ref[bi, g, kk] > 0
        ks.append(jnp.where(fresh, new_k, kb_refs[kk][0, :, g, :]))
        vs.append(jnp.where(fresh, new_v, vb_refs[kk][0, :, g, :]))
    ks = jnp.concatenate(ks, axis=0)
    vs = jnp.concatenate(vs, axis=0)
    s_s = _dot_nt(q3, ks) * scale + bs_ref[0, 0]
    p_s = _masked_softmax(s_s, ds_ref[0, 0] >= 0)
    o_s = _dot(p_s, vs)

    pad_w = jnp.zeros((SUBLANE - 1, LANE), F32)
    kw = jnp.concatenate([wk_ref[0, :, g, :], new_rows[2:3], pad_w], axis=0)
    vw = jnp.concatenate([wv_ref[0, :, g, :], new_rows[3:4], pad_w], axis=0)
    nw = kw.shape[0]
    pos = win_start + lax.broadcasted_iota(jnp.int32, (SUBLANE, nw), 1)
    d_w = t_pos - pos
    m_w = (d_w >= 0) & (d_w < WINDOW) & (pos >= 0) & (pos - win_start < n_win)
    s_w = _dot_nt(q3, kw) * scale + bw_ref[0]
    p_w = _masked_softmax(s_w, m_w)
    o_w = _dot(p_w, vw)

    gt = _sigmoid(gt_ref[0, 0])
    o_ref[0, 0] = gt[:, 0:1] * oc_ref[0, 0] + gt[:, 1:2] * o_s + gt[:, 2:3] * o_w


def nsa_dec_sel(phys, isnew, pool_k, pool_v, q8, new_rows, d_s, bias_s, win_k, win_v, bias_w,
                o_c, gates8, n_win, t_pos, win_start):
    b = q8.shape[0]
    n_sel = phys.shape[-1]
    n_keys = n_sel * SEL_BLOCK
    nw = win_k.shape[1]
    nwp = nw + SUBLANE

    def blk_spec(kk):
        return pl.BlockSpec((1, SEL_BLOCK, NSA_KV_GROUPS, LANE),
                            lambda bi, g, ph, nf: (ph[bi, g, kk], 0, 0, 0))

    per_bg = lambda *shape: pl.BlockSpec((1, 1) + shape, lambda bi, g, ph, nf: (bi, g) + (0,) * len(shape))
    grid_spec = pltpu.PrefetchScalarGridSpec(
        num_scalar_prefetch=2,
        grid=(b, NSA_KV_GROUPS),
        in_specs=[blk_spec(kk) for kk in range(n_sel)] + [blk_spec(kk) for kk in range(n_sel)]
        + [per_bg(SUBLANE, LANE), per_bg(SUBLANE, LANE), per_bg(1, n_keys), per_bg(SUBLANE, n_keys),
           pl.BlockSpec((1, nw, NSA_KV_GROUPS, LANE), lambda bi, g, ph, nf: (bi, 0, 0, 0)),
           pl.BlockSpec((1, nw, NSA_KV_GROUPS, LANE), lambda bi, g, ph, nf: (bi, 0, 0, 0)),
           pl.BlockSpec((1, SUBLANE, nwp), lambda bi, g, ph, nf: (g, 0, 0)),
           per_bg(SUBLANE, LANE), per_bg(SUBLANE, SUBLANE)],
        out_specs=per_bg(SUBLANE, LANE),
    )
    return pl.pallas_call(
        functools.partial(_nsa_dec_sel_kernel, n_sel=n_sel, n_win=nw + 1, t_pos=t_pos,
                          win_start=win_start),
        grid_spec=grid_spec,
        out_shape=jax.ShapeDtypeStruct((b, NSA_KV_GROUPS, SUBLANE, LANE), F32),
        compiler_params=_params("arbitrary", "arbitrary"),
        name="nsa_dec_sel",
    )(phys, isnew, *([pool_k] * n_sel), *([pool_v] * n_sel), q8, new_rows, d_s, bias_s,
      win_k, win_v, bias_w, o_c, gates8)


def _pad_cols(w, n):
    return jnp.pad(w, ((0, 0), (0, n - w.shape[1])))


def _pad_rows(w, n):
    return jnp.pad(w, ((0, n - w.shape[0]), (0, 0)))


def _prep_weights(P):
    W = {}
    W['w_gu'] = P['ffn_gu'].astype(BF16)
    W['w_d'] = P['ffn_d'].astype(BF16)
    W['w_out'] = P['w_out'].astype(BF16)[None]
    W['rw_in'], W['rw_w2'], W['rw_a2'], W['rw_g2'], W['rw_v2'], W['rw_vec'] = [], [], [], [], [], []
    lp = LOWRANK_PAD
    for l in range(N_A):
        w_in = P['w_in_a'][l]
        v1 = P['rw_v1'][l - 1] if l > 0 else jnp.zeros((D_MODEL, lp), F32)
        W['rw_in'].append(jnp.concatenate([
            w_in, _pad_cols(P['rw_w1'][l], lp), _pad_cols(v1, lp), _pad_cols(P['rw_a1'][l], lp),
            _pad_cols(P['rw_g1'][l], lp)], axis=1).astype(BF16))
        W['rw_w2'].append(_pad_rows(P['rw_w2'][l], lp))
        W['rw_a2'].append(_pad_rows(P['rw_a2'][l], lp))
        W['rw_g2'].append(_pad_rows(P['rw_g2'][l], lp))
        W['rw_v2'].append(_pad_rows(P['rw_v2'][l - 1], lp) if l > 0 else None)
        v0 = P['rw_v0'][l - 1] if l > 0 else jnp.zeros((MAIN_W,), F32)
        W['rw_vec'].append(jnp.stack([P['rw_w0'][l], P['rw_a0'][l], v0, P['rw_kk'][l], P['rw_ka'][l],
                                      P['rw_lnw'][l], P['rw_lnb'][l], P['rw_rk'][l].reshape(-1)]))
    W['w_in_b'] = []
    for l in range(DEPTH - N_A):
        w = P['w_in_b'][l]
        W['w_in_b'].append(jnp.concatenate([
            w[:, :MAIN_W], w[:, MAIN_W + GATE_W:], _pad_cols(w[:, MAIN_W:MAIN_W + GATE_W], MXU_DIM)],
            axis=1).astype(BF16))
    W['w_kv'] = P['w_kv'].astype(BF16)
    W['w_mem_kv'] = [P['w_mem_kv'][l].astype(BF16) for l in range(DEPTH)]
    W['cmp_w1_flat'] = [_w1_flat(P['cmp_w1'][c]) for c in range(2)]
    return W


B_QM_BLOCK = MAIN_W // MEM_W
B_GATE_OFF = MAIN_W + MEM_W


def _ffn(h2, norms_l, first, W, l, i):
    hid = norm_swiglu(h2, norms_l[first], W['w_gu'], l, i, tm=1024)
    return matmul_norm_res(hid, W['w_d'], l, i, norms_l[first + 1], h2, 0.5, tk=FFN_DOWN_TK)


def _trunk(x, t_real, P, W, shift0, wkv0_bd, mem_src, make_side, attend, chunk, tb):
    b, t, d = x.shape
    h2 = x.reshape(b * t, d)
    shifts, states = [], []
    v_first_proj, ctx, side_state = None, None, None
    for l in range(DEPTH):
        n = P['norms'][l]
        h2 = _ffn(h2, n, 0, W, l, 0)
        if l < N_A:
            proj, last = rwkv_in_proj(h2.reshape(b, t, d), shift0[l], n[2], P['rw_mu'][l],
                                      W['rw_in'][l], t_real)
            main, s_bd = rwkv_scan(proj, v_first_proj if l > 0 else None, W['rw_w2'][l], W['rw_a2'][l],
                                   W['rw_g2'][l], W['rw_v2'][l], W['rw_vec'][l], wkv0_bd[l],
                                   t_real, chunk, tb)
            if l == 0:
                v_first_proj = proj
            shifts.append(last)
            states.append(_from_block_diag(s_bd))
            q_src, q_block = proj, RW_QM * MXU_DIM // MEM_W
        else:
            proj = norm_matmul(h2, n[2], W['w_in_b'][l - N_A], tn=768).reshape(b, t, -1)
            main = attend(proj, ctx)
            q_src, q_block = proj, B_QM_BLOCK
        mk, kb, mv, vb = mem_src(l)
        mo = mem_attend(q_src, q_block, mk, kb, mv, vb)
        mix = jnp.concatenate([main, mo], axis=-1).reshape(b * t, d)
        h2 = matmul_norm_res(mix, W['w_out'], 0, l, n[3], h2, 1.0, tk=1024)
        h2 = _ffn(h2, n, 4, W, l, 1)
        if l == N_A - 1:
            side = norm_matmul(h2, P['kv_norm'], W['w_kv'], tn=768).reshape(b, t, -1)
            ctx, side_state = make_side(side)
    return h2.reshape(b, t, d), jnp.stack(shifts), jnp.stack(states), side_state


def kernel(x_prompt, x_sample, mem_prompt, state_wkv, state_shift, cache_mem_k, cache_mem_v,
           cache_cmp_k, cache_cmp_v, cache_slc_k, cache_slc_v, cache_win_k, cache_win_v, page_table,
           norms, ffn_gu, ffn_d, w_in_a, w_in_b, w_out, mem_norm, w_mem_kv, kv_norm, w_kv,
           cmp_pe, cmp_w1, cmp_b1, cmp_w2, rel_bias,
           rw_mu, rw_w0, rw_w1, rw_w2, rw_a0, rw_a1, rw_a2, rw_g1, rw_g2, rw_v0, rw_v1, rw_v2,
           rw_kk, rw_ka, rw_rk, rw_lnw, rw_lnb):
    P = dict(norms=norms, ffn_gu=ffn_gu, ffn_d=ffn_d, w_in_a=w_in_a, w_in_b=w_in_b, w_out=w_out,
             kv_norm=kv_norm, w_kv=w_kv, w_mem_kv=w_mem_kv, cmp_w1=cmp_w1, rw_mu=rw_mu, rw_w0=rw_w0,
             rw_w1=rw_w1, rw_w2=rw_w2, rw_a0=rw_a0, rw_a1=rw_a1, rw_a2=rw_a2, rw_g1=rw_g1,
             rw_g2=rw_g2, rw_v0=rw_v0, rw_v1=rw_v1, rw_v2=rw_v2, rw_kk=rw_kk, rw_ka=rw_ka,
             rw_rk=rw_rk, rw_lnw=rw_lnw, rw_lnb=rw_lnb)
    W = _prep_weights(P)
    G, dh = NSA_KV_GROUPS, NSA_HEAD_DIM
    split_side = lambda side, bx, t: [side[:, :t, c * KV_LANES:(c + 1) * KV_LANES].reshape(bx, t, G, dh)
                                      for c in range(6)]

    def cmp_mlp(parts, nb, c):
        return compress(parts, nb, cmp_pe[c], cmp_w1[c], cmp_b1[c], cmp_w2[c])

    bp, tp, d = x_prompt.shape
    n_mem = mem_prompt.shape[1]
    mem2 = mem_prompt.reshape(bp * n_mem, d)
    p_mkv = [norm_matmul(mem2, mem_norm[l], W['w_mem_kv'][l], tn=512).reshape(bp, n_mem, 2 * MEM_W)
             for l in range(DEPTH)]
    p_mem_k = jnp.stack([m[..., :MEM_W].reshape(bp, n_mem, MEM_HEADS, MEM_HEAD_DIM) for m in p_mkv])
    p_mem_v = jnp.stack([m[..., MEM_W:].reshape(bp, n_mem, MEM_HEADS, MEM_HEAD_DIM) for m in p_mkv])

    nb_p = tp // CMP_STRIDE - 1
    nsb_p = tp // SEL_BLOCK
    nq_tiles = tp // NSA_TQ
    t_all = jnp.arange(tp)
    c_end = jnp.arange(LANE) * CMP_STRIDE + (CMP_BLOCK - 1)
    assert nb_p <= LANE
    bias_c_p = _bias_lookup(rel_bias, t_all[:, None] - c_end[None, :])
    ii = jnp.arange(NSA_TQ)
    cc = jnp.arange(NSA_KT)
    band_p = jnp.stack([_bias_lookup(rel_bias, dd * NSA_TQ + ii[:, None] - cc[None, :])
                        for dd in range(NSA_BANDS)], axis=1)
    assert (NSA_BANDS - 1) * NSA_TQ - (NSA_KT - 1) >= REL_MAX_DIST and tp % NSA_KT == 0
    sel_map_p = _cmp_to_sel(LANE, nb_p, nsb_p, nsb_p).T
    ident = jnp.arange(bp * tp // CMP_PAGE, dtype=jnp.int32).reshape(bp, tp // CMP_PAGE)

    def prompt_side(side):
        rows = side.reshape(bp * tp, -1)
        kc = cmp_mlp(chunk_proj(rows, ident, W['cmp_w1_flat'][0], tp // CMP_PAGE, 0), nb_p, 0)
        vc = cmp_mlp(chunk_proj(rows, ident, W['cmp_w1_flat'][1], tp // CMP_PAGE, 1), nb_p, 1)
        padb = ((0, 0), (0, LANE - nb_p), (0, 0))
        wb = min(WINDOW, tp)
        kc_r, vc_r, ks, vs, kw, vw = split_side(side, bp, tp)
        return ((jnp.pad(kc, padb), jnp.pad(vc, padb), side),
                (kc_r, vc_r, ks, vs, kw[:, tp - wb:], vw[:, tp - wb:]))

    def prompt_attend(proj, ctx):
        kc, vc, side = ctx
        gates = proj[..., B_GATE_OFF:B_GATE_OFF + GATE_W].reshape(bp, tp, 3, G, NSA_HPG)
        gates_t = jnp.transpose(gates, (0, 3, 1, 2, 4)).reshape(bp, G, tp, 3 * NSA_HPG)
        return nsa_prompt(proj, gates_t, kc, vc, bias_c_p, band_p, side, sel_map_p, nb_p, nsb_p)

    zeros_shift = jnp.zeros((N_A, bp, d), F32)
    zeros_state = jnp.zeros((N_A, bp, N_HEAD_BLOCKS, MXU_DIM, MXU_DIM), F32)
    y_prompt, p_shift, p_wkv, p_side = _trunk(
        x_prompt, tp, P, W, zeros_shift, zeros_state,
        lambda l: (p_mkv[l], 0, p_mkv[l], 1), prompt_side, prompt_attend, chunk=64, tb=256)
    p_cmp_k, p_cmp_v, p_slc_k, p_slc_v, p_win_k, p_win_v = p_side

    bd, s_new, _ = x_sample.shape
    assert s_new == 1
    ts = SUBLANE
    xs = jnp.pad(x_sample, ((0, 0), (0, ts - s_new), (0, 0)))
    past_len = page_table.shape[1] * CMP_PAGE
    n_past_blk = past_len // SEL_BLOCK
    blk_per_page = CMP_PAGE // SEL_BLOCK
    nsb_s = n_past_blk + 1
    t_pos = past_len
    nc_s = -(-(past_len + s_new) // CMP_STRIDE)
    nb_s = nc_s - 1
    wb_s = cache_win_k.shape[1]
    win_start = past_len - wb_s
    mem_k2 = cache_mem_k.reshape(DEPTH, bd, n_mem, MEM_W)
    mem_v2 = cache_mem_v.reshape(DEPTH, bd, n_mem, MEM_W)
    nsp = -(-nsb_s // LANE) * LANE
    sel_map_s = _cmp_to_sel(nb_s, nb_s, nsp, nsb_s)
    c_end_s = jnp.arange(nb_s) * CMP_STRIDE + (CMP_BLOCK - 1)
    bias_c_s = _bias_lookup(rel_bias, t_pos - c_end_s)
    bias_c_s = jnp.pad(bias_c_s, ((0, 0), (0, SUBLANE - NSA_HPG), (0, 0)))
    nwp = wb_s + SUBLANE
    bias_w_s = _bias_lookup(rel_bias, t_pos - (win_start + jnp.arange(nwp)))
    bias_w_s = jnp.pad(bias_w_s, ((0, 0), (0, SUBLANE - NSA_HPG), (0, 0)))
    ident_s = jnp.arange(bd, dtype=jnp.int32).reshape(bd, 1)
    pages_per_step = math.gcd(16, page_table.shape[1])

    def sample_side(side):
        new = side[:, :s_new]
        parts = []
        for c, pool in enumerate((cache_cmp_k, cache_cmp_v)):
            past = chunk_proj_pool(pool, page_table, W['cmp_w1_flat'][c], pages_per_step)
            fresh_page = jnp.pad(new[:, :, c * KV_LANES:(c + 1) * KV_LANES],
                                 ((0, 0), (0, CMP_PAGE - s_new), (0, 0))).reshape(bd, CMP_PAGE, G, dh)
            fresh = chunk_proj_pool(fresh_page, ident_s, W['cmp_w1_flat'][c], 1)
            n_fresh = nc_s - past[0].shape[1]
            parts.append([jnp.concatenate([p, f[:, :n_fresh]], axis=1) for p, f in zip(past, fresh)])
        kc = cmp_mlp(parts[0], nb_s, 0)
        vc = cmp_mlp(parts[1], nb_s, 1)
        kc_n, vc_n, ks_n, vs_n, kw_n, vw_n = split_side(side, bd, s_new)
        s_win_k = jnp.concatenate([cache_win_k, kw_n], axis=1)[:, s_new:]
        s_win_v = jnp.concatenate([cache_win_v, vw_n], axis=1)[:, s_new:]
        return (kc, vc, new), (kc_n, vc_n, ks_n, vs_n, s_win_k, s_win_v)

    def sample_attend(proj, ctx):
        kc, vc, new = ctx
        o_c, idx8 = nsa_dec_cmp(proj, kc, vc, bias_c_s, sel_map_s, nb_s, nsb_s, t_pos)
        idx = idx8[:, :G, :SEL_TOPK]
        is_new = idx >= n_past_blk
        jp = jnp.minimum(idx, n_past_blk - 1)
        phys = (jnp.take_along_axis(page_table[:, None, :], jp // blk_per_page, axis=2) * blk_per_page
                + jp % blk_per_page)
        k_pos = (idx[..., None] * SEL_BLOCK + jnp.arange(SEL_BLOCK)).reshape(bd, G, 1, -1)
        d_s = t_pos - k_pos
        tab = rel_bias.astype(F32).reshape(REL_BUCKETS, G, NSA_HPG)
        onehot = jax.nn.one_hot(_rel_bucket(d_s[:, :, 0]), REL_BUCKETS, dtype=F32)
        bias_s = jnp.einsum('bgkn,ngh->bghk', onehot, tab, precision=HI)
        bias_s = jnp.pad(bias_s, ((0, 0), (0, 0), (0, SUBLANE - NSA_HPG), (0, 0)))
        q8 = jnp.pad(proj[:, 0, :MAIN_W].reshape(bd, G, NSA_HPG, dh),
                     ((0, 0), (0, 0), (0, SUBLANE - NSA_HPG), (0, 0)))
        new_rows = jnp.pad(jnp.transpose(new[:, 0, 2 * KV_LANES:].reshape(bd, 4, G, dh), (0, 2, 1, 3)),
                           ((0, 0), (0, 0), (0, SUBLANE - 4), (0, 0)))
        gates = proj[:, 0, B_GATE_OFF:B_GATE_OFF + GATE_W].reshape(bd, 3, G, NSA_HPG)
        gates8 = jnp.pad(jnp.transpose(gates, (0, 2, 3, 1)),
                         ((0, 0), (0, 0), (0, SUBLANE - NSA_HPG), (0, SUBLANE - 3)))
        o = nsa_dec_sel(phys.astype(jnp.int32), is_new.astype(jnp.int32),
                        cache_slc_k.reshape(-1, SEL_BLOCK, G, dh),
                        cache_slc_v.reshape(-1, SEL_BLOCK, G, dh),
                        q8, new_rows, d_s.astype(jnp.int32), bias_s,
                        cache_win_k, cache_win_v,
                        bias_w_s, o_c, gates8, wb_s, t_pos, win_start)
        main = o[:, :, :NSA_HPG].reshape(bd, 1, MAIN_W)
        return jnp.pad(main, ((0, 0), (0, ts - 1), (0, 0))).astype(BF16)

    y_s, s_shift, s_wkv, s_side = _trunk(
        xs, s_new, P, W, state_shift, jnp.stack([_to_block_diag(state_wkv[l]) for l in range(N_A)]),
        lambda l: (mem_k2[l], 0, mem_v2[l], 0), sample_side, sample_attend, chunk=SUBLANE, tb=SUBLANE)
    y_sample = y_s[:, :s_new]
    s_cmp_k, s_cmp_v, s_slc_k, s_slc_v, s_win_k, s_win_v = s_side

    return (y_prompt, y_sample, p_mem_k, p_mem_v, p_wkv, p_shift,
            p_cmp_k, p_cmp_v, p_slc_k, p_slc_v, p_win_k, p_win_v,
            s_wkv, s_shift, s_cmp_k, s_cmp_v, s_slc_k, s_slc_v, s_win_k, s_win_v)
```

```python
import functools
import math

import jax
import jax.numpy as jnp
from jax import lax
from jax.experimental import pallas as pl
from jax.experimental.pallas import tpu as pltpu

F32 = jnp.float32
BF16 = jnp.bfloat16
HI = lax.Precision.HIGHEST

D_MODEL = 2048
DEPTH = 4
N_A = 2
MEM_HEADS = 4
MEM_HEAD_DIM = 128
MEM_W = 512
MAIN_W = 1536
RWKV_HEAD = 64
RWKV_HEADS = 24
GN_EPS = 64e-5
NSA_HEAD_DIM = 128
NSA_Q_HEADS = 12
NSA_KV_GROUPS = 4
NSA_HPG = 3
GATE_W = 36
CMP_BLOCK = 32
CMP_STRIDE = 16
SEL_BLOCK = 64
SEL_TOPK = 16
SEL_LOCAL = 2
WINDOW = 512
REL_BUCKETS = 32
REL_MAX_DIST = 128
D_FF = 5504
NORM_EPS = 1e-6
MASK_NEG = -1e30
FORCE_SCORE = 1e9

LANE = 128
SUBLANE = 8
MXU_DIM = 256
VMEM_LIMIT = 56 * 1024 * 1024

FFN_DOWN_TK = 11 * LANE
HEAD_BLOCK = MXU_DIM // RWKV_HEAD
N_HEAD_BLOCKS = RWKV_HEADS // HEAD_BLOCK
LOWRANK_PAD = MXU_DIM
NSA_TQ = 256
NSA_KT = 256
NSA_BANDS = 3


def _params(*sem):
    return pltpu.CompilerParams(dimension_semantics=sem, vmem_limit_bytes=VMEM_LIMIT)


def _rms(x, g):
    return x * lax.rsqrt(jnp.mean(x * x, axis=-1, keepdims=True) + NORM_EPS) * g


def _sigmoid(x):
    return 1.0 / (1.0 + jnp.exp(-x))


def _softplus(x):
    return jnp.maximum(x, 0.0) + jnp.log(1.0 + jnp.exp(-jnp.abs(x)))


def _gelu_tanh(x):
    return 0.5 * x * (1.0 + jnp.tanh(math.sqrt(2.0 / math.pi) * (x + 0.044715 * x * x * x)))


def _dot(a, b, precision=None):
    return jnp.dot(a, b, preferred_element_type=F32, precision=precision)


def _dot_nt(a, b, precision=None):
    return lax.dot_general(a, b, (((1,), (1,)), ((), ())), preferred_element_type=F32,
                           precision=precision)


def _dot_tn(a, b, precision=None):
    return lax.dot_general(a, b, (((0,), (0,)), ((), ())), preferred_element_type=F32,
                           precision=precision)


BF16_BITS_OF_F32 = 0xFFFF0000


def _split_hi_lo(x):
    bits = lax.bitcast_convert_type(x, jnp.uint32) & jnp.uint32(BF16_BITS_OF_F32)
    hi = lax.bitcast_convert_type(bits, F32)
    return hi, x - hi


def _masked_softmax(s, mask):
    s = jnp.where(mask, s, MASK_NEG)
    p = jnp.exp(s - jnp.max(s, axis=-1, keepdims=True)) * mask.astype(F32)
    den = jnp.sum(p, axis=-1, keepdims=True)
    return p / jnp.where(den > 0, den, 1.0)


def _row_tile(rows, target):
    t = min(rows, target)
    while rows % t:
        t -= SUBLANE
    return t


def _norm_matmul_kernel(x_ref, g_ref, w_ref, o_ref, xn_ref):
    @pl.when(pl.program_id(1) == 0)
    def _():
        xn_ref[...] = _rms(x_ref[...], g_ref[...]).astype(BF16)

    o_ref[...] = _dot(xn_ref[...], w_ref[...]).astype(o_ref.dtype)


def norm_matmul(x, g, w, tn, out_dtype=F32, tm=512):
    rows, d = x.shape
    n = w.shape[1]
    tm = _row_tile(rows, tm)
    assert n % tn == 0
    return pl.pallas_call(
        _norm_matmul_kernel,
        grid=(rows // tm, n // tn),
        in_specs=[pl.BlockSpec((tm, d), lambda i, j: (i, 0)),
                  pl.BlockSpec((1, d), lambda i, j: (0, 0)),
                  pl.BlockSpec((d, tn), lambda i, j: (0, j))],
        out_specs=pl.BlockSpec((tm, tn), lambda i, j: (i, j)),
        out_shape=jax.ShapeDtypeStruct((rows, n), out_dtype),
        scratch_shapes=[pltpu.VMEM((tm, d), BF16)],
        compiler_params=_params("arbitrary", "arbitrary"),
        name="norm_matmul",
    )(x, g.reshape(1, d), w)


def _norm_swiglu_kernel(x_ref, g_ref, wg_ref, wu_ref, o_ref, xn_ref, *, tail):
    j = pl.program_id(1)
    last = pl.num_programs(1) - 1

    @pl.when(j == 0)
    def _():
        xn_ref[...] = _rms(x_ref[...], g_ref[...]).astype(BF16)

    xn = xn_ref[...]
    tn = o_ref.shape[1]
    sub = min(tn, MXU_DIM)

    def tile(up_shift):
        c0 = 0
        while c0 < tn - up_shift:
            w = min(sub, tn - up_shift - c0)
            gate = _dot(xn, wg_ref[0, 0, :, c0:c0 + w])
            up = _dot(xn, wu_ref[0, 0, :, c0 + up_shift:c0 + up_shift + w])
            o_ref[:, c0:c0 + w] = (gate * _sigmoid(gate) * up).astype(o_ref.dtype)
            c0 += w
        if up_shift:
            o_ref[:, c0:] = jnp.zeros((o_ref.shape[0], up_shift), o_ref.dtype)

    if tail == tn:
        tile(0)
    else:
        pl.when(j != last)(lambda: tile(0))
        pl.when(j == last)(lambda: tile(tn - tail))


def norm_swiglu(x, g, w_gu, l, i, tn=512, tm=512):
    rows, d = x.shape
    f = w_gu.shape[-1] // 2
    assert f % LANE == 0 and tn % LANE == 0
    tm = _row_tile(rows, tm)
    n_tiles = -(-f // tn)
    tail = f - (n_tiles - 1) * tn
    el = pl.Element

    def up_col(j):
        return pl.multiple_of(jnp.minimum(f + j * tn, 2 * f - tn), LANE)

    return pl.pallas_call(
        functools.partial(_norm_swiglu_kernel, tail=tail),
        grid=(rows // tm, n_tiles),
        in_specs=[pl.BlockSpec((tm, d), lambda r, j: (r, 0)),
                  pl.BlockSpec((1, d), lambda r, j: (0, 0)),
                  pl.BlockSpec((el(1), el(1), el(d), el(tn)), lambda r, j: (l, i, 0, j * tn)),
                  pl.BlockSpec((el(1), el(1), el(d), el(tn)), lambda r, j: (l, i, 0, up_col(j)))],
        out_specs=pl.BlockSpec((tm, tn), lambda r, j: (r, j)),
        out_shape=jax.ShapeDtypeStruct((rows, n_tiles * tn), BF16),
        scratch_shapes=[pltpu.VMEM((tm, d), BF16)],
        compiler_params=_params("arbitrary", "arbitrary"),
        name="norm_swiglu",
    )(x, g.reshape(1, d), w_gu, w_gu)


def _matmul_norm_res_kernel(a_ref, w_ref, g_ref, h_ref, o_ref, *, scale, overlap):
    k = pl.program_id(1)
    last = pl.num_programs(1) - 1

    @pl.when(k == 0)
    def _():
        o_ref[...] = jnp.zeros_like(o_ref)

    a = a_ref[...]
    if overlap:
        col = lax.broadcasted_iota(jnp.int32, a.shape, 1)
        a = jnp.where(col < jnp.where(k == last, overlap, 0), jnp.zeros_like(a), a)
    n_out = o_ref.shape[1]
    sub = min(n_out, 2 * MXU_DIM)
    for c0 in range(0, n_out, sub):
        o_ref[:, c0:c0 + sub] += _dot(a, w_ref[0, 0, :, c0:c0 + sub])

    @pl.when(k == last)
    def _():
        o_ref[...] = h_ref[...] + scale * _rms(o_ref[...], g_ref[...])


def matmul_norm_res(a, w, l, i, g, h, scale, tk=512, tm=512):
    rows = a.shape[0]
    kdim, d = w.shape[-2:]
    assert kdim % LANE == 0 and tk % LANE == 0 and a.shape[1] >= kdim
    tm = _row_tile(rows, tm)
    n_k = -(-kdim // tk)
    overlap = n_k * tk - kdim
    el = pl.Element

    def k_off(k):
        return pl.multiple_of(jnp.minimum(k * tk, kdim - tk), LANE)

    return pl.pallas_call(
        functools.partial(_matmul_norm_res_kernel, scale=scale, overlap=overlap),
        grid=(rows // tm, n_k),
        in_specs=[pl.BlockSpec((el(tm), el(tk)), lambda r, k: (r * tm, k_off(k))),
                  pl.BlockSpec((el(1), el(1), el(tk), el(d)), lambda r, k: (l, i, k_off(k), 0)),
                  pl.BlockSpec((1, d), lambda r, k: (0, 0)),
                  pl.BlockSpec((tm, d), lambda r, k: (r, 0))],
        out_specs=pl.BlockSpec((tm, d), lambda r, k: (r, 0)),
        out_shape=jax.ShapeDtypeStruct((rows, d), F32),
        compiler_params=_params("arbitrary", "arbitrary"),
        name="matmul_norm_res",
    )(a, w, g.reshape(1, d), h)


RW_R, RW_K, RW_V, RW_QM, RW_W1, RW_V1, RW_A1, RW_G1 = 0, 6, 12, 18, 20, 21, 22, 23
RW_NBLK = 24
RW_STEP_MIXES = ((0,), (0,), (0,), (2,), (2,), (2,), (3,), (3,), (3,), (None,), (1, 3), (4, 5))
RW_FIRST_SPLIT_STEP = 10


def _rwkv_in_kernel(x_ref, sp_ref, g_ref, mu_ref, w_ref, o_ref, last_ref,
                    u_sc, xx_sc, xm_sc, carry_sc, *, tm, t_seq, n_seq, last_tile, last_row):
    ti = pl.program_id(1)
    j = pl.program_id(2)

    @pl.when(j == 0)
    def _():
        @pl.when(ti == 0)
        def _():
            carry_sc[...] = sp_ref[0]

        u = _rms(x_ref[0], g_ref[...])
        rows = lax.broadcasted_iota(jnp.int32, u.shape, 0)
        prev = pltpu.roll(u, 1, axis=0)
        for s in range(n_seq):
            prev = jnp.where(rows == s * t_seq, carry_sc[s:s + 1, :], prev)
        u_sc[...] = u
        xx_sc[...] = prev - u
        if n_seq == 1:
            carry_sc[...] = u[tm - 1:tm, :]

        @pl.when(ti == last_tile)
        def _():
            for s in range(n_seq):
                last_ref[0, s:s + 1, :] = u[s * t_seq + last_row:s * t_seq + last_row + 1, :]

    def set_mix(mix):
        if mix is None:
            xm_sc[...] = u_sc[...].astype(BF16)
        else:
            xm_sc[...] = (u_sc[...] + xx_sc[...] * mu_ref[mix:mix + 1, :]).astype(BF16)

    tn = o_ref.shape[2]
    for step, mixes in enumerate(RW_STEP_MIXES):
        if len(mixes) == 1:
            assert step < RW_FIRST_SPLIT_STEP
            if step == 0 or RW_STEP_MIXES[step - 1] != mixes:
                pl.when(j == step)(functools.partial(set_mix, mixes[0]))
        else:
            assert step >= RW_FIRST_SPLIT_STEP

            @pl.when(j == step)
            def _(mixes=mixes):
                part = tn // len(mixes)
                for q, mix in enumerate(mixes):
                    set_mix(mix)
                    o_ref[0, :, q * part:(q + 1) * part] = _dot(xm_sc[...],
                                                                w_ref[:, q * part:(q + 1) * part])

    @pl.when(j < RW_FIRST_SPLIT_STEP)
    def _():
        o_ref[0] = _dot(xm_sc[...], w_ref[...])


def rwkv_in_proj(h, shift_prev, g, mu, w_cat, t_real, tm=1024):
    b, t, d = h.shape
    n_seq = math.gcd(b, max(1, tm // t))
    bg, tg = b // n_seq, t * n_seq
    tm = _row_tile(tg, tm)
    assert n_seq == 1 or tm == tg
    n_steps = len(RW_STEP_MIXES)
    tn = RW_NBLK * MXU_DIM // n_steps
    last_tile, last_row = ((t_real - 1) // tm, (t_real - 1) % tm) if n_seq == 1 else (0, t_real - 1)
    mu8 = jnp.concatenate([mu, jnp.zeros((SUBLANE - mu.shape[0], d), F32)], axis=0)
    proj, last = pl.pallas_call(
        functools.partial(_rwkv_in_kernel, tm=tm, t_seq=t, n_seq=n_seq, last_tile=last_tile,
                          last_row=last_row),
        grid=(bg, tg // tm, n_steps),
        in_specs=[pl.BlockSpec((1, tm, d), lambda bi, ti, j: (bi, ti, 0)),
                  pl.BlockSpec((1, n_seq, d), lambda bi, ti, j: (bi, 0, 0)),
                  pl.BlockSpec((1, d), lambda bi, ti, j: (0, 0)),
                  pl.BlockSpec((SUBLANE, d), lambda bi, ti, j: (0, 0)),
                  pl.BlockSpec((d, tn), lambda bi, ti, j: (0, j))],
        out_specs=[pl.BlockSpec((1, tm, tn), lambda bi, ti, j: (bi, ti, j)),
                   pl.BlockSpec((1, n_seq, d), lambda bi, ti, j: (bi, 0, 0))],
        out_shape=[jax.ShapeDtypeStruct((bg, tg, n_steps * tn), F32),
                   jax.ShapeDtypeStruct((bg, n_seq, d), F32)],
        scratch_shapes=[pltpu.VMEM((tm, d), F32), pltpu.VMEM((tm, d), F32),
                        pltpu.VMEM((tm, d), BF16), pltpu.VMEM((n_seq, d), F32)],
        compiler_params=_params("arbitrary", "arbitrary", "arbitrary"),
        name="rwkv_in_proj",
    )(h.reshape(bg, tg, d), shift_prev.reshape(bg, n_seq, d), g.reshape(1, d), mu8, w_cat)
    return proj.reshape(b, t, n_steps * tn), last.reshape(b, d)


def _rwkv_scan_kernel(*refs, chunk, n_chunks, has_vres, t_valid, n_hb):
    if has_vres:
        (r_ref, k_ref, v_ref, tw_ref, ta_ref, tg_ref, tv_ref, vf_ref,
         w2_ref, a2_ref, g2_ref, v2_ref, vec_ref, s0_ref, y_ref, sout_ref, s_sc) = refs
    else:
        (r_ref, k_ref, v_ref, tw_ref, ta_ref, tg_ref,
         w2_ref, a2_ref, g2_ref, vec_ref, s0_ref, y_ref, sout_ref, s_sc) = refs
    ti = pl.program_id(2)
    c_len = chunk
    lanes = MXU_DIM
    rows4 = HEAD_BLOCK * c_len

    @pl.when(ti == 0)
    def _():
        s_sc[...] = s0_ref[0]

    li = lax.broadcasted_iota(jnp.int32, (lanes, lanes), 0) // RWKV_HEAD
    lj = lax.broadcasted_iota(jnp.int32, (lanes, lanes), 1) // RWKV_HEAD
    seg = (li == lj).astype(BF16)
    ci = lax.broadcasted_iota(jnp.int32, (c_len, c_len), 0)
    cj = lax.broadcasted_iota(jnp.int32, (c_len, c_len), 1)
    tri_c = (cj <= ci).astype(BF16)

    split2 = _split_hi_lo

    def seg_sums(xs):
        parts = [p for x in xs for p in split2(x)]
        res = _dot(jnp.concatenate(parts, axis=0).astype(BF16), seg)
        return [res[(2 * n) * c_len:(2 * n + 1) * c_len] + res[(2 * n + 1) * c_len:(2 * n + 2) * c_len]
                for n in range(len(xs))]

    lane_head = lax.broadcasted_iota(jnp.int32, (c_len, lanes), 1) // RWKV_HEAD
    ri = lax.broadcasted_iota(jnp.int32, (rows4, 2 * rows4), 0)
    rj = lax.broadcasted_iota(jnp.int32, (rows4, 2 * rows4), 1) & (rows4 - 1)
    strict = rj < ri
    incl = rj <= ri
    ei = lax.broadcasted_iota(jnp.int32, (rows4, rows4), 0)
    ej = lax.broadcasted_iota(jnp.int32, (rows4, rows4), 1)
    eye = (ei == ej).astype(F32)
    n_double = int(math.log2(c_len)) - 1

    def stack(x):
        return jnp.concatenate(
            [jnp.where(lane_head == hh, x, 0.0) for hh in range(HEAD_BLOCK)], axis=0)

    def one_chunk(c, carry):
        sl = pl.ds(pl.multiple_of(c * c_len, c_len), c_len)
        tw_act = jnp.tanh(tw_ref[0, sl, :])
        ta_act = ta_ref[0, sl, :]
        tg_act = _sigmoid(tg_ref[0, sl, :])
        tv_act = tv_ref[0, sl, :] if has_vres else None
        if t_valid is not None:
            t_idx = ti * (n_chunks * c_len) + c * c_len + lax.broadcasted_iota(
                jnp.int32, (c_len, lanes), 0)
            live = t_idx < t_valid
        st = [dict() for _ in range(n_hb)]

        def prep(hb):
            e = st[hb]
            hl = slice(hb * lanes, (hb + 1) * lanes)
            w0, a0, v0 = vec_ref[0:1, hl], vec_ref[1:2, hl], vec_ref[2:3, hl]
            kkw, kaw, rk = vec_ref[3:4, hl], vec_ref[4:5, hl], vec_ref[7:8, hl]
            r = r_ref[0, sl, hl]
            k = k_ref[0, sl, hl]
            v = v_ref[0, sl, hl]
            logw = -_softplus(-(w0 + _dot(tw_act, w2_ref[:, hl]))) - 0.5
            dlog = -jnp.exp(logw)
            rate = _sigmoid(a0 + _dot(ta_act, a2_ref[:, hl]))
            e['gate'] = _dot(tg_act, g2_ref[:, hl])
            if has_vres:
                v = v + (vf_ref[0, sl, hl] - v) * _sigmoid(v0 + _dot(tv_act, v2_ref[:, hl]))
            kk = k * kkw
            k = k * (1.0 + (rate - 1.0) * kaw)
            kk_sq, rk_sum = seg_sums([kk * kk, r * k * rk])
            kk = kk / jnp.maximum(jnp.sqrt(kk_sq), 1e-12)
            if t_valid is not None:
                dlog = jnp.where(live, dlog, 0.0)
                kk = jnp.where(live, kk, 0.0)
                k_live = jnp.where(live, k, 0.0)
            else:
                k_live = k
            d_hi, d_lo = split2(dlog)
            cum2 = _dot(tri_c, jnp.concatenate([d_hi, d_lo], axis=1).astype(BF16))
            cum = cum2[:, :lanes] + cum2[:, lanes:]
            inv = jnp.exp(-cum)
            e['ar'] = jnp.concatenate([stack(-kk * jnp.exp(cum - dlog)), stack(r * jnp.exp(cum))],
                                      axis=0).astype(BF16)
            e['bk'] = jnp.concatenate([stack(kk * rate * inv), stack(k_live * inv)],
                                      axis=0).astype(BF16)
            e['v_s'] = stack(v)
            e['g_end'] = jnp.exp(cum[c_len - 1:c_len, :])
            e['bonus'] = rk_sum * v

        def products(hb):
            e = st[hb]
            e['s_old'] = s_sc[hb]
            big = _dot_nt(e['ar'], jnp.concatenate([e['bk'], e['s_old'].astype(BF16)], axis=0))
            a_bk = jnp.where(strict, big[:rows4, :2 * rows4], 0.0)
            e['r_bk'] = jnp.where(incl, big[rows4:, :2 * rows4], 0.0)
            e['a_s0'], e['r_s0'] = big[:rows4, 2 * rows4:], big[rows4:, 2 * rows4:]
            e['a_k'] = a_bk[:, rows4:]
            lmat = a_bk[:, :rows4]
            e['tinv'] = eye + lmat
            e['lpow'] = lmat.astype(BF16)

        def square(hb):
            e = st[hb]
            e['lpow'] = _dot(e['lpow'], e['lpow']).astype(BF16)

        def double(hb):
            e = st[hb]
            both = _dot(jnp.concatenate([e['lpow'], e['tinv'].astype(BF16)], axis=0), e['lpow'])
            e['tinv'] = e['tinv'] + both[rows4:]
            e['lpow'] = both[:rows4].astype(BF16)

        def solve(hb):
            e = st[hb]
            tinv = e['tinv'] + _dot(e['tinv'].astype(BF16), e['lpow'])
            u = _dot(tinv, e['a_s0'] + _dot(e['a_k'], e['v_s']))
            e['uv'] = jnp.concatenate([u, e['v_s']], axis=0).astype(BF16)

        def outputs(hb):
            e = st[hb]
            y_s = e['r_s0'] + _dot(e['r_bk'].astype(BF16), e['uv'])
            s_sc[hb] = (e['s_old'] + _dot_tn(e['uv'], e['bk'])) * e['g_end']
            y = y_s[0:c_len]
            for hh in range(1, HEAD_BLOCK):
                y = y + y_s[hh * c_len:(hh + 1) * c_len]
            e['y'] = y

        def group_norm(hb):
            e = st[hb]
            hl = slice(hb * lanes, (hb + 1) * lanes)
            y = e['y']
            mean = seg_sums([y])[0] * (1.0 / RWKV_HEAD)
            yc = y - mean
            var = seg_sums([yc * yc])[0] * (1.0 / RWKV_HEAD)
            yn = yc * lax.rsqrt(var + GN_EPS) * vec_ref[5:6, hl] + vec_ref[6:7, hl]
            y_ref[0, sl, hl] = ((yn + e['bonus']) * e['gate']).astype(y_ref.dtype)

        for stage in [prep, products, square] + [double] * (n_double - 1) + [solve, outputs, group_norm]:
            for hb in range(n_hb):
                stage(hb)
        return carry

    lax.fori_loop(0, n_chunks, one_chunk, 0)

    @pl.when(ti == pl.num_programs(2) - 1)
    def _():
        sout_ref[0] = s_sc[...]


def rwkv_scan(proj, v_first_proj, w2, a2, g2, v2, vec, s0_bd, t_real, chunk, tb, n_hb=6):
    b, t, _ = proj.shape
    lanes = MXU_DIM
    tb = min(tb, t)
    assert t % tb == 0 and tb % chunk == 0 and N_HEAD_BLOCKS % n_hb == 0
    has_vres = v_first_proj is not None
    t_valid = None if t_real == t else t_real

    wide = n_hb * lanes

    def col(block0):
        assert block0 % n_hb == 0
        return pl.BlockSpec((1, tb, wide), lambda bi, hg, ti: (bi, ti, block0 // n_hb + hg))

    def fixed(block):
        return pl.BlockSpec((1, tb, lanes), lambda bi, hg, ti: (bi, ti, block))

    def wcol():
        return pl.BlockSpec((LOWRANK_PAD, wide), lambda bi, hg, ti: (0, hg))

    st = pl.BlockSpec((1, n_hb, lanes, lanes), lambda bi, hg, ti: (bi, hg, 0, 0))
    if has_vres:
        in_specs = [col(RW_R), col(RW_K), col(RW_V), fixed(RW_W1), fixed(RW_A1), fixed(RW_G1),
                    fixed(RW_V1), col(RW_V), wcol(), wcol(), wcol(), wcol()]
        args = [proj, proj, proj, proj, proj, proj, proj, v_first_proj, w2, a2, g2, v2]
    else:
        in_specs = [col(RW_R), col(RW_K), col(RW_V), fixed(RW_W1), fixed(RW_A1), fixed(RW_G1),
                    wcol(), wcol(), wcol()]
        args = [proj, proj, proj, proj, proj, proj, w2, a2, g2]
    in_specs += [pl.BlockSpec((SUBLANE, wide), lambda bi, hg, ti: (0, hg)), st]
    args += [vec, s0_bd]
    return pl.pallas_call(
        functools.partial(_rwkv_scan_kernel, chunk=chunk, n_chunks=tb // chunk,
                          has_vres=has_vres, t_valid=t_valid, n_hb=n_hb),
        grid=(b, N_HEAD_BLOCKS // n_hb, t // tb),
        in_specs=in_specs,
        out_specs=[pl.BlockSpec((1, tb, wide), lambda bi, hg, ti: (bi, ti, hg)), st],
        out_shape=[jax.ShapeDtypeStruct((b, t, MAIN_W), BF16),
                   jax.ShapeDtypeStruct(s0_bd.shape, F32)],
        scratch_shapes=[pltpu.VMEM((n_hb, lanes, lanes), F32)],
        compiler_params=_params("arbitrary", "arbitrary", "arbitrary"),
        name="rwkv_scan",
    )(*args)


def _to_block_diag(s):
    b = s.shape[0]
    s = s.reshape(b, N_HEAD_BLOCKS, HEAD_BLOCK, RWKV_HEAD, RWKV_HEAD)
    eye = jnp.eye(HEAD_BLOCK, dtype=s.dtype)
    bd = s[:, :, :, :, None, :] * eye[None, None, :, None, :, None]
    return bd.reshape(b, N_HEAD_BLOCKS, MXU_DIM, MXU_DIM)


def _from_block_diag(bd):
    b = bd.shape[0]
    n = RWKV_HEAD
    x = jnp.stack([bd[:, :, hh * n:(hh + 1) * n, hh * n:(hh + 1) * n] for hh in range(HEAD_BLOCK)],
                  axis=2)
    return x.reshape(b, RWKV_HEADS, RWKV_HEAD, RWKV_HEAD)


def _mem_attn_kernel(q_ref, k_ref, v_ref, o_ref):
    scale = MEM_HEAD_DIM ** -0.5
    for hh in range(MEM_HEADS):
        sl = slice(hh * MEM_HEAD_DIM, (hh + 1) * MEM_HEAD_DIM)
        s = _dot_nt(q_ref[0, :, sl], k_ref[0, :, sl]) * scale
        p = jnp.exp(s - jnp.max(s, axis=-1, keepdims=True))
        p = p / jnp.sum(p, axis=-1, keepdims=True)
        o_ref[0, :, sl] = _dot(p, v_ref[0, :, sl]).astype(o_ref.dtype)


def mem_attend(qsrc, q_block, ksrc, k_block, vsrc, v_block, tq=512):
    b, t, _ = qsrc.shape
    m = ksrc.shape[1]
    tq = _row_tile(t, tq)
    return pl.pallas_call(
        _mem_attn_kernel,
        grid=(b, t // tq),
        in_specs=[pl.BlockSpec((1, tq, MEM_W), lambda bi, ti: (bi, ti, q_block)),
                  pl.BlockSpec((1, m, MEM_W), lambda bi, ti: (bi, 0, k_block)),
                  pl.BlockSpec((1, m, MEM_W), lambda bi, ti: (bi, 0, v_block))],
        out_specs=pl.BlockSpec((1, tq, MEM_W), lambda bi, ti: (bi, ti, 0)),
        out_shape=jax.ShapeDtypeStruct((b, t, MEM_W), BF16),
        compiler_params=_params("arbitrary", "arbitrary"),
        name="mem_attend",
    )(qsrc, ksrc, vsrc)


CMP_PAGE = 128
CMP_CHUNKS = CMP_PAGE // CMP_STRIDE
KV_LANES = NSA_KV_GROUPS * NSA_HEAD_DIM


def _chunk_proj_kernel(pt_ref, *refs, n_pages):
    del pt_ref
    page_refs = refs[:n_pages]
    w_ref, o0_ref, o1_ref, x_sc = refs[n_pages:]
    for kp in range(n_pages):
        for pos in range(CMP_STRIDE):
            x_sc[kp * CMP_CHUNKS:(kp + 1) * CMP_CHUNKS, pos * LANE:(pos + 1) * LANE] = (
                page_refs[kp][pl.ds(pos, CMP_CHUNKS, stride=CMP_STRIDE), :])
    res = _dot(x_sc[...].astype(BF16), w_ref[...])
    o0_ref[0] = res[:, :LANE]
    o1_ref[0] = res[:, LANE:]


def chunk_proj(rows2d, table, w_flat, n_pages, col_block=0):
    b, n_tab = table.shape
    assert n_tab % n_pages == 0

    def page_spec(kp):
        return pl.BlockSpec(
            (CMP_PAGE, LANE),
            lambda bi, pg, g, pt: (pt[bi, pg * n_pages + kp], col_block * NSA_KV_GROUPS + g))

    out_spec = pl.BlockSpec((1, n_pages * CMP_CHUNKS, LANE), lambda bi, pg, g, pt: (bi, pg, g))
    out_shape = jax.ShapeDtypeStruct((b, n_tab * CMP_CHUNKS, KV_LANES), F32)
    grid_spec = pltpu.PrefetchScalarGridSpec(
        num_scalar_prefetch=1,
        grid=(b, n_tab // n_pages, NSA_KV_GROUPS),
        in_specs=[page_spec(kp) for kp in range(n_pages)]
        + [pl.BlockSpec(w_flat.shape, lambda bi, pg, g, pt: (0, 0))],
        out_specs=[out_spec, out_spec],
        scratch_shapes=[pltpu.VMEM((n_pages * CMP_CHUNKS, CMP_STRIDE * LANE), F32)],
    )
    return pl.pallas_call(
        functools.partial(_chunk_proj_kernel, n_pages=n_pages),
        grid_spec=grid_spec,
        out_shape=[out_shape, out_shape],
        compiler_params=_params("arbitrary", "arbitrary", "arbitrary"),
        name="chunk_proj",
    )(table, *([rows2d] * n_pages), w_flat)


def _chunk_proj_pool_kernel(pt_ref, *refs, n_pages):
    del pt_ref
    page_refs = refs[:n_pages]
    w_ref, o0_ref, o1_ref, x_sc = refs[n_pages:]
    for g in range(NSA_KV_GROUPS):
        for kp in range(n_pages):
            for pos in range(CMP_STRIDE):
                x_sc[kp * CMP_CHUNKS:(kp + 1) * CMP_CHUNKS, pos * LANE:(pos + 1) * LANE] = (
                    page_refs[kp][0, pl.ds(pos, CMP_CHUNKS, stride=CMP_STRIDE), g, :])
        res = _dot(x_sc[...].astype(BF16), w_ref[...])
        o0_ref[0, :, g * LANE:(g + 1) * LANE] = res[:, :LANE]
        o1_ref[0, :, g * LANE:(g + 1) * LANE] = res[:, LANE:]


def chunk_proj_pool(pool, table, w_flat, n_pages):
    b, n_tab = table.shape
    assert n_tab % n_pages == 0

    def page_spec(kp):
        return pl.BlockSpec((1, CMP_PAGE, NSA_KV_GROUPS, LANE),
                            lambda bi, pg, pt: (pt[bi, pg * n_pages + kp], 0, 0, 0))

    out_spec = pl.BlockSpec((1, n_pages * CMP_CHUNKS, KV_LANES), lambda bi, pg, pt: (bi, pg, 0))
    out_shape = jax.ShapeDtypeStruct((b, n_tab * CMP_CHUNKS, KV_LANES), F32)
    grid_spec = pltpu.PrefetchScalarGridSpec(
        num_scalar_prefetch=1,
        grid=(b, n_tab // n_pages),
        in_specs=[page_spec(kp) for kp in range(n_pages)]
        + [pl.BlockSpec(w_flat.shape, lambda bi, pg, pt: (0, 0))],
        out_specs=[out_spec, out_spec],
        scratch_shapes=[pltpu.VMEM((n_pages * CMP_CHUNKS, CMP_STRIDE * LANE), F32)],
    )
    return pl.pallas_call(
        functools.partial(_chunk_proj_pool_kernel, n_pages=n_pages),
        grid_spec=grid_spec,
        out_shape=[out_shape, out_shape],
        compiler_params=_params("arbitrary", "arbitrary"),
        name="chunk_proj_pool",
    )(table, *([pool] * n_pages), w_flat)


def _block_mlp_kernel(p0_ref, p1_ref, pe_ref, w1_ref, b1_ref, w2_ref, o_ref):
    const = _dot(pe_ref[...], w1_ref[...])[0:1, :] + b1_ref[...]
    w2 = w2_ref[...]
    for g in range(NSA_KV_GROUPS):
        sl = slice(g * LANE, (g + 1) * LANE)
        hid = const + p0_ref[0, :, sl] + p1_ref[0, :, sl]
        o_ref[0, :, sl] = _dot(_gelu_tanh(hid), w2)


def block_mlp(p0, p1, pe, w1, b1, w2, tb=1024):
    b, nb, _ = p0.shape
    tb = _row_tile(nb, tb)
    pe8 = jnp.concatenate([pe.reshape(1, -1), jnp.zeros((SUBLANE - 1, pe.size), F32)], axis=0)
    spec = pl.BlockSpec((1, tb, KV_LANES), lambda bi, i: (bi, i, 0))
    full = lambda a: pl.BlockSpec(a.shape, lambda bi, i: (0,) * a.ndim)
    b1r = b1.reshape(1, -1)
    return pl.pallas_call(
        _block_mlp_kernel,
        grid=(b, nb // tb),
        in_specs=[spec, spec, full(pe8), full(w1), full(b1r), full(w2)],
        out_specs=spec,
        out_shape=jax.ShapeDtypeStruct((b, nb, KV_LANES), F32),
        compiler_params=_params("arbitrary", "arbitrary"),
        name="block_mlp",
    )(p0, p1, pe8, w1, b1r, w2)


def _w1_flat(w1):
    r = CMP_BLOCK // CMP_STRIDE
    e = w1.shape[1]
    w = w1.reshape(r, CMP_STRIDE, NSA_HEAD_DIM, e)
    return jnp.transpose(w, (1, 2, 0, 3)).reshape(CMP_STRIDE * NSA_HEAD_DIM, r * e).astype(BF16)


def compress(parts, nb, pe, w1, b1, w2):
    return block_mlp(parts[0][:, :nb], parts[1][:, 1:nb + 1], pe, w1, b1, w2)


def _rel_bucket(dist):
    n = jnp.maximum(dist, 0)
    max_exact = REL_BUCKETS // 2
    nf = jnp.maximum(n, 1).astype(F32)
    large = max_exact + (jnp.log(nf / max_exact) / math.log(REL_MAX_DIST / max_exact)
                         * (REL_BUCKETS - max_exact)).astype(jnp.int32)
    return jnp.where(n < max_exact, n, jnp.minimum(large, REL_BUCKETS - 1))


def _bias_lookup(rel_bias, dist):
    tab = rel_bias.astype(F32).reshape(REL_BUCKETS, NSA_KV_GROUPS, NSA_HPG)
    onehot = jax.nn.one_hot(_rel_bucket(dist), REL_BUCKETS, dtype=F32)
    out = jnp.dot(onehot, tab.reshape(REL_BUCKETS, -1), precision=HI)
    out = out.reshape(dist.shape + (NSA_KV_GROUPS, NSA_HPG))
    return jnp.moveaxis(out, (-2, -1), (0, 1))


def _cmp_to_sel(nb_pad, nb, nsb_pad, nsb):
    i = jnp.arange(nb_pad)[:, None]
    j = jnp.arange(nsb_pad)[None, :]
    start = i * CMP_STRIDE
    hit = (start < (j + 1) * SEL_BLOCK) & (start + CMP_BLOCK > j * SEL_BLOCK) & (i < nb) & (j < nsb)
    return hit.astype(F32)


def _nsa_prompt_kernel(q_ref, gt_ref, kc_ref, vc_ref, bc_ref, band_ref, ks_ref, vs_ref,
                       kw_ref, vw_ref, m_ref, o_ref, s_sc, *, nb, nsb):
    tq = NSA_TQ
    rows = NSA_HPG * tq
    i = pl.program_id(2)
    scale = NSA_HEAD_DIM ** -0.5
    qb = q_ref[0]
    q3 = jnp.concatenate([qb[:, hh * LANE:(hh + 1) * LANE] for hh in range(NSA_HPG)],
                         axis=0).astype(BF16)
    row_q = lax.broadcasted_iota(jnp.int32, (rows, LANE), 0) & (tq - 1)
    lane = lax.broadcasted_iota(jnp.int32, (rows, LANE), 1)
    t_pos = i * tq + row_q
    kt_w = NSA_KT
    tiles_per_kt = kt_w // tq
    key_lane = lax.broadcasted_iota(jnp.int32, (rows, kt_w), 1)
    t_pos_k = i * tq + (lax.broadcasted_iota(jnp.int32, (rows, kt_w), 0) & (tq - 1))

    def masked_scores(k_ref, kt, mask_of):
        ksl = pl.ds(pl.multiple_of(kt * kt_w, kt_w), kt_w)
        band = band_ref[0, jnp.clip(i - kt * tiles_per_kt, 0, NSA_BANDS - 1)].reshape(rows, kt_w)
        s = _dot_nt(q3, k_ref[0, ksl, :].astype(BF16)) * scale + band
        return jnp.where(mask_of(t_pos_k - (kt * kt_w + key_lane)), s, MASK_NEG), ksl

    n_win_tiles = WINDOW // kt_w + 1
    win_tiles = []
    m_w = jnp.full((rows, 1), MASK_NEG, F32)
    for jw in range(n_win_tiles):
        kt_raw = i // tiles_per_kt - (n_win_tiles - 1) + jw
        reach = jnp.where(kt_raw >= 0, WINDOW, 0)
        s_w, ksl_w = masked_scores(kw_ref, jnp.maximum(kt_raw, 0),
                                   lambda dist, reach=reach: (dist >= 0) & (dist < reach))
        win_tiles.append((s_w, ksl_w))
        m_w = jnp.maximum(m_w, jnp.max(s_w, axis=-1, keepdims=True))

    s_c = _dot_nt(q3, kc_ref[0].astype(BF16)) * scale + bc_ref[0].reshape(rows, LANE)
    m_c = (t_pos - (lane * CMP_STRIDE + (CMP_BLOCK - 1)) >= 0) & (lane < nb)
    p_c = _masked_softmax(s_c, m_c)
    o_c = _dot(p_c, vc_ref[0])

    p_hi, p_lo = _split_hi_lo(p_c)
    sel_map = m_ref[...].astype(BF16)
    imp3 = _dot_nt(sel_map, p_hi.astype(BF16)) + _dot_nt(sel_map, p_lo.astype(BF16))
    imp = imp3[:, 0:tq]
    for hh in range(1, NSA_HPG):
        imp = imp + imp3[:, hh * tq:(hh + 1) * tq]
    jb = lax.broadcasted_iota(jnp.int32, (nsb, tq), 0)
    cur = (i * tq + lax.broadcasted_iota(jnp.int32, (nsb, tq), 1)) // SEL_BLOCK
    valid = jb <= cur
    forced = valid & ((jb == 0) | (jb > cur - SEL_LOCAL))
    imp = jnp.where(forced, FORCE_SCORE, jnp.where(valid, imp, -FORCE_SCORE))
    rank = jnp.zeros((nsb, tq), jnp.int32)
    for jp in range(nsb):
        other = imp[jp:jp + 1, :]
        rank = rank + ((other > imp) | ((other == imp) & (jb > jp))).astype(jnp.int32)
    sel = (rank < min(SEL_TOPK, nsb)).astype(F32).T
    sel3 = jnp.concatenate([sel] * NSA_HPG, axis=0).astype(BF16)
    blk_row = lax.broadcasted_iota(jnp.int32, (nsb, kt_w), 0)
    blk_lane = lax.broadcasted_iota(jnp.int32, (nsb, kt_w), 1) // SEL_BLOCK

    def finish(m_fin, l_fin, acc):
        return jnp.where(m_fin > MASK_NEG, acc / jnp.where(l_fin > 0, l_fin, 1.0), 0.0)

    l_w = jnp.zeros((rows, 1), F32)
    acc_w = jnp.zeros((rows, LANE), F32)
    for s_w, ksl_w in win_tiles:
        p_w = jnp.exp(s_w - m_w)
        l_w = l_w + jnp.sum(p_w, axis=-1, keepdims=True)
        acc_w = acc_w + _dot(p_w, vw_ref[0, ksl_w, :])
    o_w = finish(m_w, l_w, acc_w)

    n_kt = i // tiles_per_kt + 1

    def scores(kt, m_run):
        expand = (blk_row == kt * (kt_w // SEL_BLOCK) + blk_lane).astype(BF16)
        chosen = _dot(sel3, expand) > 0.5
        s, _ = masked_scores(ks_ref, kt, lambda dist: chosen & (dist >= 0))
        s_sc[kt] = s
        return jnp.maximum(m_run, jnp.max(s, axis=-1, keepdims=True))

    m_s = lax.fori_loop(0, n_kt, scores, jnp.full((rows, 1), MASK_NEG, F32))

    def accum(kt, carry):
        l_run, acc = carry
        ksl = pl.ds(pl.multiple_of(kt * kt_w, kt_w), kt_w)
        p = jnp.exp(s_sc[kt] - m_s)
        return (l_run + jnp.sum(p, axis=-1, keepdims=True), acc + _dot(p, vs_ref[0, ksl, :]))

    l_s, acc_s = lax.fori_loop(0, n_kt, accum, (jnp.zeros((rows, 1), F32),
                                                jnp.zeros((rows, LANE), F32)))
    o_s = finish(m_s, l_s, acc_s)

    gt = _sigmoid(gt_ref[0, 0])
    gcol = lambda br: jnp.concatenate(
        [gt[:, br * NSA_HPG + hh:br * NSA_HPG + hh + 1] for hh in range(NSA_HPG)], axis=0)
    o = gcol(0) * o_c + gcol(1) * o_s + gcol(2) * o_w
    for hh in range(NSA_HPG):
        o_ref[0, :, hh * LANE:(hh + 1) * LANE] = o[hh * tq:(hh + 1) * tq].astype(o_ref.dtype)


def nsa_prompt(proj, gates_t, kc, vc, bias_c, band, side, sel_map, nb, nsb):
    b, t, _ = proj.shape
    tq = NSA_TQ
    qw = NSA_HPG * LANE
    kv = lambda off: pl.BlockSpec((1, t, LANE), lambda bi, g, i: (bi, 0, off * NSA_KV_GROUPS + g))
    return pl.pallas_call(
        functools.partial(_nsa_prompt_kernel, nb=nb, nsb=nsb),
        grid=(b, NSA_KV_GROUPS, t // tq),
        in_specs=[pl.BlockSpec((1, tq, qw), lambda bi, g, i: (bi, i, g)),
                  pl.BlockSpec((1, 1, tq, NSA_HPG * 3), lambda bi, g, i: (bi, g, i, 0)),
                  pl.BlockSpec((1, LANE, LANE), lambda bi, g, i: (bi, 0, g)),
                  pl.BlockSpec((1, LANE, LANE), lambda bi, g, i: (bi, 0, g)),
                  pl.BlockSpec((1, NSA_HPG, tq, LANE), lambda bi, g, i: (g, 0, i, 0)),
                  pl.BlockSpec((1, NSA_BANDS, NSA_HPG, tq, NSA_KT), lambda bi, g, i: (g, 0, 0, 0, 0)),
                  kv(2), kv(3), kv(4), kv(5),
                  pl.BlockSpec(sel_map.shape, lambda bi, g, i: (0, 0))],
        out_specs=pl.BlockSpec((1, tq, qw), lambda bi, g, i: (bi, i, g)),
        out_shape=jax.ShapeDtypeStruct((b, t, MAIN_W), BF16),
        scratch_shapes=[pltpu.VMEM((t // NSA_KT, NSA_HPG * tq, NSA_KT), F32)],
        compiler_params=_params("arbitrary", "arbitrary", "arbitrary"),
        name="nsa_prompt",
    )(proj, gates_t, kc, vc, bias_c, band, side, side, side, side, sel_map)


def _nsa_dec_cmp_kernel(q_ref, kc_ref, vc_ref, bc_ref, m_ref, oc_ref, idx_ref, *, nb, nsb, t_pos):
    scale = NSA_HEAD_DIM ** -0.5
    nbp = kc_ref.shape[1]
    nsp = m_ref.shape[1]
    n_sel = min(SEL_TOPK, nsb)
    lane_b = lax.broadcasted_iota(jnp.int32, (SUBLANE, nbp), 1)
    m_c = (t_pos - (lane_b * CMP_STRIDE + (CMP_BLOCK - 1)) >= 0) & (lane_b < nb)
    row_s = lax.broadcasted_iota(jnp.int32, (SUBLANE, nsp), 0)
    jb = lax.broadcasted_iota(jnp.int32, (1, nsp), 1)
    cur = t_pos // SEL_BLOCK
    valid = jb <= cur
    forced = valid & ((jb == 0) | (jb > cur - SEL_LOCAL))
    out_lane = lax.broadcasted_iota(jnp.int32, (1, LANE), 1)
    idx_rows = []
    for g in range(NSA_KV_GROUPS):
        q3 = jnp.concatenate(
            [q_ref[0, 0:1, (g * NSA_HPG + hh) * LANE:(g * NSA_HPG + hh + 1) * LANE]
             for hh in range(NSA_HPG)] + [jnp.zeros((SUBLANE - NSA_HPG, LANE), F32)], axis=0)
        sl = slice(g * LANE, (g + 1) * LANE)
        s_c = _dot_nt(q3, kc_ref[0, :, sl]) * scale + bc_ref[g]
        p_c = _masked_softmax(s_c, m_c)
        oc_ref[0, g] = _dot(p_c, vc_ref[0, :, sl])
        imp8 = jnp.where(row_s < NSA_HPG, _dot(p_c, m_ref[...], HI), 0.0)
        imp = jnp.sum(imp8, axis=0, keepdims=True)
        imp = jnp.where(forced, FORCE_SCORE, jnp.where(valid, imp, -FORCE_SCORE))
        imp = jnp.where(jb < nsb, imp, -jnp.inf)
        jbf = jb.astype(F32)
        picks = jnp.zeros((1, LANE), F32)
        for kk in range(n_sel):
            best = jnp.max(imp, axis=-1, keepdims=True)
            arg = jnp.min(jnp.where(imp == best, jbf, float(nsp)), axis=-1, keepdims=True)
            picks = jnp.where(out_lane == kk, arg, picks)
            imp = jnp.where(jbf == arg, -jnp.inf, imp)
        idx_rows.append(picks.astype(jnp.int32))
    idx_rows.append(jnp.zeros((SUBLANE - NSA_KV_GROUPS, LANE), jnp.int32))
    idx_ref[0] = jnp.concatenate(idx_rows, axis=0)


def nsa_dec_cmp(proj, kc, vc, bias_c, sel_map, nb, nsb, t_pos):
    b, tp, _ = proj.shape
    nbp = kc.shape[1]
    return pl.pallas_call(
        functools.partial(_nsa_dec_cmp_kernel, nb=nb, nsb=nsb, t_pos=t_pos),
        grid=(b,),
        in_specs=[pl.BlockSpec((1, tp, MAIN_W), lambda bi: (bi, 0, 0)),
                  pl.BlockSpec((1, nbp, KV_LANES), lambda bi: (bi, 0, 0)),
                  pl.BlockSpec((1, nbp, KV_LANES), lambda bi: (bi, 0, 0)),
                  pl.BlockSpec(bias_c.shape, lambda bi: (0, 0, 0)),
                  pl.BlockSpec(sel_map.shape, lambda bi: (0, 0))],
        out_specs=[pl.BlockSpec((1, NSA_KV_GROUPS, SUBLANE, LANE), lambda bi: (bi, 0, 0, 0)),
                   pl.BlockSpec((1, SUBLANE, LANE), lambda bi: (bi, 0, 0))],
        out_shape=[jax.ShapeDtypeStruct((b, NSA_KV_GROUPS, SUBLANE, LANE), F32),
                   jax.ShapeDtypeStruct((b, SUBLANE, LANE), jnp.int32)],
        compiler_params=_params("arbitrary"),
        name="nsa_dec_cmp",
    )(proj, kc, vc, bias_c, sel_map)


def _nsa_dec_sel_kernel(phys_ref, isnew_ref, *refs, n_sel, n_win, t_pos, win_start):
    del phys_ref
    kb_refs = refs[:n_sel]
    vb_refs = refs[n_sel:2 * n_sel]
    (q_ref, new_ref, ds_ref, bs_ref, wk_ref, wv_ref, bw_ref, oc_ref, gt_ref, o_ref) = refs[2 * n_sel:]
    bi = pl.program_id(0)
    g = pl.program_id(1)
    scale = NSA_HEAD_DIM ** -0.5
    q3 = q_ref[0, 0]
    new_rows = new_ref[0, 0]
    pad_blk = jnp.zeros((SEL_BLOCK - 1, LANE), F32)
    new_k = jnp.concatenate([new_rows[0:1], pad_blk], axis=0)
    new_v = jnp.concatenate([new_rows[1:2], pad_blk], axis=0)
    ks, vs = [], []
    for kk in range(n_sel):
        fresh = isnew_ref[bi, g, kk] > 0
        ks.append(jnp.where(fresh, new_k, kb_refs[kk][0, :, g, :]))
        vs.append(jnp.where(fresh, new_v, vb_refs[kk][0, :, g, :]))
    ks = jnp.concatenate(ks, axis=0)
    vs = jnp.concatenate(vs, axis=0)
    s_s = _dot_nt(q3, ks) * scale + bs_ref[0, 0]
    p_s = _masked_softmax(s_s, ds_ref[0, 0] >= 0)
    o_s = _dot(p_s, vs)

    pad_w = jnp.zeros((SUBLANE - 1, LANE), F32)
    kw = jnp.concatenate([wk_ref[0, :, g, :], new_rows[2:3], pad_w], axis=0)
    vw = jnp.concatenate([wv_ref[0, :, g, :], new_rows[3:4], pad_w], axis=0)
    nw = kw.shape[0]
    pos = win_start + lax.broadcasted_iota(jnp.int32, (SUBLANE, nw), 1)
    d_w = t_pos - pos
    m_w = (d_w >= 0) & (d_w < WINDOW) & (pos >= 0) & (pos - win_start < n_win)
    s_w = _dot_nt(q3, kw) * scale + bw_ref[0]
    p_w = _masked_softmax(s_w, m_w)
    o_w = _dot(p_w, vw)

    gt = _sigmoid(gt_ref[0, 0])
    o_ref[0, 0] = gt[:, 0:1] * oc_ref[0, 0] + gt[:, 1:2] * o_s + gt[:, 2:3] * o_w


def nsa_dec_sel(phys, isnew, pool_k, pool_v, q8, new_rows, d_s, bias_s, win_k, win_v, bias_w,
                o_c, gates8, n_win, t_pos, win_start):
    b = q8.shape[0]
    n_sel = phys.shape[-1]
    n_keys = n_sel * SEL_BLOCK
    nw = win_k.shape[1]
    nwp = nw + SUBLANE

    def blk_spec(kk):
        return pl.BlockSpec((1, SEL_BLOCK, NSA_KV_GROUPS, LANE),
                            lambda bi, g, ph, nf: (ph[bi, g, kk], 0, 0, 0))

    per_bg = lambda *shape: pl.BlockSpec((1, 1) + shape, lambda bi, g, ph, nf: (bi, g) + (0,) * len(shape))
    grid_spec = pltpu.PrefetchScalarGridSpec(
        num_scalar_prefetch=2,
        grid=(b, NSA_KV_GROUPS),
        in_specs=[blk_spec(kk) for kk in range(n_sel)] + [blk_spec(kk) for kk in range(n_sel)]
        + [per_bg(SUBLANE, LANE), per_bg(SUBLANE, LANE), per_bg(1, n_keys), per_bg(SUBLANE, n_keys),
           pl.BlockSpec((1, nw, NSA_KV_GROUPS, LANE), lambda bi, g, ph, nf: (bi, 0, 0, 0)),
           pl.BlockSpec((1, nw, NSA_KV_GROUPS, LANE), lambda bi, g, ph, nf: (bi, 0, 0, 0)),
           pl.BlockSpec((1, SUBLANE, nwp), lambda bi, g, ph, nf: (g, 0, 0)),
           per_bg(SUBLANE, LANE), per_bg(SUBLANE, SUBLANE)],
        out_specs=per_bg(SUBLANE, LANE),
    )
    return pl.pallas_call(
        functools.partial(_nsa_dec_sel_kernel, n_sel=n_sel, n_win=nw + 1, t_pos=t_pos,
                          win_start=win_start),
        grid_spec=grid_spec,
        out_shape=jax.ShapeDtypeStruct((b, NSA_KV_GROUPS, SUBLANE, LANE), F32),
        compiler_params=_params("arbitrary", "arbitrary"),
        name="nsa_dec_sel",
    )(phys, isnew, *([pool_k] * n_sel), *([pool_v] * n_sel), q8, new_rows, d_s, bias_s,
      win_k, win_v, bias_w, o_c, gates8)


def _pad_cols(w, n):
    return jnp.pad(w, ((0, 0), (0, n - w.shape[1])))


def _pad_rows(w, n):
    return jnp.pad(w, ((0, n - w.shape[0]), (0, 0)))


def _prep_weights(P):
    W = {}
    W['w_gu'] = P['ffn_gu'].astype(BF16)
    W['w_d'] = P['ffn_d'].astype(BF16)
    W['w_out'] = P['w_out'].astype(BF16)[None]
    W['rw_in'], W['rw_w2'], W['rw_a2'], W['rw_g2'], W['rw_v2'], W['rw_vec'] = [], [], [], [], [], []
    lp = LOWRANK_PAD
    for l in range(N_A):
        w_in = P['w_in_a'][l]
        v1 = P['rw_v1'][l - 1] if l > 0 else jnp.zeros((D_MODEL, lp), F32)
        W['rw_in'].append(jnp.concatenate([
            w_in, _pad_cols(P['rw_w1'][l], lp), _pad_cols(v1, lp), _pad_cols(P['rw_a1'][l], lp),
            _pad_cols(P['rw_g1'][l], lp)], axis=1).astype(BF16))
        W['rw_w2'].append(_pad_rows(P['rw_w2'][l], lp))
        W['rw_a2'].append(_pad_rows(P['rw_a2'][l], lp))
        W['rw_g2'].append(_pad_rows(P['rw_g2'][l], lp))
        W['rw_v2'].append(_pad_rows(P['rw_v2'][l - 1], lp) if l > 0 else None)
        v0 = P['rw_v0'][l - 1] if l > 0 else jnp.zeros((MAIN_W,), F32)
        W['rw_vec'].append(jnp.stack([P['rw_w0'][l], P['rw_a0'][l], v0, P['rw_kk'][l], P['rw_ka'][l],
                                      P['rw_lnw'][l], P['rw_lnb'][l], P['rw_rk'][l].reshape(-1)]))
    W['w_in_b'] = []
    for l in range(DEPTH - N_A):
        w = P['w_in_b'][l]
        W['w_in_b'].append(jnp.concatenate([
            w[:, :MAIN_W], w[:, MAIN_W + GATE_W:], _pad_cols(w[:, MAIN_W:MAIN_W + GATE_W], MXU_DIM)],
            axis=1).astype(BF16))
    W['w_kv'] = P['w_kv'].astype(BF16)
    W['w_mem_kv'] = [P['w_mem_kv'][l].astype(BF16) for l in range(DEPTH)]
    W['cmp_w1_flat'] = [_w1_flat(P['cmp_w1'][c]) for c in range(2)]
    return W


B_QM_BLOCK = MAIN_W // MEM_W
B_GATE_OFF = MAIN_W + MEM_W


def _ffn(h2, norms_l, first, W, l, i):
    hid = norm_swiglu(h2, norms_l[first], W['w_gu'], l, i, tm=1024)
    return matmul_norm_res(hid, W['w_d'], l, i, norms_l[first + 1], h2, 0.5, tk=FFN_DOWN_TK)


def _trunk(x, t_real, P, W, shift0, wkv0_bd, mem_src, make_side, attend, chunk, tb):
    b, t, d = x.shape
    h2 = x.reshape(b * t, d)
    shifts, states = [], []
    v_first_proj, ctx, side_state = None, None, None
    for l in range(DEPTH):
        n = P['norms'][l]
        h2 = _ffn(h2, n, 0, W, l, 0)
        if l < N_A:
            proj, last = rwkv_in_proj(h2.reshape(b, t, d), shift0[l], n[2], P['rw_mu'][l],
                                      W['rw_in'][l], t_real)
            main, s_bd = rwkv_scan(proj, v_first_proj if l > 0 else None, W['rw_w2'][l], W['rw_a2'][l],
                                   W['rw_g2'][l], W['rw_v2'][l], W['rw_vec'][l], wkv0_bd[l],
                                   t_real, chunk, tb)
            if l == 0:
                v_first_proj = proj
            shifts.append(last)
            states.append(_from_block_diag(s_bd))
            q_src, q_block = proj, RW_QM * MXU_DIM // MEM_W
        else:
            proj = norm_matmul(h2, n[2], W['w_in_b'][l - N_A], tn=768).reshape(b, t, -1)
            main = attend(proj, ctx)
            q_src, q_block = proj, B_QM_BLOCK
        mk, kb, mv, vb = mem_src(l)
        mo = mem_attend(q_src, q_block, mk, kb, mv, vb)
        mix = jnp.concatenate([main, mo], axis=-1).reshape(b * t, d)
        h2 = matmul_norm_res(mix, W['w_out'], 0, l, n[3], h2, 1.0, tk=2048)
        h2 = _ffn(h2, n, 4, W, l, 1)
        if l == N_A - 1:
            side = norm_matmul(h2, P['kv_norm'], W['w_kv'], tn=768).reshape(b, t, -1)
            ctx, side_state = make_side(side)
    return h2.reshape(b, t, d), jnp.stack(shifts), jnp.stack(states), side_state


def kernel(x_prompt, x_sample, mem_prompt, state_wkv, state_shift, cache_mem_k, cache_mem_v,
           cache_cmp_k, cache_cmp_v, cache_slc_k, cache_slc_v, cache_win_k, cache_win_v, page_table,
           norms, ffn_gu, ffn_d, w_in_a, w_in_b, w_out, mem_norm, w_mem_kv, kv_norm, w_kv,
           cmp_pe, cmp_w1, cmp_b1, cmp_w2, rel_bias,
           rw_mu, rw_w0, rw_w1, rw_w2, rw_a0, rw_a1, rw_a2, rw_g1, rw_g2, rw_v0, rw_v1, rw_v2,
           rw_kk, rw_ka, rw_rk, rw_lnw, rw_lnb):
    P = dict(norms=norms, ffn_gu=ffn_gu, ffn_d=ffn_d, w_in_a=w_in_a, w_in_b=w_in_b, w_out=w_out,
             kv_norm=kv_norm, w_kv=w_kv, w_mem_kv=w_mem_kv, cmp_w1=cmp_w1, rw_mu=rw_mu, rw_w0=rw_w0,
             rw_w1=rw_w1, rw_w2=rw_w2, rw_a0=rw_a0, rw_a1=rw_a1, rw_a2=rw_a2, rw_g1=rw_g1,
             rw_g2=rw_g2, rw_v0=rw_v0, rw_v1=rw_v1, rw_v2=rw_v2, rw_kk=rw_kk, rw_ka=rw_ka,
             rw_rk=rw_rk, rw_lnw=rw_lnw, rw_lnb=rw_lnb)
    W = _prep_weights(P)
    G, dh = NSA_KV_GROUPS, NSA_HEAD_DIM
    split_side = lambda side, bx, t: [side[:, :t, c * KV_LANES:(c + 1) * KV_LANES].reshape(bx, t, G, dh)
                                      for c in range(6)]

    def cmp_mlp(parts, nb, c):
        return compress(parts, nb, cmp_pe[c], cmp_w1[c], cmp_b1[c], cmp_w2[c])

    bp, tp, d = x_prompt.shape
    n_mem = mem_prompt.shape[1]
    mem2 = mem_prompt.reshape(bp * n_mem, d)
    p_mkv = [norm_matmul(mem2, mem_norm[l], W['w_mem_kv'][l], tn=512).reshape(bp, n_mem, 2 * MEM_W)
             for l in range(DEPTH)]
    p_mem_k = jnp.stack([m[..., :MEM_W].reshape(bp, n_mem, MEM_HEADS, MEM_HEAD_DIM) for m in p_mkv])
    p_mem_v = jnp.stack([m[..., MEM_W:].reshape(bp, n_mem, MEM_HEADS, MEM_HEAD_DIM) for m in p_mkv])

    nb_p = tp // CMP_STRIDE - 1
    nsb_p = tp // SEL_BLOCK
    nq_tiles = tp // NSA_TQ
    t_all = jnp.arange(tp)
    c_end = jnp.arange(LANE) * CMP_STRIDE + (CMP_BLOCK - 1)
    assert nb_p <= LANE
    bias_c_p = _bias_lookup(rel_bias, t_all[:, None] - c_end[None, :])
    ii = jnp.arange(NSA_TQ)
    cc = jnp.arange(NSA_KT)
    band_p = jnp.stack([_bias_lookup(rel_bias, dd * NSA_TQ + ii[:, None] - cc[None, :])
                        for dd in range(NSA_BANDS)], axis=1)
    assert (NSA_BANDS - 1) * NSA_TQ - (NSA_KT - 1) >= REL_MAX_DIST and tp % NSA_KT == 0
    sel_map_p = _cmp_to_sel(LANE, nb_p, nsb_p, nsb_p).T
    ident = jnp.arange(bp * tp // CMP_PAGE, dtype=jnp.int32).reshape(bp, tp // CMP_PAGE)

    def prompt_side(side):
        rows = side.reshape(bp * tp, -1)
        kc = cmp_mlp(chunk_proj(rows, ident, W['cmp_w1_flat'][0], tp // CMP_PAGE, 0), nb_p, 0)
        vc = cmp_mlp(chunk_proj(rows, ident, W['cmp_w1_flat'][1], tp // CMP_PAGE, 1), nb_p, 1)
        padb = ((0, 0), (0, LANE - nb_p), (0, 0))
        wb = min(WINDOW, tp)
        kc_r, vc_r, ks, vs, kw, vw = split_side(side, bp, tp)
        return ((jnp.pad(kc, padb), jnp.pad(vc, padb), side),
                (kc_r, vc_r, ks, vs, kw[:, tp - wb:], vw[:, tp - wb:]))

    def prompt_attend(proj, ctx):
        kc, vc, side = ctx
        gates = proj[..., B_GATE_OFF:B_GATE_OFF + GATE_W].reshape(bp, tp, 3, G, NSA_HPG)
        gates_t = jnp.transpose(gates, (0, 3, 1, 2, 4)).reshape(bp, G, tp, 3 * NSA_HPG)
        return nsa_prompt(proj, gates_t, kc, vc, bias_c_p, band_p, side, sel_map_p, nb_p, nsb_p)

    zeros_shift = jnp.zeros((N_A, bp, d), F32)
    zeros_state = jnp.zeros((N_A, bp, N_HEAD_BLOCKS, MXU_DIM, MXU_DIM), F32)
    y_prompt, p_shift, p_wkv, p_side = _trunk(
        x_prompt, tp, P, W, zeros_shift, zeros_state,
        lambda l: (p_mkv[l], 0, p_mkv[l], 1), prompt_side, prompt_attend, chunk=64, tb=256)
    p_cmp_k, p_cmp_v, p_slc_k, p_slc_v, p_win_k, p_win_v = p_side

    bd, s_new, _ = x_sample.shape
    assert s_new == 1
    ts = SUBLANE
    xs = jnp.pad(x_sample, ((0, 0), (0, ts - s_new), (0, 0)))
    past_len = page_table.shape[1] * CMP_PAGE
    n_past_blk = past_len // SEL_BLOCK
    blk_per_page = CMP_PAGE // SEL_BLOCK
    nsb_s = n_past_blk + 1
    t_pos = past_len
    nc_s = -(-(past_len + s_new) // CMP_STRIDE)
    nb_s = nc_s - 1
    wb_s = cache_win_k.shape[1]
    win_start = past_len - wb_s
    mem_k2 = cache_mem_k.reshape(DEPTH, bd, n_mem, MEM_W)
    mem_v2 = cache_mem_v.reshape(DEPTH, bd, n_mem, MEM_W)
    nsp = -(-nsb_s // LANE) * LANE
    sel_map_s = _cmp_to_sel(nb_s, nb_s, nsp, nsb_s)
    c_end_s = jnp.arange(nb_s) * CMP_STRIDE + (CMP_BLOCK - 1)
    bias_c_s = _bias_lookup(rel_bias, t_pos - c_end_s)
    bias_c_s = jnp.pad(bias_c_s, ((0, 0), (0, SUBLANE - NSA_HPG), (0, 0)))
    nwp = wb_s + SUBLANE
    bias_w_s = _bias_lookup(rel_bias, t_pos - (win_start + jnp.arange(nwp)))
    bias_w_s = jnp.pad(bias_w_s, ((0, 0), (0, SUBLANE - NSA_HPG), (0, 0)))
    ident_s = jnp.arange(bd, dtype=jnp.int32).reshape(bd, 1)
    pages_per_step = math.gcd(16, page_table.shape[1])

    def sample_side(side):
        new = side[:, :s_new]
        parts = []
        for c, pool in enumerate((cache_cmp_k, cache_cmp_v)):
            past = chunk_proj_pool(pool, page_table, W['cmp_w1_flat'][c], pages_per_step)
            fresh_page = jnp.pad(new[:, :, c * KV_LANES:(c + 1) * KV_LANES],
                                 ((0, 0), (0, CMP_PAGE - s_new), (0, 0))).reshape(bd, CMP_PAGE, G, dh)
            fresh = chunk_proj_pool(fresh_page, ident_s, W['cmp_w1_flat'][c], 1)
            n_fresh = nc_s - past[0].shape[1]
            parts.append([jnp.concatenate([p, f[:, :n_fresh]], axis=1) for p, f in zip(past, fresh)])
        kc = cmp_mlp(parts[0], nb_s, 0)
        vc = cmp_mlp(parts[1], nb_s, 1)
        kc_n, vc_n, ks_n, vs_n, kw_n, vw_n = split_side(side, bd, s_new)
        s_win_k = jnp.concatenate([cache_win_k, kw_n], axis=1)[:, s_new:]
        s_win_v = jnp.concatenate([cache_win_v, vw_n], axis=1)[:, s_new:]
        return (kc, vc, new), (kc_n, vc_n, ks_n, vs_n, s_win_k, s_win_v)

    def sample_attend(proj, ctx):
        kc, vc, new = ctx
        o_c, idx8 = nsa_dec_cmp(proj, kc, vc, bias_c_s, sel_map_s, nb_s, nsb_s, t_pos)
        idx = idx8[:, :G, :SEL_TOPK]
        is_new = idx >= n_past_blk
        jp = jnp.minimum(idx, n_past_blk - 1)
        phys = (jnp.take_along_axis(page_table[:, None, :], jp // blk_per_page, axis=2) * blk_per_page
                + jp % blk_per_page)
        k_pos = (idx[..., None] * SEL_BLOCK + jnp.arange(SEL_BLOCK)).reshape(bd, G, 1, -1)
        d_s = t_pos - k_pos
        tab = rel_bias.astype(F32).reshape(REL_BUCKETS, G, NSA_HPG)
        onehot = jax.nn.one_hot(_rel_bucket(d_s[:, :, 0]), REL_BUCKETS, dtype=F32)
        bias_s = jnp.einsum('bgkn,ngh->bghk', onehot, tab, precision=HI)
        bias_s = jnp.pad(bias_s, ((0, 0), (0, 0), (0, SUBLANE - NSA_HPG), (0, 0)))
        q8 = jnp.pad(proj[:, 0, :MAIN_W].reshape(bd, G, NSA_HPG, dh),
                     ((0, 0), (0, 0), (0, SUBLANE - NSA_HPG), (0, 0)))
        new_rows = jnp.pad(jnp.transpose(new[:, 0, 2 * KV_LANES:].reshape(bd, 4, G, dh), (0, 2, 1, 3)),
                           ((0, 0), (0, 0), (0, SUBLANE - 4), (0, 0)))
        gates = proj[:, 0, B_GATE_OFF:B_GATE_OFF + GATE_W].reshape(bd, 3, G, NSA_HPG)
        gates8 = jnp.pad(jnp.transpose(gates, (0, 2, 3, 1)),
                         ((0, 0), (0, 0), (0, SUBLANE - NSA_HPG), (0, SUBLANE - 3)))
        o = nsa_dec_sel(phys.astype(jnp.int32), is_new.astype(jnp.int32),
                        cache_slc_k.reshape(-1, SEL_BLOCK, G, dh),
                        cache_slc_v.reshape(-1, SEL_BLOCK, G, dh),
                        q8, new_rows, d_s.astype(jnp.int32), bias_s,
                        cache_win_k, cache_win_v,
                        bias_w_s, o_c, gates8, wb_s, t_pos, win_start)
        main = o[:, :, :NSA_HPG].reshape(bd, 1, MAIN_W)
        return jnp.pad(main, ((0, 0), (0, ts - 1), (0, 0))).astype(BF16)

    y_s, s_shift, s_wkv, s_side = _trunk(
        xs, s_new, P, W, state_shift, jnp.stack([_to_block_diag(state_wkv[l]) for l in range(N_A)]),
        lambda l: (mem_k2[l], 0, mem_v2[l], 0), sample_side, sample_attend, chunk=SUBLANE, tb=SUBLANE)
    y_sample = y_s[:, :s_new]
    s_cmp_k, s_cmp_v, s_slc_k, s_slc_v, s_win_k, s_win_v = s_side

    return (y_prompt, y_sample, p_mem_k, p_mem_v, p_wkv, p_shift,
            p_cmp_k, p_cmp_v, p_slc_k, p_slc_v, p_win_k, p_win_v,
            s_wkv, s_shift, s_cmp_k, s_cmp_v, s_slc_k, s_slc_v, s_win_k, s_win_v)
```

```python
import functools
import math

import jax
import jax.numpy as jnp
from jax import lax
from jax.experimental import pallas as pl
from jax.experimental.pallas import tpu as pltpu

F32 = jnp.float32
BF16 = jnp.bfloat16
HI = lax.Precision.HIGHEST

D_MODEL = 2048
DEPTH = 4
N_A = 2
MEM_HEADS = 4
MEM_HEAD_DIM = 128
MEM_W = 512
MAIN_W = 1536
RWKV_HEAD = 64
RWKV_HEADS = 24
GN_EPS = 64e-5
NSA_HEAD_DIM = 128
NSA_Q_HEADS = 12
NSA_KV_GROUPS = 4
NSA_HPG = 3
GATE_W = 36
CMP_BLOCK = 32
CMP_STRIDE = 16
SEL_BLOCK = 64
SEL_TOPK = 16
SEL_LOCAL = 2
WINDOW = 512
REL_BUCKETS = 32
REL_MAX_DIST = 128
D_FF = 5504
NORM_EPS = 1e-6
MASK_NEG = -1e30
FORCE_SCORE = 1e9

LANE = 128
SUBLANE = 8
MXU_DIM = 256
VMEM_LIMIT = 56 * 1024 * 1024

FFN_DOWN_TK = 11 * LANE
HEAD_BLOCK = MXU_DIM // RWKV_HEAD
N_HEAD_BLOCKS = RWKV_HEADS // HEAD_BLOCK
LOWRANK_PAD = MXU_DIM
NSA_TQ = 256
NSA_KT = 256
NSA_BANDS = 3


def _params(*sem):
    return pltpu.CompilerParams(dimension_semantics=sem, vmem_limit_bytes=VMEM_LIMIT)


def _rms(x, g):
    return x * lax.rsqrt(jnp.mean(x * x, axis=-1, keepdims=True) + NORM_EPS) * g


def _sigmoid(x):
    return 1.0 / (1.0 + jnp.exp(-x))


def _softplus(x):
    return jnp.maximum(x, 0.0) + jnp.log(1.0 + jnp.exp(-jnp.abs(x)))


def _gelu_tanh(x):
    return 0.5 * x * (1.0 + jnp.tanh(math.sqrt(2.0 / math.pi) * (x + 0.044715 * x * x * x)))


def _dot(a, b, precision=None):
    return jnp.dot(a, b, preferred_element_type=F32, precision=precision)


def _dot_nt(a, b, precision=None):
    return lax.dot_general(a, b, (((1,), (1,)), ((), ())), preferred_element_type=F32,
                           precision=precision)


def _dot_tn(a, b, precision=None):
    return lax.dot_general(a, b, (((0,), (0,)), ((), ())), preferred_element_type=F32,
                           precision=precision)


BF16_BITS_OF_F32 = 0xFFFF0000


def _split_hi_lo(x):
    bits = lax.bitcast_convert_type(x, jnp.uint32) & jnp.uint32(BF16_BITS_OF_F32)
    hi = lax.bitcast_convert_type(bits, F32)
    return hi, x - hi


def _masked_softmax(s, mask):
    s = jnp.where(mask, s, MASK_NEG)
    p = jnp.exp(s - jnp.max(s, axis=-1, keepdims=True)) * mask.astype(F32)
    den = jnp.sum(p, axis=-1, keepdims=True)
    return p / jnp.where(den > 0, den, 1.0)


def _row_tile(rows, target):
    t = min(rows, target)
    while rows % t:
        t -= SUBLANE
    return t


def _norm_matmul_kernel(x_ref, g_ref, w_ref, o_ref, xn_ref):
    @pl.when(pl.program_id(1) == 0)
    def _():
        xn_ref[...] = _rms(x_ref[...], g_ref[...]).astype(BF16)

    o_ref[...] = _dot(xn_ref[...], w_ref[...]).astype(o_ref.dtype)


def norm_matmul(x, g, w, tn, out_dtype=F32, tm=512):
    rows, d = x.shape
    n = w.shape[1]
    tm = _row_tile(rows, tm)
    assert n % tn == 0
    return pl.pallas_call(
        _norm_matmul_kernel,
        grid=(rows // tm, n // tn),
        in_specs=[pl.BlockSpec((tm, d), lambda i, j: (i, 0)),
                  pl.BlockSpec((1, d), lambda i, j: (0, 0)),
                  pl.BlockSpec((d, tn), lambda i, j: (0, j))],
        out_specs=pl.BlockSpec((tm, tn), lambda i, j: (i, j)),
        out_shape=jax.ShapeDtypeStruct((rows, n), out_dtype),
        scratch_shapes=[pltpu.VMEM((tm, d), BF16)],
        compiler_params=_params("arbitrary", "arbitrary"),
        name="norm_matmul",
    )(x, g.reshape(1, d), w)


def _norm_swiglu_kernel(x_ref, g_ref, wg_ref, wu_ref, o_ref, xn_ref, *, tail):
    j = pl.program_id(1)
    last = pl.num_programs(1) - 1

    @pl.when(j == 0)
    def _():
        xn_ref[...] = _rms(x_ref[...], g_ref[...]).astype(BF16)

    xn = xn_ref[...]
    tn = o_ref.shape[1]
    sub = min(tn, MXU_DIM)

    def tile(up_shift):
        c0 = 0
        while c0 < tn - up_shift:
            w = min(sub, tn - up_shift - c0)
            gate = _dot(xn, wg_ref[0, 0, :, c0:c0 + w])
            up = _dot(xn, wu_ref[0, 0, :, c0 + up_shift:c0 + up_shift + w])
            o_ref[:, c0:c0 + w] = (gate * _sigmoid(gate) * up).astype(o_ref.dtype)
            c0 += w
        if up_shift:
            o_ref[:, c0:] = jnp.zeros((o_ref.shape[0], up_shift), o_ref.dtype)

    if tail == tn:
        tile(0)
    else:
        pl.when(j != last)(lambda: tile(0))
        pl.when(j == last)(lambda: tile(tn - tail))


def norm_swiglu(x, g, w_gu, l, i, tn=512, tm=512):
    rows, d = x.shape
    f = w_gu.shape[-1] // 2
    assert f % LANE == 0 and tn % LANE == 0
    tm = _row_tile(rows, tm)
    n_tiles = -(-f // tn)
    tail = f - (n_tiles - 1) * tn
    el = pl.Element

    def up_col(j):
        return pl.multiple_of(jnp.minimum(f + j * tn, 2 * f - tn), LANE)

    return pl.pallas_call(
        functools.partial(_norm_swiglu_kernel, tail=tail),
        grid=(rows // tm, n_tiles),
        in_specs=[pl.BlockSpec((tm, d), lambda r, j: (r, 0)),
                  pl.BlockSpec((1, d), lambda r, j: (0, 0)),
                  pl.BlockSpec((el(1), el(1), el(d), el(tn)), lambda r, j: (l, i, 0, j * tn)),
                  pl.BlockSpec((el(1), el(1), el(d), el(tn)), lambda r, j: (l, i, 0, up_col(j)))],
        out_specs=pl.BlockSpec((tm, tn), lambda r, j: (r, j)),
        out_shape=jax.ShapeDtypeStruct((rows, n_tiles * tn), BF16),
        scratch_shapes=[pltpu.VMEM((tm, d), BF16)],
        compiler_params=_params("arbitrary", "arbitrary"),
        name="norm_swiglu",
    )(x, g.reshape(1, d), w_gu, w_gu)


def _matmul_norm_res_kernel(a_ref, w_ref, g_ref, h_ref, o_ref, *, scale, overlap):
    k = pl.program_id(1)
    last = pl.num_programs(1) - 1

    @pl.when(k == 0)
    def _():
        o_ref[...] = jnp.zeros_like(o_ref)

    a = a_ref[...]
    if overlap:
        col = lax.broadcasted_iota(jnp.int32, a.shape, 1)
        a = jnp.where(col < jnp.where(k == last, overlap, 0), jnp.zeros_like(a), a)
    n_out = o_ref.shape[1]
    sub = min(n_out, 2 * MXU_DIM)
    for c0 in range(0, n_out, sub):
        o_ref[:, c0:c0 + sub] += _dot(a, w_ref[0, 0, :, c0:c0 + sub])

    @pl.when(k == last)
    def _():
        o_ref[...] = h_ref[...] + scale * _rms(o_ref[...], g_ref[...])


def matmul_norm_res(a, w, l, i, g, h, scale, tk=512, tm=512):
    rows = a.shape[0]
    kdim, d = w.shape[-2:]
    assert kdim % LANE == 0 and tk % LANE == 0 and a.shape[1] >= kdim
    tm = _row_tile(rows, tm)
    n_k = -(-kdim // tk)
    overlap = n_k * tk - kdim
    el = pl.Element

    def k_off(k):
        return pl.multiple_of(jnp.minimum(k * tk, kdim - tk), LANE)

    return pl.pallas_call(
        functools.partial(_matmul_norm_res_kernel, scale=scale, overlap=overlap),
        grid=(rows // tm, n_k),
        in_specs=[pl.BlockSpec((el(tm), el(tk)), lambda r, k: (r * tm, k_off(k))),
                  pl.BlockSpec((el(1), el(1), el(tk), el(d)), lambda r, k: (l, i, k_off(k), 0)),
                  pl.BlockSpec((1, d), lambda r, k: (0, 0)),
                  pl.BlockSpec((tm, d), lambda r, k: (r, 0))],
        out_specs=pl.BlockSpec((tm, d), lambda r, k: (r, 0)),
        out_shape=jax.ShapeDtypeStruct((rows, d), F32),
        compiler_params=_params("arbitrary", "arbitrary"),
        name="matmul_norm_res",
    )(a, w, g.reshape(1, d), h)


def _out_proj_kernel(main_ref, mo_ref, w_ref, g_ref, h_ref, o_ref):
    k_main = main_ref.shape[1]
    y = _dot(main_ref[...], w_ref[0, :k_main, :]) + _dot(mo_ref[...], w_ref[0, k_main:, :])
    o_ref[...] = h_ref[...] + _rms(y, g_ref[...])


def out_proj(main, mo, w, l, g, h, tm=512):
    rows, k_main = main.shape
    k_mo = mo.shape[1]
    d = w.shape[-1]
    tm = _row_tile(rows, tm)
    return pl.pallas_call(
        _out_proj_kernel,
        grid=(rows // tm,),
        in_specs=[pl.BlockSpec((tm, k_main), lambda r: (r, 0)),
                  pl.BlockSpec((tm, k_mo), lambda r: (r, 0)),
                  pl.BlockSpec((1, k_main + k_mo, d), lambda r: (l, 0, 0)),
                  pl.BlockSpec((1, d), lambda r: (0, 0)),
                  pl.BlockSpec((tm, d), lambda r: (r, 0))],
        out_specs=pl.BlockSpec((tm, d), lambda r: (r, 0)),
        out_shape=jax.ShapeDtypeStruct((rows, d), F32),
        compiler_params=_params("arbitrary"),
        name="out_proj",
    )(main, mo, w, g.reshape(1, d), h)


RW_R, RW_K, RW_V, RW_QM, RW_W1, RW_V1, RW_A1, RW_G1 = 0, 6, 12, 18, 20, 21, 22, 23
RW_NBLK = 24
RW_STEP_MIXES = ((0,), (0,), (0,), (2,), (2,), (2,), (3,), (3,), (3,), (None,), (1, 3), (4, 5))
RW_FIRST_SPLIT_STEP = 10


def _rwkv_in_kernel(x_ref, sp_ref, g_ref, mu_ref, w_ref, o_ref, last_ref,
                    u_sc, xx_sc, xm_sc, carry_sc, *, tm, t_seq, n_seq, last_tile, last_row):
    ti = pl.program_id(1)
    j = pl.program_id(2)

    @pl.when(j == 0)
    def _():
        @pl.when(ti == 0)
        def _():
            carry_sc[...] = sp_ref[0]

        u = _rms(x_ref[0], g_ref[...])
        rows = lax.broadcasted_iota(jnp.int32, u.shape, 0)
        prev = pltpu.roll(u, 1, axis=0)
        for s in range(n_seq):
            prev = jnp.where(rows == s * t_seq, carry_sc[s:s + 1, :], prev)
        u_sc[...] = u
        xx_sc[...] = prev - u
        if n_seq == 1:
            carry_sc[...] = u[tm - 1:tm, :]

        @pl.when(ti == last_tile)
        def _():
            for s in range(n_seq):
                last_ref[0, s:s + 1, :] = u[s * t_seq + last_row:s * t_seq + last_row + 1, :]

    def set_mix(mix):
        if mix is None:
            xm_sc[...] = u_sc[...].astype(BF16)
        else:
            xm_sc[...] = (u_sc[...] + xx_sc[...] * mu_ref[mix:mix + 1, :]).astype(BF16)

    tn = o_ref.shape[2]
    for step, mixes in enumerate(RW_STEP_MIXES):
        if len(mixes) == 1:
            assert step < RW_FIRST_SPLIT_STEP
            if step == 0 or RW_STEP_MIXES[step - 1] != mixes:
                pl.when(j == step)(functools.partial(set_mix, mixes[0]))
        else:
            assert step >= RW_FIRST_SPLIT_STEP

            @pl.when(j == step)
            def _(mixes=mixes):
                part = tn // len(mixes)
                for q, mix in enumerate(mixes):
                    set_mix(mix)
                    o_ref[0, :, q * part:(q + 1) * part] = _dot(xm_sc[...],
                                                                w_ref[:, q * part:(q + 1) * part])

    @pl.when(j < RW_FIRST_SPLIT_STEP)
    def _():
        o_ref[0] = _dot(xm_sc[...], w_ref[...])


def rwkv_in_proj(h, shift_prev, g, mu, w_cat, t_real, tm=1024):
    b, t, d = h.shape
    n_seq = math.gcd(b, max(1, tm // t))
    bg, tg = b // n_seq, t * n_seq
    tm = _row_tile(tg, tm)
    assert n_seq == 1 or tm == tg
    n_steps = len(RW_STEP_MIXES)
    tn = RW_NBLK * MXU_DIM // n_steps
    last_tile, last_row = ((t_real - 1) // tm, (t_real - 1) % tm) if n_seq == 1 else (0, t_real - 1)
    mu8 = jnp.concatenate([mu, jnp.zeros((SUBLANE - mu.shape[0], d), F32)], axis=0)
    proj, last = pl.pallas_call(
        functools.partial(_rwkv_in_kernel, tm=tm, t_seq=t, n_seq=n_seq, last_tile=last_tile,
                          last_row=last_row),
        grid=(bg, tg // tm, n_steps),
        in_specs=[pl.BlockSpec((1, tm, d), lambda bi, ti, j: (bi, ti, 0)),
                  pl.BlockSpec((1, n_seq, d), lambda bi, ti, j: (bi, 0, 0)),
                  pl.BlockSpec((1, d), lambda bi, ti, j: (0, 0)),
                  pl.BlockSpec((SUBLANE, d), lambda bi, ti, j: (0, 0)),
                  pl.BlockSpec((d, tn), lambda bi, ti, j: (0, j))],
        out_specs=[pl.BlockSpec((1, tm, tn), lambda bi, ti, j: (bi, ti, j)),
                   pl.BlockSpec((1, n_seq, d), lambda bi, ti, j: (bi, 0, 0))],
        out_shape=[jax.ShapeDtypeStruct((bg, tg, n_steps * tn), F32),
                   jax.ShapeDtypeStruct((bg, n_seq, d), F32)],
        scratch_shapes=[pltpu.VMEM((tm, d), F32), pltpu.VMEM((tm, d), F32),
                        pltpu.VMEM((tm, d), BF16), pltpu.VMEM((n_seq, d), F32)],
        compiler_params=_params("arbitrary", "arbitrary", "arbitrary"),
        name="rwkv_in_proj",
    )(h.reshape(bg, tg, d), shift_prev.reshape(bg, n_seq, d), g.reshape(1, d), mu8, w_cat)
    return proj.reshape(b, t, n_steps * tn), last.reshape(b, d)


def _rwkv_scan_kernel(*refs, chunk, n_chunks, has_vres, t_valid, n_hb):
    if has_vres:
        (r_ref, k_ref, v_ref, tw_ref, ta_ref, tg_ref, tv_ref, vf_ref,
         w2_ref, a2_ref, g2_ref, v2_ref, vec_ref, s0_ref, y_ref, sout_ref, s_sc) = refs
    else:
        (r_ref, k_ref, v_ref, tw_ref, ta_ref, tg_ref,
         w2_ref, a2_ref, g2_ref, vec_ref, s0_ref, y_ref, sout_ref, s_sc) = refs
    ti = pl.program_id(2)
    c_len = chunk
    lanes = MXU_DIM
    rows4 = HEAD_BLOCK * c_len

    @pl.when(ti == 0)
    def _():
        s_sc[...] = s0_ref[0]

    li = lax.broadcasted_iota(jnp.int32, (lanes, lanes), 0) // RWKV_HEAD
    lj = lax.broadcasted_iota(jnp.int32, (lanes, lanes), 1) // RWKV_HEAD
    seg = (li == lj).astype(BF16)
    ci = lax.broadcasted_iota(jnp.int32, (c_len, c_len), 0)
    cj = lax.broadcasted_iota(jnp.int32, (c_len, c_len), 1)
    tri_c = (cj <= ci).astype(BF16)

    split2 = _split_hi_lo

    def seg_sums(xs):
        parts = [p for x in xs for p in split2(x)]
        res = _dot(jnp.concatenate(parts, axis=0).astype(BF16), seg)
        return [res[(2 * n) * c_len:(2 * n + 1) * c_len] + res[(2 * n + 1) * c_len:(2 * n + 2) * c_len]
                for n in range(len(xs))]

    lane_head = lax.broadcasted_iota(jnp.int32, (c_len, lanes), 1) // RWKV_HEAD
    ri = lax.broadcasted_iota(jnp.int32, (rows4, 2 * rows4), 0)
    rj = lax.broadcasted_iota(jnp.int32, (rows4, 2 * rows4), 1) & (rows4 - 1)
    strict = rj < ri
    incl = rj <= ri
    ei = lax.broadcasted_iota(jnp.int32, (rows4, rows4), 0)
    ej = lax.broadcasted_iota(jnp.int32, (rows4, rows4), 1)
    eye = (ei == ej).astype(F32)
    n_double = int(math.log2(c_len)) - 1

    def stack(x):
        return jnp.concatenate(
            [jnp.where(lane_head == hh, x, 0.0) for hh in range(HEAD_BLOCK)], axis=0)

    def one_chunk(c, carry):
        sl = pl.ds(pl.multiple_of(c * c_len, c_len), c_len)
        tw_act = jnp.tanh(tw_ref[0, sl, :])
        ta_act = ta_ref[0, sl, :]
        tg_act = _sigmoid(tg_ref[0, sl, :])
        tv_act = tv_ref[0, sl, :] if has_vres else None
        if t_valid is not None:
            t_idx = ti * (n_chunks * c_len) + c * c_len + lax.broadcasted_iota(
                jnp.int32, (c_len, lanes), 0)
            live = t_idx < t_valid
        st = [dict() for _ in range(n_hb)]

        def prep(hb):
            e = st[hb]
            hl = slice(hb * lanes, (hb + 1) * lanes)
            w0, a0, v0 = vec_ref[0:1, hl], vec_ref[1:2, hl], vec_ref[2:3, hl]
            kkw, kaw, rk = vec_ref[3:4, hl], vec_ref[4:5, hl], vec_ref[7:8, hl]
            r = r_ref[0, sl, hl]
            k = k_ref[0, sl, hl]
            v = v_ref[0, sl, hl]
            logw = -_softplus(-(w0 + _dot(tw_act, w2_ref[:, hl]))) - 0.5
            dlog = -jnp.exp(logw)
            rate = _sigmoid(a0 + _dot(ta_act, a2_ref[:, hl]))
            e['gate'] = _dot(tg_act, g2_ref[:, hl])
            if has_vres:
                v = v + (vf_ref[0, sl, hl] - v) * _sigmoid(v0 + _dot(tv_act, v2_ref[:, hl]))
            kk = k * kkw
            k = k * (1.0 + (rate - 1.0) * kaw)
            kk_sq, rk_sum = seg_sums([kk * kk, r * k * rk])
            kk = kk / jnp.maximum(jnp.sqrt(kk_sq), 1e-12)
            if t_valid is not None:
                dlog = jnp.where(live, dlog, 0.0)
                kk = jnp.where(live, kk, 0.0)
                k_live = jnp.where(live, k, 0.0)
            else:
                k_live = k
            d_hi, d_lo = split2(dlog)
            cum2 = _dot(tri_c, jnp.concatenate([d_hi, d_lo], axis=1).astype(BF16))
            cum = cum2[:, :lanes] + cum2[:, lanes:]
            inv = jnp.exp(-cum)
            e['ar'] = jnp.concatenate([stack(-kk * jnp.exp(cum - dlog)), stack(r * jnp.exp(cum))],
                                      axis=0).astype(BF16)
            e['bk'] = jnp.concatenate([stack(kk * rate * inv), stack(k_live * inv)],
                                      axis=0).astype(BF16)
            e['v_s'] = stack(v)
            e['g_end'] = jnp.exp(cum[c_len - 1:c_len, :])
            e['bonus'] = rk_sum * v

        def products(hb):
            e = st[hb]
            e['s_old'] = s_sc[hb]
            big = _dot_nt(e['ar'], jnp.concatenate([e['bk'], e['s_old'].astype(BF16)], axis=0))
            a_bk = jnp.where(strict, big[:rows4, :2 * rows4], 0.0)
            e['r_bk'] = jnp.where(incl, big[rows4:, :2 * rows4], 0.0)
            e['a_s0'], e['r_s0'] = big[:rows4, 2 * rows4:], big[rows4:, 2 * rows4:]
            e['a_k'] = a_bk[:, rows4:]
            lmat = a_bk[:, :rows4]
            e['tinv'] = eye + lmat
            e['lpow'] = lmat.astype(BF16)

        def square(hb):
            e = st[hb]
            e['lpow'] = _dot(e['lpow'], e['lpow']).astype(BF16)

        def double(hb):
            e = st[hb]
            both = _dot(jnp.concatenate([e['lpow'], e['tinv'].astype(BF16)], axis=0), e['lpow'])
            e['tinv'] = e['tinv'] + both[rows4:]
            e['lpow'] = both[:rows4].astype(BF16)

        def solve(hb):
            e = st[hb]
            tinv = e['tinv'] + _dot(e['tinv'].astype(BF16), e['lpow'])
            u = _dot(tinv, e['a_s0'] + _dot(e['a_k'], e['v_s']))
            e['uv'] = jnp.concatenate([u, e['v_s']], axis=0).astype(BF16)

        def outputs(hb):
            e = st[hb]
            y_s = e['r_s0'] + _dot(e['r_bk'].astype(BF16), e['uv'])
            s_sc[hb] = (e['s_old'] + _dot_tn(e['uv'], e['bk'])) * e['g_end']
            y = y_s[0:c_len]
            for hh in range(1, HEAD_BLOCK):
                y = y + y_s[hh * c_len:(hh + 1) * c_len]
            e['y'] = y

        def group_norm(hb):
            e = st[hb]
            hl = slice(hb * lanes, (hb + 1) * lanes)
            y = e['y']
            mean = seg_sums([y])[0] * (1.0 / RWKV_HEAD)
            yc = y - mean
            var = seg_sums([yc * yc])[0] * (1.0 / RWKV_HEAD)
            yn = yc * lax.rsqrt(var + GN_EPS) * vec_ref[5:6, hl] + vec_ref[6:7, hl]
            y_ref[0, sl, hl] = ((yn + e['bonus']) * e['gate']).astype(y_ref.dtype)

        for stage in [prep, products, square] + [double] * (n_double - 1) + [solve, outputs, group_norm]:
            for hb in range(n_hb):
                stage(hb)
        return carry

    lax.fori_loop(0, n_chunks, one_chunk, 0)

    @pl.when(ti == pl.num_programs(2) - 1)
    def _():
        sout_ref[0] = s_sc[...]


def rwkv_scan(proj, v_first_proj, w2, a2, g2, v2, vec, s0_bd, t_real, chunk, tb, n_hb=6):
    b, t, _ = proj.shape
    lanes = MXU_DIM
    tb = min(tb, t)
    assert t % tb == 0 and tb % chunk == 0 and N_HEAD_BLOCKS % n_hb == 0
    has_vres = v_first_proj is not None
    t_valid = None if t_real == t else t_real

    wide = n_hb * lanes

    def col(block0):
        assert block0 % n_hb == 0
        return pl.BlockSpec((1, tb, wide), lambda bi, hg, ti: (bi, ti, block0 // n_hb + hg))

    def fixed(block):
        return pl.BlockSpec((1, tb, lanes), lambda bi, hg, ti: (bi, ti, block))

    def wcol():
        return pl.BlockSpec((LOWRANK_PAD, wide), lambda bi, hg, ti: (0, hg))

    st = pl.BlockSpec((1, n_hb, lanes, lanes), lambda bi, hg, ti: (bi, hg, 0, 0))
    if has_vres:
        in_specs = [col(RW_R), col(RW_K), col(RW_V), fixed(RW_W1), fixed(RW_A1), fixed(RW_G1),
                    fixed(RW_V1), col(RW_V), wcol(), wcol(), wcol(), wcol()]
        args = [proj, proj, proj, proj, proj, proj, proj, v_first_proj, w2, a2, g2, v2]
    else:
        in_specs = [col(RW_R), col(RW_K), col(RW_V), fixed(RW_W1), fixed(RW_A1), fixed(RW_G1),
                    wcol(), wcol(), wcol()]
        args = [proj, proj, proj, proj, proj, proj, w2, a2, g2]
    in_specs += [pl.BlockSpec((SUBLANE, wide), lambda bi, hg, ti: (0, hg)), st]
    args += [vec, s0_bd]
    return pl.pallas_call(
        functools.partial(_rwkv_scan_kernel, chunk=chunk, n_chunks=tb // chunk,
                          has_vres=has_vres, t_valid=t_valid, n_hb=n_hb),
        grid=(b, N_HEAD_BLOCKS // n_hb, t // tb),
        in_specs=in_specs,
        out_specs=[pl.BlockSpec((1, tb, wide), lambda bi, hg, ti: (bi, ti, hg)), st],
        out_shape=[jax.ShapeDtypeStruct((b, t, MAIN_W), BF16),
                   jax.ShapeDtypeStruct(s0_bd.shape, F32)],
        scratch_shapes=[pltpu.VMEM((n_hb, lanes, lanes), F32)],
        compiler_params=_params("arbitrary", "arbitrary", "arbitrary"),
        name="rwkv_scan",
    )(*args)


def _to_block_diag(s):
    b = s.shape[0]
    s = s.reshape(b, N_HEAD_BLOCKS, HEAD_BLOCK, RWKV_HEAD, RWKV_HEAD)
    eye = jnp.eye(HEAD_BLOCK, dtype=s.dtype)
    bd = s[:, :, :, :, None, :] * eye[None, None, :, None, :, None]
    return bd.reshape(b, N_HEAD_BLOCKS, MXU_DIM, MXU_DIM)


def _from_block_diag(bd):
    b = bd.shape[0]
    n = RWKV_HEAD
    x = jnp.stack([bd[:, :, hh * n:(hh + 1) * n, hh * n:(hh + 1) * n] for hh in range(HEAD_BLOCK)],
                  axis=2)
    return x.reshape(b, RWKV_HEADS, RWKV_HEAD, RWKV_HEAD)


def _mem_attn_kernel(q_ref, k_ref, v_ref, o_ref, *, per_head):
    scale = MEM_HEAD_DIM ** -0.5
    for hh in range(MEM_HEADS):
        sl = slice(hh * MEM_HEAD_DIM, (hh + 1) * MEM_HEAD_DIM)
        k = k_ref[0, 0, :, hh, :] if per_head else k_ref[0, :, sl]
        v = v_ref[0, 0, :, hh, :] if per_head else v_ref[0, :, sl]
        s = _dot_nt(q_ref[0, :, sl], k) * scale
        p = jnp.exp(s - jnp.max(s, axis=-1, keepdims=True))
        p = p / jnp.sum(p, axis=-1, keepdims=True)
        o_ref[0, :, sl] = _dot(p, v).astype(o_ref.dtype)


def mem_attend(qsrc, q_block, ksrc, k_block, vsrc, v_block, tq=512):
    b, t, _ = qsrc.shape
    per_head = ksrc.ndim == 5
    m = ksrc.shape[-3] if per_head else ksrc.shape[1]
    tq = _row_tile(t, tq)
    if per_head:
        kv_spec = lambda blk: pl.BlockSpec((1, 1, m, MEM_HEADS, MEM_HEAD_DIM),
                                           lambda bi, ti: (blk, bi, 0, 0, 0))
    else:
        kv_spec = lambda blk: pl.BlockSpec((1, m, MEM_W), lambda bi, ti: (bi, 0, blk))
    return pl.pallas_call(
        functools.partial(_mem_attn_kernel, per_head=per_head),
        grid=(b, t // tq),
        in_specs=[pl.BlockSpec((1, tq, MEM_W), lambda bi, ti: (bi, ti, q_block)),
                  kv_spec(k_block), kv_spec(v_block)],
        out_specs=pl.BlockSpec((1, tq, MEM_W), lambda bi, ti: (bi, ti, 0)),
        out_shape=jax.ShapeDtypeStruct((b, t, MEM_W), BF16),
        compiler_params=_params("arbitrary", "arbitrary"),
        name="mem_attend",
    )(qsrc, ksrc, vsrc)


CMP_PAGE = 128
CMP_CHUNKS = CMP_PAGE // CMP_STRIDE
KV_LANES = NSA_KV_GROUPS * NSA_HEAD_DIM


def _chunk_proj_kernel(pt_ref, *refs, n_pages):
    del pt_ref
    page_refs = refs[:n_pages]
    w_ref, o0_ref, o1_ref, x_sc = refs[n_pages:]
    for kp in range(n_pages):
        for pos in range(CMP_STRIDE):
            x_sc[kp * CMP_CHUNKS:(kp + 1) * CMP_CHUNKS, pos * LANE:(pos + 1) * LANE] = (
                page_refs[kp][pl.ds(pos, CMP_CHUNKS, stride=CMP_STRIDE), :])
    res = _dot(x_sc[...].astype(BF16), w_ref[...])
    o0_ref[0] = res[:, :LANE]
    o1_ref[0] = res[:, LANE:]


def chunk_proj(rows2d, table, w_flat, n_pages, col_block=0):
    b, n_tab = table.shape
    assert n_tab % n_pages == 0

    def page_spec(kp):
        return pl.BlockSpec(
            (CMP_PAGE, LANE),
            lambda bi, pg, g, pt: (pt[bi, pg * n_pages + kp], col_block * NSA_KV_GROUPS + g))

    out_spec = pl.BlockSpec((1, n_pages * CMP_CHUNKS, LANE), lambda bi, pg, g, pt: (bi, pg, g))
    out_shape = jax.ShapeDtypeStruct((b, n_tab * CMP_CHUNKS, KV_LANES), F32)
    grid_spec = pltpu.PrefetchScalarGridSpec(
        num_scalar_prefetch=1,
        grid=(b, n_tab // n_pages, NSA_KV_GROUPS),
        in_specs=[page_spec(kp) for kp in range(n_pages)]
        + [pl.BlockSpec(w_flat.shape, lambda bi, pg, g, pt: (0, 0))],
        out_specs=[out_spec, out_spec],
        scratch_shapes=[pltpu.VMEM((n_pages * CMP_CHUNKS, CMP_STRIDE * LANE), F32)],
    )
    return pl.pallas_call(
        functools.partial(_chunk_proj_kernel, n_pages=n_pages),
        grid_spec=grid_spec,
        out_shape=[out_shape, out_shape],
        compiler_params=_params("arbitrary", "arbitrary", "arbitrary"),
        name="chunk_proj",
    )(table, *([rows2d] * n_pages), w_flat)


def _chunk_proj_pool_kernel(pt_ref, *refs, n_pages):
    del pt_ref
    page_refs = refs[:n_pages]
    w_ref, o0_ref, o1_ref, x_sc = refs[n_pages:]
    for g in range(NSA_KV_GROUPS):
        for kp in range(n_pages):
            for pos in range(CMP_STRIDE):
                x_sc[kp * CMP_CHUNKS:(kp + 1) * CMP_CHUNKS, pos * LANE:(pos + 1) * LANE] = (
                    page_refs[kp][0, pl.ds(pos, CMP_CHUNKS, stride=CMP_STRIDE), g, :])
        res = _dot(x_sc[...].astype(BF16), w_ref[...])
        o0_ref[0, :, g * LANE:(g + 1) * LANE] = res[:, :LANE]
        o1_ref[0, :, g * LANE:(g + 1) * LANE] = res[:, LANE:]


def chunk_proj_pool(pool, table, w_flat, n_pages):
    b, n_tab = table.shape
    assert n_tab % n_pages == 0

    def page_spec(kp):
        return pl.BlockSpec((1, CMP_PAGE, NSA_KV_GROUPS, LANE),
                            lambda bi, pg, pt: (pt[bi, pg * n_pages + kp], 0, 0, 0))

    out_spec = pl.BlockSpec((1, n_pages * CMP_CHUNKS, KV_LANES), lambda bi, pg, pt: (bi, pg, 0))
    out_shape = jax.ShapeDtypeStruct((b, n_tab * CMP_CHUNKS, KV_LANES), F32)
    grid_spec = pltpu.PrefetchScalarGridSpec(
        num_scalar_prefetch=1,
        grid=(b, n_tab // n_pages),
        in_specs=[page_spec(kp) for kp in range(n_pages)]
        + [pl.BlockSpec(w_flat.shape, lambda bi, pg, pt: (0, 0))],
        out_specs=[out_spec, out_spec],
        scratch_shapes=[pltpu.VMEM((n_pages * CMP_CHUNKS, CMP_STRIDE * LANE), F32)],
    )
    return pl.pallas_call(
        functools.partial(_chunk_proj_pool_kernel, n_pages=n_pages),
        grid_spec=grid_spec,
        out_shape=[out_shape, out_shape],
        compiler_params=_params("arbitrary", "arbitrary"),
        name="chunk_proj_pool",
    )(table, *([pool] * n_pages), w_flat)


def _block_mlp_kernel(p0_ref, p1_ref, pe_ref, w1_ref, b1_ref, w2_ref, o_ref):
    const = _dot(pe_ref[...], w1_ref[...])[0:1, :] + b1_ref[...]
    w2 = w2_ref[...]
    for g in range(NSA_KV_GROUPS):
        sl = slice(g * LANE, (g + 1) * LANE)
        hid = const + p0_ref[0, :, sl] + p1_ref[0, :, sl]
        o_ref[0, :, sl] = _dot(_gelu_tanh(hid), w2)


def block_mlp(p0, p1, pe, w1, b1, w2, tb=1024):
    b, nb, _ = p0.shape
    tb = _row_tile(nb, tb)
    pe8 = jnp.concatenate([pe.reshape(1, -1), jnp.zeros((SUBLANE - 1, pe.size), F32)], axis=0)
    spec = pl.BlockSpec((1, tb, KV_LANES), lambda bi, i: (bi, i, 0))
    full = lambda a: pl.BlockSpec(a.shape, lambda bi, i: (0,) * a.ndim)
    b1r = b1.reshape(1, -1)
    return pl.pallas_call(
        _block_mlp_kernel,
        grid=(b, nb // tb),
        in_specs=[spec, spec, full(pe8), full(w1), full(b1r), full(w2)],
        out_specs=spec,
        out_shape=jax.ShapeDtypeStruct((b, nb, KV_LANES), F32),
        compiler_params=_params("arbitrary", "arbitrary"),
        name="block_mlp",
    )(p0, p1, pe8, w1, b1r, w2)


def _w1_flat(w1):
    r = CMP_BLOCK // CMP_STRIDE
    e = w1.shape[1]
    w = w1.reshape(r, CMP_STRIDE, NSA_HEAD_DIM, e)
    return jnp.transpose(w, (1, 2, 0, 3)).reshape(CMP_STRIDE * NSA_HEAD_DIM, r * e).astype(BF16)


def compress(parts, nb, pe, w1, b1, w2):
    return block_mlp(parts[0][:, :nb], parts[1][:, 1:nb + 1], pe, w1, b1, w2)


def _rel_bucket(dist):
    n = jnp.maximum(dist, 0)
    max_exact = REL_BUCKETS // 2
    nf = jnp.maximum(n, 1).astype(F32)
    large = max_exact + (jnp.log(nf / max_exact) / math.log(REL_MAX_DIST / max_exact)
                         * (REL_BUCKETS - max_exact)).astype(jnp.int32)
    return jnp.where(n < max_exact, n, jnp.minimum(large, REL_BUCKETS - 1))


def _bias_lookup(rel_bias, dist):
    tab = rel_bias.astype(F32).reshape(REL_BUCKETS, NSA_KV_GROUPS, NSA_HPG)
    onehot = jax.nn.one_hot(_rel_bucket(dist), REL_BUCKETS, dtype=F32)
    out = jnp.dot(onehot, tab.reshape(REL_BUCKETS, -1), precision=HI)
    out = out.reshape(dist.shape + (NSA_KV_GROUPS, NSA_HPG))
    return jnp.moveaxis(out, (-2, -1), (0, 1))


def _cmp_to_sel(nb_pad, nb, nsb_pad, nsb):
    i = jnp.arange(nb_pad)[:, None]
    j = jnp.arange(nsb_pad)[None, :]
    start = i * CMP_STRIDE
    hit = (start < (j + 1) * SEL_BLOCK) & (start + CMP_BLOCK > j * SEL_BLOCK) & (i < nb) & (j < nsb)
    return hit.astype(F32)


def _nsa_prompt_kernel(q_ref, gt_ref, kc_ref, vc_ref, bc_ref, band_ref, ks_ref, vs_ref,
                       kw_ref, vw_ref, m_ref, o_ref, s_sc, *, nb, nsb):
    tq = NSA_TQ
    rows = NSA_HPG * tq
    i = pl.program_id(2)
    scale = NSA_HEAD_DIM ** -0.5
    qb = q_ref[0]
    q3 = jnp.concatenate([qb[:, hh * LANE:(hh + 1) * LANE] for hh in range(NSA_HPG)],
                         axis=0).astype(BF16)
    row_q = lax.broadcasted_iota(jnp.int32, (rows, LANE), 0) & (tq - 1)
    lane = lax.broadcasted_iota(jnp.int32, (rows, LANE), 1)
    t_pos = i * tq + row_q
    kt_w = NSA_KT
    tiles_per_kt = kt_w // tq
    key_lane = lax.broadcasted_iota(jnp.int32, (rows, kt_w), 1)
    t_pos_k = i * tq + (lax.broadcasted_iota(jnp.int32, (rows, kt_w), 0) & (tq - 1))

    def masked_scores(k_ref, kt, mask_of):
        ksl = pl.ds(pl.multiple_of(kt * kt_w, kt_w), kt_w)
        band = band_ref[0, jnp.clip(i - kt * tiles_per_kt, 0, NSA_BANDS - 1)].reshape(rows, kt_w)
        s = _dot_nt(q3, k_ref[0, ksl, :].astype(BF16)) * scale + band
        return jnp.where(mask_of(t_pos_k - (kt * kt_w + key_lane)), s, MASK_NEG), ksl

    n_win_tiles = WINDOW // kt_w + 1
    win_tiles = []
    m_w = jnp.full((rows, 1), MASK_NEG, F32)
    for jw in range(n_win_tiles):
        kt_raw = i // tiles_per_kt - (n_win_tiles - 1) + jw
        reach = jnp.where(kt_raw >= 0, WINDOW, 0)
        s_w, ksl_w = masked_scores(kw_ref, jnp.maximum(kt_raw, 0),
                                   lambda dist, reach=reach: (dist >= 0) & (dist < reach))
        win_tiles.append((s_w, ksl_w))
        m_w = jnp.maximum(m_w, jnp.max(s_w, axis=-1, keepdims=True))

    s_c = _dot_nt(q3, kc_ref[0].astype(BF16)) * scale + bc_ref[0].reshape(rows, LANE)
    m_c = (t_pos - (lane * CMP_STRIDE + (CMP_BLOCK - 1)) >= 0) & (lane < nb)
    p_c = _masked_softmax(s_c, m_c)
    o_c = _dot(p_c, vc_ref[0])

    p_hi, p_lo = _split_hi_lo(p_c)
    sel_map = m_ref[...].astype(BF16)
    imp3 = _dot_nt(sel_map, p_hi.astype(BF16)) + _dot_nt(sel_map, p_lo.astype(BF16))
    imp = imp3[:, 0:tq]
    for hh in range(1, NSA_HPG):
        imp = imp + imp3[:, hh * tq:(hh + 1) * tq]
    jb = lax.broadcasted_iota(jnp.int32, (nsb, tq), 0)
    cur = (i * tq + lax.broadcasted_iota(jnp.int32, (nsb, tq), 1)) // SEL_BLOCK
    valid = jb <= cur
    forced = valid & ((jb == 0) | (jb > cur - SEL_LOCAL))
    imp = jnp.where(forced, FORCE_SCORE, jnp.where(valid, imp, -FORCE_SCORE))
    rank = jnp.zeros((nsb, tq), jnp.int32)
    for jp in range(nsb):
        other = imp[jp:jp + 1, :]
        rank = rank + ((other > imp) | ((other == imp) & (jb > jp))).astype(jnp.int32)
    sel = (rank < min(SEL_TOPK, nsb)).astype(F32).T
    sel3 = jnp.concatenate([sel] * NSA_HPG, axis=0).astype(BF16)
    blk_row = lax.broadcasted_iota(jnp.int32, (nsb, kt_w), 0)
    blk_lane = lax.broadcasted_iota(jnp.int32, (nsb, kt_w), 1) // SEL_BLOCK

    def finish(m_fin, l_fin, acc):
        return jnp.where(m_fin > MASK_NEG, acc / jnp.where(l_fin > 0, l_fin, 1.0), 0.0)

    l_w = jnp.zeros((rows, 1), F32)
    acc_w = jnp.zeros((rows, LANE), F32)
    for s_w, ksl_w in win_tiles:
        p_w = jnp.exp(s_w - m_w)
        l_w = l_w + jnp.sum(p_w, axis=-1, keepdims=True)
        acc_w = acc_w + _dot(p_w, vw_ref[0, ksl_w, :])
    o_w = finish(m_w, l_w, acc_w)

    n_kt = i // tiles_per_kt + 1

    def scores(kt, m_run):
        expand = (blk_row == kt * (kt_w // SEL_BLOCK) + blk_lane).astype(BF16)
        chosen = _dot(sel3, expand) > 0.5
        s, _ = masked_scores(ks_ref, kt, lambda dist: chosen & (dist >= 0))
        s_sc[kt] = s
        return jnp.maximum(m_run, jnp.max(s, axis=-1, keepdims=True))

    m_s = lax.fori_loop(0, n_kt, scores, jnp.full((rows, 1), MASK_NEG, F32))

    def accum(kt, carry):
        l_run, acc = carry
        ksl = pl.ds(pl.multiple_of(kt * kt_w, kt_w), kt_w)
        p = jnp.exp(s_sc[kt] - m_s)
        return (l_run + jnp.sum(p, axis=-1, keepdims=True), acc + _dot(p, vs_ref[0, ksl, :]))

    l_s, acc_s = lax.fori_loop(0, n_kt, accum, (jnp.zeros((rows, 1), F32),
                                                jnp.zeros((rows, LANE), F32)))
    o_s = finish(m_s, l_s, acc_s)

    gt = _sigmoid(gt_ref[0, 0])
    gcol = lambda br: jnp.concatenate(
        [gt[:, br * NSA_HPG + hh:br * NSA_HPG + hh + 1] for hh in range(NSA_HPG)], axis=0)
    o = gcol(0) * o_c + gcol(1) * o_s + gcol(2) * o_w
    for hh in range(NSA_HPG):
        o_ref[0, :, hh * LANE:(hh + 1) * LANE] = o[hh * tq:(hh + 1) * tq].astype(o_ref.dtype)


def nsa_prompt(proj, gates_t, kc, vc, bias_c, band, side, sel_map, nb, nsb):
    b, t, _ = proj.shape
    tq = NSA_TQ
    qw = NSA_HPG * LANE
    kv = lambda off: pl.BlockSpec((1, t, LANE), lambda bi, g, i: (bi, 0, off * NSA_KV_GROUPS + g))
    return pl.pallas_call(
        functools.partial(_nsa_prompt_kernel, nb=nb, nsb=nsb),
        grid=(b, NSA_KV_GROUPS, t // tq),
        in_specs=[pl.BlockSpec((1, tq, qw), lambda bi, g, i: (bi, i, g)),
                  pl.BlockSpec((1, 1, tq, NSA_HPG * 3), lambda bi, g, i: (bi, g, i, 0)),
                  pl.BlockSpec((1, LANE, LANE), lambda bi, g, i: (bi, 0, g)),
                  pl.BlockSpec((1, LANE, LANE), lambda bi, g, i: (bi, 0, g)),
                  pl.BlockSpec((1, NSA_HPG, tq, LANE), lambda bi, g, i: (g, 0, i, 0)),
                  pl.BlockSpec((1, NSA_BANDS, NSA_HPG, tq, NSA_KT), lambda bi, g, i: (g, 0, 0, 0, 0)),
                  kv(2), kv(3), kv(4), kv(5),
                  pl.BlockSpec(sel_map.shape, lambda bi, g, i: (0, 0))],
        out_specs=pl.BlockSpec((1, tq, qw), lambda bi, g, i: (bi, i, g)),
        out_shape=jax.ShapeDtypeStruct((b, t, MAIN_W), BF16),
        scratch_shapes=[pltpu.VMEM((t // NSA_KT, NSA_HPG * tq, NSA_KT), F32)],
        compiler_params=_params("arbitrary", "arbitrary", "arbitrary"),
        name="nsa_prompt",
    )(proj, gates_t, kc, vc, bias_c, band, side, side, side, side, sel_map)


def _nsa_dec_cmp_kernel(q_ref, kc_ref, vc_ref, bc_ref, m_ref, oc_ref, idx_ref, *, nb, nsb, t_pos):
    scale = NSA_HEAD_DIM ** -0.5
    nbp = kc_ref.shape[1]
    nsp = m_ref.shape[1]
    n_sel = min(SEL_TOPK, nsb)
    lane_b = lax.broadcasted_iota(jnp.int32, (SUBLANE, nbp), 1)
    m_c = (t_pos - (lane_b * CMP_STRIDE + (CMP_BLOCK - 1)) >= 0) & (lane_b < nb)
    row_s = lax.broadcasted_iota(jnp.int32, (SUBLANE, nsp), 0)
    jb = lax.broadcasted_iota(jnp.int32, (1, nsp), 1)
    cur = t_pos // SEL_BLOCK
    valid = jb <= cur
    forced = valid & ((jb == 0) | (jb > cur - SEL_LOCAL))
    out_lane = lax.broadcasted_iota(jnp.int32, (1, LANE), 1)
    idx_rows = []
    for g in range(NSA_KV_GROUPS):
        q3 = jnp.concatenate(
            [q_ref[0, 0:1, (g * NSA_HPG + hh) * LANE:(g * NSA_HPG + hh + 1) * LANE]
             for hh in range(NSA_HPG)] + [jnp.zeros((SUBLANE - NSA_HPG, LANE), F32)], axis=0)
        sl = slice(g * LANE, (g + 1) * LANE)
        s_c = _dot_nt(q3, kc_ref[0, :, sl]) * scale + bc_ref[g]
        p_c = _masked_softmax(s_c, m_c)
        oc_ref[0, g] = _dot(p_c, vc_ref[0, :, sl])
        imp8 = jnp.where(row_s < NSA_HPG, _dot(p_c, m_ref[...], HI), 0.0)
        imp = jnp.sum(imp8, axis=0, keepdims=True)
        imp = jnp.where(forced, FORCE_SCORE, jnp.where(valid, imp, -FORCE_SCORE))
        imp = jnp.where(jb < nsb, imp, -jnp.inf)
        jbf = jb.astype(F32)
        picks = jnp.zeros((1, LANE), F32)
        for kk in range(n_sel):
            best = jnp.max(imp, axis=-1, keepdims=True)
            arg = jnp.min(jnp.where(imp == best, jbf, float(nsp)), axis=-1, keepdims=True)
            picks = jnp.where(out_lane == kk, arg, picks)
            imp = jnp.where(jbf == arg, -jnp.inf, imp)
        idx_rows.append(picks.astype(jnp.int32))
    idx_rows.append(jnp.zeros((SUBLANE - NSA_KV_GROUPS, LANE), jnp.int32))
    idx_ref[0] = jnp.concatenate(idx_rows, axis=0)


def nsa_dec_cmp(proj, kc, vc, bias_c, sel_map, nb, nsb, t_pos):
    b, tp, _ = proj.shape
    nbp = kc.shape[1]
    return pl.pallas_call(
        functools.partial(_nsa_dec_cmp_kernel, nb=nb, nsb=nsb, t_pos=t_pos),
        grid=(b,),
        in_specs=[pl.BlockSpec((1, tp, MAIN_W), lambda bi: (bi, 0, 0)),
                  pl.BlockSpec((1, nbp, KV_LANES), lambda bi: (bi, 0, 0)),
                  pl.BlockSpec((1, nbp, KV_LANES), lambda bi: (bi, 0, 0)),
                  pl.BlockSpec(bias_c.shape, lambda bi: (0, 0, 0)),
                  pl.BlockSpec(sel_map.shape, lambda bi: (0, 0))],
        out_specs=[pl.BlockSpec((1, NSA_KV_GROUPS, SUBLANE, LANE), lambda bi: (bi, 0, 0, 0)),
                   pl.BlockSpec((1, SUBLANE, LANE), lambda bi: (bi, 0, 0))],
        out_shape=[jax.ShapeDtypeStruct((b, NSA_KV_GROUPS, SUBLANE, LANE), F32),
                   jax.ShapeDtypeStruct((b, SUBLANE, LANE), jnp.int32)],
        compiler_params=_params("arbitrary"),
        name="nsa_dec_cmp",
    )(proj, kc, vc, bias_c, sel_map)


def _nsa_dec_sel_kernel(phys_ref, isnew_ref, *refs, n_sel, n_win, t_pos, win_start):
    del phys_ref
    kb_refs = refs[:n_sel]
    vb_refs = refs[n_sel:2 * n_sel]
    (q_ref, new_ref, ds_ref, bs_ref, wk_ref, wv_ref, bw_ref, oc_ref, gt_ref, o_ref) = refs[2 * n_sel:]
    bi = pl.program_id(0)
    g = pl.program_id(1)
    scale = NSA_HEAD_DIM ** -0.5
    q3 = q_ref[0, 0]
    new_rows = new_ref[0, 0]
    pad_blk = jnp.zeros((SEL_BLOCK - 1, LANE), F32)
    new_k = jnp.concatenate([new_rows[0:1], pad_blk], axis=0)
    new_v = jnp.concatenate([new_rows[1:2], pad_blk], axis=0)
    ks, vs = [], []
    for kk in range(n_sel):
        fresh = isnew_ref[bi, g, kk] > 0
        ks.append(jnp.where(fresh, new_k, kb_refs[kk][0, :, g, :]))
        vs.append(jnp.where(fresh, new_v, vb_refs[kk][0, :, g, :]))
    ks = jnp.concatenate(ks, axis=0)
    vs = jnp.concatenate(vs, axis=0)
    s_s = _dot_nt(q3, ks) * scale + bs_ref[0, 0]
    p_s = _masked_softmax(s_s, ds_ref[0, 0] >= 0)
    o_s = _dot(p_s, vs)

    pad_w = jnp.zeros((SUBLANE - 1, LANE), F32)
    kw = jnp.concatenate([wk_ref[0, :, g, :], new_rows[2:3], pad_w], axis=0)
    vw = jnp.concatenate([wv_ref[0, :, g, :], new_rows[3:4], pad_w], axis=0)
    nw = kw.shape[0]
    pos = win_start + lax.broadcasted_iota(jnp.int32, (SUBLANE, nw), 1)
    d_w = t_pos - pos
    m_w = (d_w >= 0) & (d_w < WINDOW) & (pos >= 0) & (pos - win_start < n_win)
    s_w = _dot_nt(q3, kw) * scale + bw_ref[0]
    p_w = _masked_softmax(s_w, m_w)
    o_w = _dot(p_w, vw)

    gt = _sigmoid(gt_ref[0, 0])
    o_ref[0, 0] = gt[:, 0:1] * oc_ref[0, 0] + gt[:, 1:2] * o_s + gt[:, 2:3] * o_w


def nsa_dec_sel(phys, isnew, pool_k, pool_v, q8, new_rows, d_s, bias_s, win_k, win_v, bias_w,
                o_c, gates8, n_win, t_pos, win_start):
    b = q8.shape[0]
    n_sel = phys.shape[-1]
    n_keys = n_sel * SEL_BLOCK
    nw = win_k.shape[1]
    nwp = nw + SUBLANE

    def blk_spec(kk):
        return pl.BlockSpec((1, SEL_BLOCK, NSA_KV_GROUPS, LANE),
                            lambda bi, g, ph, nf: (ph[bi, g, kk], 0, 0, 0))

    per_bg = lambda *shape: pl.BlockSpec((1, 1) + shape, lambda bi, g, ph, nf: (bi, g) + (0,) * len(shape))
    grid_spec = pltpu.PrefetchScalarGridSpec(
        num_scalar_prefetch=2,
        grid=(b, NSA_KV_GROUPS),
        in_specs=[blk_spec(kk) for kk in range(n_sel)] + [blk_spec(kk) for kk in range(n_sel)]
        + [per_bg(SUBLANE, LANE), per_bg(SUBLANE, LANE), per_bg(1, n_keys), per_bg(SUBLANE, n_keys),
           pl.BlockSpec((1, nw, NSA_KV_GROUPS, LANE), lambda bi, g, ph, nf: (bi, 0, 0, 0)),
           pl.BlockSpec((1, nw, NSA_KV_GROUPS, LANE), lambda bi, g, ph, nf: (bi, 0, 0, 0)),
           pl.BlockSpec((1, SUBLANE, nwp), lambda bi, g, ph, nf: (g, 0, 0)),
           per_bg(SUBLANE, LANE), per_bg(SUBLANE, SUBLANE)],
        out_specs=per_bg(SUBLANE, LANE),
    )
    return pl.pallas_call(
        functools.partial(_nsa_dec_sel_kernel, n_sel=n_sel, n_win=nw + 1, t_pos=t_pos,
                          win_start=win_start),
        grid_spec=grid_spec,
        out_shape=jax.ShapeDtypeStruct((b, NSA_KV_GROUPS, SUBLANE, LANE), F32),
        compiler_params=_params("arbitrary", "arbitrary"),
        name="nsa_dec_sel",
    )(phys, isnew, *([pool_k] * n_sel), *([pool_v] * n_sel), q8, new_rows, d_s, bias_s,
      win_k, win_v, bias_w, o_c, gates8)


def _pad_cols(w, n):
    return jnp.pad(w, ((0, 0), (0, n - w.shape[1])))


def _pad_rows(w, n):
    return jnp.pad(w, ((0, n - w.shape[0]), (0, 0)))


def _prep_weights(P):
    W = {}
    W['w_gu'] = P['ffn_gu'].astype(BF16)
    W['w_d'] = P['ffn_d'].astype(BF16)
    W['w_out'] = P['w_out'].astype(BF16)
    W['rw_in'], W['rw_w2'], W['rw_a2'], W['rw_g2'], W['rw_v2'], W['rw_vec'] = [], [], [], [], [], []
    lp = LOWRANK_PAD
    for l in range(N_A):
        w_in = P['w_in_a'][l]
        v1 = P['rw_v1'][l - 1] if l > 0 else jnp.zeros((D_MODEL, lp), F32)
        W['rw_in'].append(jnp.concatenate([
            w_in, _pad_cols(P['rw_w1'][l], lp), _pad_cols(v1, lp), _pad_cols(P['rw_a1'][l], lp),
            _pad_cols(P['rw_g1'][l], lp)], axis=1).astype(BF16))
        W['rw_w2'].append(_pad_rows(P['rw_w2'][l], lp))
        W['rw_a2'].append(_pad_rows(P['rw_a2'][l], lp))
        W['rw_g2'].append(_pad_rows(P['rw_g2'][l], lp))
        W['rw_v2'].append(_pad_rows(P['rw_v2'][l - 1], lp) if l > 0 else None)
        v0 = P['rw_v0'][l - 1] if l > 0 else jnp.zeros((MAIN_W,), F32)
        W['rw_vec'].append(jnp.stack([P['rw_w0'][l], P['rw_a0'][l], v0, P['rw_kk'][l], P['rw_ka'][l],
                                      P['rw_lnw'][l], P['rw_lnb'][l], P['rw_rk'][l].reshape(-1)]))
    W['w_in_b'] = []
    for l in range(DEPTH - N_A):
        w = P['w_in_b'][l]
        W['w_in_b'].append(jnp.concatenate([
            w[:, :MAIN_W], w[:, MAIN_W + GATE_W:], _pad_cols(w[:, MAIN_W:MAIN_W + GATE_W], MXU_DIM)],
            axis=1).astype(BF16))
    W['w_kv'] = P['w_kv'].astype(BF16)
    W['w_mem_kv'] = [P['w_mem_kv'][l].astype(BF16) for l in range(DEPTH)]
    W['cmp_w1_flat'] = [_w1_flat(P['cmp_w1'][c]) for c in range(2)]
    return W


B_QM_BLOCK = MAIN_W // MEM_W
B_GATE_OFF = MAIN_W + MEM_W


def _ffn(h2, norms_l, first, W, l, i):
    hid = norm_swiglu(h2, norms_l[first], W['w_gu'], l, i, tn=1024, tm=1024)
    return matmul_norm_res(hid, W['w_d'], l, i, norms_l[first + 1], h2, 0.5, tk=FFN_DOWN_TK)


def _trunk(x, t_real, P, W, shift0, wkv0_bd, mem_src, make_side, attend, chunk, tb):
    b, t, d = x.shape
    h2 = x.reshape(b * t, d)
    shifts, states = [], []
    v_first_proj, ctx, side_state = None, None, None
    for l in range(DEPTH):
        n = P['norms'][l]
        h2 = _ffn(h2, n, 0, W, l, 0)
        if l < N_A:
            proj, last = rwkv_in_proj(h2.reshape(b, t, d), shift0[l], n[2], P['rw_mu'][l],
                                      W['rw_in'][l], t_real)
            main, s_bd = rwkv_scan(proj, v_first_proj if l > 0 else None, W['rw_w2'][l], W['rw_a2'][l],
                                   W['rw_g2'][l], W['rw_v2'][l], W['rw_vec'][l], wkv0_bd[l],
                                   t_real, chunk, tb)
            if l == 0:
                v_first_proj = proj
            shifts.append(last)
            states.append(_from_block_diag(s_bd))
            q_src, q_block = proj, RW_QM * MXU_DIM // MEM_W
        else:
            proj = norm_matmul(h2, n[2], W['w_in_b'][l - N_A], tn=768, tm=1024).reshape(b, t, -1)
            main = attend(proj, ctx)
            q_src, q_block = proj, B_QM_BLOCK
        mk, kb, mv, vb = mem_src(l)
        mo = mem_attend(q_src, q_block, mk, kb, mv, vb)
        h2 = out_proj(main.reshape(b * t, MAIN_W), mo.reshape(b * t, MEM_W), W['w_out'], l, n[3], h2)
        h2 = _ffn(h2, n, 4, W, l, 1)
        if l == N_A - 1:
            side = norm_matmul(h2, P['kv_norm'], W['w_kv'], tn=768, tm=1024).reshape(b, t, -1)
            ctx, side_state = make_side(side)
    return h2.reshape(b, t, d), jnp.stack(shifts), jnp.stack(states), side_state


def kernel(x_prompt, x_sample, mem_prompt, state_wkv, state_shift, cache_mem_k, cache_mem_v,
           cache_cmp_k, cache_cmp_v, cache_slc_k, cache_slc_v, cache_win_k, cache_win_v, page_table,
           norms, ffn_gu, ffn_d, w_in_a, w_in_b, w_out, mem_norm, w_mem_kv, kv_norm, w_kv,
           cmp_pe, cmp_w1, cmp_b1, cmp_w2, rel_bias,
           rw_mu, rw_w0, rw_w1, rw_w2, rw_a0, rw_a1, rw_a2, rw_g1, rw_g2, rw_v0, rw_v1, rw_v2,
           rw_kk, rw_ka, rw_rk, rw_lnw, rw_lnb):
    P = dict(norms=norms, ffn_gu=ffn_gu, ffn_d=ffn_d, w_in_a=w_in_a, w_in_b=w_in_b, w_out=w_out,
             kv_norm=kv_norm, w_kv=w_kv, w_mem_kv=w_mem_kv, cmp_w1=cmp_w1, rw_mu=rw_mu, rw_w0=rw_w0,
             rw_w1=rw_w1, rw_w2=rw_w2, rw_a0=rw_a0, rw_a1=rw_a1, rw_a2=rw_a2, rw_g1=rw_g1,
             rw_g2=rw_g2, rw_v0=rw_v0, rw_v1=rw_v1, rw_v2=rw_v2, rw_kk=rw_kk, rw_ka=rw_ka,
             rw_rk=rw_rk, rw_lnw=rw_lnw, rw_lnb=rw_lnb)
    W = _prep_weights(P)
    G, dh = NSA_KV_GROUPS, NSA_HEAD_DIM
    split_side = lambda side, bx, t: [side[:, :t, c * KV_LANES:(c + 1) * KV_LANES].reshape(bx, t, G, dh)
                                      for c in range(6)]

    def cmp_mlp(parts, nb, c):
        return compress(parts, nb, cmp_pe[c], cmp_w1[c], cmp_b1[c], cmp_w2[c])

    bp, tp, d = x_prompt.shape
    n_mem = mem_prompt.shape[1]
    mem2 = mem_prompt.reshape(bp * n_mem, d)
    p_mkv = [norm_matmul(mem2, mem_norm[l], W['w_mem_kv'][l], tn=512).reshape(bp, n_mem, 2 * MEM_W)
             for l in range(DEPTH)]
    p_mem_k = jnp.stack([m[..., :MEM_W].reshape(bp, n_mem, MEM_HEADS, MEM_HEAD_DIM) for m in p_mkv])
    p_mem_v = jnp.stack([m[..., MEM_W:].reshape(bp, n_mem, MEM_HEADS, MEM_HEAD_DIM) for m in p_mkv])

    nb_p = tp // CMP_STRIDE - 1
    nsb_p = tp // SEL_BLOCK
    nq_tiles = tp // NSA_TQ
    t_all = jnp.arange(tp)
    c_end = jnp.arange(LANE) * CMP_STRIDE + (CMP_BLOCK - 1)
    assert nb_p <= LANE
    bias_c_p = _bias_lookup(rel_bias, t_all[:, None] - c_end[None, :])
    ii = jnp.arange(NSA_TQ)
    cc = jnp.arange(NSA_KT)
    band_p = jnp.stack([_bias_lookup(rel_bias, dd * NSA_TQ + ii[:, None] - cc[None, :])
                        for dd in range(NSA_BANDS)], axis=1)
    assert (NSA_BANDS - 1) * NSA_TQ - (NSA_KT - 1) >= REL_MAX_DIST and tp % NSA_KT == 0
    sel_map_p = _cmp_to_sel(LANE, nb_p, nsb_p, nsb_p).T
    ident = jnp.arange(bp * tp // CMP_PAGE, dtype=jnp.int32).reshape(bp, tp // CMP_PAGE)

    def prompt_side(side):
        rows = side.reshape(bp * tp, -1)
        kc = cmp_mlp(chunk_proj(rows, ident, W['cmp_w1_flat'][0], tp // CMP_PAGE, 0), nb_p, 0)
        vc = cmp_mlp(chunk_proj(rows, ident, W['cmp_w1_flat'][1], tp // CMP_PAGE, 1), nb_p, 1)
        padb = ((0, 0), (0, LANE - nb_p), (0, 0))
        wb = min(WINDOW, tp)
        kc_r, vc_r, ks, vs, kw, vw = split_side(side, bp, tp)
        return ((jnp.pad(kc, padb), jnp.pad(vc, padb), side),
                (kc_r, vc_r, ks, vs, kw[:, tp - wb:], vw[:, tp - wb:]))

    def prompt_attend(proj, ctx):
        kc, vc, side = ctx
        gates = proj[..., B_GATE_OFF:B_GATE_OFF + GATE_W].reshape(bp, tp, 3, G, NSA_HPG)
        gates_t = jnp.transpose(gates, (0, 3, 1, 2, 4)).reshape(bp, G, tp, 3 * NSA_HPG)
        return nsa_prompt(proj, gates_t, kc, vc, bias_c_p, band_p, side, sel_map_p, nb_p, nsb_p)

    zeros_shift = jnp.zeros((N_A, bp, d), F32)
    zeros_state = jnp.zeros((N_A, bp, N_HEAD_BLOCKS, MXU_DIM, MXU_DIM), F32)
    y_prompt, p_shift, p_wkv, p_side = _trunk(
        x_prompt, tp, P, W, zeros_shift, zeros_state,
        lambda l: (p_mkv[l], 0, p_mkv[l], 1), prompt_side, prompt_attend, chunk=64, tb=256)
    p_cmp_k, p_cmp_v, p_slc_k, p_slc_v, p_win_k, p_win_v = p_side

    bd, s_new, _ = x_sample.shape
    assert s_new == 1
    ts = SUBLANE
    xs = jnp.pad(x_sample, ((0, 0), (0, ts - s_new), (0, 0)))
    past_len = page_table.shape[1] * CMP_PAGE
    n_past_blk = past_len // SEL_BLOCK
    blk_per_page = CMP_PAGE // SEL_BLOCK
    nsb_s = n_past_blk + 1
    t_pos = past_len
    nc_s = -(-(past_len + s_new) // CMP_STRIDE)
    nb_s = nc_s - 1
    wb_s = cache_win_k.shape[1]
    win_start = past_len - wb_s
    nsp = -(-nsb_s // LANE) * LANE
    sel_map_s = _cmp_to_sel(nb_s, nb_s, nsp, nsb_s)
    c_end_s = jnp.arange(nb_s) * CMP_STRIDE + (CMP_BLOCK - 1)
    bias_c_s = _bias_lookup(rel_bias, t_pos - c_end_s)
    bias_c_s = jnp.pad(bias_c_s, ((0, 0), (0, SUBLANE - NSA_HPG), (0, 0)))
    nwp = wb_s + SUBLANE
    bias_w_s = _bias_lookup(rel_bias, t_pos - (win_start + jnp.arange(nwp)))
    bias_w_s = jnp.pad(bias_w_s, ((0, 0), (0, SUBLANE - NSA_HPG), (0, 0)))
    ident_s = jnp.arange(bd, dtype=jnp.int32).reshape(bd, 1)
    pages_per_step = math.gcd(16, page_table.shape[1])

    def sample_side(side):
        new = side[:, :s_new]
        parts = []
        for c, pool in enumerate((cache_cmp_k, cache_cmp_v)):
            past = chunk_proj_pool(pool, page_table, W['cmp_w1_flat'][c], pages_per_step)
            fresh_page = jnp.pad(new[:, :, c * KV_LANES:(c + 1) * KV_LANES],
                                 ((0, 0), (0, CMP_PAGE - s_new), (0, 0))).reshape(bd, CMP_PAGE, G, dh)
            fresh = chunk_proj_pool(fresh_page, ident_s, W['cmp_w1_flat'][c], 1)
            n_fresh = nc_s - past[0].shape[1]
            parts.append([jnp.concatenate([p, f[:, :n_fresh]], axis=1) for p, f in zip(past, fresh)])
        kc = cmp_mlp(parts[0], nb_s, 0)
        vc = cmp_mlp(parts[1], nb_s, 1)
        kc_n, vc_n, ks_n, vs_n, kw_n, vw_n = split_side(side, bd, s_new)
        s_win_k = jnp.concatenate([cache_win_k, kw_n], axis=1)[:, s_new:]
        s_win_v = jnp.concatenate([cache_win_v, vw_n], axis=1)[:, s_new:]
        return (kc, vc, new), (kc_n, vc_n, ks_n, vs_n, s_win_k, s_win_v)

    def sample_attend(proj, ctx):
        kc, vc, new = ctx
        o_c, idx8 = nsa_dec_cmp(proj, kc, vc, bias_c_s, sel_map_s, nb_s, nsb_s, t_pos)
        idx = idx8[:, :G, :SEL_TOPK]
        is_new = idx >= n_past_blk
        jp = jnp.minimum(idx, n_past_blk - 1)
        phys = (jnp.take_along_axis(page_table[:, None, :], jp // blk_per_page, axis=2) * blk_per_page
                + jp % blk_per_page)
        k_pos = (idx[..., None] * SEL_BLOCK + jnp.arange(SEL_BLOCK)).reshape(bd, G, 1, -1)
        d_s = t_pos - k_pos
        tab = rel_bias.astype(F32).reshape(REL_BUCKETS, G, NSA_HPG)
        onehot = jax.nn.one_hot(_rel_bucket(d_s[:, :, 0]), REL_BUCKETS, dtype=F32)
        bias_s = jnp.einsum('bgkn,ngh->bghk', onehot, tab, precision=HI)
        bias_s = jnp.pad(bias_s, ((0, 0), (0, 0), (0, SUBLANE - NSA_HPG), (0, 0)))
        q8 = jnp.pad(proj[:, 0, :MAIN_W].reshape(bd, G, NSA_HPG, dh),
                     ((0, 0), (0, 0), (0, SUBLANE - NSA_HPG), (0, 0)))
        new_rows = jnp.pad(jnp.transpose(new[:, 0, 2 * KV_LANES:].reshape(bd, 4, G, dh), (0, 2, 1, 3)),
                           ((0, 0), (0, 0), (0, SUBLANE - 4), (0, 0)))
        gates = proj[:, 0, B_GATE_OFF:B_GATE_OFF + GATE_W].reshape(bd, 3, G, NSA_HPG)
        gates8 = jnp.pad(jnp.transpose(gates, (0, 2, 3, 1)),
                         ((0, 0), (0, 0), (0, SUBLANE - NSA_HPG), (0, SUBLANE - 3)))
        o = nsa_dec_sel(phys.astype(jnp.int32), is_new.astype(jnp.int32),
                        cache_slc_k.reshape(-1, SEL_BLOCK, G, dh),
                        cache_slc_v.reshape(-1, SEL_BLOCK, G, dh),
                        q8, new_rows, d_s.astype(jnp.int32), bias_s,
                        cache_win_k, cache_win_v,
                        bias_w_s, o_c, gates8, wb_s, t_pos, win_start)
        main = o[:, :, :NSA_HPG].reshape(bd, 1, MAIN_W)
        return jnp.pad(main, ((0, 0), (0, ts - 1), (0, 0))).astype(BF16)

    y_s, s_shift, s_wkv, s_side = _trunk(
        xs, s_new, P, W, state_shift, jnp.stack([_to_block_diag(state_wkv[l]) for l in range(N_A)]),
        lambda l: (cache_mem_k, l, cache_mem_v, l), sample_side, sample_attend, chunk=SUBLANE, tb=SUBLANE)
    y_sample = y_s[:, :s_new]
    s_cmp_k, s_cmp_v, s_slc_k, s_slc_v, s_win_k, s_win_v = s_side

    return (y_prompt, y_sample, p_mem_k, p_mem_v, p_wkv, p_shift,
            p_cmp_k, p_cmp_v, p_slc_k, p_slc_v, p_win_k, p_win_v,
            s_wkv, s_shift, s_cmp_k, s_cmp_v, s_slc_k, s_slc_v, s_win_k, s_win_v)
```

```python
import functools
import math

import jax
import jax.numpy as jnp
from jax import lax
from jax.experimental import pallas as pl
from jax.experimental.pallas import tpu as pltpu

F32 = jnp.float32
BF16 = jnp.bfloat16
HI = lax.Precision.HIGHEST

D_MODEL = 2048
DEPTH = 4
N_A = 2
MEM_HEADS = 4
MEM_HEAD_DIM = 128
MEM_W = 512
MAIN_W = 1536
RWKV_HEAD = 64
RWKV_HEADS = 24
GN_EPS = 64e-5
NSA_HEAD_DIM = 128
NSA_Q_HEADS = 12
NSA_KV_GROUPS = 4
NSA_HPG = 3
GATE_W = 36
CMP_BLOCK = 32
CMP_STRIDE = 16
SEL_BLOCK = 64
SEL_TOPK = 16
SEL_LOCAL = 2
WINDOW = 512
REL_BUCKETS = 32
REL_MAX_DIST = 128
D_FF = 5504
NORM_EPS = 1e-6
MASK_NEG = -1e30
FORCE_SCORE = 1e9

LANE = 128
SUBLANE = 8
MXU_DIM = 256
VMEM_LIMIT = 56 * 1024 * 1024

FFN_DOWN_TK = 11 * LANE
HEAD_BLOCK = MXU_DIM // RWKV_HEAD
N_HEAD_BLOCKS = RWKV_HEADS // HEAD_BLOCK
LOWRANK_PAD = MXU_DIM
NSA_TQ = 256
NSA_KT = 256
NSA_BANDS = 3


def _params(*sem):
    return pltpu.CompilerParams(dimension_semantics=sem, vmem_limit_bytes=VMEM_LIMIT)


def _rms(x, g):
    return x * lax.rsqrt(jnp.mean(x * x, axis=-1, keepdims=True) + NORM_EPS) * g


def _sigmoid(x):
    return 1.0 / (1.0 + jnp.exp(-x))


def _softplus(x):
    return jnp.maximum(x, 0.0) + jnp.log(1.0 + jnp.exp(-jnp.abs(x)))


def _gelu_tanh(x):
    return 0.5 * x * (1.0 + jnp.tanh(math.sqrt(2.0 / math.pi) * (x + 0.044715 * x * x * x)))


def _dot(a, b, precision=None):
    return jnp.dot(a, b, preferred_element_type=F32, precision=precision)


def _dot_nt(a, b, precision=None):
    return lax.dot_general(a, b, (((1,), (1,)), ((), ())), preferred_element_type=F32,
                           precision=precision)


def _dot_tn(a, b, precision=None):
    return lax.dot_general(a, b, (((0,), (0,)), ((), ())), preferred_element_type=F32,
                           precision=precision)


BF16_BITS_OF_F32 = 0xFFFF0000


def _split_hi_lo(x):
    bits = lax.bitcast_convert_type(x, jnp.uint32) & jnp.uint32(BF16_BITS_OF_F32)
    hi = lax.bitcast_convert_type(bits, F32)
    return hi, x - hi


def _masked_softmax(s, mask):
    s = jnp.where(mask, s, MASK_NEG)
    p = jnp.exp(s - jnp.max(s, axis=-1, keepdims=True)) * mask.astype(F32)
    den = jnp.sum(p, axis=-1, keepdims=True)
    return p / jnp.where(den > 0, den, 1.0)


def _row_tile(rows, target):
    t = min(rows, target)
    while rows % t:
        t -= SUBLANE
    return t


def _norm_matmul_kernel(x_ref, g_ref, w_ref, o_ref, xn_ref):
    @pl.when(pl.program_id(1) == 0)
    def _():
        xn_ref[...] = _rms(x_ref[...], g_ref[...]).astype(BF16)

    o_ref[...] = _dot(xn_ref[...], w_ref[...]).astype(o_ref.dtype)


def norm_matmul(x, g, w, tn, out_dtype=F32, tm=512):
    rows, d = x.shape
    n = w.shape[1]
    tm = _row_tile(rows, tm)
    assert n % tn == 0
    return pl.pallas_call(
        _norm_matmul_kernel,
        grid=(rows // tm, n // tn),
        in_specs=[pl.BlockSpec((tm, d), lambda i, j: (i, 0)),
                  pl.BlockSpec((1, d), lambda i, j: (0, 0)),
                  pl.BlockSpec((d, tn), lambda i, j: (0, j))],
        out_specs=pl.BlockSpec((tm, tn), lambda i, j: (i, j)),
        out_shape=jax.ShapeDtypeStruct((rows, n), out_dtype),
        scratch_shapes=[pltpu.VMEM((tm, d), BF16)],
        compiler_params=_params("arbitrary", "arbitrary"),
        name="norm_matmul",
    )(x, g.reshape(1, d), w)


def _norm_swiglu_kernel(x_ref, g_ref, wg_ref, wu_ref, o_ref, xn_ref, *, tail):
    j = pl.program_id(1)
    last = pl.num_programs(1) - 1

    @pl.when(j == 0)
    def _():
        xn_ref[...] = _rms(x_ref[...], g_ref[...]).astype(BF16)

    xn = xn_ref[...]
    tn = o_ref.shape[1]
    sub = min(tn, MXU_DIM)

    def tile(up_shift):
        c0 = 0
        while c0 < tn - up_shift:
            w = min(sub, tn - up_shift - c0)
            gate = _dot(xn, wg_ref[0, 0, :, c0:c0 + w])
            up = _dot(xn, wu_ref[0, 0, :, c0 + up_shift:c0 + up_shift + w])
            o_ref[:, c0:c0 + w] = (gate * _sigmoid(gate) * up).astype(o_ref.dtype)
            c0 += w
        if up_shift:
            o_ref[:, c0:] = jnp.zeros((o_ref.shape[0], up_shift), o_ref.dtype)

    if tail == tn:
        tile(0)
    else:
        pl.when(j != last)(lambda: tile(0))
        pl.when(j == last)(lambda: tile(tn - tail))


def norm_swiglu(x, g, w_gu, l, i, tn=512, tm=512):
    rows, d = x.shape
    f = w_gu.shape[-1] // 2
    assert f % LANE == 0 and tn % LANE == 0
    tm = _row_tile(rows, tm)
    n_tiles = -(-f // tn)
    tail = f - (n_tiles - 1) * tn
    el = pl.Element

    def up_col(j):
        return pl.multiple_of(jnp.minimum(f + j * tn, 2 * f - tn), LANE)

    return pl.pallas_call(
        functools.partial(_norm_swiglu_kernel, tail=tail),
        grid=(rows // tm, n_tiles),
        in_specs=[pl.BlockSpec((tm, d), lambda r, j: (r, 0)),
                  pl.BlockSpec((1, d), lambda r, j: (0, 0)),
                  pl.BlockSpec((el(1), el(1), el(d), el(tn)), lambda r, j: (l, i, 0, j * tn)),
                  pl.BlockSpec((el(1), el(1), el(d), el(tn)), lambda r, j: (l, i, 0, up_col(j)))],
        out_specs=pl.BlockSpec((tm, tn), lambda r, j: (r, j)),
        out_shape=jax.ShapeDtypeStruct((rows, n_tiles * tn), BF16),
        scratch_shapes=[pltpu.VMEM((tm, d), BF16)],
        compiler_params=_params("arbitrary", "arbitrary"),
        name="norm_swiglu",
    )(x, g.reshape(1, d), w_gu, w_gu)


def _matmul_norm_res_kernel(a_ref, w_ref, g_ref, h_ref, o_ref, acc_ref, *, scale, overlap):
    k = pl.program_id(1)
    last = pl.num_programs(1) - 1

    @pl.when(k == 0)
    def _():
        acc_ref[...] = jnp.zeros_like(acc_ref)

    a = a_ref[...]
    if overlap:
        col = lax.broadcasted_iota(jnp.int32, a.shape, 1)
        a = jnp.where(col < jnp.where(k == last, overlap, 0), jnp.zeros_like(a), a)
    acc_ref[...] += _dot(a, w_ref[0, 0])

    @pl.when(k == last)
    def _():
        o_ref[...] = h_ref[...] + scale * _rms(acc_ref[...], g_ref[...])


def matmul_norm_res(a, w, l, i, g, h, scale, tk=512, tm=512):
    rows = a.shape[0]
    kdim, d = w.shape[-2:]
    assert kdim % LANE == 0 and tk % LANE == 0 and a.shape[1] >= kdim
    tm = _row_tile(rows, tm)
    n_k = -(-kdim // tk)
    overlap = n_k * tk - kdim
    el = pl.Element

    def k_off(k):
        return pl.multiple_of(jnp.minimum(k * tk, kdim - tk), LANE)

    return pl.pallas_call(
        functools.partial(_matmul_norm_res_kernel, scale=scale, overlap=overlap),
        grid=(rows // tm, n_k),
        in_specs=[pl.BlockSpec((el(tm), el(tk)), lambda r, k: (r * tm, k_off(k))),
                  pl.BlockSpec((el(1), el(1), el(tk), el(d)), lambda r, k: (l, i, k_off(k), 0)),
                  pl.BlockSpec((1, d), lambda r, k: (0, 0)),
                  pl.BlockSpec((tm, d), lambda r, k: (r, 0))],
        out_specs=pl.BlockSpec((tm, d), lambda r, k: (r, 0)),
        out_shape=jax.ShapeDtypeStruct((rows, d), F32),
        scratch_shapes=[pltpu.VMEM((tm, d), F32)],
        compiler_params=_params("arbitrary", "arbitrary"),
        name="matmul_norm_res",
    )(a, w, g.reshape(1, d), h)


def _out_proj_kernel(main_ref, mo_ref, w_ref, g_ref, h_ref, o_ref):
    k_main = main_ref.shape[1]
    y = _dot(main_ref[...], w_ref[0, :k_main, :]) + _dot(mo_ref[...], w_ref[0, k_main:, :])
    o_ref[...] = h_ref[...] + _rms(y, g_ref[...])


def out_proj(main, mo, w, l, g, h, tm=512):
    rows, k_main = main.shape
    k_mo = mo.shape[1]
    d = w.shape[-1]
    tm = _row_tile(rows, tm)
    return pl.pallas_call(
        _out_proj_kernel,
        grid=(rows // tm,),
        in_specs=[pl.BlockSpec((tm, k_main), lambda r: (r, 0)),
                  pl.BlockSpec((tm, k_mo), lambda r: (r, 0)),
                  pl.BlockSpec((1, k_main + k_mo, d), lambda r: (l, 0, 0)),
                  pl.BlockSpec((1, d), lambda r: (0, 0)),
                  pl.BlockSpec((tm, d), lambda r: (r, 0))],
        out_specs=pl.BlockSpec((tm, d), lambda r: (r, 0)),
        out_shape=jax.ShapeDtypeStruct((rows, d), F32),
        compiler_params=_params("arbitrary"),
        name="out_proj",
    )(main, mo, w, g.reshape(1, d), h)


RW_R, RW_K, RW_V, RW_QM, RW_W1, RW_V1, RW_A1, RW_G1 = 0, 6, 12, 18, 20, 21, 22, 23
RW_NBLK = 24
RW_STEP_MIXES = ((0,), (0,), (0,), (2,), (2,), (2,), (3,), (3,), (3,), (None,), (1, 3), (4, 5))
RW_FIRST_SPLIT_STEP = 10


def _rwkv_in_kernel(x_ref, sp_ref, g_ref, mu_ref, w_ref, o_ref, last_ref,
                    u_sc, xx_sc, xm_sc, carry_sc, *, tm, t_seq, n_seq, last_tile, last_row):
    ti = pl.program_id(1)
    j = pl.program_id(2)

    @pl.when(j == 0)
    def _():
        @pl.when(ti == 0)
        def _():
            carry_sc[...] = sp_ref[0]

        u = _rms(x_ref[0], g_ref[...])
        rows = lax.broadcasted_iota(jnp.int32, u.shape, 0)
        prev = pltpu.roll(u, 1, axis=0)
        for s in range(n_seq):
            prev = jnp.where(rows == s * t_seq, carry_sc[s:s + 1, :], prev)
        u_sc[...] = u
        xx_sc[...] = prev - u
        if n_seq == 1:
            carry_sc[...] = u[tm - 1:tm, :]

        @pl.when(ti == last_tile)
        def _():
            for s in range(n_seq):
                last_ref[0, s:s + 1, :] = u[s * t_seq + last_row:s * t_seq + last_row + 1, :]

    def set_mix(mix):
        if mix is None:
            xm_sc[...] = u_sc[...].astype(BF16)
        else:
            xm_sc[...] = (u_sc[...] + xx_sc[...] * mu_ref[mix:mix + 1, :]).astype(BF16)

    tn = o_ref.shape[2]
    for step, mixes in enumerate(RW_STEP_MIXES):
        if len(mixes) == 1:
            assert step < RW_FIRST_SPLIT_STEP
            if step == 0 or RW_STEP_MIXES[step - 1] != mixes:
                pl.when(j == step)(functools.partial(set_mix, mixes[0]))
        else:
            assert step >= RW_FIRST_SPLIT_STEP

            @pl.when(j == step)
            def _(mixes=mixes):
                part = tn // len(mixes)
                for q, mix in enumerate(mixes):
                    set_mix(mix)
                    o_ref[0, :, q * part:(q + 1) * part] = _dot(xm_sc[...],
                                                                w_ref[:, q * part:(q + 1) * part])

    @pl.when(j < RW_FIRST_SPLIT_STEP)
    def _():
        o_ref[0] = _dot(xm_sc[...], w_ref[...])


def rwkv_in_proj(h, shift_prev, g, mu, w_cat, t_real, tm=1024):
    b, t, d = h.shape
    n_seq = math.gcd(b, max(1, tm // t))
    bg, tg = b // n_seq, t * n_seq
    tm = _row_tile(tg, tm)
    assert n_seq == 1 or tm == tg
    n_steps = len(RW_STEP_MIXES)
    tn = RW_NBLK * MXU_DIM // n_steps
    last_tile, last_row = ((t_real - 1) // tm, (t_real - 1) % tm) if n_seq == 1 else (0, t_real - 1)
    mu8 = jnp.concatenate([mu, jnp.zeros((SUBLANE - mu.shape[0], d), F32)], axis=0)
    proj, last = pl.pallas_call(
        functools.partial(_rwkv_in_kernel, tm=tm, t_seq=t, n_seq=n_seq, last_tile=last_tile,
                          last_row=last_row),
        grid=(bg, tg // tm, n_steps),
        in_specs=[pl.BlockSpec((1, tm, d), lambda bi, ti, j: (bi, ti, 0)),
                  pl.BlockSpec((1, n_seq, d), lambda bi, ti, j: (bi, 0, 0)),
                  pl.BlockSpec((1, d), lambda bi, ti, j: (0, 0)),
                  pl.BlockSpec((SUBLANE, d), lambda bi, ti, j: (0, 0)),
                  pl.BlockSpec((d, tn), lambda bi, ti, j: (0, j))],
        out_specs=[pl.BlockSpec((1, tm, tn), lambda bi, ti, j: (bi, ti, j)),
                   pl.BlockSpec((1, n_seq, d), lambda bi, ti, j: (bi, 0, 0))],
        out_shape=[jax.ShapeDtypeStruct((bg, tg, n_steps * tn), F32),
                   jax.ShapeDtypeStruct((bg, n_seq, d), F32)],
        scratch_shapes=[pltpu.VMEM((tm, d), F32), pltpu.VMEM((tm, d), F32),
                        pltpu.VMEM((tm, d), BF16), pltpu.VMEM((n_seq, d), F32)],
        compiler_params=_params("arbitrary", "arbitrary", "arbitrary"),
        name="rwkv_in_proj",
    )(h.reshape(bg, tg, d), shift_prev.reshape(bg, n_seq, d), g.reshape(1, d), mu8, w_cat)
    return proj.reshape(b, t, n_steps * tn), last.reshape(b, d)


def _rwkv_scan_kernel(*refs, chunk, n_chunks, has_vres, t_valid, n_hb):
    if has_vres:
        (r_ref, k_ref, v_ref, tw_ref, ta_ref, tg_ref, tv_ref, vf_ref,
         w2_ref, a2_ref, g2_ref, v2_ref, vec_ref, s0_ref, y_ref, sout_ref, s_sc) = refs
    else:
        (r_ref, k_ref, v_ref, tw_ref, ta_ref, tg_ref,
         w2_ref, a2_ref, g2_ref, vec_ref, s0_ref, y_ref, sout_ref, s_sc) = refs
    ti = pl.program_id(2)
    c_len = chunk
    lanes = MXU_DIM
    rows4 = HEAD_BLOCK * c_len

    @pl.when(ti == 0)
    def _():
        s_sc[...] = s0_ref[0]

    li = lax.broadcasted_iota(jnp.int32, (lanes, lanes), 0) // RWKV_HEAD
    lj = lax.broadcasted_iota(jnp.int32, (lanes, lanes), 1) // RWKV_HEAD
    seg = (li == lj).astype(BF16)
    ci = lax.broadcasted_iota(jnp.int32, (c_len, c_len), 0)
    cj = lax.broadcasted_iota(jnp.int32, (c_len, c_len), 1)
    tri_c = (cj <= ci).astype(BF16)

    split2 = _split_hi_lo

    def seg_sums(xs):
        parts = [p for x in xs for p in split2(x)]
        res = _dot(jnp.concatenate(parts, axis=0).astype(BF16), seg)
        return [res[(2 * n) * c_len:(2 * n + 1) * c_len] + res[(2 * n + 1) * c_len:(2 * n + 2) * c_len]
                for n in range(len(xs))]

    lane_head = lax.broadcasted_iota(jnp.int32, (c_len, lanes), 1) // RWKV_HEAD
    step_head = lax.broadcasted_iota(jnp.int32, (c_len, rows4), 1) // c_len
    t_row = lax.broadcasted_iota(jnp.int32, (c_len, 2 * rows4), 0)
    s_col = lax.broadcasted_iota(jnp.int32, (c_len, 2 * rows4), 1) & (c_len - 1)
    strict = s_col < t_row
    incl = s_col <= t_row
    eye_row = ((lax.broadcasted_iota(jnp.int32, (c_len, rows4), 1) & (c_len - 1))
               == lax.broadcasted_iota(jnp.int32, (c_len, rows4), 0)).astype(F32)
    same_head = li == lj
    n_double = int(math.log2(c_len)) - 1

    def stack(x):
        return jnp.concatenate(
            [jnp.where(lane_head == hh, x, 0.0) for hh in range(HEAD_BLOCK)], axis=0)

    def stack_steps(x):
        return jnp.concatenate(
            [jnp.where(step_head == hh, x, 0.0) for hh in range(HEAD_BLOCK)], axis=0)

    def one_chunk(c, carry):
        sl = pl.ds(pl.multiple_of(c * c_len, c_len), c_len)
        tw_act = jnp.tanh(tw_ref[0, sl, :])
        ta_act = ta_ref[0, sl, :]
        tg_act = _sigmoid(tg_ref[0, sl, :])
        tv_act = tv_ref[0, sl, :] if has_vres else None
        if t_valid is not None:
            t_idx = ti * (n_chunks * c_len) + c * c_len + lax.broadcasted_iota(
                jnp.int32, (c_len, lanes), 0)
            live = t_idx < t_valid
        st = [dict() for _ in range(n_hb)]

        def prep(hb):
            e = st[hb]
            hl = slice(hb * lanes, (hb + 1) * lanes)
            w0, a0, v0 = vec_ref[0:1, hl], vec_ref[1:2, hl], vec_ref[2:3, hl]
            kkw, kaw, rk = vec_ref[3:4, hl], vec_ref[4:5, hl], vec_ref[7:8, hl]
            r = r_ref[0, sl, hl]
            k = k_ref[0, sl, hl]
            v = v_ref[0, sl, hl]
            logw = -_softplus(-(w0 + _dot(tw_act, w2_ref[:, hl]))) - 0.5
            dlog = -jnp.exp(logw)
            rate = _sigmoid(a0 + _dot(ta_act, a2_ref[:, hl]))
            e['gate'] = _dot(tg_act, g2_ref[:, hl])
            if has_vres:
                v = v + (vf_ref[0, sl, hl] - v) * _sigmoid(v0 + _dot(tv_act, v2_ref[:, hl]))
            kk = k * kkw
            k = k * (1.0 + (rate - 1.0) * kaw)
            kk_sq, rk_sum = seg_sums([kk * kk, r * k * rk])
            kk = kk / jnp.maximum(jnp.sqrt(kk_sq), 1e-12)
            if t_valid is not None:
                dlog = jnp.where(live, dlog, 0.0)
                kk = jnp.where(live, kk, 0.0)
                k_live = jnp.where(live, k, 0.0)
            else:
                k_live = k
            d_hi, d_lo = split2(dlog)
            cum2 = _dot(tri_c, jnp.concatenate([d_hi, d_lo], axis=1).astype(BF16))
            cum = cum2[:, :lanes] + cum2[:, lanes:]
            inv = jnp.exp(-cum)
            b_row = kk * rate * inv
            k_row = k_live * inv
            e['ar'] = jnp.concatenate([-kk * jnp.exp(cum - dlog), r * jnp.exp(cum)],
                                      axis=0).astype(BF16)
            e['bk_row'] = jnp.concatenate([b_row, k_row], axis=0).astype(BF16)
            e['bk'] = jnp.concatenate([stack(b_row), stack(k_row)],
                                      axis=0).astype(BF16)
            e['v'] = v
            e['v_s'] = stack(v).astype(BF16)
            e['g_end'] = jnp.exp(cum[c_len - 1:c_len, :])
            e['bonus'] = rk_sum * v

        def products(hb):
            e = st[hb]
            e['s_old'] = s_sc[hb]
            big = _dot_nt(e['ar'], jnp.concatenate([e['bk'], e['s_old'].astype(BF16)], axis=0))
            a_bk = jnp.where(strict, big[:c_len, :2 * rows4], 0.0)
            e['r_bk'] = jnp.where(incl, big[c_len:, :2 * rows4], 0.0)
            e['a_s0'], e['r_s0'] = big[:c_len, 2 * rows4:], big[c_len:, 2 * rows4:]
            e['a_k'] = a_bk[:, rows4:]
            lrow = a_bk[:, :rows4]
            e['tinv'] = eye_row + lrow
            e['lpow'] = lrow

        def mx(x):
            return x.astype(BF16) if x.shape[0] % (2 * SUBLANE) == 0 else x

        def square(hb):
            e = st[hb]
            e['lpow'] = _dot(mx(e['lpow']), mx(stack_steps(e['lpow'])))

        def double(hb):
            e = st[hb]
            both = _dot(mx(jnp.concatenate([e['lpow'], e['tinv']], axis=0)), mx(stack_steps(e['lpow'])))
            e['tinv'] = e['tinv'] + both[c_len:]
            e['lpow'] = both[:c_len]

        def solve(hb):
            e = st[hb]
            tinv = e['tinv'] + _dot(mx(e['tinv']), mx(stack_steps(e['lpow'])))
            rhs = e['a_s0'] + _dot(mx(e['a_k']), e['v_s'])
            e['u'] = _dot(mx(tinv), mx(stack(rhs)))

        def outputs(hb):
            e = st[hb]
            uv_s = jnp.concatenate([stack(e['u']).astype(BF16), e['v_s']], axis=0)
            e['y'] = e['r_s0'] + _dot(mx(e['r_bk']), uv_s)
            uv_row = jnp.concatenate([e['u'], e['v']], axis=0).astype(BF16)
            s_sc[hb] = (e['s_old'] + jnp.where(same_head, _dot_tn(uv_row, e['bk_row']), 0.0)) * e['g_end']

        def group_norm(hb):
            e = st[hb]
            hl = slice(hb * lanes, (hb + 1) * lanes)
            y = e['y']
            mean = seg_sums([y])[0] * (1.0 / RWKV_HEAD)
            yc = y - mean
            var = seg_sums([yc * yc])[0] * (1.0 / RWKV_HEAD)
            yn = yc * lax.rsqrt(var + GN_EPS) * vec_ref[5:6, hl] + vec_ref[6:7, hl]
            y_ref[0, sl, hl] = ((yn + e['bonus']) * e['gate']).astype(y_ref.dtype)

        for stage in [prep, products, square] + [double] * (n_double - 1) + [solve, outputs, group_norm]:
            for hb in range(n_hb):
                stage(hb)
        return carry

    lax.fori_loop(0, n_chunks, one_chunk, 0)

    @pl.when(ti == pl.num_programs(2) - 1)
    def _():
        sout_ref[0] = s_sc[...]


def rwkv_scan(proj, v_first_proj, w2, a2, g2, v2, vec, s0_bd, t_real, chunk, tb, n_hb=6):
    b, t, _ = proj.shape
    lanes = MXU_DIM
    tb = min(tb, t)
    assert t % tb == 0 and tb % chunk == 0 and N_HEAD_BLOCKS % n_hb == 0
    has_vres = v_first_proj is not None
    t_valid = None if t_real == t else t_real

    wide = n_hb * lanes

    def col(block0):
        assert block0 % n_hb == 0
        return pl.BlockSpec((1, tb, wide), lambda bi, hg, ti: (bi, ti, block0 // n_hb + hg))

    def fixed(block):
        return pl.BlockSpec((1, tb, lanes), lambda bi, hg, ti: (bi, ti, block))

    def wcol():
        return pl.BlockSpec((LOWRANK_PAD, wide), lambda bi, hg, ti: (0, hg))

    st = pl.BlockSpec((1, n_hb, lanes, lanes), lambda bi, hg, ti: (bi, hg, 0, 0))
    if has_vres:
        in_specs = [col(RW_R), col(RW_K), col(RW_V), fixed(RW_W1), fixed(RW_A1), fixed(RW_G1),
                    fixed(RW_V1), col(RW_V), wcol(), wcol(), wcol(), wcol()]
        args = [proj, proj, proj, proj, proj, proj, proj, v_first_proj, w2, a2, g2, v2]
    else:
        in_specs = [col(RW_R), col(RW_K), col(RW_V), fixed(RW_W1), fixed(RW_A1), fixed(RW_G1),
                    wcol(), wcol(), wcol()]
        args = [proj, proj, proj, proj, proj, proj, w2, a2, g2]
    in_specs += [pl.BlockSpec((SUBLANE, wide), lambda bi, hg, ti: (0, hg)), st]
    args += [vec, s0_bd]
    return pl.pallas_call(
        functools.partial(_rwkv_scan_kernel, chunk=chunk, n_chunks=tb // chunk,
                          has_vres=has_vres, t_valid=t_valid, n_hb=n_hb),
        grid=(b, N_HEAD_BLOCKS // n_hb, t // tb),
        in_specs=in_specs,
        out_specs=[pl.BlockSpec((1, tb, wide), lambda bi, hg, ti: (bi, ti, hg)), st],
        out_shape=[jax.ShapeDtypeStruct((b, t, MAIN_W), BF16),
                   jax.ShapeDtypeStruct(s0_bd.shape, F32)],
        scratch_shapes=[pltpu.VMEM((n_hb, lanes, lanes), F32)],
        compiler_params=_params("arbitrary", "arbitrary", "arbitrary"),
        name="rwkv_scan",
    )(*args)


def _to_block_diag(s):
    b = s.shape[0]
    s = s.reshape(b, N_HEAD_BLOCKS, HEAD_BLOCK, RWKV_HEAD, RWKV_HEAD)
    eye = jnp.eye(HEAD_BLOCK, dtype=s.dtype)
    bd = s[:, :, :, :, None, :] * eye[None, None, :, None, :, None]
    return bd.reshape(b, N_HEAD_BLOCKS, MXU_DIM, MXU_DIM)


def _from_block_diag(bd):
    b = bd.shape[0]
    n = RWKV_HEAD
    x = jnp.stack([bd[:, :, hh * n:(hh + 1) * n, hh * n:(hh + 1) * n] for hh in range(HEAD_BLOCK)],
                  axis=2)
    return x.reshape(b, RWKV_HEADS, RWKV_HEAD, RWKV_HEAD)


def _mem_attn_kernel(q_ref, k_ref, v_ref, o_ref, *, per_head):
    scale = MEM_HEAD_DIM ** -0.5
    for hh in range(MEM_HEADS):
        sl = slice(hh * MEM_HEAD_DIM, (hh + 1) * MEM_HEAD_DIM)
        k = k_ref[0, 0, :, hh, :] if per_head else k_ref[0, :, sl]
        v = v_ref[0, 0, :, hh, :] if per_head else v_ref[0, :, sl]
        s = _dot_nt(q_ref[0, :, sl], k) * scale
        p = jnp.exp(s - jnp.max(s, axis=-1, keepdims=True))
        p = p / jnp.sum(p, axis=-1, keepdims=True)
        o_ref[0, :, sl] = _dot(p, v).astype(o_ref.dtype)


def mem_attend(qsrc, q_block, ksrc, k_block, vsrc, v_block, tq=512):
    b, t, _ = qsrc.shape
    per_head = ksrc.ndim == 5
    m = ksrc.shape[-3] if per_head else ksrc.shape[1]
    tq = _row_tile(t, tq)
    if per_head:
        kv_spec = lambda blk: pl.BlockSpec((1, 1, m, MEM_HEADS, MEM_HEAD_DIM),
                                           lambda bi, ti: (blk, bi, 0, 0, 0))
    else:
        kv_spec = lambda blk: pl.BlockSpec((1, m, MEM_W), lambda bi, ti: (bi, 0, blk))
    return pl.pallas_call(
        functools.partial(_mem_attn_kernel, per_head=per_head),
        grid=(b, t // tq),
        in_specs=[pl.BlockSpec((1, tq, MEM_W), lambda bi, ti: (bi, ti, q_block)),
                  kv_spec(k_block), kv_spec(v_block)],
        out_specs=pl.BlockSpec((1, tq, MEM_W), lambda bi, ti: (bi, ti, 0)),
        out_shape=jax.ShapeDtypeStruct((b, t, MEM_W), BF16),
        compiler_params=_params("arbitrary", "arbitrary"),
        name="mem_attend",
    )(qsrc, ksrc, vsrc)


CMP_PAGE = 128
CMP_CHUNKS = CMP_PAGE // CMP_STRIDE
KV_LANES = NSA_KV_GROUPS * NSA_HEAD_DIM


def _chunk_proj_kernel(pt_ref, *refs, n_pages):
    del pt_ref
    page_refs = refs[:n_pages]
    w_ref, o0_ref, o1_ref, x_sc = refs[n_pages:]
    for kp in range(n_pages):
        for pos in range(CMP_STRIDE):
            x_sc[kp * CMP_CHUNKS:(kp + 1) * CMP_CHUNKS, pos * LANE:(pos + 1) * LANE] = (
                page_refs[kp][pl.ds(pos, CMP_CHUNKS, stride=CMP_STRIDE), :])
    res = _dot(x_sc[...].astype(BF16), w_ref[...])
    o0_ref[0] = res[:, :LANE]
    o1_ref[0] = res[:, LANE:]


def chunk_proj(rows2d, table, w_flat, n_pages, col_block=0):
    b, n_tab = table.shape
    assert n_tab % n_pages == 0

    def page_spec(kp):
        return pl.BlockSpec(
            (CMP_PAGE, LANE),
            lambda bi, pg, g, pt: (pt[bi, pg * n_pages + kp], col_block * NSA_KV_GROUPS + g))

    out_spec = pl.BlockSpec((1, n_pages * CMP_CHUNKS, LANE), lambda bi, pg, g, pt: (bi, pg, g))
    out_shape = jax.ShapeDtypeStruct((b, n_tab * CMP_CHUNKS, KV_LANES), F32)
    grid_spec = pltpu.PrefetchScalarGridSpec(
        num_scalar_prefetch=1,
        grid=(b, n_tab // n_pages, NSA_KV_GROUPS),
        in_specs=[page_spec(kp) for kp in range(n_pages)]
        + [pl.BlockSpec(w_flat.shape, lambda bi, pg, g, pt: (0, 0))],
        out_specs=[out_spec, out_spec],
        scratch_shapes=[pltpu.VMEM((n_pages * CMP_CHUNKS, CMP_STRIDE * LANE), F32)],
    )
    return pl.pallas_call(
        functools.partial(_chunk_proj_kernel, n_pages=n_pages),
        grid_spec=grid_spec,
        out_shape=[out_shape, out_shape],
        compiler_params=_params("arbitrary", "arbitrary", "arbitrary"),
        name="chunk_proj",
    )(table, *([rows2d] * n_pages), w_flat)


def _chunk_proj_pool_kernel(pt_ref, *refs, n_pages):
    del pt_ref
    page_refs = refs[:n_pages]
    w_ref, o0_ref, o1_ref, x_sc = refs[n_pages:]
    for g in range(NSA_KV_GROUPS):
        for kp in range(n_pages):
            for pos in range(CMP_STRIDE):
                x_sc[kp * CMP_CHUNKS:(kp + 1) * CMP_CHUNKS, pos * LANE:(pos + 1) * LANE] = (
                    page_refs[kp][0, pl.ds(pos, CMP_CHUNKS, stride=CMP_STRIDE), g, :])
        res = _dot(x_sc[...].astype(BF16), w_ref[...])
        o0_ref[0, :, g * LANE:(g + 1) * LANE] = res[:, :LANE]
        o1_ref[0, :, g * LANE:(g + 1) * LANE] = res[:, LANE:]


def chunk_proj_pool(pool, table, w_flat, n_pages):
    b, n_tab = table.shape
    assert n_tab % n_pages == 0

    def page_spec(kp):
        return pl.BlockSpec((1, CMP_PAGE, NSA_KV_GROUPS, LANE),
                            lambda bi, pg, pt: (pt[bi, pg * n_pages + kp], 0, 0, 0))

    out_spec = pl.BlockSpec((1, n_pages * CMP_CHUNKS, KV_LANES), lambda bi, pg, pt: (bi, pg, 0))
    out_shape = jax.ShapeDtypeStruct((b, n_tab * CMP_CHUNKS, KV_LANES), F32)
    grid_spec = pltpu.PrefetchScalarGridSpec(
        num_scalar_prefetch=1,
        grid=(b, n_tab // n_pages),
        in_specs=[page_spec(kp) for kp in range(n_pages)]
        + [pl.BlockSpec(w_flat.shape, lambda bi, pg, pt: (0, 0))],
        out_specs=[out_spec, out_spec],
        scratch_shapes=[pltpu.VMEM((n_pages * CMP_CHUNKS, CMP_STRIDE * LANE), F32)],
    )
    return pl.pallas_call(
        functools.partial(_chunk_proj_pool_kernel, n_pages=n_pages),
        grid_spec=grid_spec,
        out_shape=[out_shape, out_shape],
        compiler_params=_params("arbitrary", "arbitrary"),
        name="chunk_proj_pool",
    )(table, *([pool] * n_pages), w_flat)


def _block_mlp_kernel(p0_ref, p1_ref, pe_ref, w1_ref, b1_ref, w2_ref, o_ref):
    const = _dot(pe_ref[...], w1_ref[...])[0:1, :] + b1_ref[...]
    w2 = w2_ref[...]
    for g in range(NSA_KV_GROUPS):
        sl = slice(g * LANE, (g + 1) * LANE)
        hid = const + p0_ref[0, :, sl] + p1_ref[0, :, sl]
        o_ref[0, :, sl] = _dot(_gelu_tanh(hid), w2)


def block_mlp(p0, p1, pe, w1, b1, w2, tb=1024):
    b, nb, _ = p0.shape
    tb = _row_tile(nb, tb)
    pe8 = jnp.concatenate([pe.reshape(1, -1), jnp.zeros((SUBLANE - 1, pe.size), F32)], axis=0)
    spec = pl.BlockSpec((1, tb, KV_LANES), lambda bi, i: (bi, i, 0))
    full = lambda a: pl.BlockSpec(a.shape, lambda bi, i: (0,) * a.ndim)
    b1r = b1.reshape(1, -1)
    return pl.pallas_call(
        _block_mlp_kernel,
        grid=(b, nb // tb),
        in_specs=[spec, spec, full(pe8), full(w1), full(b1r), full(w2)],
        out_specs=spec,
        out_shape=jax.ShapeDtypeStruct((b, nb, KV_LANES), F32),
        compiler_params=_params("arbitrary", "arbitrary"),
        name="block_mlp",
    )(p0, p1, pe8, w1, b1r, w2)


def _w1_flat(w1):
    r = CMP_BLOCK // CMP_STRIDE
    e = w1.shape[1]
    w = w1.reshape(r, CMP_STRIDE, NSA_HEAD_DIM, e)
    return jnp.transpose(w, (1, 2, 0, 3)).reshape(CMP_STRIDE * NSA_HEAD_DIM, r * e).astype(BF16)


def compress(parts, nb, pe, w1, b1, w2):
    return block_mlp(parts[0][:, :nb], parts[1][:, 1:nb + 1], pe, w1, b1, w2)


def _rel_bucket(dist):
    n = jnp.maximum(dist, 0)
    max_exact = REL_BUCKETS // 2
    nf = jnp.maximum(n, 1).astype(F32)
    large = max_exact + (jnp.log(nf / max_exact) / math.log(REL_MAX_DIST / max_exact)
                         * (REL_BUCKETS - max_exact)).astype(jnp.int32)
    return jnp.where(n < max_exact, n, jnp.minimum(large, REL_BUCKETS - 1))


def _bias_lookup(rel_bias, dist):
    tab = rel_bias.astype(F32).reshape(REL_BUCKETS, NSA_KV_GROUPS, NSA_HPG)
    onehot = jax.nn.one_hot(_rel_bucket(dist), REL_BUCKETS, dtype=F32)
    out = jnp.dot(onehot, tab.reshape(REL_BUCKETS, -1), precision=HI)
    out = out.reshape(dist.shape + (NSA_KV_GROUPS, NSA_HPG))
    return jnp.moveaxis(out, (-2, -1), (0, 1))


def _cmp_to_sel(nb_pad, nb, nsb_pad, nsb):
    i = jnp.arange(nb_pad)[:, None]
    j = jnp.arange(nsb_pad)[None, :]
    start = i * CMP_STRIDE
    hit = (start < (j + 1) * SEL_BLOCK) & (start + CMP_BLOCK > j * SEL_BLOCK) & (i < nb) & (j < nsb)
    return hit.astype(F32)


def _nsa_prompt_kernel(q_ref, gt_ref, kc_ref, vc_ref, bc_ref, band_ref, ks_ref, vs_ref,
                       kw_ref, vw_ref, m_ref, o_ref, s_sc, *, nb, nsb):
    tq = NSA_TQ
    rows = NSA_HPG * tq
    i = pl.program_id(2)
    scale = NSA_HEAD_DIM ** -0.5
    qb = q_ref[0]
    q3 = jnp.concatenate([qb[:, hh * LANE:(hh + 1) * LANE] for hh in range(NSA_HPG)],
                         axis=0).astype(BF16)
    row_q = lax.broadcasted_iota(jnp.int32, (rows, LANE), 0) & (tq - 1)
    lane = lax.broadcasted_iota(jnp.int32, (rows, LANE), 1)
    t_pos = i * tq + row_q
    kt_w = NSA_KT
    tiles_per_kt = kt_w // tq
    key_lane = lax.broadcasted_iota(jnp.int32, (rows, kt_w), 1)
    t_pos_k = i * tq + (lax.broadcasted_iota(jnp.int32, (rows, kt_w), 0) & (tq - 1))

    def masked_scores(k_ref, kt, mask_of):
        ksl = pl.ds(pl.multiple_of(kt * kt_w, kt_w), kt_w)
        band = band_ref[0, jnp.clip(i - kt * tiles_per_kt, 0, NSA_BANDS - 1)].reshape(rows, kt_w)
        s = _dot_nt(q3, k_ref[0, ksl, :].astype(BF16)) * scale + band
        return jnp.where(mask_of(t_pos_k - (kt * kt_w + key_lane)), s, MASK_NEG), ksl

    n_win_tiles = WINDOW // kt_w + 1
    win_tiles = []
    m_w = jnp.full((rows, 1), MASK_NEG, F32)
    for jw in range(n_win_tiles):
        kt_raw = i // tiles_per_kt - (n_win_tiles - 1) + jw
        reach = jnp.where(kt_raw >= 0, WINDOW, 0)
        s_w, ksl_w = masked_scores(kw_ref, jnp.maximum(kt_raw, 0),
                                   lambda dist, reach=reach: (dist >= 0) & (dist < reach))
        win_tiles.append((s_w, ksl_w))
        m_w = jnp.maximum(m_w, jnp.max(s_w, axis=-1, keepdims=True))

    s_c = _dot_nt(q3, kc_ref[0].astype(BF16)) * scale + bc_ref[0].reshape(rows, LANE)
    m_c = (t_pos - (lane * CMP_STRIDE + (CMP_BLOCK - 1)) >= 0) & (lane < nb)
    p_c = _masked_softmax(s_c, m_c)
    o_c = _dot(p_c, vc_ref[0])

    p_hi, p_lo = _split_hi_lo(p_c)
    sel_map = m_ref[...].astype(BF16)
    imp3 = _dot_nt(sel_map, p_hi.astype(BF16)) + _dot_nt(sel_map, p_lo.astype(BF16))
    imp = imp3[:, 0:tq]
    for hh in range(1, NSA_HPG):
        imp = imp + imp3[:, hh * tq:(hh + 1) * tq]
    jb = lax.broadcasted_iota(jnp.int32, (nsb, tq), 0)
    cur = (i * tq + lax.broadcasted_iota(jnp.int32, (nsb, tq), 1)) // SEL_BLOCK
    valid = jb <= cur
    forced = valid & ((jb == 0) | (jb > cur - SEL_LOCAL))
    imp = jnp.where(forced, FORCE_SCORE, jnp.where(valid, imp, -FORCE_SCORE))
    rank = jnp.zeros((nsb, tq), jnp.int32)
    for jp in range(nsb):
        other = imp[jp:jp + 1, :]
        rank = rank + ((other > imp) | ((other == imp) & (jb > jp))).astype(jnp.int32)
    sel = (rank < min(SEL_TOPK, nsb)).astype(F32).T
    sel3 = jnp.concatenate([sel] * NSA_HPG, axis=0).astype(BF16)
    blk_row = lax.broadcasted_iota(jnp.int32, (nsb, kt_w), 0)
    blk_lane = lax.broadcasted_iota(jnp.int32, (nsb, kt_w), 1) // SEL_BLOCK

    def finish(m_fin, l_fin, acc):
        return jnp.where(m_fin > MASK_NEG, acc / jnp.where(l_fin > 0, l_fin, 1.0), 0.0)

    l_w = jnp.zeros((rows, 1), F32)
    acc_w = jnp.zeros((rows, LANE), F32)
    for s_w, ksl_w in win_tiles:
        p_w = jnp.exp(s_w - m_w)
        l_w = l_w + jnp.sum(p_w, axis=-1, keepdims=True)
        acc_w = acc_w + _dot(p_w, vw_ref[0, ksl_w, :])
    o_w = finish(m_w, l_w, acc_w)

    n_kt = i // tiles_per_kt + 1

    def scores(kt, m_run):
        expand = (blk_row == kt * (kt_w // SEL_BLOCK) + blk_lane).astype(BF16)
        chosen = _dot(sel3, expand) > 0.5
        s, _ = masked_scores(ks_ref, kt, lambda dist: chosen & (dist >= 0))
        s_sc[kt] = s
        return jnp.maximum(m_run, jnp.max(s, axis=-1, keepdims=True))

    m_s = lax.fori_loop(0, n_kt, scores, jnp.full((rows, 1), MASK_NEG, F32))

    def accum(kt, carry):
        l_run, acc = carry
        ksl = pl.ds(pl.multiple_of(kt * kt_w, kt_w), kt_w)
        p = jnp.exp(s_sc[kt] - m_s)
        return (l_run + jnp.sum(p, axis=-1, keepdims=True), acc + _dot(p, vs_ref[0, ksl, :]))

    l_s, acc_s = lax.fori_loop(0, n_kt, accum, (jnp.zeros((rows, 1), F32),
                                                jnp.zeros((rows, LANE), F32)))
    o_s = finish(m_s, l_s, acc_s)

    gt = _sigmoid(gt_ref[0, 0])
    gcol = lambda br: jnp.concatenate(
        [gt[:, br * NSA_HPG + hh:br * NSA_HPG + hh + 1] for hh in range(NSA_HPG)], axis=0)
    o = gcol(0) * o_c + gcol(1) * o_s + gcol(2) * o_w
    for hh in range(NSA_HPG):
        o_ref[0, :, hh * LANE:(hh + 1) * LANE] = o[hh * tq:(hh + 1) * tq].astype(o_ref.dtype)


def nsa_prompt(proj, gates_t, kc, vc, bias_c, band, side, sel_map, nb, nsb):
    b, t, _ = proj.shape
    tq = NSA_TQ
    qw = NSA_HPG * LANE
    kv = lambda off: pl.BlockSpec((1, t, LANE), lambda bi, g, i: (bi, 0, off * NSA_KV_GROUPS + g))
    return pl.pallas_call(
        functools.partial(_nsa_prompt_kernel, nb=nb, nsb=nsb),
        grid=(b, NSA_KV_GROUPS, t // tq),
        in_specs=[pl.BlockSpec((1, tq, qw), lambda bi, g, i: (bi, i, g)),
                  pl.BlockSpec((1, 1, tq, NSA_HPG * 3), lambda bi, g, i: (bi, g, i, 0)),
                  pl.BlockSpec((1, LANE, LANE), lambda bi, g, i: (bi, 0, g)),
                  pl.BlockSpec((1, LANE, LANE), lambda bi, g, i: (bi, 0, g)),
                  pl.BlockSpec((1, NSA_HPG, tq, LANE), lambda bi, g, i: (g, 0, i, 0)),
                  pl.BlockSpec((1, NSA_BANDS, NSA_HPG, tq, NSA_KT), lambda bi, g, i: (g, 0, 0, 0, 0)),
                  kv(2), kv(3), kv(4), kv(5),
                  pl.BlockSpec(sel_map.shape, lambda bi, g, i: (0, 0))],
        out_specs=pl.BlockSpec((1, tq, qw), lambda bi, g, i: (bi, i, g)),
        out_shape=jax.ShapeDtypeStruct((b, t, MAIN_W), BF16),
        scratch_shapes=[pltpu.VMEM((t // NSA_KT, NSA_HPG * tq, NSA_KT), F32)],
        compiler_params=_params("arbitrary", "arbitrary", "arbitrary"),
        name="nsa_prompt",
    )(proj, gates_t, kc, vc, bias_c, band, side, side, side, side, sel_map)


def _nsa_dec_cmp_kernel(q_ref, kc_ref, vc_ref, bc_ref, m_ref, oc_ref, idx_ref, *, nb, nsb, t_pos):
    scale = NSA_HEAD_DIM ** -0.5
    nbp = kc_ref.shape[1]
    nsp = m_ref.shape[1]
    n_sel = min(SEL_TOPK, nsb)
    lane_b = lax.broadcasted_iota(jnp.int32, (SUBLANE, nbp), 1)
    m_c = (t_pos - (lane_b * CMP_STRIDE + (CMP_BLOCK - 1)) >= 0) & (lane_b < nb)
    row_s = lax.broadcasted_iota(jnp.int32, (SUBLANE, nsp), 0)
    jb = lax.broadcasted_iota(jnp.int32, (1, nsp), 1)
    cur = t_pos // SEL_BLOCK
    valid = jb <= cur
    forced = valid & ((jb == 0) | (jb > cur - SEL_LOCAL))
    out_lane = lax.broadcasted_iota(jnp.int32, (1, LANE), 1)
    idx_rows = []
    for g in range(NSA_KV_GROUPS):
        q3 = jnp.concatenate(
            [q_ref[0, 0:1, (g * NSA_HPG + hh) * LANE:(g * NSA_HPG + hh + 1) * LANE]
             for hh in range(NSA_HPG)] + [jnp.zeros((SUBLANE - NSA_HPG, LANE), F32)], axis=0)
        sl = slice(g * LANE, (g + 1) * LANE)
        s_c = _dot_nt(q3, kc_ref[0, :, sl]) * scale + bc_ref[g]
        p_c = _masked_softmax(s_c, m_c)
        oc_ref[0, g] = _dot(p_c, vc_ref[0, :, sl])
        imp8 = jnp.where(row_s < NSA_HPG, _dot(p_c, m_ref[...], HI), 0.0)
        imp = jnp.sum(imp8, axis=0, keepdims=True)
        imp = jnp.where(forced, FORCE_SCORE, jnp.where(valid, imp, -FORCE_SCORE))
        imp = jnp.where(jb < nsb, imp, -jnp.inf)
        jbf = jb.astype(F32)
        picks = jnp.zeros((1, LANE), F32)
        for kk in range(n_sel):
            best = jnp.max(imp, axis=-1, keepdims=True)
            arg = jnp.min(jnp.where(imp == best, jbf, float(nsp)), axis=-1, keepdims=True)
            picks = jnp.where(out_lane == kk, arg, picks)
            imp = jnp.where(jbf == arg, -jnp.inf, imp)
        idx_rows.append(picks.astype(jnp.int32))
    idx_rows.append(jnp.zeros((SUBLANE - NSA_KV_GROUPS, LANE), jnp.int32))
    idx_ref[0] = jnp.concatenate(idx_rows, axis=0)


def nsa_dec_cmp(proj, kc, vc, bias_c, sel_map, nb, nsb, t_pos):
    b, tp, _ = proj.shape
    nbp = kc.shape[1]
    return pl.pallas_call(
        functools.partial(_nsa_dec_cmp_kernel, nb=nb, nsb=nsb, t_pos=t_pos),
        grid=(b,),
        in_specs=[pl.BlockSpec((1, tp, MAIN_W), lambda bi: (bi, 0, 0)),
                  pl.BlockSpec((1, nbp, KV_LANES), lambda bi: (bi, 0, 0)),
                  pl.BlockSpec((1, nbp, KV_LANES), lambda bi: (bi, 0, 0)),
                  pl.BlockSpec(bias_c.shape, lambda bi: (0, 0, 0)),
                  pl.BlockSpec(sel_map.shape, lambda bi: (0, 0))],
        out_specs=[pl.BlockSpec((1, NSA_KV_GROUPS, SUBLANE, LANE), lambda bi: (bi, 0, 0, 0)),
                   pl.BlockSpec((1, SUBLANE, LANE), lambda bi: (bi, 0, 0))],
        out_shape=[jax.ShapeDtypeStruct((b, NSA_KV_GROUPS, SUBLANE, LANE), F32),
                   jax.ShapeDtypeStruct((b, SUBLANE, LANE), jnp.int32)],
        compiler_params=_params("arbitrary"),
        name="nsa_dec_cmp",
    )(proj, kc, vc, bias_c, sel_map)


def _nsa_dec_sel_kernel(phys_ref, isnew_ref, *refs, n_sel, n_win, t_pos, win_start):
    del phys_ref
    kb_refs = refs[:n_sel]
    vb_refs = refs[n_sel:2 * n_sel]
    (q_ref, new_ref, ds_ref, bs_ref, wk_ref, wv_ref, bw_ref, oc_ref, gt_ref, o_ref) = refs[2 * n_sel:]
    bi = pl.program_id(0)
    g = pl.program_id(1)
    scale = NSA_HEAD_DIM ** -0.5
    q3 = q_ref[0, 0]
    new_rows = new_ref[0, 0]
    pad_blk = jnp.zeros((SEL_BLOCK - 1, LANE), F32)
    new_k = jnp.concatenate([new_rows[0:1], pad_blk], axis=0)
    new_v = jnp.concatenate([new_rows[1:2], pad_blk], axis=0)
    ks, vs = [], []
    for kk in range(n_sel):
        fresh = isnew_ref[bi, g, kk] > 0
        ks.append(jnp.where(fresh, new_k, kb_refs[kk][0, :, g, :]))
        vs.append(jnp.where(fresh, new_v, vb_refs[kk][0, :, g, :]))
    ks = jnp.concatenate(ks, axis=0)
    vs = jnp.concatenate(vs, axis=0)
    s_s = _dot_nt(q3, ks) * scale + bs_ref[0, 0]
    p_s = _masked_softmax(s_s, ds_ref[0, 0] >= 0)
    o_s = _dot(p_s, vs)

    pad_w = jnp.zeros((SUBLANE - 1, LANE), F32)
    kw = jnp.concatenate([wk_ref[0, :, g, :], new_rows[2:3], pad_w], axis=0)
    vw = jnp.concatenate([wv_ref[0, :, g, :], new_rows[3:4], pad_w], axis=0)
    nw = kw.shape[0]
    pos = win_start + lax.broadcasted_iota(jnp.int32, (SUBLANE, nw), 1)
    d_w = t_pos - pos
    m_w = (d_w >= 0) & (d_w < WINDOW) & (pos >= 0) & (pos - win_start < n_win)
    s_w = _dot_nt(q3, kw) * scale + bw_ref[0]
    p_w = _masked_softmax(s_w, m_w)
    o_w = _dot(p_w, vw)

    gt = _sigmoid(gt_ref[0, 0])
    o_ref[0, 0] = gt[:, 0:1] * oc_ref[0, 0] + gt[:, 1:2] * o_s + gt[:, 2:3] * o_w


def nsa_dec_sel(phys, isnew, pool_k, pool_v, q8, new_rows, d_s, bias_s, win_k, win_v, bias_w,
                o_c, gates8, n_win, t_pos, win_start):
    b = q8.shape[0]
    n_sel = phys.shape[-1]
    n_keys = n_sel * SEL_BLOCK
    nw = win_k.shape[1]
    nwp = nw + SUBLANE

    def blk_spec(kk):
        return pl.BlockSpec((1, SEL_BLOCK, NSA_KV_GROUPS, LANE),
                            lambda bi, g, ph, nf: (ph[bi, g, kk], 0, 0, 0))

    per_bg = lambda *shape: pl.BlockSpec((1, 1) + shape, lambda bi, g, ph, nf: (bi, g) + (0,) * len(shape))
    grid_spec = pltpu.PrefetchScalarGridSpec(
        num_scalar_prefetch=2,
        grid=(b, NSA_KV_GROUPS),
        in_specs=[blk_spec(kk) for kk in range(n_sel)] + [blk_spec(kk) for kk in range(n_sel)]
        + [per_bg(SUBLANE, LANE), per_bg(SUBLANE, LANE), per_bg(1, n_keys), per_bg(SUBLANE, n_keys),
           pl.BlockSpec((1, nw, NSA_KV_GROUPS, LANE), lambda bi, g, ph, nf: (bi, 0, 0, 0)),
           pl.BlockSpec((1, nw, NSA_KV_GROUPS, LANE), lambda bi, g, ph, nf: (bi, 0, 0, 0)),
           pl.BlockSpec((1, SUBLANE, nwp), lambda bi, g, ph, nf: (g, 0, 0)),
           per_bg(SUBLANE, LANE), per_bg(SUBLANE, SUBLANE)],
        out_specs=per_bg(SUBLANE, LANE),
    )
    return pl.pallas_call(
        functools.partial(_nsa_dec_sel_kernel, n_sel=n_sel, n_win=nw + 1, t_pos=t_pos,
                          win_start=win_start),
        grid_spec=grid_spec,
        out_shape=jax.ShapeDtypeStruct((b, NSA_KV_GROUPS, SUBLANE, LANE), F32),
        compiler_params=_params("arbitrary", "arbitrary"),
        name="nsa_dec_sel",
    )(phys, isnew, *([pool_k] * n_sel), *([pool_v] * n_sel), q8, new_rows, d_s, bias_s,
      win_k, win_v, bias_w, o_c, gates8)


def _pad_cols(w, n):
    return jnp.pad(w, ((0, 0), (0, n - w.shape[1])))


def _pad_rows(w, n):
    return jnp.pad(w, ((0, n - w.shape[0]), (0, 0)))


def _prep_weights(P):
    W = {}
    W['w_gu'] = P['ffn_gu'].astype(BF16)
    W['w_d'] = P['ffn_d'].astype(BF16)
    W['w_out'] = P['w_out'].astype(BF16)
    W['rw_in'], W['rw_w2'], W['rw_a2'], W['rw_g2'], W['rw_v2'], W['rw_vec'] = [], [], [], [], [], []
    lp = LOWRANK_PAD
    for l in range(N_A):
        w_in = P['w_in_a'][l]
        v1 = P['rw_v1'][l - 1] if l > 0 else jnp.zeros((D_MODEL, lp), F32)
        W['rw_in'].append(jnp.concatenate([
            w_in, _pad_cols(P['rw_w1'][l], lp), _pad_cols(v1, lp), _pad_cols(P['rw_a1'][l], lp),
            _pad_cols(P['rw_g1'][l], lp)], axis=1).astype(BF16))
        W['rw_w2'].append(_pad_rows(P['rw_w2'][l], lp))
        W['rw_a2'].append(_pad_rows(P['rw_a2'][l], lp))
        W['rw_g2'].append(_pad_rows(P['rw_g2'][l], lp))
        W['rw_v2'].append(_pad_rows(P['rw_v2'][l - 1], lp) if l > 0 else None)
        v0 = P['rw_v0'][l - 1] if l > 0 else jnp.zeros((MAIN_W,), F32)
        W['rw_vec'].append(jnp.stack([P['rw_w0'][l], P['rw_a0'][l], v0, P['rw_kk'][l], P['rw_ka'][l],
                                      P['rw_lnw'][l], P['rw_lnb'][l], P['rw_rk'][l].reshape(-1)]))
    W['w_in_b'] = []
    for l in range(DEPTH - N_A):
        w = P['w_in_b'][l]
        W['w_in_b'].append(jnp.concatenate([
            w[:, :MAIN_W], w[:, MAIN_W + GATE_W:], _pad_cols(w[:, MAIN_W:MAIN_W + GATE_W], MXU_DIM)],
            axis=1).astype(BF16))
    W['w_kv'] = P['w_kv'].astype(BF16)
    W['w_mem_kv'] = [P['w_mem_kv'][l].astype(BF16) for l in range(DEPTH)]
    W['cmp_w1_flat'] = [_w1_flat(P['cmp_w1'][c]) for c in range(2)]
    return W


B_QM_BLOCK = MAIN_W // MEM_W
B_GATE_OFF = MAIN_W + MEM_W


def _ffn(h2, norms_l, first, W, l, i):
    hid = norm_swiglu(h2, norms_l[first], W['w_gu'], l, i, tn=1024, tm=1024)
    return matmul_norm_res(hid, W['w_d'], l, i, norms_l[first + 1], h2, 0.5, tk=FFN_DOWN_TK)


def _trunk(x, t_real, P, W, shift0, wkv0_bd, mem_src, make_side, attend, chunk, tb):
    b, t, d = x.shape
    h2 = x.reshape(b * t, d)
    shifts, states = [], []
    v_first_proj, ctx, side_state = None, None, None
    for l in range(DEPTH):
        n = P['norms'][l]
        h2 = _ffn(h2, n, 0, W, l, 0)
        if l < N_A:
            proj, last = rwkv_in_proj(h2.reshape(b, t, d), shift0[l], n[2], P['rw_mu'][l],
                                      W['rw_in'][l], t_real)
            main, s_bd = rwkv_scan(proj, v_first_proj if l > 0 else None, W['rw_w2'][l], W['rw_a2'][l],
                                   W['rw_g2'][l], W['rw_v2'][l], W['rw_vec'][l], wkv0_bd[l],
                                   t_real, chunk, tb)
            if l == 0:
                v_first_proj = proj
            shifts.append(last)
            states.append(_from_block_diag(s_bd))
            q_src, q_block = proj, RW_QM * MXU_DIM // MEM_W
        else:
            proj = norm_matmul(h2, n[2], W['w_in_b'][l - N_A], tn=768, tm=1024).reshape(b, t, -1)
            main = attend(proj, ctx)
            q_src, q_block = proj, B_QM_BLOCK
        mk, kb, mv, vb = mem_src(l)
        mo = mem_attend(q_src, q_block, mk, kb, mv, vb)
        h2 = out_proj(main.reshape(b * t, MAIN_W), mo.reshape(b * t, MEM_W), W['w_out'], l, n[3], h2)
        h2 = _ffn(h2, n, 4, W, l, 1)
        if l == N_A - 1:
            side = norm_matmul(h2, P['kv_norm'], W['w_kv'], tn=768, tm=1024).reshape(b, t, -1)
            ctx, side_state = make_side(side)
    return h2.reshape(b, t, d), jnp.stack(shifts), jnp.stack(states), side_state


def kernel(x_prompt, x_sample, mem_prompt, state_wkv, state_shift, cache_mem_k, cache_mem_v,
           cache_cmp_k, cache_cmp_v, cache_slc_k, cache_slc_v, cache_win_k, cache_win_v, page_table,
           norms, ffn_gu, ffn_d, w_in_a, w_in_b, w_out, mem_norm, w_mem_kv, kv_norm, w_kv,
           cmp_pe, cmp_w1, cmp_b1, cmp_w2, rel_bias,
           rw_mu, rw_w0, rw_w1, rw_w2, rw_a0, rw_a1, rw_a2, rw_g1, rw_g2, rw_v0, rw_v1, rw_v2,
           rw_kk, rw_ka, rw_rk, rw_lnw, rw_lnb):
    P = dict(norms=norms, ffn_gu=ffn_gu, ffn_d=ffn_d, w_in_a=w_in_a, w_in_b=w_in_b, w_out=w_out,
             kv_norm=kv_norm, w_kv=w_kv, w_mem_kv=w_mem_kv, cmp_w1=cmp_w1, rw_mu=rw_mu, rw_w0=rw_w0,
             rw_w1=rw_w1, rw_w2=rw_w2, rw_a0=rw_a0, rw_a1=rw_a1, rw_a2=rw_a2, rw_g1=rw_g1,
             rw_g2=rw_g2, rw_v0=rw_v0, rw_v1=rw_v1, rw_v2=rw_v2, rw_kk=rw_kk, rw_ka=rw_ka,
             rw_rk=rw_rk, rw_lnw=rw_lnw, rw_lnb=rw_lnb)
    W = _prep_weights(P)
    G, dh = NSA_KV_GROUPS, NSA_HEAD_DIM
    split_side = lambda side, bx, t: [side[:, :t, c * KV_LANES:(c + 1) * KV_LANES].reshape(bx, t, G, dh)
                                      for c in range(6)]

    def cmp_mlp(parts, nb, c):
        return compress(parts, nb, cmp_pe[c], cmp_w1[c], cmp_b1[c], cmp_w2[c])

    bp, tp, d = x_prompt.shape
    n_mem = mem_prompt.shape[1]
    mem2 = mem_prompt.reshape(bp * n_mem, d)
    p_mkv = [norm_matmul(mem2, mem_norm[l], W['w_mem_kv'][l], tn=512).reshape(bp, n_mem, 2 * MEM_W)
             for l in range(DEPTH)]
    p_mem_k = jnp.stack([m[..., :MEM_W].reshape(bp, n_mem, MEM_HEADS, MEM_HEAD_DIM) for m in p_mkv])
    p_mem_v = jnp.stack([m[..., MEM_W:].reshape(bp, n_mem, MEM_HEADS, MEM_HEAD_DIM) for m in p_mkv])

    nb_p = tp // CMP_STRIDE - 1
    nsb_p = tp // SEL_BLOCK
    nq_tiles = tp // NSA_TQ
    t_all = jnp.arange(tp)
    c_end = jnp.arange(LANE) * CMP_STRIDE + (CMP_BLOCK - 1)
    assert nb_p <= LANE
    bias_c_p = _bias_lookup(rel_bias, t_all[:, None] - c_end[None, :])
    ii = jnp.arange(NSA_TQ)
    cc = jnp.arange(NSA_KT)
    band_p = jnp.stack([_bias_lookup(rel_bias, dd * NSA_TQ + ii[:, None] - cc[None, :])
                        for dd in range(NSA_BANDS)], axis=1)
    assert (NSA_BANDS - 1) * NSA_TQ - (NSA_KT - 1) >= REL_MAX_DIST and tp % NSA_KT == 0
    sel_map_p = _cmp_to_sel(LANE, nb_p, nsb_p, nsb_p).T
    ident = jnp.arange(bp * tp // CMP_PAGE, dtype=jnp.int32).reshape(bp, tp // CMP_PAGE)

    def prompt_side(side):
        rows = side.reshape(bp * tp, -1)
        kc = cmp_mlp(chunk_proj(rows, ident, W['cmp_w1_flat'][0], tp // CMP_PAGE, 0), nb_p, 0)
        vc = cmp_mlp(chunk_proj(rows, ident, W['cmp_w1_flat'][1], tp // CMP_PAGE, 1), nb_p, 1)
        padb = ((0, 0), (0, LANE - nb_p), (0, 0))
        wb = min(WINDOW, tp)
        kc_r, vc_r, ks, vs, kw, vw = split_side(side, bp, tp)
        return ((jnp.pad(kc, padb), jnp.pad(vc, padb), side),
                (kc_r, vc_r, ks, vs, kw[:, tp - wb:], vw[:, tp - wb:]))

    def prompt_attend(proj, ctx):
        kc, vc, side = ctx
        gates = proj[..., B_GATE_OFF:B_GATE_OFF + GATE_W].reshape(bp, tp, 3, G, NSA_HPG)
        gates_t = jnp.transpose(gates, (0, 3, 1, 2, 4)).reshape(bp, G, tp, 3 * NSA_HPG)
        return nsa_prompt(proj, gates_t, kc, vc, bias_c_p, band_p, side, sel_map_p, nb_p, nsb_p)

    zeros_shift = jnp.zeros((N_A, bp, d), F32)
    zeros_state = jnp.zeros((N_A, bp, N_HEAD_BLOCKS, MXU_DIM, MXU_DIM), F32)
    y_prompt, p_shift, p_wkv, p_side = _trunk(
        x_prompt, tp, P, W, zeros_shift, zeros_state,
        lambda l: (p_mkv[l], 0, p_mkv[l], 1), prompt_side, prompt_attend, chunk=64, tb=256)
    p_cmp_k, p_cmp_v, p_slc_k, p_slc_v, p_win_k, p_win_v = p_side

    bd, s_new, _ = x_sample.shape
    assert s_new == 1
    ts = SUBLANE
    xs = jnp.pad(x_sample, ((0, 0), (0, ts - s_new), (0, 0)))
    past_len = page_table.shape[1] * CMP_PAGE
    n_past_blk = past_len // SEL_BLOCK
    blk_per_page = CMP_PAGE // SEL_BLOCK
    nsb_s = n_past_blk + 1
    t_pos = past_len
    nc_s = -(-(past_len + s_new) // CMP_STRIDE)
    nb_s = nc_s - 1
    wb_s = cache_win_k.shape[1]
    win_start = past_len - wb_s
    nsp = -(-nsb_s // LANE) * LANE
    sel_map_s = _cmp_to_sel(nb_s, nb_s, nsp, nsb_s)
    c_end_s = jnp.arange(nb_s) * CMP_STRIDE + (CMP_BLOCK - 1)
    bias_c_s = _bias_lookup(rel_bias, t_pos - c_end_s)
    bias_c_s = jnp.pad(bias_c_s, ((0, 0), (0, SUBLANE - NSA_HPG), (0, 0)))
    nwp = wb_s + SUBLANE
    bias_w_s = _bias_lookup(rel_bias, t_pos - (win_start + jnp.arange(nwp)))
    bias_w_s = jnp.pad(bias_w_s, ((0, 0), (0, SUBLANE - NSA_HPG), (0, 0)))
    ident_s = jnp.arange(bd, dtype=jnp.int32).reshape(bd, 1)
    pages_per_step = math.gcd(16, page_table.shape[1])

    def sample_side(side):
        new = side[:, :s_new]
        parts = []
        for c, pool in enumerate((cache_cmp_k, cache_cmp_v)):
            past = chunk_proj_pool(pool, page_table, W['cmp_w1_flat'][c], pages_per_step)
            fresh_page = jnp.pad(new[:, :, c * KV_LANES:(c + 1) * KV_LANES],
                                 ((0, 0), (0, CMP_PAGE - s_new), (0, 0))).reshape(bd, CMP_PAGE, G, dh)
            fresh = chunk_proj_pool(fresh_page, ident_s, W['cmp_w1_flat'][c], 1)
            n_fresh = nc_s - past[0].shape[1]
            parts.append([jnp.concatenate([p, f[:, :n_fresh]], axis=1) for p, f in zip(past, fresh)])
        kc = cmp_mlp(parts[0], nb_s, 0)
        vc = cmp_mlp(parts[1], nb_s, 1)
        kc_n, vc_n, ks_n, vs_n, kw_n, vw_n = split_side(side, bd, s_new)
        s_win_k = jnp.concatenate([cache_win_k, kw_n], axis=1)[:, s_new:]
        s_win_v = jnp.concatenate([cache_win_v, vw_n], axis=1)[:, s_new:]
        return (kc, vc, new), (kc_n, vc_n, ks_n, vs_n, s_win_k, s_win_v)

    def sample_attend(proj, ctx):
        kc, vc, new = ctx
        o_c, idx8 = nsa_dec_cmp(proj, kc, vc, bias_c_s, sel_map_s, nb_s, nsb_s, t_pos)
        idx = idx8[:, :G, :SEL_TOPK]
        is_new = idx >= n_past_blk
        jp = jnp.minimum(idx, n_past_blk - 1)
        phys = (jnp.take_along_axis(page_table[:, None, :], jp // blk_per_page, axis=2) * blk_per_page
                + jp % blk_per_page)
        k_pos = (idx[..., None] * SEL_BLOCK + jnp.arange(SEL_BLOCK)).reshape(bd, G, 1, -1)
        d_s = t_pos - k_pos
        tab = rel_bias.astype(F32).reshape(REL_BUCKETS, G, NSA_HPG)
        onehot = jax.nn.one_hot(_rel_bucket(d_s[:, :, 0]), REL_BUCKETS, dtype=F32)
        bias_s = jnp.einsum('bgkn,ngh->bghk', onehot, tab, precision=HI)
        bias_s = jnp.pad(bias_s, ((0, 0), (0, 0), (0, SUBLANE - NSA_HPG), (0, 0)))
        q8 = jnp.pad(proj[:, 0, :MAIN_W].reshape(bd, G, NSA_HPG, dh),
                     ((0, 0), (0, 0), (0, SUBLANE - NSA_HPG), (0, 0)))
        new_rows = jnp.pad(jnp.transpose(new[:, 0, 2 * KV_LANES:].reshape(bd, 4, G, dh), (0, 2, 1, 3)),
                           ((0, 0), (0, 0), (0, SUBLANE - 4), (0, 0)))
        gates = proj[:, 0, B_GATE_OFF:B_GATE_OFF + GATE_W].reshape(bd, 3, G, NSA_HPG)
        gates8 = jnp.pad(jnp.transpose(gates, (0, 2, 3, 1)),
                         ((0, 0), (0, 0), (0, SUBLANE - NSA_HPG), (0, SUBLANE - 3)))
        o = nsa_dec_sel(phys.astype(jnp.int32), is_new.astype(jnp.int32),
                        cache_slc_k.reshape(-1, SEL_BLOCK, G, dh),
                        cache_slc_v.reshape(-1, SEL_BLOCK, G, dh),
                        q8, new_rows, d_s.astype(jnp.int32), bias_s,
                        cache_win_k, cache_win_v,
                        bias_w_s, o_c, gates8, wb_s, t_pos, win_start)
        main = o[:, :, :NSA_HPG].reshape(bd, 1, MAIN_W)
        return jnp.pad(main, ((0, 0), (0, ts - 1), (0, 0))).astype(BF16)

    y_s, s_shift, s_wkv, s_side = _trunk(
        xs, s_new, P, W, state_shift, jnp.stack([_to_block_diag(state_wkv[l]) for l in range(N_A)]),
        lambda l: (cache_mem_k, l, cache_mem_v, l), sample_side, sample_attend, chunk=SUBLANE, tb=SUBLANE)
    y_sample = y_s[:, :s_new]
    s_cmp_k, s_cmp_v, s_slc_k, s_slc_v, s_win_k, s_win_v = s_side

    return (y_prompt, y_sample, p_mem_k, p_mem_v, p_wkv, p_shift,
            p_cmp_k, p_cmp_v, p_slc_k, p_slc_v, p_win_k, p_win_v,
            s_wkv, s_shift, s_cmp_k, s_cmp_v, s_slc_k, s_slc_v, s_win_k, s_win_v)
```

```python
import functools
import math

import jax
import jax.numpy as jnp
from jax import lax
from jax.experimental import pallas as pl
from jax.experimental.pallas import tpu as pltpu

F32 = jnp.float32
BF16 = jnp.bfloat16
HI = lax.Precision.HIGHEST

D_MODEL = 2048
DEPTH = 4
N_A = 2
MEM_HEADS = 4
MEM_HEAD_DIM = 128
MEM_W = 512
MAIN_W = 1536
RWKV_HEAD = 64
RWKV_HEADS = 24
GN_EPS = 64e-5
NSA_HEAD_DIM = 128
NSA_Q_HEADS = 12
NSA_KV_GROUPS = 4
NSA_HPG = 3
GATE_W = 36
CMP_BLOCK = 32
CMP_STRIDE = 16
SEL_BLOCK = 64
SEL_TOPK = 16
SEL_LOCAL = 2
WINDOW = 512
REL_BUCKETS = 32
REL_MAX_DIST = 128
D_FF = 5504
NORM_EPS = 1e-6
MASK_NEG = -1e30
FORCE_SCORE = 1e9

LANE = 128
SUBLANE = 8
MXU_DIM = 256
VMEM_LIMIT = 56 * 1024 * 1024

FFN_DOWN_TK = 11 * LANE
HEAD_BLOCK = MXU_DIM // RWKV_HEAD
N_HEAD_BLOCKS = RWKV_HEADS // HEAD_BLOCK
LOWRANK_PAD = MXU_DIM
NSA_TQ = 256
NSA_KT = 256
NSA_BANDS = 3


def _params(*sem):
    return pltpu.CompilerParams(dimension_semantics=sem, vmem_limit_bytes=VMEM_LIMIT)


def _rms(x, g):
    return x * lax.rsqrt(jnp.mean(x * x, axis=-1, keepdims=True) + NORM_EPS) * g


def _sigmoid(x):
    return 1.0 / (1.0 + jnp.exp(-x))


def _softplus(x):
    return jnp.maximum(x, 0.0) + jnp.log(1.0 + jnp.exp(-jnp.abs(x)))


def _gelu_tanh(x):
    return 0.5 * x * (1.0 + jnp.tanh(math.sqrt(2.0 / math.pi) * (x + 0.044715 * x * x * x)))


def _dot(a, b, precision=None):
    return jnp.dot(a, b, preferred_element_type=F32, precision=precision)


def _dot_nt(a, b, precision=None):
    return lax.dot_general(a, b, (((1,), (1,)), ((), ())), preferred_element_type=F32,
                           precision=precision)


def _dot_tn(a, b, precision=None):
    return lax.dot_general(a, b, (((0,), (0,)), ((), ())), preferred_element_type=F32,
                           precision=precision)


BF16_BITS_OF_F32 = 0xFFFF0000


def _split_hi_lo(x):
    bits = lax.bitcast_convert_type(x, jnp.uint32) & jnp.uint32(BF16_BITS_OF_F32)
    hi = lax.bitcast_convert_type(bits, F32)
    return hi, x - hi


def _masked_softmax(s, mask):
    s = jnp.where(mask, s, MASK_NEG)
    p = jnp.exp(s - jnp.max(s, axis=-1, keepdims=True)) * mask.astype(F32)
    den = jnp.sum(p, axis=-1, keepdims=True)
    return p / jnp.where(den > 0, den, 1.0)


def _row_tile(rows, target):
    t = min(rows, target)
    while rows % t:
        t -= SUBLANE
    return t


def _norm_matmul_kernel(x_ref, g_ref, w_ref, o_ref, xn_ref):
    @pl.when(pl.program_id(1) == 0)
    def _():
        xn_ref[...] = _rms(x_ref[...], g_ref[...]).astype(BF16)

    o_ref[...] = _dot(xn_ref[...], w_ref[...]).astype(o_ref.dtype)


def norm_matmul(x, g, w, tn, out_dtype=F32, tm=512):
    rows, d = x.shape
    n = w.shape[1]
    tm = _row_tile(rows, tm)
    assert n % tn == 0
    return pl.pallas_call(
        _norm_matmul_kernel,
        grid=(rows // tm, n // tn),
        in_specs=[pl.BlockSpec((tm, d), lambda i, j: (i, 0)),
                  pl.BlockSpec((1, d), lambda i, j: (0, 0)),
                  pl.BlockSpec((d, tn), lambda i, j: (0, j))],
        out_specs=pl.BlockSpec((tm, tn), lambda i, j: (i, j)),
        out_shape=jax.ShapeDtypeStruct((rows, n), out_dtype),
        scratch_shapes=[pltpu.VMEM((tm, d), BF16)],
        compiler_params=_params("arbitrary", "arbitrary"),
        name="norm_matmul",
    )(x, g.reshape(1, d), w)


def _norm_swiglu_kernel(x_ref, g_ref, wg_ref, wu_ref, o_ref, xn_ref, *, tail):
    j = pl.program_id(1)
    last = pl.num_programs(1) - 1

    @pl.when(j == 0)
    def _():
        xn_ref[...] = _rms(x_ref[...], g_ref[...]).astype(BF16)

    xn = xn_ref[...]
    tn = o_ref.shape[1]
    sub = min(tn, MXU_DIM)

    def tile(up_shift):
        c0 = 0
        while c0 < tn - up_shift:
            w = min(sub, tn - up_shift - c0)
            gate = _dot(xn, wg_ref[0, 0, :, c0:c0 + w])
            up = _dot(xn, wu_ref[0, 0, :, c0 + up_shift:c0 + up_shift + w])
            o_ref[:, c0:c0 + w] = (gate * _sigmoid(gate) * up).astype(o_ref.dtype)
            c0 += w
        if up_shift:
            o_ref[:, c0:] = jnp.zeros((o_ref.shape[0], up_shift), o_ref.dtype)

    if tail == tn:
        tile(0)
    else:
        pl.when(j != last)(lambda: tile(0))
        pl.when(j == last)(lambda: tile(tn - tail))


def norm_swiglu(x, g, w_gu, l, i, tn=512, tm=512):
    rows, d = x.shape
    f = w_gu.shape[-1] // 2
    assert f % LANE == 0 and tn % LANE == 0
    tm = _row_tile(rows, tm)
    n_tiles = -(-f // tn)
    tail = f - (n_tiles - 1) * tn
    el = pl.Element

    def up_col(j):
        return pl.multiple_of(jnp.minimum(f + j * tn, 2 * f - tn), LANE)

    return pl.pallas_call(
        functools.partial(_norm_swiglu_kernel, tail=tail),
        grid=(rows // tm, n_tiles),
        in_specs=[pl.BlockSpec((tm, d), lambda r, j: (r, 0)),
                  pl.BlockSpec((1, d), lambda r, j: (0, 0)),
                  pl.BlockSpec((el(1), el(1), el(d), el(tn)), lambda r, j: (l, i, 0, j * tn)),
                  pl.BlockSpec((el(1), el(1), el(d), el(tn)), lambda r, j: (l, i, 0, up_col(j)))],
        out_specs=pl.BlockSpec((tm, tn), lambda r, j: (r, j)),
        out_shape=jax.ShapeDtypeStruct((rows, n_tiles * tn), BF16),
        scratch_shapes=[pltpu.VMEM((tm, d), BF16)],
        compiler_params=_params("arbitrary", "arbitrary"),
        name="norm_swiglu",
    )(x, g.reshape(1, d), w_gu, w_gu)


def _matmul_norm_res_kernel(a_ref, w_ref, g_ref, h_ref, o_ref, acc_ref, *, scale, overlap):
    k = pl.program_id(1)
    last = pl.num_programs(1) - 1

    @pl.when(k == 0)
    def _():
        acc_ref[...] = jnp.zeros_like(acc_ref)

    a = a_ref[...]
    if overlap:
        col = lax.broadcasted_iota(jnp.int32, a.shape, 1)
        a = jnp.where(col < jnp.where(k == last, overlap, 0), jnp.zeros_like(a), a)
    acc_ref[...] += _dot(a, w_ref[0, 0])

    @pl.when(k == last)
    def _():
        o_ref[...] = h_ref[...] + scale * _rms(acc_ref[...], g_ref[...])


def matmul_norm_res(a, w, l, i, g, h, scale, tk=512, tm=512):
    rows = a.shape[0]
    kdim, d = w.shape[-2:]
    assert kdim % LANE == 0 and tk % LANE == 0 and a.shape[1] >= kdim
    tm = _row_tile(rows, tm)
    n_k = -(-kdim // tk)
    overlap = n_k * tk - kdim
    el = pl.Element

    def k_off(k):
        return pl.multiple_of(jnp.minimum(k * tk, kdim - tk), LANE)

    return pl.pallas_call(
        functools.partial(_matmul_norm_res_kernel, scale=scale, overlap=overlap),
        grid=(rows // tm, n_k),
        in_specs=[pl.BlockSpec((el(tm), el(tk)), lambda r, k: (r * tm, k_off(k))),
                  pl.BlockSpec((el(1), el(1), el(tk), el(d)), lambda r, k: (l, i, k_off(k), 0)),
                  pl.BlockSpec((1, d), lambda r, k: (0, 0)),
                  pl.BlockSpec((tm, d), lambda r, k: (r, 0))],
        out_specs=pl.BlockSpec((tm, d), lambda r, k: (r, 0)),
        out_shape=jax.ShapeDtypeStruct((rows, d), F32),
        scratch_shapes=[pltpu.VMEM((tm, d), F32)],
        compiler_params=_params("arbitrary", "arbitrary"),
        name="matmul_norm_res",
    )(a, w, g.reshape(1, d), h)


def _out_proj_kernel(main_ref, mo_ref, w_ref, g_ref, h_ref, o_ref):
    k_main = main_ref.shape[1]
    y = _dot(main_ref[...], w_ref[0, :k_main, :]) + _dot(mo_ref[...], w_ref[0, k_main:, :])
    o_ref[...] = h_ref[...] + _rms(y, g_ref[...])


def out_proj(main, mo, w, l, g, h, tm=512):
    rows, k_main = main.shape
    k_mo = mo.shape[1]
    d = w.shape[-1]
    tm = _row_tile(rows, tm)
    return pl.pallas_call(
        _out_proj_kernel,
        grid=(rows // tm,),
        in_specs=[pl.BlockSpec((tm, k_main), lambda r: (r, 0)),
                  pl.BlockSpec((tm, k_mo), lambda r: (r, 0)),
                  pl.BlockSpec((1, k_main + k_mo, d), lambda r: (l, 0, 0)),
                  pl.BlockSpec((1, d), lambda r: (0, 0)),
                  pl.BlockSpec((tm, d), lambda r: (r, 0))],
        out_specs=pl.BlockSpec((tm, d), lambda r: (r, 0)),
        out_shape=jax.ShapeDtypeStruct((rows, d), F32),
        compiler_params=_params("arbitrary"),
        name="out_proj",
    )(main, mo, w, g.reshape(1, d), h)


RW_R, RW_K, RW_V, RW_QM, RW_W1, RW_V1, RW_A1, RW_G1 = 0, 6, 12, 18, 20, 21, 22, 23
RW_NBLK = 24
RW_STEP_MIXES = ((0,), (0,), (0,), (2,), (2,), (2,), (3,), (3,), (3,), (None,), (1, 3), (4, 5))
RW_FIRST_SPLIT_STEP = 10


def _rwkv_in_kernel(x_ref, sp_ref, g_ref, mu_ref, w_ref, o_ref, last_ref,
                    u_sc, xx_sc, xm_sc, carry_sc, *, tm, t_seq, n_seq, last_tile, last_row):
    ti = pl.program_id(1)
    j = pl.program_id(2)

    @pl.when(j == 0)
    def _():
        @pl.when(ti == 0)
        def _():
            carry_sc[...] = sp_ref[0]

        u = _rms(x_ref[0], g_ref[...])
        rows = lax.broadcasted_iota(jnp.int32, u.shape, 0)
        prev = pltpu.roll(u, 1, axis=0)
        for s in range(n_seq):
            prev = jnp.where(rows == s * t_seq, carry_sc[s:s + 1, :], prev)
        u_sc[...] = u
        xx_sc[...] = prev - u
        if n_seq == 1:
            carry_sc[...] = u[tm - 1:tm, :]

        @pl.when(ti == last_tile)
        def _():
            for s in range(n_seq):
                last_ref[0, s:s + 1, :] = u[s * t_seq + last_row:s * t_seq + last_row + 1, :]

    def set_mix(mix):
        if mix is None:
            xm_sc[...] = u_sc[...].astype(BF16)
        else:
            xm_sc[...] = (u_sc[...] + xx_sc[...] * mu_ref[mix:mix + 1, :]).astype(BF16)

    tn = o_ref.shape[2]
    for step, mixes in enumerate(RW_STEP_MIXES):
        if len(mixes) == 1:
            assert step < RW_FIRST_SPLIT_STEP
            if step == 0 or RW_STEP_MIXES[step - 1] != mixes:
                pl.when(j == step)(functools.partial(set_mix, mixes[0]))
        else:
            assert step >= RW_FIRST_SPLIT_STEP

            @pl.when(j == step)
            def _(mixes=mixes):
                part = tn // len(mixes)
                for q, mix in enumerate(mixes):
                    set_mix(mix)
                    o_ref[0, :, q * part:(q + 1) * part] = _dot(xm_sc[...],
                                                                w_ref[:, q * part:(q + 1) * part])

    @pl.when(j < RW_FIRST_SPLIT_STEP)
    def _():
        o_ref[0] = _dot(xm_sc[...], w_ref[...])


def rwkv_in_proj(h, shift_prev, g, mu, w_cat, t_real, tm=1024):
    b, t, d = h.shape
    n_seq = math.gcd(b, max(1, tm // t))
    bg, tg = b // n_seq, t * n_seq
    tm = _row_tile(tg, tm)
    assert n_seq == 1 or tm == tg
    n_steps = len(RW_STEP_MIXES)
    tn = RW_NBLK * MXU_DIM // n_steps
    last_tile, last_row = ((t_real - 1) // tm, (t_real - 1) % tm) if n_seq == 1 else (0, t_real - 1)
    mu8 = jnp.concatenate([mu, jnp.zeros((SUBLANE - mu.shape[0], d), F32)], axis=0)
    proj, last = pl.pallas_call(
        functools.partial(_rwkv_in_kernel, tm=tm, t_seq=t, n_seq=n_seq, last_tile=last_tile,
                          last_row=last_row),
        grid=(bg, tg // tm, n_steps),
        in_specs=[pl.BlockSpec((1, tm, d), lambda bi, ti, j: (bi, ti, 0)),
                  pl.BlockSpec((1, n_seq, d), lambda bi, ti, j: (bi, 0, 0)),
                  pl.BlockSpec((1, d), lambda bi, ti, j: (0, 0)),
                  pl.BlockSpec((SUBLANE, d), lambda bi, ti, j: (0, 0)),
                  pl.BlockSpec((d, tn), lambda bi, ti, j: (0, j))],
        out_specs=[pl.BlockSpec((1, tm, tn), lambda bi, ti, j: (bi, ti, j)),
                   pl.BlockSpec((1, n_seq, d), lambda bi, ti, j: (bi, 0, 0))],
        out_shape=[jax.ShapeDtypeStruct((bg, tg, n_steps * tn), F32),
                   jax.ShapeDtypeStruct((bg, n_seq, d), F32)],
        scratch_shapes=[pltpu.VMEM((tm, d), F32), pltpu.VMEM((tm, d), F32),
                        pltpu.VMEM((tm, d), BF16), pltpu.VMEM((n_seq, d), F32)],
        compiler_params=_params("arbitrary", "arbitrary", "arbitrary"),
        name="rwkv_in_proj",
    )(h.reshape(bg, tg, d), shift_prev.reshape(bg, n_seq, d), g.reshape(1, d), mu8, w_cat)
    return proj.reshape(b, t, n_steps * tn), last.reshape(b, d)


def _rwkv_scan_kernel(*refs, chunk, n_chunks, has_vres, t_valid, n_hb):
    if has_vres:
        (r_ref, k_ref, v_ref, tw_ref, ta_ref, tg_ref, tv_ref, vf_ref,
         w2_ref, a2_ref, g2_ref, v2_ref, vec_ref, s0_ref, y_ref, sout_ref, s_sc) = refs
    else:
        (r_ref, k_ref, v_ref, tw_ref, ta_ref, tg_ref,
         w2_ref, a2_ref, g2_ref, vec_ref, s0_ref, y_ref, sout_ref, s_sc) = refs
    ti = pl.program_id(2)
    c_len = chunk
    lanes = MXU_DIM
    rows4 = HEAD_BLOCK * c_len

    @pl.when(ti == 0)
    def _():
        s_sc[...] = s0_ref[0]

    li = lax.broadcasted_iota(jnp.int32, (lanes, lanes), 0) // RWKV_HEAD
    lj = lax.broadcasted_iota(jnp.int32, (lanes, lanes), 1) // RWKV_HEAD
    seg = (li == lj).astype(BF16)
    ci = lax.broadcasted_iota(jnp.int32, (c_len, c_len), 0)
    cj = lax.broadcasted_iota(jnp.int32, (c_len, c_len), 1)
    tri_c = (cj <= ci).astype(BF16)

    split2 = _split_hi_lo

    def seg_sums(xs):
        parts = [p for x in xs for p in split2(x)]
        res = _dot(jnp.concatenate(parts, axis=0).astype(BF16), seg)
        return [res[(2 * n) * c_len:(2 * n + 1) * c_len] + res[(2 * n + 1) * c_len:(2 * n + 2) * c_len]
                for n in range(len(xs))]

    lane_head = lax.broadcasted_iota(jnp.int32, (c_len, lanes), 1) // RWKV_HEAD
    step_head = lax.broadcasted_iota(jnp.int32, (c_len, rows4), 1) // c_len
    t_row = lax.broadcasted_iota(jnp.int32, (c_len, 2 * rows4), 0)
    s_col = lax.broadcasted_iota(jnp.int32, (c_len, 2 * rows4), 1) & (c_len - 1)
    strict = s_col < t_row
    incl = s_col <= t_row
    eye_row = ((lax.broadcasted_iota(jnp.int32, (c_len, rows4), 1) & (c_len - 1))
               == lax.broadcasted_iota(jnp.int32, (c_len, rows4), 0)).astype(F32)
    same_head = li == lj
    n_double = int(math.log2(c_len)) - 1

    def stack(x):
        return jnp.concatenate(
            [jnp.where(lane_head == hh, x, 0.0) for hh in range(HEAD_BLOCK)], axis=0)

    def stack_steps(x):
        return jnp.concatenate(
            [jnp.where(step_head == hh, x, 0.0) for hh in range(HEAD_BLOCK)], axis=0)

    def one_chunk(c, carry):
        sl = pl.ds(pl.multiple_of(c * c_len, c_len), c_len)
        tw_act = jnp.tanh(tw_ref[0, sl, :])
        ta_act = ta_ref[0, sl, :]
        tg_act = _sigmoid(tg_ref[0, sl, :])
        tv_act = tv_ref[0, sl, :] if has_vres else None
        if t_valid is not None:
            t_idx = ti * (n_chunks * c_len) + c * c_len + lax.broadcasted_iota(
                jnp.int32, (c_len, lanes), 0)
            live = t_idx < t_valid
        st = [dict() for _ in range(n_hb)]

        def prep(hb):
            e = st[hb]
            hl = slice(hb * lanes, (hb + 1) * lanes)
            w0, a0, v0 = vec_ref[0:1, hl], vec_ref[1:2, hl], vec_ref[2:3, hl]
            kkw, kaw, rk = vec_ref[3:4, hl], vec_ref[4:5, hl], vec_ref[7:8, hl]
            r = r_ref[0, sl, hl]
            k = k_ref[0, sl, hl]
            v = v_ref[0, sl, hl]
            logw = -_softplus(-(w0 + _dot(tw_act, w2_ref[:, hl]))) - 0.5
            dlog = -jnp.exp(logw)
            rate = _sigmoid(a0 + _dot(ta_act, a2_ref[:, hl]))
            e['gate'] = _dot(tg_act, g2_ref[:, hl])
            if has_vres:
                v = v + (vf_ref[0, sl, hl] - v) * _sigmoid(v0 + _dot(tv_act, v2_ref[:, hl]))
            kk = k * kkw
            k = k * (1.0 + (rate - 1.0) * kaw)
            kk_sq, rk_sum = seg_sums([kk * kk, r * k * rk])
            kk = kk / jnp.maximum(jnp.sqrt(kk_sq), 1e-12)
            if t_valid is not None:
                dlog = jnp.where(live, dlog, 0.0)
                kk = jnp.where(live, kk, 0.0)
                k_live = jnp.where(live, k, 0.0)
            else:
                k_live = k
            d_hi, d_lo = split2(dlog)
            cum2 = _dot(tri_c, jnp.concatenate([d_hi, d_lo], axis=1).astype(BF16))
            cum = cum2[:, :lanes] + cum2[:, lanes:]
            inv = jnp.exp(-cum)
            b_row = kk * rate * inv
            k_row = k_live * inv
            e['ar'] = jnp.concatenate([-kk * jnp.exp(cum - dlog), r * jnp.exp(cum)],
                                      axis=0).astype(BF16)
            e['bk_row'] = jnp.concatenate([b_row, k_row], axis=0).astype(BF16)
            e['bk'] = jnp.concatenate([stack(b_row), stack(k_row)],
                                      axis=0).astype(BF16)
            e['v'] = v
            e['v_s'] = stack(v).astype(BF16)
            e['g_end'] = jnp.exp(cum[c_len - 1:c_len, :])
            e['bonus'] = rk_sum * v

        def products(hb):
            e = st[hb]
            e['s_old'] = s_sc[hb]
            big = _dot_nt(e['ar'], jnp.concatenate([e['bk'], e['s_old'].astype(BF16)], axis=0))
            a_bk = jnp.where(strict, big[:c_len, :2 * rows4], 0.0)
            e['r_bk'] = jnp.where(incl, big[c_len:, :2 * rows4], 0.0)
            e['a_s0'], e['r_s0'] = big[:c_len, 2 * rows4:], big[c_len:, 2 * rows4:]
            e['a_k'] = a_bk[:, rows4:]
            lrow = a_bk[:, :rows4]
            e['tinv'] = eye_row + lrow
            e['lpow'] = lrow

        def mx(x):
            return x.astype(BF16) if x.shape[0] % (2 * SUBLANE) == 0 else x

        def square(hb):
            e = st[hb]
            e['lpow'] = _dot(mx(e['lpow']), mx(stack_steps(e['lpow'])))

        def double(hb):
            e = st[hb]
            both = _dot(mx(jnp.concatenate([e['lpow'], e['tinv']], axis=0)), mx(stack_steps(e['lpow'])))
            e['tinv'] = e['tinv'] + both[c_len:]
            e['lpow'] = both[:c_len]

        def solve(hb):
            e = st[hb]
            tinv = e['tinv'] + _dot(mx(e['tinv']), mx(stack_steps(e['lpow'])))
            rhs = e['a_s0'] + _dot(mx(e['a_k']), e['v_s'])
            e['u'] = _dot(mx(tinv), mx(stack(rhs)))

        def outputs(hb):
            e = st[hb]
            uv_s = jnp.concatenate([stack(e['u']).astype(BF16), e['v_s']], axis=0)
            e['y'] = e['r_s0'] + _dot(mx(e['r_bk']), uv_s)
            uv_row = jnp.concatenate([e['u'], e['v']], axis=0).astype(BF16)
            s_sc[hb] = (e['s_old'] + jnp.where(same_head, _dot_tn(uv_row, e['bk_row']), 0.0)) * e['g_end']

        def group_norm(hb):
            e = st[hb]
            hl = slice(hb * lanes, (hb + 1) * lanes)
            y = e['y']
            mean = seg_sums([y])[0] * (1.0 / RWKV_HEAD)
            yc = y - mean
            var = seg_sums([yc * yc])[0] * (1.0 / RWKV_HEAD)
            yn = yc * lax.rsqrt(var + GN_EPS) * vec_ref[5:6, hl] + vec_ref[6:7, hl]
            y_ref[0, sl, hl] = ((yn + e['bonus']) * e['gate']).astype(y_ref.dtype)

        for stage in [prep, products, square] + [double] * (n_double - 1) + [solve, outputs, group_norm]:
            for hb in range(n_hb):
                stage(hb)
        return carry

    lax.fori_loop(0, n_chunks, one_chunk, 0)

    @pl.when(ti == pl.num_programs(2) - 1)
    def _():
        sout_ref[0] = s_sc[...]


def rwkv_scan(proj, v_first_proj, w2, a2, g2, v2, vec, s0_bd, t_real, chunk, tb, n_hb=6):
    b, t, _ = proj.shape
    lanes = MXU_DIM
    tb = min(tb, t)
    assert t % tb == 0 and tb % chunk == 0 and N_HEAD_BLOCKS % n_hb == 0
    has_vres = v_first_proj is not None
    t_valid = None if t_real == t else t_real

    wide = n_hb * lanes

    def col(block0):
        assert block0 % n_hb == 0
        return pl.BlockSpec((1, tb, wide), lambda bi, hg, ti: (bi, ti, block0 // n_hb + hg))

    def fixed(block):
        return pl.BlockSpec((1, tb, lanes), lambda bi, hg, ti: (bi, ti, block))

    def wcol():
        return pl.BlockSpec((LOWRANK_PAD, wide), lambda bi, hg, ti: (0, hg))

    st = pl.BlockSpec((1, n_hb, lanes, lanes), lambda bi, hg, ti: (bi, hg, 0, 0))
    if has_vres:
        in_specs = [col(RW_R), col(RW_K), col(RW_V), fixed(RW_W1), fixed(RW_A1), fixed(RW_G1),
                    fixed(RW_V1), col(RW_V), wcol(), wcol(), wcol(), wcol()]
        args = [proj, proj, proj, proj, proj, proj, proj, v_first_proj, w2, a2, g2, v2]
    else:
        in_specs = [col(RW_R), col(RW_K), col(RW_V), fixed(RW_W1), fixed(RW_A1), fixed(RW_G1),
                    wcol(), wcol(), wcol()]
        args = [proj, proj, proj, proj, proj, proj, w2, a2, g2]
    in_specs += [pl.BlockSpec((SUBLANE, wide), lambda bi, hg, ti: (0, hg)), st]
    args += [vec, s0_bd]
    return pl.pallas_call(
        functools.partial(_rwkv_scan_kernel, chunk=chunk, n_chunks=tb // chunk,
                          has_vres=has_vres, t_valid=t_valid, n_hb=n_hb),
        grid=(b, N_HEAD_BLOCKS // n_hb, t // tb),
        in_specs=in_specs,
        out_specs=[pl.BlockSpec((1, tb, wide), lambda bi, hg, ti: (bi, ti, hg)), st],
        out_shape=[jax.ShapeDtypeStruct((b, t, MAIN_W), BF16),
                   jax.ShapeDtypeStruct(s0_bd.shape, F32)],
        scratch_shapes=[pltpu.VMEM((n_hb, lanes, lanes), F32)],
        compiler_params=_params("arbitrary", "arbitrary", "arbitrary"),
        name="rwkv_scan",
    )(*args)


def _to_block_diag(s):
    b = s.shape[0]
    s = s.reshape(b, N_HEAD_BLOCKS, HEAD_BLOCK, RWKV_HEAD, RWKV_HEAD)
    eye = jnp.eye(HEAD_BLOCK, dtype=s.dtype)
    bd = s[:, :, :, :, None, :] * eye[None, None, :, None, :, None]
    return bd.reshape(b, N_HEAD_BLOCKS, MXU_DIM, MXU_DIM)


def _from_block_diag(bd):
    b = bd.shape[0]
    n = RWKV_HEAD
    x = jnp.stack([bd[:, :, hh * n:(hh + 1) * n, hh * n:(hh + 1) * n] for hh in range(HEAD_BLOCK)],
                  axis=2)
    return x.reshape(b, RWKV_HEADS, RWKV_HEAD, RWKV_HEAD)


def _mem_attn_kernel(q_ref, k_ref, v_ref, o_ref, *, per_head):
    scale = MEM_HEAD_DIM ** -0.5
    for hh in range(MEM_HEADS):
        sl = slice(hh * MEM_HEAD_DIM, (hh + 1) * MEM_HEAD_DIM)
        k = k_ref[0, 0, :, hh, :] if per_head else k_ref[0, :, sl]
        v = v_ref[0, 0, :, hh, :] if per_head else v_ref[0, :, sl]
        s = _dot_nt(q_ref[0, :, sl], k) * scale
        p = jnp.exp(s - jnp.max(s, axis=-1, keepdims=True))
        p = p / jnp.sum(p, axis=-1, keepdims=True)
        o_ref[0, :, sl] = _dot(p, v).astype(o_ref.dtype)


def mem_attend(qsrc, q_block, ksrc, k_block, vsrc, v_block, tq=512):
    b, t, _ = qsrc.shape
    per_head = ksrc.ndim == 5
    m = ksrc.shape[-3] if per_head else ksrc.shape[1]
    tq = _row_tile(t, tq)
    if per_head:
        kv_spec = lambda blk: pl.BlockSpec((1, 1, m, MEM_HEADS, MEM_HEAD_DIM),
                                           lambda bi, ti: (blk, bi, 0, 0, 0))
    else:
        kv_spec = lambda blk: pl.BlockSpec((1, m, MEM_W), lambda bi, ti: (bi, 0, blk))
    return pl.pallas_call(
        functools.partial(_mem_attn_kernel, per_head=per_head),
        grid=(b, t // tq),
        in_specs=[pl.BlockSpec((1, tq, MEM_W), lambda bi, ti: (bi, ti, q_block)),
                  kv_spec(k_block), kv_spec(v_block)],
        out_specs=pl.BlockSpec((1, tq, MEM_W), lambda bi, ti: (bi, ti, 0)),
        out_shape=jax.ShapeDtypeStruct((b, t, MEM_W), BF16),
        compiler_params=_params("arbitrary", "arbitrary"),
        name="mem_attend",
    )(qsrc, ksrc, vsrc)


CMP_PAGE = 128
CMP_CHUNKS = CMP_PAGE // CMP_STRIDE
KV_LANES = NSA_KV_GROUPS * NSA_HEAD_DIM


def _chunk_proj_kernel(pt_ref, *refs, n_pages):
    del pt_ref
    page_refs = refs[:n_pages]
    w_ref, o0_ref, o1_ref, x_sc = refs[n_pages:]
    for kp in range(n_pages):
        for pos in range(CMP_STRIDE):
            x_sc[kp * CMP_CHUNKS:(kp + 1) * CMP_CHUNKS, pos * LANE:(pos + 1) * LANE] = (
                page_refs[kp][pl.ds(pos, CMP_CHUNKS, stride=CMP_STRIDE), :])
    res = _dot(x_sc[...].astype(BF16), w_ref[...])
    o0_ref[0] = res[:, :LANE]
    o1_ref[0] = res[:, LANE:]


def chunk_proj(rows2d, table, w_flat, n_pages, col_block=0):
    b, n_tab = table.shape
    assert n_tab % n_pages == 0

    def page_spec(kp):
        return pl.BlockSpec(
            (CMP_PAGE, LANE),
            lambda bi, pg, g, pt: (pt[bi, pg * n_pages + kp], col_block * NSA_KV_GROUPS + g))

    out_spec = pl.BlockSpec((1, n_pages * CMP_CHUNKS, LANE), lambda bi, pg, g, pt: (bi, pg, g))
    out_shape = jax.ShapeDtypeStruct((b, n_tab * CMP_CHUNKS, KV_LANES), F32)
    grid_spec = pltpu.PrefetchScalarGridSpec(
        num_scalar_prefetch=1,
        grid=(b, n_tab // n_pages, NSA_KV_GROUPS),
        in_specs=[page_spec(kp) for kp in range(n_pages)]
        + [pl.BlockSpec(w_flat.shape, lambda bi, pg, g, pt: (0, 0))],
        out_specs=[out_spec, out_spec],
        scratch_shapes=[pltpu.VMEM((n_pages * CMP_CHUNKS, CMP_STRIDE * LANE), F32)],
    )
    return pl.pallas_call(
        functools.partial(_chunk_proj_kernel, n_pages=n_pages),
        grid_spec=grid_spec,
        out_shape=[out_shape, out_shape],
        compiler_params=_params("arbitrary", "arbitrary", "arbitrary"),
        name="chunk_proj",
    )(table, *([rows2d] * n_pages), w_flat)


def _chunk_proj_pool_kernel(pt_ref, *refs, n_pages):
    del pt_ref
    page_refs = refs[:n_pages]
    w_ref, o0_ref, o1_ref, x_sc = refs[n_pages:]
    for g in range(NSA_KV_GROUPS):
        for kp in range(n_pages):
            for pos in range(CMP_STRIDE):
                x_sc[kp * CMP_CHUNKS:(kp + 1) * CMP_CHUNKS, pos * LANE:(pos + 1) * LANE] = (
                    page_refs[kp][0, pl.ds(pos, CMP_CHUNKS, stride=CMP_STRIDE), g, :])
        res = _dot(x_sc[...].astype(BF16), w_ref[...])
        o0_ref[0, :, g * LANE:(g + 1) * LANE] = res[:, :LANE]
        o1_ref[0, :, g * LANE:(g + 1) * LANE] = res[:, LANE:]


def chunk_proj_pool(pool, table, w_flat, n_pages):
    b, n_tab = table.shape
    assert n_tab % n_pages == 0

    def page_spec(kp):
        return pl.BlockSpec((1, CMP_PAGE, NSA_KV_GROUPS, LANE),
                            lambda bi, pg, pt: (pt[bi, pg * n_pages + kp], 0, 0, 0))

    out_spec = pl.BlockSpec((1, n_pages * CMP_CHUNKS, KV_LANES), lambda bi, pg, pt: (bi, pg, 0))
    out_shape = jax.ShapeDtypeStruct((b, n_tab * CMP_CHUNKS, KV_LANES), F32)
    grid_spec = pltpu.PrefetchScalarGridSpec(
        num_scalar_prefetch=1,
        grid=(b, n_tab // n_pages),
        in_specs=[page_spec(kp) for kp in range(n_pages)]
        + [pl.BlockSpec(w_flat.shape, lambda bi, pg, pt: (0, 0))],
        out_specs=[out_spec, out_spec],
        scratch_shapes=[pltpu.VMEM((n_pages * CMP_CHUNKS, CMP_STRIDE * LANE), F32)],
    )
    return pl.pallas_call(
        functools.partial(_chunk_proj_pool_kernel, n_pages=n_pages),
        grid_spec=grid_spec,
        out_shape=[out_shape, out_shape],
        compiler_params=_params("arbitrary", "arbitrary"),
        name="chunk_proj_pool",
    )(table, *([pool] * n_pages), w_flat)


def _block_mlp_kernel(p0_ref, p1_ref, pe_ref, w1_ref, b1_ref, w2_ref, o_ref):
    const = _dot(pe_ref[...], w1_ref[...])[0:1, :] + b1_ref[...]
    w2 = w2_ref[...]
    for g in range(NSA_KV_GROUPS):
        sl = slice(g * LANE, (g + 1) * LANE)
        hid = const + p0_ref[0, :, sl] + p1_ref[0, :, sl]
        o_ref[0, :, sl] = _dot(_gelu_tanh(hid), w2)


def block_mlp(p0, p1, pe, w1, b1, w2, tb=1024):
    b, nb, _ = p0.shape
    tb = _row_tile(nb, tb)
    pe8 = jnp.concatenate([pe.reshape(1, -1), jnp.zeros((SUBLANE - 1, pe.size), F32)], axis=0)
    spec = pl.BlockSpec((1, tb, KV_LANES), lambda bi, i: (bi, i, 0))
    full = lambda a: pl.BlockSpec(a.shape, lambda bi, i: (0,) * a.ndim)
    b1r = b1.reshape(1, -1)
    return pl.pallas_call(
        _block_mlp_kernel,
        grid=(b, nb // tb),
        in_specs=[spec, spec, full(pe8), full(w1), full(b1r), full(w2)],
        out_specs=spec,
        out_shape=jax.ShapeDtypeStruct((b, nb, KV_LANES), F32),
        compiler_params=_params("arbitrary", "arbitrary"),
        name="block_mlp",
    )(p0, p1, pe8, w1, b1r, w2)


def _w1_flat(w1):
    r = CMP_BLOCK // CMP_STRIDE
    e = w1.shape[1]
    w = w1.reshape(r, CMP_STRIDE, NSA_HEAD_DIM, e)
    return jnp.transpose(w, (1, 2, 0, 3)).reshape(CMP_STRIDE * NSA_HEAD_DIM, r * e).astype(BF16)


def compress(parts, nb, pe, w1, b1, w2):
    return block_mlp(parts[0][:, :nb], parts[1][:, 1:nb + 1], pe, w1, b1, w2)


def _rel_bucket(dist):
    n = jnp.maximum(dist, 0)
    max_exact = REL_BUCKETS // 2
    nf = jnp.maximum(n, 1).astype(F32)
    large = max_exact + (jnp.log(nf / max_exact) / math.log(REL_MAX_DIST / max_exact)
                         * (REL_BUCKETS - max_exact)).astype(jnp.int32)
    return jnp.where(n < max_exact, n, jnp.minimum(large, REL_BUCKETS - 1))


def _bias_lookup(rel_bias, dist):
    tab = rel_bias.astype(F32).reshape(REL_BUCKETS, NSA_KV_GROUPS, NSA_HPG)
    onehot = jax.nn.one_hot(_rel_bucket(dist), REL_BUCKETS, dtype=F32)
    out = jnp.dot(onehot, tab.reshape(REL_BUCKETS, -1), precision=HI)
    out = out.reshape(dist.shape + (NSA_KV_GROUPS, NSA_HPG))
    return jnp.moveaxis(out, (-2, -1), (0, 1))


def _cmp_to_sel(nb_pad, nb, nsb_pad, nsb):
    i = jnp.arange(nb_pad)[:, None]
    j = jnp.arange(nsb_pad)[None, :]
    start = i * CMP_STRIDE
    hit = (start < (j + 1) * SEL_BLOCK) & (start + CMP_BLOCK > j * SEL_BLOCK) & (i < nb) & (j < nsb)
    return hit.astype(F32)


def _nsa_prompt_kernel(q_ref, gt_ref, kc_ref, vc_ref, bc_ref, band_ref, ks_ref, vs_ref,
                       kw_ref, vw_ref, m_ref, o_ref, s_sc, *, nb, nsb, n_g):
    tq = NSA_TQ
    rows = NSA_HPG * tq
    qw = NSA_HPG * LANE
    i = pl.program_id(2)
    scale = NSA_HEAD_DIM ** -0.5
    row_q = lax.broadcasted_iota(jnp.int32, (rows, LANE), 0) & (tq - 1)
    lane = lax.broadcasted_iota(jnp.int32, (rows, LANE), 1)
    t_pos = i * tq + row_q
    kt_w = NSA_KT
    tiles_per_kt = kt_w // tq
    key_lane = lax.broadcasted_iota(jnp.int32, (rows, kt_w), 1)
    t_pos_k = i * tq + (lax.broadcasted_iota(jnp.int32, (rows, kt_w), 0) & (tq - 1))
    m_c = (t_pos - (lane * CMP_STRIDE + (CMP_BLOCK - 1)) >= 0) & (lane < nb)
    jb = lax.broadcasted_iota(jnp.int32, (nsb, tq), 0)
    cur = (i * tq + lax.broadcasted_iota(jnp.int32, (nsb, tq), 1)) // SEL_BLOCK
    valid = jb <= cur
    forced = valid & ((jb == 0) | (jb > cur - SEL_LOCAL))
    blk_row = lax.broadcasted_iota(jnp.int32, (nsb, kt_w), 0)
    blk_lane = lax.broadcasted_iota(jnp.int32, (nsb, kt_w), 1) // SEL_BLOCK
    sel_map = m_ref[...].astype(BF16)
    n_win_tiles = WINDOW // kt_w + 1
    n_kt = i // tiles_per_kt + 1
    st = [dict() for _ in range(n_g)]
    gl = lambda g: slice(g * LANE, (g + 1) * LANE)

    def masked_scores(g, k_ref, kt, mask_of):
        ksl = pl.ds(pl.multiple_of(kt * kt_w, kt_w), kt_w)
        band = band_ref[g, jnp.clip(i - kt * tiles_per_kt, 0, NSA_BANDS - 1)].reshape(rows, kt_w)
        s = _dot_nt(st[g]['q3'], k_ref[0, ksl, gl(g)].astype(BF16)) + band
        return jnp.where(mask_of(t_pos_k - (kt * kt_w + key_lane)), s, MASK_NEG), ksl

    def finish(m_fin, l_fin, acc):
        return jnp.where(m_fin > MASK_NEG, acc / jnp.where(l_fin > 0, l_fin, 1.0), 0.0)

    def window_scores(g):
        e = st[g]
        e['q3'] = (jnp.concatenate([q_ref[0, :, g * qw + hh * LANE:g * qw + (hh + 1) * LANE]
                                    for hh in range(NSA_HPG)], axis=0) * scale).astype(BF16)
        e['win'] = []
        m_w = jnp.full((rows, 1), MASK_NEG, F32)
        for jw in range(n_win_tiles):
            kt_raw = i // tiles_per_kt - (n_win_tiles - 1) + jw
            reach = jnp.where(kt_raw >= 0, WINDOW, 0)
            s_w, ksl_w = masked_scores(g, kw_ref, jnp.maximum(kt_raw, 0),
                                       lambda dist, reach=reach: (dist >= 0) & (dist < reach))
            e['win'].append((s_w, ksl_w))
            m_w = jnp.maximum(m_w, jnp.max(s_w, axis=-1, keepdims=True))
        e['m_w'] = m_w

    def compressed(g):
        e = st[g]
        s_c = _dot_nt(e['q3'], kc_ref[0, :, gl(g)].astype(BF16)) + bc_ref[g].reshape(rows, LANE)
        e['p_c'] = _masked_softmax(s_c, m_c)
        e['o_c'] = _dot(e['p_c'], vc_ref[0, :, gl(g)])

    def select(g):
        e = st[g]
        p_hi, p_lo = _split_hi_lo(e['p_c'])
        imp3 = _dot_nt(sel_map, p_hi.astype(BF16)) + _dot_nt(sel_map, p_lo.astype(BF16))
        imp = imp3[:, 0:tq]
        for hh in range(1, NSA_HPG):
            imp = imp + imp3[:, hh * tq:(hh + 1) * tq]
        imp = jnp.where(forced, FORCE_SCORE, jnp.where(valid, imp, -FORCE_SCORE))
        rank = jnp.zeros((nsb, tq), jnp.int32)
        for jp in range(nsb):
            other = imp[jp:jp + 1, :]
            rank = rank + ((other > imp) | ((other == imp) & (jb > jp))).astype(jnp.int32)
        sel = (rank < min(SEL_TOPK, nsb)).astype(F32).T
        e['sel3'] = jnp.concatenate([sel] * NSA_HPG, axis=0).astype(BF16)

    def window_values(g):
        e = st[g]
        l_w = jnp.zeros((rows, 1), F32)
        acc_w = jnp.zeros((rows, LANE), F32)
        for s_w, ksl_w in e['win']:
            p_w = jnp.exp(s_w - e['m_w'])
            l_w = l_w + jnp.sum(p_w, axis=-1, keepdims=True)
            acc_w = acc_w + _dot(p_w, vw_ref[0, ksl_w, gl(g)])
        e['o_w'] = finish(e['m_w'], l_w, acc_w)

    for stage in (window_scores, compressed, select, window_values):
        for g in range(n_g):
            stage(g)

    def scores(kt, m_run):
        expand = (blk_row == kt * (kt_w // SEL_BLOCK) + blk_lane).astype(BF16)
        out = []
        for g in range(n_g):
            chosen = _dot(st[g]['sel3'], expand) > 0.5
            s, _ = masked_scores(g, ks_ref, kt, lambda dist, chosen=chosen: chosen & (dist >= 0))
            s_sc[g, kt] = s
            out.append(jnp.maximum(m_run[g], jnp.max(s, axis=-1, keepdims=True)))
        return tuple(out)

    m_s = lax.fori_loop(0, n_kt, scores, tuple(jnp.full((rows, 1), MASK_NEG, F32) for _ in range(n_g)))

    def accum(kt, carry):
        ksl = pl.ds(pl.multiple_of(kt * kt_w, kt_w), kt_w)
        out = []
        for g in range(n_g):
            l_run, acc = carry[g]
            p = jnp.exp(s_sc[g, kt] - m_s[g])
            out.append((l_run + jnp.sum(p, axis=-1, keepdims=True), acc + _dot(p, vs_ref[0, ksl, gl(g)])))
        return tuple(out)

    fin = lax.fori_loop(0, n_kt, accum, tuple((jnp.zeros((rows, 1), F32), jnp.zeros((rows, LANE), F32))
                                              for _ in range(n_g)))
    for g in range(n_g):
        e = st[g]
        o_s = finish(m_s[g], fin[g][0], fin[g][1])
        gt = _sigmoid(gt_ref[0, g])
        gcol = lambda br, gt=gt: jnp.concatenate(
            [gt[:, br * NSA_HPG + hh:br * NSA_HPG + hh + 1] for hh in range(NSA_HPG)], axis=0)
        o = gcol(0) * e['o_c'] + gcol(1) * o_s + gcol(2) * e['o_w']
        for hh in range(NSA_HPG):
            o_ref[0, :, g * qw + hh * LANE:g * qw + (hh + 1) * LANE] = (
                o[hh * tq:(hh + 1) * tq].astype(o_ref.dtype))


def nsa_prompt(proj, gates_t, kc, vc, bias_c, band, side, sel_map, nb, nsb, n_g=2):
    b, t, _ = proj.shape
    tq = NSA_TQ
    qw = n_g * NSA_HPG * LANE
    kvw = n_g * LANE
    n_gp = NSA_KV_GROUPS // n_g
    kv = lambda off: pl.BlockSpec((1, t, kvw), lambda bi, gp, i: (bi, 0, off * n_gp + gp))
    return pl.pallas_call(
        functools.partial(_nsa_prompt_kernel, nb=nb, nsb=nsb, n_g=n_g),
        grid=(b, n_gp, t // tq),
        in_specs=[pl.BlockSpec((1, tq, qw), lambda bi, gp, i: (bi, i, gp)),
                  pl.BlockSpec((1, n_g, tq, NSA_HPG * 3), lambda bi, gp, i: (bi, gp, i, 0)),
                  pl.BlockSpec((1, LANE, kvw), lambda bi, gp, i: (bi, 0, gp)),
                  pl.BlockSpec((1, LANE, kvw), lambda bi, gp, i: (bi, 0, gp)),
                  pl.BlockSpec((n_g, NSA_HPG, tq, LANE), lambda bi, gp, i: (gp, 0, i, 0)),
                  pl.BlockSpec((n_g, NSA_BANDS, NSA_HPG, tq, NSA_KT), lambda bi, gp, i: (gp, 0, 0, 0, 0)),
                  kv(2), kv(3), kv(4), kv(5),
                  pl.BlockSpec(sel_map.shape, lambda bi, gp, i: (0, 0))],
        out_specs=pl.BlockSpec((1, tq, qw), lambda bi, gp, i: (bi, i, gp)),
        out_shape=jax.ShapeDtypeStruct((b, t, MAIN_W), BF16),
        scratch_shapes=[pltpu.VMEM((n_g, t // NSA_KT, NSA_HPG * tq, NSA_KT), F32)],
        compiler_params=_params("arbitrary", "arbitrary", "arbitrary"),
        name="nsa_prompt",
    )(proj, gates_t, kc, vc, bias_c, band, side, side, side, side, sel_map)


def _nsa_dec_cmp_kernel(q_ref, kc_ref, vc_ref, bc_ref, m_ref, oc_ref, idx_ref, *, nb, nsb, t_pos):
    scale = NSA_HEAD_DIM ** -0.5
    nbp = kc_ref.shape[1]
    nsp = m_ref.shape[1]
    n_sel = min(SEL_TOPK, nsb)
    lane_b = lax.broadcasted_iota(jnp.int32, (SUBLANE, nbp), 1)
    m_c = (t_pos - (lane_b * CMP_STRIDE + (CMP_BLOCK - 1)) >= 0) & (lane_b < nb)
    row_s = lax.broadcasted_iota(jnp.int32, (SUBLANE, nsp), 0)
    jb = lax.broadcasted_iota(jnp.int32, (1, nsp), 1)
    cur = t_pos // SEL_BLOCK
    valid = jb <= cur
    forced = valid & ((jb == 0) | (jb > cur - SEL_LOCAL))
    out_lane = lax.broadcasted_iota(jnp.int32, (1, LANE), 1)
    idx_rows = []
    for g in range(NSA_KV_GROUPS):
        q3 = jnp.concatenate(
            [q_ref[0, 0:1, (g * NSA_HPG + hh) * LANE:(g * NSA_HPG + hh + 1) * LANE]
             for hh in range(NSA_HPG)] + [jnp.zeros((SUBLANE - NSA_HPG, LANE), F32)], axis=0)
        sl = slice(g * LANE, (g + 1) * LANE)
        s_c = _dot_nt(q3, kc_ref[0, :, sl]) * scale + bc_ref[g]
        p_c = _masked_softmax(s_c, m_c)
        oc_ref[0, g] = _dot(p_c, vc_ref[0, :, sl])
        imp8 = jnp.where(row_s < NSA_HPG, _dot(p_c, m_ref[...], HI), 0.0)
        imp = jnp.sum(imp8, axis=0, keepdims=True)
        imp = jnp.where(forced, FORCE_SCORE, jnp.where(valid, imp, -FORCE_SCORE))
        imp = jnp.where(jb < nsb, imp, -jnp.inf)
        jbf = jb.astype(F32)
        picks = jnp.zeros((1, LANE), F32)
        for kk in range(n_sel):
            best = jnp.max(imp, axis=-1, keepdims=True)
            arg = jnp.min(jnp.where(imp == best, jbf, float(nsp)), axis=-1, keepdims=True)
            picks = jnp.where(out_lane == kk, arg, picks)
            imp = jnp.where(jbf == arg, -jnp.inf, imp)
        idx_rows.append(picks.astype(jnp.int32))
    idx_rows.append(jnp.zeros((SUBLANE - NSA_KV_GROUPS, LANE), jnp.int32))
    idx_ref[0] = jnp.concatenate(idx_rows, axis=0)


def nsa_dec_cmp(proj, kc, vc, bias_c, sel_map, nb, nsb, t_pos):
    b, tp, _ = proj.shape
    nbp = kc.shape[1]
    return pl.pallas_call(
        functools.partial(_nsa_dec_cmp_kernel, nb=nb, nsb=nsb, t_pos=t_pos),
        grid=(b,),
        in_specs=[pl.BlockSpec((1, tp, MAIN_W), lambda bi: (bi, 0, 0)),
                  pl.BlockSpec((1, nbp, KV_LANES), lambda bi: (bi, 0, 0)),
                  pl.BlockSpec((1, nbp, KV_LANES), lambda bi: (bi, 0, 0)),
                  pl.BlockSpec(bias_c.shape, lambda bi: (0, 0, 0)),
                  pl.BlockSpec(sel_map.shape, lambda bi: (0, 0))],
        out_specs=[pl.BlockSpec((1, NSA_KV_GROUPS, SUBLANE, LANE), lambda bi: (bi, 0, 0, 0)),
                   pl.BlockSpec((1, SUBLANE, LANE), lambda bi: (bi, 0, 0))],
        out_shape=[jax.ShapeDtypeStruct((b, NSA_KV_GROUPS, SUBLANE, LANE), F32),
                   jax.ShapeDtypeStruct((b, SUBLANE, LANE), jnp.int32)],
        compiler_params=_params("arbitrary"),
        name="nsa_dec_cmp",
    )(proj, kc, vc, bias_c, sel_map)


def _nsa_dec_sel_kernel(phys_ref, isnew_ref, *refs, n_sel, n_win, t_pos, win_start):
    del phys_ref
    kb_refs = refs[:n_sel]
    vb_refs = refs[n_sel:2 * n_sel]
    (q_ref, new_ref, ds_ref, bs_ref, wk_ref, wv_ref, bw_ref, oc_ref, gt_ref, o_ref) = refs[2 * n_sel:]
    bi = pl.program_id(0)
    g = pl.program_id(1)
    scale = NSA_HEAD_DIM ** -0.5
    q3 = q_ref[0, 0]
    new_rows = new_ref[0, 0]
    pad_blk = jnp.zeros((SEL_BLOCK - 1, LANE), F32)
    new_k = jnp.concatenate([new_rows[0:1], pad_blk], axis=0)
    new_v = jnp.concatenate([new_rows[1:2], pad_blk], axis=0)
    ks, vs = [], []
    for kk in range(n_sel):
        fresh = isnew_ref[bi, g, kk] > 0
        ks.append(jnp.where(fresh, new_k, kb_refs[kk][0, :, g, :]))
        vs.append(jnp.where(fresh, new_v, vb_refs[kk][0, :, g, :]))
    ks = jnp.concatenate(ks, axis=0)
    vs = jnp.concatenate(vs, axis=0)
    s_s = _dot_nt(q3, ks) * scale + bs_ref[0, 0]
    p_s = _masked_softmax(s_s, ds_ref[0, 0] >= 0)
    o_s = _dot(p_s, vs)

    pad_w = jnp.zeros((SUBLANE - 1, LANE), F32)
    kw = jnp.concatenate([wk_ref[0, :, g, :], new_rows[2:3], pad_w], axis=0)
    vw = jnp.concatenate([wv_ref[0, :, g, :], new_rows[3:4], pad_w], axis=0)
    nw = kw.shape[0]
    pos = win_start + lax.broadcasted_iota(jnp.int32, (SUBLANE, nw), 1)
    d_w = t_pos - pos
    m_w = (d_w >= 0) & (d_w < WINDOW) & (pos >= 0) & (pos - win_start < n_win)
    s_w = _dot_nt(q3, kw) * scale + bw_ref[0]
    p_w = _masked_softmax(s_w, m_w)
    o_w = _dot(p_w, vw)

    gt = _sigmoid(gt_ref[0, 0])
    o_ref[0, 0] = gt[:, 0:1] * oc_ref[0, 0] + gt[:, 1:2] * o_s + gt[:, 2:3] * o_w


def nsa_dec_sel(phys, isnew, pool_k, pool_v, q8, new_rows, d_s, bias_s, win_k, win_v, bias_w,
                o_c, gates8, n_win, t_pos, win_start):
    b = q8.shape[0]
    n_sel = phys.shape[-1]
    n_keys = n_sel * SEL_BLOCK
    nw = win_k.shape[1]
    nwp = nw + SUBLANE

    def blk_spec(kk):
        return pl.BlockSpec((1, SEL_BLOCK, NSA_KV_GROUPS, LANE),
                            lambda bi, g, ph, nf: (ph[bi, g, kk], 0, 0, 0))

    per_bg = lambda *shape: pl.BlockSpec((1, 1) + shape, lambda bi, g, ph, nf: (bi, g) + (0,) * len(shape))
    grid_spec = pltpu.PrefetchScalarGridSpec(
        num_scalar_prefetch=2,
        grid=(b, NSA_KV_GROUPS),
        in_specs=[blk_spec(kk) for kk in range(n_sel)] + [blk_spec(kk) for kk in range(n_sel)]
        + [per_bg(SUBLANE, LANE), per_bg(SUBLANE, LANE), per_bg(1, n_keys), per_bg(SUBLANE, n_keys),
           pl.BlockSpec((1, nw, NSA_KV_GROUPS, LANE), lambda bi, g, ph, nf: (bi, 0, 0, 0)),
           pl.BlockSpec((1, nw, NSA_KV_GROUPS, LANE), lambda bi, g, ph, nf: (bi, 0, 0, 0)),
           pl.BlockSpec((1, SUBLANE, nwp), lambda bi, g, ph, nf: (g, 0, 0)),
           per_bg(SUBLANE, LANE), per_bg(SUBLANE, SUBLANE)],
        out_specs=per_bg(SUBLANE, LANE),
    )
    return pl.pallas_call(
        functools.partial(_nsa_dec_sel_kernel, n_sel=n_sel, n_win=nw + 1, t_pos=t_pos,
                          win_start=win_start),
        grid_spec=grid_spec,
        out_shape=jax.ShapeDtypeStruct((b, NSA_KV_GROUPS, SUBLANE, LANE), F32),
        compiler_params=_params("arbitrary", "arbitrary"),
        name="nsa_dec_sel",
    )(phys, isnew, *([pool_k] * n_sel), *([pool_v] * n_sel), q8, new_rows, d_s, bias_s,
      win_k, win_v, bias_w, o_c, gates8)


def _pad_cols(w, n):
    return jnp.pad(w, ((0, 0), (0, n - w.shape[1])))


def _pad_rows(w, n):
    return jnp.pad(w, ((0, n - w.shape[0]), (0, 0)))


def _prep_weights(P):
    W = {}
    W['w_gu'] = P['ffn_gu'].astype(BF16)
    W['w_d'] = P['ffn_d'].astype(BF16)
    W['w_out'] = P['w_out'].astype(BF16)
    W['rw_in'], W['rw_w2'], W['rw_a2'], W['rw_g2'], W['rw_v2'], W['rw_vec'] = [], [], [], [], [], []
    lp = LOWRANK_PAD
    for l in range(N_A):
        w_in = P['w_in_a'][l]
        v1 = P['rw_v1'][l - 1] if l > 0 else jnp.zeros((D_MODEL, lp), F32)
        W['rw_in'].append(jnp.concatenate([
            w_in, _pad_cols(P['rw_w1'][l], lp), _pad_cols(v1, lp), _pad_cols(P['rw_a1'][l], lp),
            _pad_cols(P['rw_g1'][l], lp)], axis=1).astype(BF16))
        W['rw_w2'].append(_pad_rows(P['rw_w2'][l], lp))
        W['rw_a2'].append(_pad_rows(P['rw_a2'][l], lp))
        W['rw_g2'].append(_pad_rows(P['rw_g2'][l], lp))
        W['rw_v2'].append(_pad_rows(P['rw_v2'][l - 1], lp) if l > 0 else None)
        v0 = P['rw_v0'][l - 1] if l > 0 else jnp.zeros((MAIN_W,), F32)
        W['rw_vec'].append(jnp.stack([P['rw_w0'][l], P['rw_a0'][l], v0, P['rw_kk'][l], P['rw_ka'][l],
                                      P['rw_lnw'][l], P['rw_lnb'][l], P['rw_rk'][l].reshape(-1)]))
    W['w_in_b'] = []
    for l in range(DEPTH - N_A):
        w = P['w_in_b'][l]
        W['w_in_b'].append(jnp.concatenate([
            w[:, :MAIN_W], w[:, MAIN_W + GATE_W:], _pad_cols(w[:, MAIN_W:MAIN_W + GATE_W], MXU_DIM)],
            axis=1).astype(BF16))
    W['w_kv'] = P['w_kv'].astype(BF16)
    W['w_mem_kv'] = [P['w_mem_kv'][l].astype(BF16) for l in range(DEPTH)]
    W['cmp_w1_flat'] = [_w1_flat(P['cmp_w1'][c]) for c in range(2)]
    return W


B_QM_BLOCK = MAIN_W // MEM_W
B_GATE_OFF = MAIN_W + MEM_W


def _ffn(h2, norms_l, first, W, l, i):
    hid = norm_swiglu(h2, norms_l[first], W['w_gu'], l, i, tn=1024, tm=1024)
    return matmul_norm_res(hid, W['w_d'], l, i, norms_l[first + 1], h2, 0.5, tk=FFN_DOWN_TK)


def _trunk(x, t_real, P, W, shift0, wkv0_bd, mem_src, make_side, attend, chunk, tb):
    b, t, d = x.shape
    h2 = x.reshape(b * t, d)
    shifts, states = [], []
    v_first_proj, ctx, side_state = None, None, None
    for l in range(DEPTH):
        n = P['norms'][l]
        h2 = _ffn(h2, n, 0, W, l, 0)
        if l < N_A:
            proj, last = rwkv_in_proj(h2.reshape(b, t, d), shift0[l], n[2], P['rw_mu'][l],
                                      W['rw_in'][l], t_real)
            main, s_bd = rwkv_scan(proj, v_first_proj if l > 0 else None, W['rw_w2'][l], W['rw_a2'][l],
                                   W['rw_g2'][l], W['rw_v2'][l], W['rw_vec'][l], wkv0_bd[l],
                                   t_real, chunk, tb)
            if l == 0:
                v_first_proj = proj
            shifts.append(last)
            states.append(_from_block_diag(s_bd))
            q_src, q_block = proj, RW_QM * MXU_DIM // MEM_W
        else:
            proj = norm_matmul(h2, n[2], W['w_in_b'][l - N_A], tn=768, tm=1024).reshape(b, t, -1)
            main = attend(proj, ctx)
            q_src, q_block = proj, B_QM_BLOCK
        mk, kb, mv, vb = mem_src(l)
        mo = mem_attend(q_src, q_block, mk, kb, mv, vb)
        h2 = out_proj(main.reshape(b * t, MAIN_W), mo.reshape(b * t, MEM_W), W['w_out'], l, n[3], h2)
        h2 = _ffn(h2, n, 4, W, l, 1)
        if l == N_A - 1:
            side = norm_matmul(h2, P['kv_norm'], W['w_kv'], tn=768, tm=1024).reshape(b, t, -1)
            ctx, side_state = make_side(side)
    return h2.reshape(b, t, d), jnp.stack(shifts), jnp.stack(states), side_state


def kernel(x_prompt, x_sample, mem_prompt, state_wkv, state_shift, cache_mem_k, cache_mem_v,
           cache_cmp_k, cache_cmp_v, cache_slc_k, cache_slc_v, cache_win_k, cache_win_v, page_table,
           norms, ffn_gu, ffn_d, w_in_a, w_in_b, w_out, mem_norm, w_mem_kv, kv_norm, w_kv,
           cmp_pe, cmp_w1, cmp_b1, cmp_w2, rel_bias,
           rw_mu, rw_w0, rw_w1, rw_w2, rw_a0, rw_a1, rw_a2, rw_g1, rw_g2, rw_v0, rw_v1, rw_v2,
           rw_kk, rw_ka, rw_rk, rw_lnw, rw_lnb):
    P = dict(norms=norms, ffn_gu=ffn_gu, ffn_d=ffn_d, w_in_a=w_in_a, w_in_b=w_in_b, w_out=w_out,
             kv_norm=kv_norm, w_kv=w_kv, w_mem_kv=w_mem_kv, cmp_w1=cmp_w1, rw_mu=rw_mu, rw_w0=rw_w0,
             rw_w1=rw_w1, rw_w2=rw_w2, rw_a0=rw_a0, rw_a1=rw_a1, rw_a2=rw_a2, rw_g1=rw_g1,
             rw_g2=rw_g2, rw_v0=rw_v0, rw_v1=rw_v1, rw_v2=rw_v2, rw_kk=rw_kk, rw_ka=rw_ka,
             rw_rk=rw_rk, rw_lnw=rw_lnw, rw_lnb=rw_lnb)
    W = _prep_weights(P)
    G, dh = NSA_KV_GROUPS, NSA_HEAD_DIM
    split_side = lambda side, bx, t: [side[:, :t, c * KV_LANES:(c + 1) * KV_LANES].reshape(bx, t, G, dh)
                                      for c in range(6)]

    def cmp_mlp(parts, nb, c):
        return compress(parts, nb, cmp_pe[c], cmp_w1[c], cmp_b1[c], cmp_w2[c])

    bp, tp, d = x_prompt.shape
    n_mem = mem_prompt.shape[1]
    mem2 = mem_prompt.reshape(bp * n_mem, d)
    p_mkv = [norm_matmul(mem2, mem_norm[l], W['w_mem_kv'][l], tn=512).reshape(bp, n_mem, 2 * MEM_W)
             for l in range(DEPTH)]
    p_mem_k = jnp.stack([m[..., :MEM_W].reshape(bp, n_mem, MEM_HEADS, MEM_HEAD_DIM) for m in p_mkv])
    p_mem_v = jnp.stack([m[..., MEM_W:].reshape(bp, n_mem, MEM_HEADS, MEM_HEAD_DIM) for m in p_mkv])

    nb_p = tp // CMP_STRIDE - 1
    nsb_p = tp // SEL_BLOCK
    nq_tiles = tp // NSA_TQ
    t_all = jnp.arange(tp)
    c_end = jnp.arange(LANE) * CMP_STRIDE + (CMP_BLOCK - 1)
    assert nb_p <= LANE
    bias_c_p = _bias_lookup(rel_bias, t_all[:, None] - c_end[None, :])
    ii = jnp.arange(NSA_TQ)
    cc = jnp.arange(NSA_KT)
    band_p = jnp.stack([_bias_lookup(rel_bias, dd * NSA_TQ + ii[:, None] - cc[None, :])
                        for dd in range(NSA_BANDS)], axis=1)
    assert (NSA_BANDS - 1) * NSA_TQ - (NSA_KT - 1) >= REL_MAX_DIST and tp % NSA_KT == 0
    sel_map_p = _cmp_to_sel(LANE, nb_p, nsb_p, nsb_p).T
    ident = jnp.arange(bp * tp // CMP_PAGE, dtype=jnp.int32).reshape(bp, tp // CMP_PAGE)

    def prompt_side(side):
        rows = side.reshape(bp * tp, -1)
        kc = cmp_mlp(chunk_proj(rows, ident, W['cmp_w1_flat'][0], tp // CMP_PAGE, 0), nb_p, 0)
        vc = cmp_mlp(chunk_proj(rows, ident, W['cmp_w1_flat'][1], tp // CMP_PAGE, 1), nb_p, 1)
        padb = ((0, 0), (0, LANE - nb_p), (0, 0))
        wb = min(WINDOW, tp)
        kc_r, vc_r, ks, vs, kw, vw = split_side(side, bp, tp)
        return ((jnp.pad(kc, padb), jnp.pad(vc, padb), side),
                (kc_r, vc_r, ks, vs, kw[:, tp - wb:], vw[:, tp - wb:]))

    def prompt_attend(proj, ctx):
        kc, vc, side = ctx
        gates = proj[..., B_GATE_OFF:B_GATE_OFF + GATE_W].reshape(bp, tp, 3, G, NSA_HPG)
        gates_t = jnp.transpose(gates, (0, 3, 1, 2, 4)).reshape(bp, G, tp, 3 * NSA_HPG)
        return nsa_prompt(proj, gates_t, kc, vc, bias_c_p, band_p, side, sel_map_p, nb_p, nsb_p)

    zeros_shift = jnp.zeros((N_A, bp, d), F32)
    zeros_state = jnp.zeros((N_A, bp, N_HEAD_BLOCKS, MXU_DIM, MXU_DIM), F32)
    y_prompt, p_shift, p_wkv, p_side = _trunk(
        x_prompt, tp, P, W, zeros_shift, zeros_state,
        lambda l: (p_mkv[l], 0, p_mkv[l], 1), prompt_side, prompt_attend, chunk=64, tb=256)
    p_cmp_k, p_cmp_v, p_slc_k, p_slc_v, p_win_k, p_win_v = p_side

    bd, s_new, _ = x_sample.shape
    assert s_new == 1
    ts = SUBLANE
    xs = jnp.pad(x_sample, ((0, 0), (0, ts - s_new), (0, 0)))
    past_len = page_table.shape[1] * CMP_PAGE
    n_past_blk = past_len // SEL_BLOCK
    blk_per_page = CMP_PAGE // SEL_BLOCK
    nsb_s = n_past_blk + 1
    t_pos = past_len
    nc_s = -(-(past_len + s_new) // CMP_STRIDE)
    nb_s = nc_s - 1
    wb_s = cache_win_k.shape[1]
    win_start = past_len - wb_s
    nsp = -(-nsb_s // LANE) * LANE
    sel_map_s = _cmp_to_sel(nb_s, nb_s, nsp, nsb_s)
    c_end_s = jnp.arange(nb_s) * CMP_STRIDE + (CMP_BLOCK - 1)
    bias_c_s = _bias_lookup(rel_bias, t_pos - c_end_s)
    bias_c_s = jnp.pad(bias_c_s, ((0, 0), (0, SUBLANE - NSA_HPG), (0, 0)))
    nwp = wb_s + SUBLANE
    bias_w_s = _bias_lookup(rel_bias, t_pos - (win_start + jnp.arange(nwp)))
    bias_w_s = jnp.pad(bias_w_s, ((0, 0), (0, SUBLANE - NSA_HPG), (0, 0)))
    ident_s = jnp.arange(bd, dtype=jnp.int32).reshape(bd, 1)
    pages_per_step = math.gcd(16, page_table.shape[1])

    def sample_side(side):
        new = side[:, :s_new]
        parts = []
        for c, pool in enumerate((cache_cmp_k, cache_cmp_v)):
            past = chunk_proj_pool(pool, page_table, W['cmp_w1_flat'][c], pages_per_step)
            fresh_page = jnp.pad(new[:, :, c * KV_LANES:(c + 1) * KV_LANES],
                                 ((0, 0), (0, CMP_PAGE - s_new), (0, 0))).reshape(bd, CMP_PAGE, G, dh)
            fresh = chunk_proj_pool(fresh_page, ident_s, W['cmp_w1_flat'][c], 1)
            n_fresh = nc_s - past[0].shape[1]
            parts.append([jnp.concatenate([p, f[:, :n_fresh]], axis=1) for p, f in zip(past, fresh)])
        kc = cmp_mlp(parts[0], nb_s, 0)
        vc = cmp_mlp(parts[1], nb_s, 1)
        kc_n, vc_n, ks_n, vs_n, kw_n, vw_n = split_side(side, bd, s_new)
        s_win_k = jnp.concatenate([cache_win_k, kw_n], axis=1)[:, s_new:]
        s_win_v = jnp.concatenate([cache_win_v, vw_n], axis=1)[:, s_new:]
        return (kc, vc, new), (kc_n, vc_n, ks_n, vs_n, s_win_k, s_win_v)

    def sample_attend(proj, ctx):
        kc, vc, new = ctx
        o_c, idx8 = nsa_dec_cmp(proj, kc, vc, bias_c_s, sel_map_s, nb_s, nsb_s, t_pos)
        idx = idx8[:, :G, :SEL_TOPK]
        is_new = idx >= n_past_blk
        jp = jnp.minimum(idx, n_past_blk - 1)
        phys = (jnp.take_along_axis(page_table[:, None, :], jp // blk_per_page, axis=2) * blk_per_page
                + jp % blk_per_page)
        k_pos = (idx[..., None] * SEL_BLOCK + jnp.arange(SEL_BLOCK)).reshape(bd, G, 1, -1)
        d_s = t_pos - k_pos
        tab = rel_bias.astype(F32).reshape(REL_BUCKETS, G, NSA_HPG)
        onehot = jax.nn.one_hot(_rel_bucket(d_s[:, :, 0]), REL_BUCKETS, dtype=F32)
        bias_s = jnp.einsum('bgkn,ngh->bghk', onehot, tab, precision=HI)
        bias_s = jnp.pad(bias_s, ((0, 0), (0, 0), (0, SUBLANE - NSA_HPG), (0, 0)))
        q8 = jnp.pad(proj[:, 0, :MAIN_W].reshape(bd, G, NSA_HPG, dh),
                     ((0, 0), (0, 0), (0, SUBLANE - NSA_HPG), (0, 0)))
        new_rows = jnp.pad(jnp.transpose(new[:, 0, 2 * KV_LANES:].reshape(bd, 4, G, dh), (0, 2, 1, 3)),
                           ((0, 0), (0, 0), (0, SUBLANE - 4), (0, 0)))
        gates = proj[:, 0, B_GATE_OFF:B_GATE_OFF + GATE_W].reshape(bd, 3, G, NSA_HPG)
        gates8 = jnp.pad(jnp.transpose(gates, (0, 2, 3, 1)),
                         ((0, 0), (0, 0), (0, SUBLANE - NSA_HPG), (0, SUBLANE - 3)))
        o = nsa_dec_sel(phys.astype(jnp.int32), is_new.astype(jnp.int32),
                        cache_slc_k.reshape(-1, SEL_BLOCK, G, dh),
                        cache_slc_v.reshape(-1, SEL_BLOCK, G, dh),
                        q8, new_rows, d_s.astype(jnp.int32), bias_s,
                        cache_win_k, cache_win_v,
                        bias_w_s, o_c, gates8, wb_s, t_pos, win_start)
        main = o[:, :, :NSA_HPG].reshape(bd, 1, MAIN_W)
        return jnp.pad(main, ((0, 0), (0, ts - 1), (0, 0))).astype(BF16)

    y_s, s_shift, s_wkv, s_side = _trunk(
        xs, s_new, P, W, state_shift, jnp.stack([_to_block_diag(state_wkv[l]) for l in range(N_A)]),
        lambda l: (cache_mem_k, l, cache_mem_v, l), sample_side, sample_attend, chunk=SUBLANE, tb=SUBLANE)
    y_sample = y_s[:, :s_new]
    s_cmp_k, s_cmp_v, s_slc_k, s_slc_v, s_win_k, s_win_v = s_side

    return (y_prompt, y_sample, p_mem_k, p_mem_v, p_wkv, p_shift,
            p_cmp_k, p_cmp_v, p_slc_k, p_slc_v, p_win_k, p_win_v,
            s_wkv, s_shift, s_cmp_k, s_cmp_v, s_slc_k, s_slc_v, s_win_k, s_win_v)
```

```python
import functools
import math

import jax
import jax.numpy as jnp
from jax import lax
from jax.experimental import pallas as pl
from jax.experimental.pallas import tpu as pltpu

F32 = jnp.float32
BF16 = jnp.bfloat16
HI = lax.Precision.HIGHEST

D_MODEL = 2048
DEPTH = 4
N_A = 2
MEM_HEADS = 4
MEM_HEAD_DIM = 128
MEM_W = 512
MAIN_W = 1536
RWKV_HEAD = 64
RWKV_HEADS = 24
GN_EPS = 64e-5
NSA_HEAD_DIM = 128
NSA_Q_HEADS = 12
NSA_KV_GROUPS = 4
NSA_HPG = 3
GATE_W = 36
CMP_BLOCK = 32
CMP_STRIDE = 16
SEL_BLOCK = 64
SEL_TOPK = 16
SEL_LOCAL = 2
WINDOW = 512
REL_BUCKETS = 32
REL_MAX_DIST = 128
D_FF = 5504
NORM_EPS = 1e-6
MASK_NEG = -1e30
FORCE_SCORE = 1e9

LANE = 128
SUBLANE = 8
MXU_DIM = 256
VMEM_LIMIT = 56 * 1024 * 1024

FFN_DOWN_TK = 11 * LANE
HEAD_BLOCK = MXU_DIM // RWKV_HEAD
N_HEAD_BLOCKS = RWKV_HEADS // HEAD_BLOCK
LOWRANK_PAD = MXU_DIM
NSA_TQ = 256
NSA_KT = 256
NSA_BANDS = 3


def _params(*sem):
    return pltpu.CompilerParams(dimension_semantics=sem, vmem_limit_bytes=VMEM_LIMIT)


def _rms(x, g):
    return x * lax.rsqrt(jnp.mean(x * x, axis=-1, keepdims=True) + NORM_EPS) * g


def _sigmoid(x):
    return 1.0 / (1.0 + jnp.exp(-x))


def _softplus(x):
    return jnp.maximum(x, 0.0) + jnp.log(1.0 + jnp.exp(-jnp.abs(x)))


def _gelu_tanh(x):
    return 0.5 * x * (1.0 + jnp.tanh(math.sqrt(2.0 / math.pi) * (x + 0.044715 * x * x * x)))


def _dot(a, b, precision=None):
    return jnp.dot(a, b, preferred_element_type=F32, precision=precision)


def _dot_nt(a, b, precision=None):
    return lax.dot_general(a, b, (((1,), (1,)), ((), ())), preferred_element_type=F32,
                           precision=precision)


def _dot_tn(a, b, precision=None):
    return lax.dot_general(a, b, (((0,), (0,)), ((), ())), preferred_element_type=F32,
                           precision=precision)


BF16_BITS_OF_F32 = 0xFFFF0000


def _split_hi_lo(x):
    bits = lax.bitcast_convert_type(x, jnp.uint32) & jnp.uint32(BF16_BITS_OF_F32)
    hi = lax.bitcast_convert_type(bits, F32)
    return hi, x - hi


def _masked_softmax(s, mask):
    s = jnp.where(mask, s, MASK_NEG)
    p = jnp.exp(s - jnp.max(s, axis=-1, keepdims=True)) * mask.astype(F32)
    den = jnp.sum(p, axis=-1, keepdims=True)
    return p / jnp.where(den > 0, den, 1.0)


def _row_tile(rows, target):
    t = min(rows, target)
    while rows % t:
        t -= SUBLANE
    return t


def _norm_matmul_kernel(x_ref, g_ref, w_ref, o_ref, xn_ref):
    @pl.when(pl.program_id(1) == 0)
    def _():
        xn_ref[...] = _rms(x_ref[...], g_ref[...]).astype(BF16)

    o_ref[...] = _dot(xn_ref[...], w_ref[...]).astype(o_ref.dtype)


def norm_matmul(x, g, w, tn, out_dtype=F32, tm=512):
    rows, d = x.shape
    n = w.shape[1]
    tm = _row_tile(rows, tm)
    assert n % tn == 0
    return pl.pallas_call(
        _norm_matmul_kernel,
        grid=(rows // tm, n // tn),
        in_specs=[pl.BlockSpec((tm, d), lambda i, j: (i, 0)),
                  pl.BlockSpec((1, d), lambda i, j: (0, 0)),
                  pl.BlockSpec((d, tn), lambda i, j: (0, j))],
        out_specs=pl.BlockSpec((tm, tn), lambda i, j: (i, j)),
        out_shape=jax.ShapeDtypeStruct((rows, n), out_dtype),
        scratch_shapes=[pltpu.VMEM((tm, d), BF16)],
        compiler_params=_params("arbitrary", "arbitrary"),
        name="norm_matmul",
    )(x, g.reshape(1, d), w)


def _norm_swiglu_kernel(x_ref, g_ref, wg_ref, wu_ref, o_ref, xn_ref, *, tail):
    j = pl.program_id(1)
    last = pl.num_programs(1) - 1

    @pl.when(j == 0)
    def _():
        xn_ref[...] = _rms(x_ref[...], g_ref[...]).astype(BF16)

    xn = xn_ref[...]
    tn = o_ref.shape[1]
    sub = min(tn, MXU_DIM)

    def tile(up_shift):
        c0 = 0
        while c0 < tn - up_shift:
            w = min(sub, tn - up_shift - c0)
            gate = _dot(xn, wg_ref[0, 0, :, c0:c0 + w].astype(BF16))
            up = _dot(xn, wu_ref[0, 0, :, c0 + up_shift:c0 + up_shift + w].astype(BF16))
            o_ref[:, c0:c0 + w] = (gate * _sigmoid(gate) * up).astype(o_ref.dtype)
            c0 += w
        if up_shift:
            o_ref[:, c0:] = jnp.zeros((o_ref.shape[0], up_shift), o_ref.dtype)

    if tail == tn:
        tile(0)
    else:
        pl.when(j != last)(lambda: tile(0))
        pl.when(j == last)(lambda: tile(tn - tail))


def norm_swiglu(x, g, w_gu, l, i, tn=512, tm=512):
    rows, d = x.shape
    f = w_gu.shape[-1] // 2
    assert f % LANE == 0 and tn % LANE == 0
    tm = _row_tile(rows, tm)
    n_tiles = -(-f // tn)
    tail = f - (n_tiles - 1) * tn
    el = pl.Element

    def up_col(j):
        return pl.multiple_of(jnp.minimum(f + j * tn, 2 * f - tn), LANE)

    return pl.pallas_call(
        functools.partial(_norm_swiglu_kernel, tail=tail),
        grid=(rows // tm, n_tiles),
        in_specs=[pl.BlockSpec((tm, d), lambda r, j: (r, 0)),
                  pl.BlockSpec((1, d), lambda r, j: (0, 0)),
                  pl.BlockSpec((el(1), el(1), el(d), el(tn)), lambda r, j: (l, i, 0, j * tn)),
                  pl.BlockSpec((el(1), el(1), el(d), el(tn)), lambda r, j: (l, i, 0, up_col(j)))],
        out_specs=pl.BlockSpec((tm, tn), lambda r, j: (r, j)),
        out_shape=jax.ShapeDtypeStruct((rows, n_tiles * tn), BF16),
        scratch_shapes=[pltpu.VMEM((tm, d), BF16)],
        compiler_params=_params("arbitrary", "arbitrary"),
        name="norm_swiglu",
    )(x, g.reshape(1, d), w_gu, w_gu)


def _matmul_norm_res_kernel(a_ref, w_ref, g_ref, h_ref, o_ref, acc_ref, *, scale, overlap):
    k = pl.program_id(1)
    last = pl.num_programs(1) - 1

    @pl.when(k == 0)
    def _():
        acc_ref[...] = jnp.zeros_like(acc_ref)

    a = a_ref[...]
    if overlap:
        col = lax.broadcasted_iota(jnp.int32, a.shape, 1)
        a = jnp.where(col < jnp.where(k == last, overlap, 0), jnp.zeros_like(a), a)
    acc_ref[...] += _dot(a, w_ref[0, 0])

    @pl.when(k == last)
    def _():
        o_ref[...] = h_ref[...] + scale * _rms(acc_ref[...], g_ref[...])


def matmul_norm_res(a, w, l, i, g, h, scale, tk=512, tm=512):
    rows = a.shape[0]
    kdim, d = w.shape[-2:]
    assert kdim % LANE == 0 and tk % LANE == 0 and a.shape[1] >= kdim
    tm = _row_tile(rows, tm)
    n_k = -(-kdim // tk)
    overlap = n_k * tk - kdim
    el = pl.Element

    def k_off(k):
        return pl.multiple_of(jnp.minimum(k * tk, kdim - tk), LANE)

    return pl.pallas_call(
        functools.partial(_matmul_norm_res_kernel, scale=scale, overlap=overlap),
        grid=(rows // tm, n_k),
        in_specs=[pl.BlockSpec((el(tm), el(tk)), lambda r, k: (r * tm, k_off(k))),
                  pl.BlockSpec((el(1), el(1), el(tk), el(d)), lambda r, k: (l, i, k_off(k), 0)),
                  pl.BlockSpec((1, d), lambda r, k: (0, 0)),
                  pl.BlockSpec((tm, d), lambda r, k: (r, 0))],
        out_specs=pl.BlockSpec((tm, d), lambda r, k: (r, 0)),
        out_shape=jax.ShapeDtypeStruct((rows, d), F32),
        scratch_shapes=[pltpu.VMEM((tm, d), F32)],
        compiler_params=_params("arbitrary", "arbitrary"),
        name="matmul_norm_res",
    )(a, w, g.reshape(1, d), h)


def _out_proj_kernel(main_ref, mo_ref, w_ref, g_ref, h_ref, o_ref):
    k_main = main_ref.shape[1]
    y = _dot(main_ref[...], w_ref[0, :k_main, :]) + _dot(mo_ref[...], w_ref[0, k_main:, :])
    o_ref[...] = h_ref[...] + _rms(y, g_ref[...])


def out_proj(main, mo, w, l, g, h, tm=512):
    rows, k_main = main.shape
    k_mo = mo.shape[1]
    d = w.shape[-1]
    tm = _row_tile(rows, tm)
    return pl.pallas_call(
        _out_proj_kernel,
        grid=(rows // tm,),
        in_specs=[pl.BlockSpec((tm, k_main), lambda r: (r, 0)),
                  pl.BlockSpec((tm, k_mo), lambda r: (r, 0)),
                  pl.BlockSpec((1, k_main + k_mo, d), lambda r: (l, 0, 0)),
                  pl.BlockSpec((1, d), lambda r: (0, 0)),
                  pl.BlockSpec((tm, d), lambda r: (r, 0))],
        out_specs=pl.BlockSpec((tm, d), lambda r: (r, 0)),
        out_shape=jax.ShapeDtypeStruct((rows, d), F32),
        compiler_params=_params("arbitrary"),
        name="out_proj",
    )(main, mo, w, g.reshape(1, d), h)


RW_R, RW_K, RW_V, RW_QM, RW_W1, RW_V1, RW_A1, RW_G1 = 0, 6, 12, 18, 20, 21, 22, 23
RW_NBLK = 24
RW_STEP_MIXES = ((0,), (0,), (0,), (2,), (2,), (2,), (3,), (3,), (3,), (None,), (1, 3), (4, 5))
RW_FIRST_SPLIT_STEP = 10


def _rwkv_in_kernel(x_ref, sp_ref, g_ref, mu_ref, w_ref, o_ref, last_ref,
                    u_sc, xx_sc, xm_sc, carry_sc, *, tm, t_seq, n_seq, last_tile, last_row):
    ti = pl.program_id(1)
    j = pl.program_id(2)

    @pl.when(j == 0)
    def _():
        @pl.when(ti == 0)
        def _():
            carry_sc[...] = sp_ref[0]

        u = _rms(x_ref[0], g_ref[...])
        rows = lax.broadcasted_iota(jnp.int32, u.shape, 0)
        prev = pltpu.roll(u, 1, axis=0)
        for s in range(n_seq):
            prev = jnp.where(rows == s * t_seq, carry_sc[s:s + 1, :], prev)
        u_sc[...] = u
        xx_sc[...] = prev - u
        if n_seq == 1:
            carry_sc[...] = u[tm - 1:tm, :]

        @pl.when(ti == last_tile)
        def _():
            for s in range(n_seq):
                last_ref[0, s:s + 1, :] = u[s * t_seq + last_row:s * t_seq + last_row + 1, :]

    def set_mix(mix):
        if mix is None:
            xm_sc[...] = u_sc[...].astype(BF16)
        else:
            xm_sc[...] = (u_sc[...] + xx_sc[...] * mu_ref[mix:mix + 1, :]).astype(BF16)

    tn = o_ref.shape[2]
    for step, mixes in enumerate(RW_STEP_MIXES):
        if len(mixes) == 1:
            assert step < RW_FIRST_SPLIT_STEP
            if step == 0 or RW_STEP_MIXES[step - 1] != mixes:
                pl.when(j == step)(functools.partial(set_mix, mixes[0]))
        else:
            assert step >= RW_FIRST_SPLIT_STEP

            @pl.when(j == step)
            def _(mixes=mixes):
                part = tn // len(mixes)
                for q, mix in enumerate(mixes):
                    set_mix(mix)
                    o_ref[0, :, q * part:(q + 1) * part] = _dot(xm_sc[...],
                                                                w_ref[:, q * part:(q + 1) * part])

    @pl.when(j < RW_FIRST_SPLIT_STEP)
    def _():
        o_ref[0] = _dot(xm_sc[...], w_ref[...])


def rwkv_in_proj(h, shift_prev, g, mu, w_cat, t_real, tm=1024):
    b, t, d = h.shape
    n_seq = math.gcd(b, max(1, tm // t))
    bg, tg = b // n_seq, t * n_seq
    tm = _row_tile(tg, tm)
    assert n_seq == 1 or tm == tg
    n_steps = len(RW_STEP_MIXES)
    tn = RW_NBLK * MXU_DIM // n_steps
    last_tile, last_row = ((t_real - 1) // tm, (t_real - 1) % tm) if n_seq == 1 else (0, t_real - 1)
    mu8 = jnp.concatenate([mu, jnp.zeros((SUBLANE - mu.shape[0], d), F32)], axis=0)
    proj, last = pl.pallas_call(
        functools.partial(_rwkv_in_kernel, tm=tm, t_seq=t, n_seq=n_seq, last_tile=last_tile,
                          last_row=last_row),
        grid=(bg, tg // tm, n_steps),
        in_specs=[pl.BlockSpec((1, tm, d), lambda bi, ti, j: (bi, ti, 0)),
                  pl.BlockSpec((1, n_seq, d), lambda bi, ti, j: (bi, 0, 0)),
                  pl.BlockSpec((1, d), lambda bi, ti, j: (0, 0)),
                  pl.BlockSpec((SUBLANE, d), lambda bi, ti, j: (0, 0)),
                  pl.BlockSpec((d, tn), lambda bi, ti, j: (0, j))],
        out_specs=[pl.BlockSpec((1, tm, tn), lambda bi, ti, j: (bi, ti, j)),
                   pl.BlockSpec((1, n_seq, d), lambda bi, ti, j: (bi, 0, 0))],
        out_shape=[jax.ShapeDtypeStruct((bg, tg, n_steps * tn), F32),
                   jax.ShapeDtypeStruct((bg, n_seq, d), F32)],
        scratch_shapes=[pltpu.VMEM((tm, d), F32), pltpu.VMEM((tm, d), F32),
                        pltpu.VMEM((tm, d), BF16), pltpu.VMEM((n_seq, d), F32)],
        compiler_params=_params("arbitrary", "arbitrary", "arbitrary"),
        name="rwkv_in_proj",
    )(h.reshape(bg, tg, d), shift_prev.reshape(bg, n_seq, d), g.reshape(1, d), mu8, w_cat)
    return proj.reshape(b, t, n_steps * tn), last.reshape(b, d)


def _rwkv_scan_kernel(*refs, chunk, n_chunks, has_vres, t_valid, n_hb):
    if has_vres:
        (r_ref, k_ref, v_ref, tw_ref, ta_ref, tg_ref, tv_ref, vf_ref,
         w2_ref, a2_ref, g2_ref, v2_ref, vec_ref, s0_ref, y_ref, sout_ref, s_sc) = refs
    else:
        (r_ref, k_ref, v_ref, tw_ref, ta_ref, tg_ref,
         w2_ref, a2_ref, g2_ref, vec_ref, s0_ref, y_ref, sout_ref, s_sc) = refs
    ti = pl.program_id(2)
    c_len = chunk
    lanes = MXU_DIM
    rows4 = HEAD_BLOCK * c_len

    @pl.when(ti == 0)
    def _():
        s_sc[...] = s0_ref[0]

    li = lax.broadcasted_iota(jnp.int32, (lanes, lanes), 0) // RWKV_HEAD
    lj = lax.broadcasted_iota(jnp.int32, (lanes, lanes), 1) // RWKV_HEAD
    seg = (li == lj).astype(BF16)
    ci = lax.broadcasted_iota(jnp.int32, (c_len, c_len), 0)
    cj = lax.broadcasted_iota(jnp.int32, (c_len, c_len), 1)
    tri_c = (cj <= ci).astype(BF16)

    split2 = _split_hi_lo

    def seg_sums(xs):
        parts = [p for x in xs for p in split2(x)]
        res = _dot(jnp.concatenate(parts, axis=0).astype(BF16), seg)
        return [res[(2 * n) * c_len:(2 * n + 1) * c_len] + res[(2 * n + 1) * c_len:(2 * n + 2) * c_len]
                for n in range(len(xs))]

    lane_head = lax.broadcasted_iota(jnp.int32, (c_len, lanes), 1) // RWKV_HEAD
    step_head = lax.broadcasted_iota(jnp.int32, (c_len, rows4), 1) // c_len
    t_row = lax.broadcasted_iota(jnp.int32, (c_len, 2 * rows4), 0)
    s_col = lax.broadcasted_iota(jnp.int32, (c_len, 2 * rows4), 1) & (c_len - 1)
    strict = s_col < t_row
    incl = s_col <= t_row
    eye_row = ((lax.broadcasted_iota(jnp.int32, (c_len, rows4), 1) & (c_len - 1))
               == lax.broadcasted_iota(jnp.int32, (c_len, rows4), 0)).astype(F32)
    same_head = li == lj
    n_double = int(math.log2(c_len)) - 1

    def stack(x):
        return jnp.concatenate(
            [jnp.where(lane_head == hh, x, 0.0) for hh in range(HEAD_BLOCK)], axis=0)

    def stack_steps(x):
        return jnp.concatenate(
            [jnp.where(step_head == hh, x, 0.0) for hh in range(HEAD_BLOCK)], axis=0)

    def one_chunk(c, carry):
        sl = pl.ds(pl.multiple_of(c * c_len, c_len), c_len)
        tw_act = jnp.tanh(tw_ref[0, sl, :])
        ta_act = ta_ref[0, sl, :]
        tg_act = _sigmoid(tg_ref[0, sl, :])
        tv_act = tv_ref[0, sl, :] if has_vres else None
        if t_valid is not None:
            t_idx = ti * (n_chunks * c_len) + c * c_len + lax.broadcasted_iota(
                jnp.int32, (c_len, lanes), 0)
            live = t_idx < t_valid
        st = [dict() for _ in range(n_hb)]

        def prep(hb):
            e = st[hb]
            hl = slice(hb * lanes, (hb + 1) * lanes)
            w0, a0, v0 = vec_ref[0:1, hl], vec_ref[1:2, hl], vec_ref[2:3, hl]
            kkw, kaw, rk = vec_ref[3:4, hl], vec_ref[4:5, hl], vec_ref[7:8, hl]
            r = r_ref[0, sl, hl]
            k = k_ref[0, sl, hl]
            v = v_ref[0, sl, hl]
            logw = -_softplus(-(w0 + _dot(tw_act, w2_ref[:, hl]))) - 0.5
            dlog = -jnp.exp(logw)
            rate = _sigmoid(a0 + _dot(ta_act, a2_ref[:, hl]))
            e['gate'] = _dot(tg_act, g2_ref[:, hl])
            if has_vres:
                v = v + (vf_ref[0, sl, hl] - v) * _sigmoid(v0 + _dot(tv_act, v2_ref[:, hl]))
            kk = k * kkw
            k = k * (1.0 + (rate - 1.0) * kaw)
            kk_sq, rk_sum = seg_sums([kk * kk, r * k * rk])
            kk = kk / jnp.maximum(jnp.sqrt(kk_sq), 1e-12)
            if t_valid is not None:
                dlog = jnp.where(live, dlog, 0.0)
                kk = jnp.where(live, kk, 0.0)
                k_live = jnp.where(live, k, 0.0)
            else:
                k_live = k
            d_hi, d_lo = split2(dlog)
            cum2 = _dot(tri_c, jnp.concatenate([d_hi, d_lo], axis=1).astype(BF16))
            cum = cum2[:, :lanes] + cum2[:, lanes:]
            inv = jnp.exp(-cum)
            b_row = kk * rate * inv
            k_row = k_live * inv
            e['ar'] = jnp.concatenate([-kk * jnp.exp(cum - dlog), r * jnp.exp(cum)],
                                      axis=0).astype(BF16)
            e['bk_row'] = jnp.concatenate([b_row, k_row], axis=0).astype(BF16)
            e['bk'] = jnp.concatenate([stack(b_row), stack(k_row)],
                                      axis=0).astype(BF16)
            e['v'] = v
            e['v_s'] = stack(v).astype(BF16)
            e['g_end'] = jnp.exp(cum[c_len - 1:c_len, :])
            e['bonus'] = rk_sum * v

        def products(hb):
            e = st[hb]
            e['s_old'] = s_sc[hb]
            big = _dot_nt(e['ar'], jnp.concatenate([e['bk'], e['s_old'].astype(BF16)], axis=0))
            a_bk = jnp.where(strict, big[:c_len, :2 * rows4], 0.0)
            e['r_bk'] = jnp.where(incl, big[c_len:, :2 * rows4], 0.0)
            e['a_s0'], e['r_s0'] = big[:c_len, 2 * rows4:], big[c_len:, 2 * rows4:]
            e['a_k'] = a_bk[:, rows4:]
            lrow = a_bk[:, :rows4]
            e['tinv'] = eye_row + lrow
            e['lpow'] = lrow

        def mx(x):
            return x.astype(BF16) if x.shape[0] % (2 * SUBLANE) == 0 else x

        def square(hb):
            e = st[hb]
            e['lpow'] = _dot(mx(e['lpow']), mx(stack_steps(e['lpow'])))

        def double(hb):
            e = st[hb]
            both = _dot(mx(jnp.concatenate([e['lpow'], e['tinv']], axis=0)), mx(stack_steps(e['lpow'])))
            e['tinv'] = e['tinv'] + both[c_len:]
            e['lpow'] = both[:c_len]

        def solve(hb):
            e = st[hb]
            tinv = e['tinv'] + _dot(mx(e['tinv']), mx(stack_steps(e['lpow'])))
            rhs = e['a_s0'] + _dot(mx(e['a_k']), e['v_s'])
            e['u'] = _dot(mx(tinv), mx(stack(rhs)))

        def outputs(hb):
            e = st[hb]
            uv_s = jnp.concatenate([stack(e['u']).astype(BF16), e['v_s']], axis=0)
            e['y'] = e['r_s0'] + _dot(mx(e['r_bk']), uv_s)
            uv_row = jnp.concatenate([e['u'], e['v']], axis=0).astype(BF16)
            s_sc[hb] = (e['s_old'] + jnp.where(same_head, _dot_tn(uv_row, e['bk_row']), 0.0)) * e['g_end']

        def group_norm(hb):
            e = st[hb]
            hl = slice(hb * lanes, (hb + 1) * lanes)
            y = e['y']
            mean = seg_sums([y])[0] * (1.0 / RWKV_HEAD)
            yc = y - mean
            var = seg_sums([yc * yc])[0] * (1.0 / RWKV_HEAD)
            yn = yc * lax.rsqrt(var + GN_EPS) * vec_ref[5:6, hl] + vec_ref[6:7, hl]
            y_ref[0, sl, hl] = ((yn + e['bonus']) * e['gate']).astype(y_ref.dtype)

        for stage in [prep, products, square] + [double] * (n_double - 1) + [solve, outputs, group_norm]:
            for hb in range(n_hb):
                stage(hb)
        return carry

    lax.fori_loop(0, n_chunks, one_chunk, 0)

    @pl.when(ti == pl.num_programs(2) - 1)
    def _():
        sout_ref[0] = s_sc[...]


def rwkv_scan(proj, v_first_proj, w2, a2, g2, v2, vec, s0_bd, t_real, chunk, tb, n_hb=6):
    b, t, _ = proj.shape
    lanes = MXU_DIM
    tb = min(tb, t)
    assert t % tb == 0 and tb % chunk == 0 and N_HEAD_BLOCKS % n_hb == 0
    has_vres = v_first_proj is not None
    t_valid = None if t_real == t else t_real

    wide = n_hb * lanes

    def col(block0):
        assert block0 % n_hb == 0
        return pl.BlockSpec((1, tb, wide), lambda bi, hg, ti: (bi, ti, block0 // n_hb + hg))

    def fixed(block):
        return pl.BlockSpec((1, tb, lanes), lambda bi, hg, ti: (bi, ti, block))

    def wcol():
        return pl.BlockSpec((LOWRANK_PAD, wide), lambda bi, hg, ti: (0, hg))

    st = pl.BlockSpec((1, n_hb, lanes, lanes), lambda bi, hg, ti: (bi, hg, 0, 0))
    if has_vres:
        in_specs = [col(RW_R), col(RW_K), col(RW_V), fixed(RW_W1), fixed(RW_A1), fixed(RW_G1),
                    fixed(RW_V1), col(RW_V), wcol(), wcol(), wcol(), wcol()]
        args = [proj, proj, proj, proj, proj, proj, proj, v_first_proj, w2, a2, g2, v2]
    else:
        in_specs = [col(RW_R), col(RW_K), col(RW_V), fixed(RW_W1), fixed(RW_A1), fixed(RW_G1),
                    wcol(), wcol(), wcol()]
        args = [proj, proj, proj, proj, proj, proj, w2, a2, g2]
    in_specs += [pl.BlockSpec((SUBLANE, wide), lambda bi, hg, ti: (0, hg)), st]
    args += [vec, s0_bd]
    return pl.pallas_call(
        functools.partial(_rwkv_scan_kernel, chunk=chunk, n_chunks=tb // chunk,
                          has_vres=has_vres, t_valid=t_valid, n_hb=n_hb),
        grid=(b, N_HEAD_BLOCKS // n_hb, t // tb),
        in_specs=in_specs,
        out_specs=[pl.BlockSpec((1, tb, wide), lambda bi, hg, ti: (bi, ti, hg)), st],
        out_shape=[jax.ShapeDtypeStruct((b, t, MAIN_W), BF16),
                   jax.ShapeDtypeStruct(s0_bd.shape, F32)],
        scratch_shapes=[pltpu.VMEM((n_hb, lanes, lanes), F32)],
        compiler_params=_params("arbitrary", "arbitrary", "arbitrary"),
        name="rwkv_scan",
    )(*args)


def _to_block_diag(s):
    b = s.shape[0]
    s = s.reshape(b, N_HEAD_BLOCKS, HEAD_BLOCK, RWKV_HEAD, RWKV_HEAD)
    eye = jnp.eye(HEAD_BLOCK, dtype=s.dtype)
    bd = s[:, :, :, :, None, :] * eye[None, None, :, None, :, None]
    return bd.reshape(b, N_HEAD_BLOCKS, MXU_DIM, MXU_DIM)


def _from_block_diag(bd):
    b = bd.shape[0]
    n = RWKV_HEAD
    x = jnp.stack([bd[:, :, hh * n:(hh + 1) * n, hh * n:(hh + 1) * n] for hh in range(HEAD_BLOCK)],
                  axis=2)
    return x.reshape(b, RWKV_HEADS, RWKV_HEAD, RWKV_HEAD)


def _mem_attn_kernel(q_ref, k_ref, v_ref, o_ref, *, per_head):
    scale = MEM_HEAD_DIM ** -0.5
    for hh in range(MEM_HEADS):
        sl = slice(hh * MEM_HEAD_DIM, (hh + 1) * MEM_HEAD_DIM)
        k = k_ref[0, 0, :, hh, :] if per_head else k_ref[0, :, sl]
        v = v_ref[0, 0, :, hh, :] if per_head else v_ref[0, :, sl]
        s = _dot_nt(q_ref[0, :, sl], k) * scale
        p = jnp.exp(s - jnp.max(s, axis=-1, keepdims=True))
        p = p / jnp.sum(p, axis=-1, keepdims=True)
        o_ref[0, :, sl] = _dot(p, v).astype(o_ref.dtype)


def mem_attend(qsrc, q_block, ksrc, k_block, vsrc, v_block, tq=512):
    b, t, _ = qsrc.shape
    per_head = ksrc.ndim == 5
    m = ksrc.shape[-3] if per_head else ksrc.shape[1]
    tq = _row_tile(t, tq)
    if per_head:
        kv_spec = lambda blk: pl.BlockSpec((1, 1, m, MEM_HEADS, MEM_HEAD_DIM),
                                           lambda bi, ti: (blk, bi, 0, 0, 0))
    else:
        kv_spec = lambda blk: pl.BlockSpec((1, m, MEM_W), lambda bi, ti: (bi, 0, blk))
    return pl.pallas_call(
        functools.partial(_mem_attn_kernel, per_head=per_head),
        grid=(b, t // tq),
        in_specs=[pl.BlockSpec((1, tq, MEM_W), lambda bi, ti: (bi, ti, q_block)),
                  kv_spec(k_block), kv_spec(v_block)],
        out_specs=pl.BlockSpec((1, tq, MEM_W), lambda bi, ti: (bi, ti, 0)),
        out_shape=jax.ShapeDtypeStruct((b, t, MEM_W), BF16),
        compiler_params=_params("arbitrary", "arbitrary"),
        name="mem_attend",
    )(qsrc, ksrc, vsrc)


CMP_PAGE = 128
CMP_CHUNKS = CMP_PAGE // CMP_STRIDE
KV_LANES = NSA_KV_GROUPS * NSA_HEAD_DIM


def _chunk_proj_kernel(pt_ref, *refs, n_pages):
    del pt_ref
    page_refs = refs[:n_pages]
    w_ref, o0_ref, o1_ref, x_sc = refs[n_pages:]
    for kp in range(n_pages):
        for pos in range(CMP_STRIDE):
            x_sc[kp * CMP_CHUNKS:(kp + 1) * CMP_CHUNKS, pos * LANE:(pos + 1) * LANE] = (
                page_refs[kp][pl.ds(pos, CMP_CHUNKS, stride=CMP_STRIDE), :])
    res = _dot(x_sc[...].astype(BF16), w_ref[...])
    o0_ref[0] = res[:, :LANE]
    o1_ref[0] = res[:, LANE:]


def chunk_proj(rows2d, table, w_flat, n_pages, col_block=0):
    b, n_tab = table.shape
    assert n_tab % n_pages == 0

    def page_spec(kp):
        return pl.BlockSpec(
            (CMP_PAGE, LANE),
            lambda bi, pg, g, pt: (pt[bi, pg * n_pages + kp], col_block * NSA_KV_GROUPS + g))

    out_spec = pl.BlockSpec((1, n_pages * CMP_CHUNKS, LANE), lambda bi, pg, g, pt: (bi, pg, g))
    out_shape = jax.ShapeDtypeStruct((b, n_tab * CMP_CHUNKS, KV_LANES), F32)
    grid_spec = pltpu.PrefetchScalarGridSpec(
        num_scalar_prefetch=1,
        grid=(b, n_tab // n_pages, NSA_KV_GROUPS),
        in_specs=[page_spec(kp) for kp in range(n_pages)]
        + [pl.BlockSpec(w_flat.shape, lambda bi, pg, g, pt: (0, 0))],
        out_specs=[out_spec, out_spec],
        scratch_shapes=[pltpu.VMEM((n_pages * CMP_CHUNKS, CMP_STRIDE * LANE), F32)],
    )
    return pl.pallas_call(
        functools.partial(_chunk_proj_kernel, n_pages=n_pages),
        grid_spec=grid_spec,
        out_shape=[out_shape, out_shape],
        compiler_params=_params("arbitrary", "arbitrary", "arbitrary"),
        name="chunk_proj",
    )(table, *([rows2d] * n_pages), w_flat)


def _chunk_proj_pool_kernel(pt_ref, *refs, n_pages):
    del pt_ref
    page_refs = refs[:n_pages]
    w_ref, o0_ref, o1_ref, x_sc = refs[n_pages:]
    for g in range(NSA_KV_GROUPS):
        for kp in range(n_pages):
            for pos in range(CMP_STRIDE):
                x_sc[kp * CMP_CHUNKS:(kp + 1) * CMP_CHUNKS, pos * LANE:(pos + 1) * LANE] = (
                    page_refs[kp][0, pl.ds(pos, CMP_CHUNKS, stride=CMP_STRIDE), g, :])
        res = _dot(x_sc[...].astype(BF16), w_ref[...])
        o0_ref[0, :, g * LANE:(g + 1) * LANE] = res[:, :LANE]
        o1_ref[0, :, g * LANE:(g + 1) * LANE] = res[:, LANE:]


def chunk_proj_pool(pool, table, w_flat, n_pages):
    b, n_tab = table.shape
    assert n_tab % n_pages == 0

    def page_spec(kp):
        return pl.BlockSpec((1, CMP_PAGE, NSA_KV_GROUPS, LANE),
                            lambda bi, pg, pt: (pt[bi, pg * n_pages + kp], 0, 0, 0))

    out_spec = pl.BlockSpec((1, n_pages * CMP_CHUNKS, KV_LANES), lambda bi, pg, pt: (bi, pg, 0))
    out_shape = jax.ShapeDtypeStruct((b, n_tab * CMP_CHUNKS, KV_LANES), F32)
    grid_spec = pltpu.PrefetchScalarGridSpec(
        num_scalar_prefetch=1,
        grid=(b, n_tab // n_pages),
        in_specs=[page_spec(kp) for kp in range(n_pages)]
        + [pl.BlockSpec(w_flat.shape, lambda bi, pg, pt: (0, 0))],
        out_specs=[out_spec, out_spec],
        scratch_shapes=[pltpu.VMEM((n_pages * CMP_CHUNKS, CMP_STRIDE * LANE), F32)],
    )
    return pl.pallas_call(
        functools.partial(_chunk_proj_pool_kernel, n_pages=n_pages),
        grid_spec=grid_spec,
        out_shape=[out_shape, out_shape],
        compiler_params=_params("arbitrary", "arbitrary"),
        name="chunk_proj_pool",
    )(table, *([pool] * n_pages), w_flat)


def _block_mlp_kernel(p0_ref, p1_ref, pe_ref, w1_ref, b1_ref, w2_ref, o_ref):
    const = _dot(pe_ref[...], w1_ref[...])[0:1, :] + b1_ref[...]
    w2 = w2_ref[...]
    for g in range(NSA_KV_GROUPS):
        sl = slice(g * LANE, (g + 1) * LANE)
        hid = const + p0_ref[0, :, sl] + p1_ref[0, :, sl]
        o_ref[0, :, sl] = _dot(_gelu_tanh(hid), w2)


def block_mlp(p0, p1, pe, w1, b1, w2, tb=1024):
    b, nb, _ = p0.shape
    tb = _row_tile(nb, tb)
    pe8 = jnp.concatenate([pe.reshape(1, -1), jnp.zeros((SUBLANE - 1, pe.size), F32)], axis=0)
    spec = pl.BlockSpec((1, tb, KV_LANES), lambda bi, i: (bi, i, 0))
    full = lambda a: pl.BlockSpec(a.shape, lambda bi, i: (0,) * a.ndim)
    b1r = b1.reshape(1, -1)
    return pl.pallas_call(
        _block_mlp_kernel,
        grid=(b, nb // tb),
        in_specs=[spec, spec, full(pe8), full(w1), full(b1r), full(w2)],
        out_specs=spec,
        out_shape=jax.ShapeDtypeStruct((b, nb, KV_LANES), F32),
        compiler_params=_params("arbitrary", "arbitrary"),
        name="block_mlp",
    )(p0, p1, pe8, w1, b1r, w2)


def _w1_flat(w1):
    r = CMP_BLOCK // CMP_STRIDE
    e = w1.shape[1]
    w = w1.reshape(r, CMP_STRIDE, NSA_HEAD_DIM, e)
    return jnp.transpose(w, (1, 2, 0, 3)).reshape(CMP_STRIDE * NSA_HEAD_DIM, r * e).astype(BF16)


def compress(parts, nb, pe, w1, b1, w2):
    return block_mlp(parts[0][:, :nb], parts[1][:, 1:nb + 1], pe, w1, b1, w2)


def _rel_bucket(dist):
    n = jnp.maximum(dist, 0)
    max_exact = REL_BUCKETS // 2
    nf = jnp.maximum(n, 1).astype(F32)
    large = max_exact + (jnp.log(nf / max_exact) / math.log(REL_MAX_DIST / max_exact)
                         * (REL_BUCKETS - max_exact)).astype(jnp.int32)
    return jnp.where(n < max_exact, n, jnp.minimum(large, REL_BUCKETS - 1))


def _bias_lookup(rel_bias, dist):
    tab = rel_bias.astype(F32).reshape(REL_BUCKETS, NSA_KV_GROUPS, NSA_HPG)
    onehot = jax.nn.one_hot(_rel_bucket(dist), REL_BUCKETS, dtype=F32)
    out = jnp.dot(onehot, tab.reshape(REL_BUCKETS, -1), precision=HI)
    out = out.reshape(dist.shape + (NSA_KV_GROUPS, NSA_HPG))
    return jnp.moveaxis(out, (-2, -1), (0, 1))


def _cmp_to_sel(nb_pad, nb, nsb_pad, nsb):
    i = jnp.arange(nb_pad)[:, None]
    j = jnp.arange(nsb_pad)[None, :]
    start = i * CMP_STRIDE
    hit = (start < (j + 1) * SEL_BLOCK) & (start + CMP_BLOCK > j * SEL_BLOCK) & (i < nb) & (j < nsb)
    return hit.astype(F32)


def _nsa_prompt_kernel(q_ref, gt_ref, kc_ref, vc_ref, bc_ref, band_ref, ks_ref, vs_ref,
                       kw_ref, vw_ref, m_ref, o_ref, s_sc, *, nb, nsb, n_g):
    tq = NSA_TQ
    rows = NSA_HPG * tq
    qw = NSA_HPG * LANE
    i = pl.program_id(2)
    scale = NSA_HEAD_DIM ** -0.5
    row_q = lax.broadcasted_iota(jnp.int32, (rows, LANE), 0) & (tq - 1)
    lane = lax.broadcasted_iota(jnp.int32, (rows, LANE), 1)
    t_pos = i * tq + row_q
    kt_w = NSA_KT
    tiles_per_kt = kt_w // tq
    key_lane = lax.broadcasted_iota(jnp.int32, (rows, kt_w), 1)
    t_pos_k = i * tq + (lax.broadcasted_iota(jnp.int32, (rows, kt_w), 0) & (tq - 1))
    m_c = (t_pos - (lane * CMP_STRIDE + (CMP_BLOCK - 1)) >= 0) & (lane < nb)
    jb = lax.broadcasted_iota(jnp.int32, (nsb, tq), 0)
    cur = (i * tq + lax.broadcasted_iota(jnp.int32, (nsb, tq), 1)) // SEL_BLOCK
    valid = jb <= cur
    forced = valid & ((jb == 0) | (jb > cur - SEL_LOCAL))
    blk_row = lax.broadcasted_iota(jnp.int32, (nsb, kt_w), 0)
    blk_lane = lax.broadcasted_iota(jnp.int32, (nsb, kt_w), 1) // SEL_BLOCK
    sel_map = m_ref[...].astype(BF16)
    n_win_tiles = WINDOW // kt_w + 1
    n_kt = i // tiles_per_kt + 1
    st = [dict() for _ in range(n_g)]
    gl = lambda g: slice(g * LANE, (g + 1) * LANE)

    def masked_scores(g, k_ref, kt, mask_of):
        ksl = pl.ds(pl.multiple_of(kt * kt_w, kt_w), kt_w)
        band = band_ref[g, jnp.clip(i - kt * tiles_per_kt, 0, NSA_BANDS - 1)].reshape(rows, kt_w)
        s = _dot_nt(st[g]['q3'], k_ref[0, ksl, gl(g)].astype(BF16)) + band
        return jnp.where(mask_of(t_pos_k - (kt * kt_w + key_lane)), s, MASK_NEG), ksl

    def finish(m_fin, l_fin, acc):
        return jnp.where(m_fin > MASK_NEG, acc / jnp.where(l_fin > 0, l_fin, 1.0), 0.0)

    def window_scores(g):
        e = st[g]
        e['q3'] = (jnp.concatenate([q_ref[0, :, g * qw + hh * LANE:g * qw + (hh + 1) * LANE]
                                    for hh in range(NSA_HPG)], axis=0) * scale).astype(BF16)
        e['win'] = []
        m_w = jnp.full((rows, 1), MASK_NEG, F32)
        for jw in range(n_win_tiles):
            kt_raw = i // tiles_per_kt - (n_win_tiles - 1) + jw
            reach = jnp.where(kt_raw >= 0, WINDOW, 0)
            s_w, ksl_w = masked_scores(g, kw_ref, jnp.maximum(kt_raw, 0),
                                       lambda dist, reach=reach: (dist >= 0) & (dist < reach))
            e['win'].append((s_w, ksl_w))
            m_w = jnp.maximum(m_w, jnp.max(s_w, axis=-1, keepdims=True))
        e['m_w'] = m_w

    def compressed(g):
        e = st[g]
        s_c = _dot_nt(e['q3'], kc_ref[0, :, gl(g)].astype(BF16)) + bc_ref[g].reshape(rows, LANE)
        e['p_c'] = _masked_softmax(s_c, m_c)
        e['o_c'] = _dot(e['p_c'], vc_ref[0, :, gl(g)])

    def select(g):
        e = st[g]
        p_hi, p_lo = _split_hi_lo(e['p_c'])
        imp3 = _dot_nt(sel_map, p_hi.astype(BF16)) + _dot_nt(sel_map, p_lo.astype(BF16))
        imp = imp3[:, 0:tq]
        for hh in range(1, NSA_HPG):
            imp = imp + imp3[:, hh * tq:(hh + 1) * tq]
        imp = jnp.where(forced, FORCE_SCORE, jnp.where(valid, imp, -FORCE_SCORE))
        rank = jnp.zeros((nsb, tq), jnp.int32)
        for jp in range(nsb):
            other = imp[jp:jp + 1, :]
            rank = rank + ((other > imp) | ((other == imp) & (jb > jp))).astype(jnp.int32)
        sel = (rank < min(SEL_TOPK, nsb)).astype(F32).T
        e['sel3'] = jnp.concatenate([sel] * NSA_HPG, axis=0).astype(BF16)

    def window_values(g):
        e = st[g]
        l_w = jnp.zeros((rows, 1), F32)
        acc_w = jnp.zeros((rows, LANE), F32)
        for s_w, ksl_w in e['win']:
            p_w = jnp.exp(s_w - e['m_w'])
            l_w = l_w + jnp.sum(p_w, axis=-1, keepdims=True)
            acc_w = acc_w + _dot(p_w, vw_ref[0, ksl_w, gl(g)])
        e['o_w'] = finish(e['m_w'], l_w, acc_w)

    for stage in (window_scores, compressed, select, window_values):
        for g in range(n_g):
            stage(g)

    def scores(kt, m_run):
        expand = (blk_row == kt * (kt_w // SEL_BLOCK) + blk_lane).astype(BF16)
        out = []
        for g in range(n_g):
            chosen = _dot(st[g]['sel3'], expand) > 0.5
            s, _ = masked_scores(g, ks_ref, kt, lambda dist, chosen=chosen: chosen & (dist >= 0))
            s_sc[g, kt] = s
            out.append(jnp.maximum(m_run[g], jnp.max(s, axis=-1, keepdims=True)))
        return tuple(out)

    m_s = lax.fori_loop(0, n_kt, scores, tuple(jnp.full((rows, 1), MASK_NEG, F32) for _ in range(n_g)))

    def accum(kt, carry):
        ksl = pl.ds(pl.multiple_of(kt * kt_w, kt_w), kt_w)
        out = []
        for g in range(n_g):
            l_run, acc = carry[g]
            p = jnp.exp(s_sc[g, kt] - m_s[g])
            out.append((l_run + jnp.sum(p, axis=-1, keepdims=True), acc + _dot(p, vs_ref[0, ksl, gl(g)])))
        return tuple(out)

    fin = lax.fori_loop(0, n_kt, accum, tuple((jnp.zeros((rows, 1), F32), jnp.zeros((rows, LANE), F32))
                                              for _ in range(n_g)))
    for g in range(n_g):
        e = st[g]
        o_s = finish(m_s[g], fin[g][0], fin[g][1])
        gt = _sigmoid(gt_ref[0, g])
        gcol = lambda br, gt=gt: jnp.concatenate(
            [gt[:, br * NSA_HPG + hh:br * NSA_HPG + hh + 1] for hh in range(NSA_HPG)], axis=0)
        o = gcol(0) * e['o_c'] + gcol(1) * o_s + gcol(2) * e['o_w']
        for hh in range(NSA_HPG):
            o_ref[0, :, g * qw + hh * LANE:g * qw + (hh + 1) * LANE] = (
                o[hh * tq:(hh + 1) * tq].astype(o_ref.dtype))


def nsa_prompt(proj, gates_t, kc, vc, bias_c, band, side, sel_map, nb, nsb, n_g=2):
    b, t, _ = proj.shape
    tq = NSA_TQ
    qw = n_g * NSA_HPG * LANE
    kvw = n_g * LANE
    n_gp = NSA_KV_GROUPS // n_g
    kv = lambda off: pl.BlockSpec((1, t, kvw), lambda bi, gp, i: (bi, 0, off * n_gp + gp))
    return pl.pallas_call(
        functools.partial(_nsa_prompt_kernel, nb=nb, nsb=nsb, n_g=n_g),
        grid=(b, n_gp, t // tq),
        in_specs=[pl.BlockSpec((1, tq, qw), lambda bi, gp, i: (bi, i, gp)),
                  pl.BlockSpec((1, n_g, tq, NSA_HPG * 3), lambda bi, gp, i: (bi, gp, i, 0)),
                  pl.BlockSpec((1, LANE, kvw), lambda bi, gp, i: (bi, 0, gp)),
                  pl.BlockSpec((1, LANE, kvw), lambda bi, gp, i: (bi, 0, gp)),
                  pl.BlockSpec((n_g, NSA_HPG, tq, LANE), lambda bi, gp, i: (gp, 0, i, 0)),
                  pl.BlockSpec((n_g, NSA_BANDS, NSA_HPG, tq, NSA_KT), lambda bi, gp, i: (gp, 0, 0, 0, 0)),
                  kv(2), kv(3), kv(4), kv(5),
                  pl.BlockSpec(sel_map.shape, lambda bi, gp, i: (0, 0))],
        out_specs=pl.BlockSpec((1, tq, qw), lambda bi, gp, i: (bi, i, gp)),
        out_shape=jax.ShapeDtypeStruct((b, t, MAIN_W), BF16),
        scratch_shapes=[pltpu.VMEM((n_g, t // NSA_KT, NSA_HPG * tq, NSA_KT), F32)],
        compiler_params=_params("arbitrary", "arbitrary", "arbitrary"),
        name="nsa_prompt",
    )(proj, gates_t, kc, vc, bias_c, band, side, side, side, side, sel_map)


def _nsa_dec_cmp_kernel(q_ref, kc_ref, vc_ref, bc_ref, m_ref, oc_ref, idx_ref, *, nb, nsb, t_pos):
    scale = NSA_HEAD_DIM ** -0.5
    nbp = kc_ref.shape[1]
    nsp = m_ref.shape[1]
    n_sel = min(SEL_TOPK, nsb)
    lane_b = lax.broadcasted_iota(jnp.int32, (SUBLANE, nbp), 1)
    m_c = (t_pos - (lane_b * CMP_STRIDE + (CMP_BLOCK - 1)) >= 0) & (lane_b < nb)
    row_s = lax.broadcasted_iota(jnp.int32, (SUBLANE, nsp), 0)
    jb = lax.broadcasted_iota(jnp.int32, (1, nsp), 1)
    cur = t_pos // SEL_BLOCK
    valid = jb <= cur
    forced = valid & ((jb == 0) | (jb > cur - SEL_LOCAL))
    out_lane = lax.broadcasted_iota(jnp.int32, (1, LANE), 1)
    idx_rows = []
    for g in range(NSA_KV_GROUPS):
        q3 = jnp.concatenate(
            [q_ref[0, 0:1, (g * NSA_HPG + hh) * LANE:(g * NSA_HPG + hh + 1) * LANE]
             for hh in range(NSA_HPG)] + [jnp.zeros((SUBLANE - NSA_HPG, LANE), F32)], axis=0)
        sl = slice(g * LANE, (g + 1) * LANE)
        s_c = _dot_nt(q3, kc_ref[0, :, sl]) * scale + bc_ref[g]
        p_c = _masked_softmax(s_c, m_c)
        oc_ref[0, g] = _dot(p_c, vc_ref[0, :, sl])
        imp8 = jnp.where(row_s < NSA_HPG, _dot(p_c, m_ref[...], HI), 0.0)
        imp = jnp.sum(imp8, axis=0, keepdims=True)
        imp = jnp.where(forced, FORCE_SCORE, jnp.where(valid, imp, -FORCE_SCORE))
        imp = jnp.where(jb < nsb, imp, -jnp.inf)
        jbf = jb.astype(F32)
        picks = jnp.zeros((1, LANE), F32)
        for kk in range(n_sel):
            best = jnp.max(imp, axis=-1, keepdims=True)
            arg = jnp.min(jnp.where(imp == best, jbf, float(nsp)), axis=-1, keepdims=True)
            picks = jnp.where(out_lane == kk, arg, picks)
            imp = jnp.where(jbf == arg, -jnp.inf, imp)
        idx_rows.append(picks.astype(jnp.int32))
    idx_rows.append(jnp.zeros((SUBLANE - NSA_KV_GROUPS, LANE), jnp.int32))
    idx_ref[0] = jnp.concatenate(idx_rows, axis=0)


def nsa_dec_cmp(proj, kc, vc, bias_c, sel_map, nb, nsb, t_pos):
    b, tp, _ = proj.shape
    nbp = kc.shape[1]
    return pl.pallas_call(
        functools.partial(_nsa_dec_cmp_kernel, nb=nb, nsb=nsb, t_pos=t_pos),
        grid=(b,),
        in_specs=[pl.BlockSpec((1, tp, MAIN_W), lambda bi: (bi, 0, 0)),
                  pl.BlockSpec((1, nbp, KV_LANES), lambda bi: (bi, 0, 0)),
                  pl.BlockSpec((1, nbp, KV_LANES), lambda bi: (bi, 0, 0)),
                  pl.BlockSpec(bias_c.shape, lambda bi: (0, 0, 0)),
                  pl.BlockSpec(sel_map.shape, lambda bi: (0, 0))],
        out_specs=[pl.BlockSpec((1, NSA_KV_GROUPS, SUBLANE, LANE), lambda bi: (bi, 0, 0, 0)),
                   pl.BlockSpec((1, SUBLANE, LANE), lambda bi: (bi, 0, 0))],
        out_shape=[jax.ShapeDtypeStruct((b, NSA_KV_GROUPS, SUBLANE, LANE), F32),
                   jax.ShapeDtypeStruct((b, SUBLANE, LANE), jnp.int32)],
        compiler_params=_params("arbitrary"),
        name="nsa_dec_cmp",
    )(proj, kc, vc, bias_c, sel_map)


def _nsa_dec_sel_kernel(phys_ref, isnew_ref, *refs, n_sel, n_win, t_pos, win_start):
    del phys_ref
    kb_refs = refs[:n_sel]
    vb_refs = refs[n_sel:2 * n_sel]
    (q_ref, new_ref, ds_ref, bs_ref, wk_ref, wv_ref, bw_ref, oc_ref, gt_ref, o_ref) = refs[2 * n_sel:]
    bi = pl.program_id(0)
    g = pl.program_id(1)
    scale = NSA_HEAD_DIM ** -0.5
    q3 = q_ref[0, 0]
    new_rows = new_ref[0, 0]
    pad_blk = jnp.zeros((SEL_BLOCK - 1, LANE), F32)
    new_k = jnp.concatenate([new_rows[0:1], pad_blk], axis=0)
    new_v = jnp.concatenate([new_rows[1:2], pad_blk], axis=0)
    ks, vs = [], []
    for kk in range(n_sel):
        fresh = isnew_ref[bi, g, kk] > 0
        ks.append(jnp.where(fresh, new_k, kb_refs[kk][0, :, g, :]))
        vs.append(jnp.where(fresh, new_v, vb_refs[kk][0, :, g, :]))
    ks = jnp.concatenate(ks, axis=0)
    vs = jnp.concatenate(vs, axis=0)
    s_s = _dot_nt(q3, ks) * scale + bs_ref[0, 0]
    p_s = _masked_softmax(s_s, ds_ref[0, 0] >= 0)
    o_s = _dot(p_s, vs)

    pad_w = jnp.zeros((SUBLANE - 1, LANE), F32)
    kw = jnp.concatenate([wk_ref[0, :, g, :], new_rows[2:3], pad_w], axis=0)
    vw = jnp.concatenate([wv_ref[0, :, g, :], new_rows[3:4], pad_w], axis=0)
    nw = kw.shape[0]
    pos = win_start + lax.broadcasted_iota(jnp.int32, (SUBLANE, nw), 1)
    d_w = t_pos - pos
    m_w = (d_w >= 0) & (d_w < WINDOW) & (pos >= 0) & (pos - win_start < n_win)
    s_w = _dot_nt(q3, kw) * scale + bw_ref[0]
    p_w = _masked_softmax(s_w, m_w)
    o_w = _dot(p_w, vw)

    gt = _sigmoid(gt_ref[0, 0])
    o_ref[0, 0] = gt[:, 0:1] * oc_ref[0, 0] + gt[:, 1:2] * o_s + gt[:, 2:3] * o_w


def nsa_dec_sel(phys, isnew, pool_k, pool_v, q8, new_rows, d_s, bias_s, win_k, win_v, bias_w,
                o_c, gates8, n_win, t_pos, win_start):
    b = q8.shape[0]
    n_sel = phys.shape[-1]
    n_keys = n_sel * SEL_BLOCK
    nw = win_k.shape[1]
    nwp = nw + SUBLANE

    def blk_spec(kk):
        return pl.BlockSpec((1, SEL_BLOCK, NSA_KV_GROUPS, LANE),
                            lambda bi, g, ph, nf: (ph[bi, g, kk], 0, 0, 0))

    per_bg = lambda *shape: pl.BlockSpec((1, 1) + shape, lambda bi, g, ph, nf: (bi, g) + (0,) * len(shape))
    grid_spec = pltpu.PrefetchScalarGridSpec(
        num_scalar_prefetch=2,
        grid=(b, NSA_KV_GROUPS),
        in_specs=[blk_spec(kk) for kk in range(n_sel)] + [blk_spec(kk) for kk in range(n_sel)]
        + [per_bg(SUBLANE, LANE), per_bg(SUBLANE, LANE), per_bg(1, n_keys), per_bg(SUBLANE, n_keys),
           pl.BlockSpec((1, nw, NSA_KV_GROUPS, LANE), lambda bi, g, ph, nf: (bi, 0, 0, 0)),
           pl.BlockSpec((1, nw, NSA_KV_GROUPS, LANE), lambda bi, g, ph, nf: (bi, 0, 0, 0)),
           pl.BlockSpec((1, SUBLANE, nwp), lambda bi, g, ph, nf: (g, 0, 0)),
           per_bg(SUBLANE, LANE), per_bg(SUBLANE, SUBLANE)],
        out_specs=per_bg(SUBLANE, LANE),
    )
    return pl.pallas_call(
        functools.partial(_nsa_dec_sel_kernel, n_sel=n_sel, n_win=nw + 1, t_pos=t_pos,
                          win_start=win_start),
        grid_spec=grid_spec,
        out_shape=jax.ShapeDtypeStruct((b, NSA_KV_GROUPS, SUBLANE, LANE), F32),
        compiler_params=_params("arbitrary", "arbitrary"),
        name="nsa_dec_sel",
    )(phys, isnew, *([pool_k] * n_sel), *([pool_v] * n_sel), q8, new_rows, d_s, bias_s,
      win_k, win_v, bias_w, o_c, gates8)


def _pad_cols(w, n):
    return jnp.pad(w, ((0, 0), (0, n - w.shape[1])))


def _pad_rows(w, n):
    return jnp.pad(w, ((0, n - w.shape[0]), (0, 0)))


def _prep_weights(P):
    W = {}
    W['w_gu'] = P['ffn_gu']
    W['w_d'] = P['ffn_d'].astype(BF16)
    W['w_out'] = P['w_out'].astype(BF16)
    W['rw_in'], W['rw_w2'], W['rw_a2'], W['rw_g2'], W['rw_v2'], W['rw_vec'] = [], [], [], [], [], []
    lp = LOWRANK_PAD
    for l in range(N_A):
        w_in = P['w_in_a'][l]
        v1 = P['rw_v1'][l - 1] if l > 0 else jnp.zeros((D_MODEL, lp), F32)
        W['rw_in'].append(jnp.concatenate([
            w_in, _pad_cols(P['rw_w1'][l], lp), _pad_cols(v1, lp), _pad_cols(P['rw_a1'][l], lp),
            _pad_cols(P['rw_g1'][l], lp)], axis=1).astype(BF16))
        W['rw_w2'].append(_pad_rows(P['rw_w2'][l], lp))
        W['rw_a2'].append(_pad_rows(P['rw_a2'][l], lp))
        W['rw_g2'].append(_pad_rows(P['rw_g2'][l], lp))
        W['rw_v2'].append(_pad_rows(P['rw_v2'][l - 1], lp) if l > 0 else None)
        v0 = P['rw_v0'][l - 1] if l > 0 else jnp.zeros((MAIN_W,), F32)
        W['rw_vec'].append(jnp.stack([P['rw_w0'][l], P['rw_a0'][l], v0, P['rw_kk'][l], P['rw_ka'][l],
                                      P['rw_lnw'][l], P['rw_lnb'][l], P['rw_rk'][l].reshape(-1)]))
    W['w_in_b'] = []
    for l in range(DEPTH - N_A):
        w = P['w_in_b'][l]
        W['w_in_b'].append(jnp.concatenate([
            w[:, :MAIN_W], w[:, MAIN_W + GATE_W:], _pad_cols(w[:, MAIN_W:MAIN_W + GATE_W], MXU_DIM)],
            axis=1).astype(BF16))
    W['w_kv'] = P['w_kv'].astype(BF16)
    W['w_mem_kv'] = [P['w_mem_kv'][l].astype(BF16) for l in range(DEPTH)]
    W['cmp_w1_flat'] = [_w1_flat(P['cmp_w1'][c]) for c in range(2)]
    return W


B_QM_BLOCK = MAIN_W // MEM_W
B_GATE_OFF = MAIN_W + MEM_W


def _ffn(h2, norms_l, first, W, l, i):
    hid = norm_swiglu(h2, norms_l[first], W['w_gu'], l, i, tn=512, tm=1024)
    return matmul_norm_res(hid, W['w_d'], l, i, norms_l[first + 1], h2, 0.5, tk=FFN_DOWN_TK)


def _trunk(x, t_real, P, W, shift0, wkv0_bd, mem_src, make_side, attend, chunk, tb):
    b, t, d = x.shape
    h2 = x.reshape(b * t, d)
    shifts, states = [], []
    v_first_proj, ctx, side_state = None, None, None
    for l in range(DEPTH):
        n = P['norms'][l]
        h2 = _ffn(h2, n, 0, W, l, 0)
        if l < N_A:
            proj, last = rwkv_in_proj(h2.reshape(b, t, d), shift0[l], n[2], P['rw_mu'][l],
                                      W['rw_in'][l], t_real)
            main, s_bd = rwkv_scan(proj, v_first_proj if l > 0 else None, W['rw_w2'][l], W['rw_a2'][l],
                                   W['rw_g2'][l], W['rw_v2'][l], W['rw_vec'][l], wkv0_bd[l],
                                   t_real, chunk, tb)
            if l == 0:
                v_first_proj = proj
            shifts.append(last)
            states.append(_from_block_diag(s_bd))
            q_src, q_block = proj, RW_QM * MXU_DIM // MEM_W
        else:
            proj = norm_matmul(h2, n[2], W['w_in_b'][l - N_A], tn=768, tm=1024).reshape(b, t, -1)
            main = attend(proj, ctx)
            q_src, q_block = proj, B_QM_BLOCK
        mk, kb, mv, vb = mem_src(l)
        mo = mem_attend(q_src, q_block, mk, kb, mv, vb)
        h2 = out_proj(main.reshape(b * t, MAIN_W), mo.reshape(b * t, MEM_W), W['w_out'], l, n[3], h2)
        h2 = _ffn(h2, n, 4, W, l, 1)
        if l == N_A - 1:
            side = norm_matmul(h2, P['kv_norm'], W['w_kv'], tn=768, tm=1024).reshape(b, t, -1)
            ctx, side_state = make_side(side)
    return h2.reshape(b, t, d), jnp.stack(shifts), jnp.stack(states), side_state


def kernel(x_prompt, x_sample, mem_prompt, state_wkv, state_shift, cache_mem_k, cache_mem_v,
           cache_cmp_k, cache_cmp_v, cache_slc_k, cache_slc_v, cache_win_k, cache_win_v, page_table,
           norms, ffn_gu, ffn_d, w_in_a, w_in_b, w_out, mem_norm, w_mem_kv, kv_norm, w_kv,
           cmp_pe, cmp_w1, cmp_b1, cmp_w2, rel_bias,
           rw_mu, rw_w0, rw_w1, rw_w2, rw_a0, rw_a1, rw_a2, rw_g1, rw_g2, rw_v0, rw_v1, rw_v2,
           rw_kk, rw_ka, rw_rk, rw_lnw, rw_lnb):
    P = dict(norms=norms, ffn_gu=ffn_gu, ffn_d=ffn_d, w_in_a=w_in_a, w_in_b=w_in_b, w_out=w_out,
             kv_norm=kv_norm, w_kv=w_kv, w_mem_kv=w_mem_kv, cmp_w1=cmp_w1, rw_mu=rw_mu, rw_w0=rw_w0,
             rw_w1=rw_w1, rw_w2=rw_w2, rw_a0=rw_a0, rw_a1=rw_a1, rw_a2=rw_a2, rw_g1=rw_g1,
             rw_g2=rw_g2, rw_v0=rw_v0, rw_v1=rw_v1, rw_v2=rw_v2, rw_kk=rw_kk, rw_ka=rw_ka,
             rw_rk=rw_rk, rw_lnw=rw_lnw, rw_lnb=rw_lnb)
    W = _prep_weights(P)
    G, dh = NSA_KV_GROUPS, NSA_HEAD_DIM
    split_side = lambda side, bx, t: [side[:, :t, c * KV_LANES:(c + 1) * KV_LANES].reshape(bx, t, G, dh)
                                      for c in range(6)]

    def cmp_mlp(parts, nb, c):
        return compress(parts, nb, cmp_pe[c], cmp_w1[c], cmp_b1[c], cmp_w2[c])

    bp, tp, d = x_prompt.shape
    n_mem = mem_prompt.shape[1]
    mem2 = mem_prompt.reshape(bp * n_mem, d)
    p_mkv = [norm_matmul(mem2, mem_norm[l], W['w_mem_kv'][l], tn=512).reshape(bp, n_mem, 2 * MEM_W)
             for l in range(DEPTH)]
    p_mem_k = jnp.stack([m[..., :MEM_W].reshape(bp, n_mem, MEM_HEADS, MEM_HEAD_DIM) for m in p_mkv])
    p_mem_v = jnp.stack([m[..., MEM_W:].reshape(bp, n_mem, MEM_HEADS, MEM_HEAD_DIM) for m in p_mkv])

    nb_p = tp // CMP_STRIDE - 1
    nsb_p = tp // SEL_BLOCK
    nq_tiles = tp // NSA_TQ
    t_all = jnp.arange(tp)
    c_end = jnp.arange(LANE) * CMP_STRIDE + (CMP_BLOCK - 1)
    assert nb_p <= LANE
    bias_c_p = _bias_lookup(rel_bias, t_all[:, None] - c_end[None, :])
    ii = jnp.arange(NSA_TQ)
    cc = jnp.arange(NSA_KT)
    band_p = jnp.stack([_bias_lookup(rel_bias, dd * NSA_TQ + ii[:, None] - cc[None, :])
                        for dd in range(NSA_BANDS)], axis=1)
    assert (NSA_BANDS - 1) * NSA_TQ - (NSA_KT - 1) >= REL_MAX_DIST and tp % NSA_KT == 0
    sel_map_p = _cmp_to_sel(LANE, nb_p, nsb_p, nsb_p).T
    ident = jnp.arange(bp * tp // CMP_PAGE, dtype=jnp.int32).reshape(bp, tp // CMP_PAGE)

    def prompt_side(side):
        rows = side.reshape(bp * tp, -1)
        kc = cmp_mlp(chunk_proj(rows, ident, W['cmp_w1_flat'][0], tp // CMP_PAGE, 0), nb_p, 0)
        vc = cmp_mlp(chunk_proj(rows, ident, W['cmp_w1_flat'][1], tp // CMP_PAGE, 1), nb_p, 1)
        padb = ((0, 0), (0, LANE - nb_p), (0, 0))
        wb = min(WINDOW, tp)
        kc_r, vc_r, ks, vs, kw, vw = split_side(side, bp, tp)
        return ((jnp.pad(kc, padb), jnp.pad(vc, padb), side),
                (kc_r, vc_r, ks, vs, kw[:, tp - wb:], vw[:, tp - wb:]))

    def prompt_attend(proj, ctx):
        kc, vc, side = ctx
        gates = proj[..., B_GATE_OFF:B_GATE_OFF + GATE_W].reshape(bp, tp, 3, G, NSA_HPG)
        gates_t = jnp.transpose(gates, (0, 3, 1, 2, 4)).reshape(bp, G, tp, 3 * NSA_HPG)
        return nsa_prompt(proj, gates_t, kc, vc, bias_c_p, band_p, side, sel_map_p, nb_p, nsb_p)

    zeros_shift = jnp.zeros((N_A, bp, d), F32)
    zeros_state = jnp.zeros((N_A, bp, N_HEAD_BLOCKS, MXU_DIM, MXU_DIM), F32)
    y_prompt, p_shift, p_wkv, p_side = _trunk(
        x_prompt, tp, P, W, zeros_shift, zeros_state,
        lambda l: (p_mkv[l], 0, p_mkv[l], 1), prompt_side, prompt_attend, chunk=64, tb=256)
    p_cmp_k, p_cmp_v, p_slc_k, p_slc_v, p_win_k, p_win_v = p_side

    bd, s_new, _ = x_sample.shape
    assert s_new == 1
    ts = SUBLANE
    xs = jnp.pad(x_sample, ((0, 0), (0, ts - s_new), (0, 0)))
    past_len = page_table.shape[1] * CMP_PAGE
    n_past_blk = past_len // SEL_BLOCK
    blk_per_page = CMP_PAGE // SEL_BLOCK
    nsb_s = n_past_blk + 1
    t_pos = past_len
    nc_s = -(-(past_len + s_new) // CMP_STRIDE)
    nb_s = nc_s - 1
    wb_s = cache_win_k.shape[1]
    win_start = past_len - wb_s
    nsp = -(-nsb_s // LANE) * LANE
    sel_map_s = _cmp_to_sel(nb_s, nb_s, nsp, nsb_s)
    c_end_s = jnp.arange(nb_s) * CMP_STRIDE + (CMP_BLOCK - 1)
    bias_c_s = _bias_lookup(rel_bias, t_pos - c_end_s)
    bias_c_s = jnp.pad(bias_c_s, ((0, 0), (0, SUBLANE - NSA_HPG), (0, 0)))
    nwp = wb_s + SUBLANE
    bias_w_s = _bias_lookup(rel_bias, t_pos - (win_start + jnp.arange(nwp)))
    bias_w_s = jnp.pad(bias_w_s, ((0, 0), (0, SUBLANE - NSA_HPG), (0, 0)))
    ident_s = jnp.arange(bd, dtype=jnp.int32).reshape(bd, 1)
    pages_per_step = math.gcd(16, page_table.shape[1])

    def sample_side(side):
        new = side[:, :s_new]
        parts = []
        for c, pool in enumerate((cache_cmp_k, cache_cmp_v)):
            past = chunk_proj_pool(pool, page_table, W['cmp_w1_flat'][c], pages_per_step)
            fresh_page = jnp.pad(new[:, :, c * KV_LANES:(c + 1) * KV_LANES],
                                 ((0, 0), (0, CMP_PAGE - s_new), (0, 0))).reshape(bd, CMP_PAGE, G, dh)
            fresh = chunk_proj_pool(fresh_page, ident_s, W['cmp_w1_flat'][c], 1)
            n_fresh = nc_s - past[0].shape[1]
            parts.append([jnp.concatenate([p, f[:, :n_fresh]], axis=1) for p, f in zip(past, fresh)])
        kc = cmp_mlp(parts[0], nb_s, 0)
        vc = cmp_mlp(parts[1], nb_s, 1)
        kc_n, vc_n, ks_n, vs_n, kw_n, vw_n = split_side(side, bd, s_new)
        s_win_k = jnp.concatenate([cache_win_k, kw_n], axis=1)[:, s_new:]
        s_win_v = jnp.concatenate([cache_win_v, vw_n], axis=1)[:, s_new:]
        return (kc, vc, new), (kc_n, vc_n, ks_n, vs_n, s_win_k, s_win_v)

    def sample_attend(proj, ctx):
        kc, vc, new = ctx
        o_c, idx8 = nsa_dec_cmp(proj, kc, vc, bias_c_s, sel_map_s, nb_s, nsb_s, t_pos)
        idx = idx8[:, :G, :SEL_TOPK]
        is_new = idx >= n_past_blk
        jp = jnp.minimum(idx, n_past_blk - 1)
        phys = (jnp.take_along_axis(page_table[:, None, :], jp // blk_per_page, axis=2) * blk_per_page
                + jp % blk_per_page)
        k_pos = (idx[..., None] * SEL_BLOCK + jnp.arange(SEL_BLOCK)).reshape(bd, G, 1, -1)
        d_s = t_pos - k_pos
        tab = rel_bias.astype(F32).reshape(REL_BUCKETS, G, NSA_HPG)
        onehot = jax.nn.one_hot(_rel_bucket(d_s[:, :, 0]), REL_BUCKETS, dtype=F32)
        bias_s = jnp.einsum('bgkn,ngh->bghk', onehot, tab, precision=HI)
        bias_s = jnp.pad(bias_s, ((0, 0), (0, 0), (0, SUBLANE - NSA_HPG), (0, 0)))
        q8 = jnp.pad(proj[:, 0, :MAIN_W].reshape(bd, G, NSA_HPG, dh),
                     ((0, 0), (0, 0), (0, SUBLANE - NSA_HPG), (0, 0)))
        new_rows = jnp.pad(jnp.transpose(new[:, 0, 2 * KV_LANES:].reshape(bd, 4, G, dh), (0, 2, 1, 3)),
                           ((0, 0), (0, 0), (0, SUBLANE - 4), (0, 0)))
        gates = proj[:, 0, B_GATE_OFF:B_GATE_OFF + GATE_W].reshape(bd, 3, G, NSA_HPG)
        gates8 = jnp.pad(jnp.transpose(gates, (0, 2, 3, 1)),
                         ((0, 0), (0, 0), (0, SUBLANE - NSA_HPG), (0, SUBLANE - 3)))
        o = nsa_dec_sel(phys.astype(jnp.int32), is_new.astype(jnp.int32),
                        cache_slc_k.reshape(-1, SEL_BLOCK, G, dh),
                        cache_slc_v.reshape(-1, SEL_BLOCK, G, dh),
                        q8, new_rows, d_s.astype(jnp.int32), bias_s,
                        cache_win_k, cache_win_v,
                        bias_w_s, o_c, gates8, wb_s, t_pos, win_start)
        main = o[:, :, :NSA_HPG].reshape(bd, 1, MAIN_W)
        return jnp.pad(main, ((0, 0), (0, ts - 1), (0, 0))).astype(BF16)

    y_s, s_shift, s_wkv, s_side = _trunk(
        xs, s_new, P, W, state_shift, jnp.stack([_to_block_diag(state_wkv[l]) for l in range(N_A)]),
        lambda l: (cache_mem_k, l, cache_mem_v, l), sample_side, sample_attend, chunk=SUBLANE, tb=SUBLANE)
    y_sample = y_s[:, :s_new]
    s_cmp_k, s_cmp_v, s_slc_k, s_slc_v, s_win_k, s_win_v = s_side

    return (y_prompt, y_sample, p_mem_k, p_mem_v, p_wkv, p_shift,
            p_cmp_k, p_cmp_v, p_slc_k, p_slc_v, p_win_k, p_win_v,
            s_wkv, s_shift, s_cmp_k, s_cmp_v, s_slc_k, s_slc_v, s_win_k, s_win_v)
```

```python
import functools
import math

import jax
import jax.numpy as jnp
from jax import lax
from jax.experimental import pallas as pl
from jax.experimental.pallas import tpu as pltpu

F32 = jnp.float32
BF16 = jnp.bfloat16
HI = lax.Precision.HIGHEST

D_MODEL = 2048
DEPTH = 4
N_A = 2
MEM_HEADS = 4
MEM_HEAD_DIM = 128
MEM_W = 512
MAIN_W = 1536
RWKV_HEAD = 64
RWKV_HEADS = 24
GN_EPS = 64e-5
NSA_HEAD_DIM = 128
NSA_Q_HEADS = 12
NSA_KV_GROUPS = 4
NSA_HPG = 3
GATE_W = 36
CMP_BLOCK = 32
CMP_STRIDE = 16
SEL_BLOCK = 64
SEL_TOPK = 16
SEL_LOCAL = 2
WINDOW = 512
REL_BUCKETS = 32
REL_MAX_DIST = 128
D_FF = 5504
NORM_EPS = 1e-6
MASK_NEG = -1e30
FORCE_SCORE = 1e9

LANE = 128
SUBLANE = 8
MXU_DIM = 256
VMEM_LIMIT = 56 * 1024 * 1024

FFN_DOWN_TK = 22 * LANE
HEAD_BLOCK = MXU_DIM // RWKV_HEAD
N_HEAD_BLOCKS = RWKV_HEADS // HEAD_BLOCK
LOWRANK_PAD = MXU_DIM
NSA_TQ = 256
NSA_KT = 256
NSA_BANDS = 3


def _params(*sem):
    return pltpu.CompilerParams(dimension_semantics=sem, vmem_limit_bytes=VMEM_LIMIT)


def _rms(x, g):
    return x * lax.rsqrt(jnp.mean(x * x, axis=-1, keepdims=True) + NORM_EPS) * g


def _sigmoid(x):
    return 1.0 / (1.0 + jnp.exp(-x))


def _softplus(x):
    return jnp.maximum(x, 0.0) + jnp.log(1.0 + jnp.exp(-jnp.abs(x)))


def _gelu_tanh(x):
    return 0.5 * x * (1.0 + jnp.tanh(math.sqrt(2.0 / math.pi) * (x + 0.044715 * x * x * x)))


def _dot(a, b, precision=None):
    return jnp.dot(a, b, preferred_element_type=F32, precision=precision)


def _dot_nt(a, b, precision=None):
    return lax.dot_general(a, b, (((1,), (1,)), ((), ())), preferred_element_type=F32,
                           precision=precision)


def _dot_tn(a, b, precision=None):
    return lax.dot_general(a, b, (((0,), (0,)), ((), ())), preferred_element_type=F32,
                           precision=precision)


BF16_BITS_OF_F32 = 0xFFFF0000


def _split_hi_lo(x):
    bits = lax.bitcast_convert_type(x, jnp.uint32) & jnp.uint32(BF16_BITS_OF_F32)
    hi = lax.bitcast_convert_type(bits, F32)
    return hi, x - hi


def _masked_softmax(s, mask):
    s = jnp.where(mask, s, MASK_NEG)
    p = jnp.exp(s - jnp.max(s, axis=-1, keepdims=True)) * mask.astype(F32)
    den = jnp.sum(p, axis=-1, keepdims=True)
    return p / jnp.where(den > 0, den, 1.0)


def _row_tile(rows, target):
    t = min(rows, target)
    while rows % t:
        t -= SUBLANE
    return t


def _norm_matmul_kernel(x_ref, g_ref, w_ref, o_ref, xn_ref):
    @pl.when(pl.program_id(1) == 0)
    def _():
        xn_ref[...] = _rms(x_ref[...], g_ref[...]).astype(BF16)

    o_ref[...] = _dot(xn_ref[...], w_ref[...]).astype(o_ref.dtype)


def norm_matmul(x, g, w, tn, out_dtype=F32, tm=512):
    rows, d = x.shape
    n = w.shape[1]
    tm = _row_tile(rows, tm)
    assert n % tn == 0
    return pl.pallas_call(
        _norm_matmul_kernel,
        grid=(rows // tm, n // tn),
        in_specs=[pl.BlockSpec((tm, d), lambda i, j: (i, 0)),
                  pl.BlockSpec((1, d), lambda i, j: (0, 0)),
                  pl.BlockSpec((d, tn), lambda i, j: (0, j))],
        out_specs=pl.BlockSpec((tm, tn), lambda i, j: (i, j)),
        out_shape=jax.ShapeDtypeStruct((rows, n), out_dtype),
        scratch_shapes=[pltpu.VMEM((tm, d), BF16)],
        compiler_params=_params("arbitrary", "arbitrary"),
        name="norm_matmul",
    )(x, g.reshape(1, d), w)


def _norm_swiglu_kernel(x_ref, g_ref, wg_ref, wu_ref, o_ref, xn_ref, *, tail):
    j = pl.program_id(1)
    last = pl.num_programs(1) - 1

    @pl.when(j == 0)
    def _():
        xn_ref[...] = _rms(x_ref[...], g_ref[...]).astype(BF16)

    xn = xn_ref[...]
    tn = o_ref.shape[1]
    sub = min(tn, MXU_DIM)

    def tile(up_shift):
        c0 = 0
        while c0 < tn - up_shift:
            w = min(sub, tn - up_shift - c0)
            gate = _dot(xn, wg_ref[0, 0, :, c0:c0 + w].astype(BF16))
            up = _dot(xn, wu_ref[0, 0, :, c0 + up_shift:c0 + up_shift + w].astype(BF16))
            o_ref[:, c0:c0 + w] = (gate * _sigmoid(gate) * up).astype(o_ref.dtype)
            c0 += w
        if up_shift:
            o_ref[:, c0:] = jnp.zeros((o_ref.shape[0], up_shift), o_ref.dtype)

    if tail == tn:
        tile(0)
    else:
        pl.when(j != last)(lambda: tile(0))
        pl.when(j == last)(lambda: tile(tn - tail))


def norm_swiglu(x, g, w_gu, l, i, tn=512, tm=512):
    rows, d = x.shape
    f = w_gu.shape[-1] // 2
    assert f % LANE == 0 and tn % LANE == 0
    tm = _row_tile(rows, tm)
    n_tiles = -(-f // tn)
    tail = f - (n_tiles - 1) * tn
    el = pl.Element

    def up_col(j):
        return pl.multiple_of(jnp.minimum(f + j * tn, 2 * f - tn), LANE)

    return pl.pallas_call(
        functools.partial(_norm_swiglu_kernel, tail=tail),
        grid=(rows // tm, n_tiles),
        in_specs=[pl.BlockSpec((tm, d), lambda r, j: (r, 0)),
                  pl.BlockSpec((1, d), lambda r, j: (0, 0)),
                  pl.BlockSpec((el(1), el(1), el(d), el(tn)), lambda r, j: (l, i, 0, j * tn)),
                  pl.BlockSpec((el(1), el(1), el(d), el(tn)), lambda r, j: (l, i, 0, up_col(j)))],
        out_specs=pl.BlockSpec((tm, tn), lambda r, j: (r, j)),
        out_shape=jax.ShapeDtypeStruct((rows, n_tiles * tn), BF16),
        scratch_shapes=[pltpu.VMEM((tm, d), BF16)],
        compiler_params=_params("arbitrary", "arbitrary"),
        name="norm_swiglu",
    )(x, g.reshape(1, d), w_gu, w_gu)


def _matmul_norm_res_kernel(a_ref, w_ref, g_ref, h_ref, o_ref, acc_ref, *, scale, overlap):
    k = pl.program_id(1)
    last = pl.num_programs(1) - 1

    @pl.when(k == 0)
    def _():
        acc_ref[...] = jnp.zeros_like(acc_ref)

    a = a_ref[...]
    if overlap:
        col = lax.broadcasted_iota(jnp.int32, a.shape, 1)
        a = jnp.where(col < jnp.where(k == last, overlap, 0), jnp.zeros_like(a), a)
    acc_ref[...] += _dot(a, w_ref[0, 0])

    @pl.when(k == last)
    def _():
        o_ref[...] = h_ref[...] + scale * _rms(acc_ref[...], g_ref[...])


def matmul_norm_res(a, w, l, i, g, h, scale, tk=512, tm=512):
    rows = a.shape[0]
    kdim, d = w.shape[-2:]
    assert kdim % LANE == 0 and tk % LANE == 0 and a.shape[1] >= kdim
    tm = _row_tile(rows, tm)
    n_k = -(-kdim // tk)
    overlap = n_k * tk - kdim
    el = pl.Element

    def k_off(k):
        return pl.multiple_of(jnp.minimum(k * tk, kdim - tk), LANE)

    return pl.pallas_call(
        functools.partial(_matmul_norm_res_kernel, scale=scale, overlap=overlap),
        grid=(rows // tm, n_k),
        in_specs=[pl.BlockSpec((el(tm), el(tk)), lambda r, k: (r * tm, k_off(k))),
                  pl.BlockSpec((el(1), el(1), el(tk), el(d)), lambda r, k: (l, i, k_off(k), 0)),
                  pl.BlockSpec((1, d), lambda r, k: (0, 0)),
                  pl.BlockSpec((tm, d), lambda r, k: (r, 0))],
        out_specs=pl.BlockSpec((tm, d), lambda r, k: (r, 0)),
        out_shape=jax.ShapeDtypeStruct((rows, d), F32),
        scratch_shapes=[pltpu.VMEM((tm, d), F32)],
        compiler_params=_params("arbitrary", "arbitrary"),
        name="matmul_norm_res",
    )(a, w, g.reshape(1, d), h)


def _out_proj_kernel(main_ref, mo_ref, w_ref, g_ref, h_ref, o_ref):
    k_main = main_ref.shape[1]
    y = _dot(main_ref[...], w_ref[0, :k_main, :]) + _dot(mo_ref[...], w_ref[0, k_main:, :])
    o_ref[...] = h_ref[...] + _rms(y, g_ref[...])


def out_proj(main, mo, w, l, g, h, tm=512):
    rows, k_main = main.shape
    k_mo = mo.shape[1]
    d = w.shape[-1]
    tm = _row_tile(rows, tm)
    return pl.pallas_call(
        _out_proj_kernel,
        grid=(rows // tm,),
        in_specs=[pl.BlockSpec((tm, k_main), lambda r: (r, 0)),
                  pl.BlockSpec((tm, k_mo), lambda r: (r, 0)),
                  pl.BlockSpec((1, k_main + k_mo, d), lambda r: (l, 0, 0)),
                  pl.BlockSpec((1, d), lambda r: (0, 0)),
                  pl.BlockSpec((tm, d), lambda r: (r, 0))],
        out_specs=pl.BlockSpec((tm, d), lambda r: (r, 0)),
        out_shape=jax.ShapeDtypeStruct((rows, d), F32),
        compiler_params=_params("arbitrary"),
        name="out_proj",
    )(main, mo, w, g.reshape(1, d), h)


RW_R, RW_K, RW_V, RW_QM, RW_W1, RW_V1, RW_A1, RW_G1 = 0, 6, 12, 18, 20, 21, 22, 23
RW_NBLK = 24
RW_STEP_MIXES = ((0,), (0,), (0,), (2,), (2,), (2,), (3,), (3,), (3,), (None,), (1, 3), (4, 5))
RW_FIRST_SPLIT_STEP = 10


def _rwkv_in_kernel(x_ref, sp_ref, g_ref, mu_ref, w_ref, o_ref, last_ref,
                    u_sc, xx_sc, xm_sc, carry_sc, *, tm, t_seq, n_seq, last_tile, last_row):
    ti = pl.program_id(1)
    j = pl.program_id(2)

    @pl.when(j == 0)
    def _():
        @pl.when(ti == 0)
        def _():
            carry_sc[...] = sp_ref[0]

        u = _rms(x_ref[0], g_ref[...])
        rows = lax.broadcasted_iota(jnp.int32, u.shape, 0)
        prev = pltpu.roll(u, 1, axis=0)
        for s in range(n_seq):
            prev = jnp.where(rows == s * t_seq, carry_sc[s:s + 1, :], prev)
        u_sc[...] = u
        xx_sc[...] = prev - u
        if n_seq == 1:
            carry_sc[...] = u[tm - 1:tm, :]

        @pl.when(ti == last_tile)
        def _():
            for s in range(n_seq):
                last_ref[0, s:s + 1, :] = u[s * t_seq + last_row:s * t_seq + last_row + 1, :]

    def set_mix(mix):
        if mix is None:
            xm_sc[...] = u_sc[...].astype(BF16)
        else:
            xm_sc[...] = (u_sc[...] + xx_sc[...] * mu_ref[mix:mix + 1, :]).astype(BF16)

    tn = o_ref.shape[2]
    for step, mixes in enumerate(RW_STEP_MIXES):
        if len(mixes) == 1:
            assert step < RW_FIRST_SPLIT_STEP
            if step == 0 or RW_STEP_MIXES[step - 1] != mixes:
                pl.when(j == step)(functools.partial(set_mix, mixes[0]))
        else:
            assert step >= RW_FIRST_SPLIT_STEP

            @pl.when(j == step)
            def _(mixes=mixes):
                part = tn // len(mixes)
                for q, mix in enumerate(mixes):
                    set_mix(mix)
                    o_ref[0, :, q * part:(q + 1) * part] = _dot(xm_sc[...],
                                                                w_ref[:, q * part:(q + 1) * part])

    @pl.when(j < RW_FIRST_SPLIT_STEP)
    def _():
        o_ref[0] = _dot(xm_sc[...], w_ref[...])


def rwkv_in_proj(h, shift_prev, g, mu, w_cat, t_real, tm=1024):
    b, t, d = h.shape
    n_seq = math.gcd(b, max(1, tm // t))
    bg, tg = b // n_seq, t * n_seq
    tm = _row_tile(tg, tm)
    assert n_seq == 1 or tm == tg
    n_steps = len(RW_STEP_MIXES)
    tn = RW_NBLK * MXU_DIM // n_steps
    last_tile, last_row = ((t_real - 1) // tm, (t_real - 1) % tm) if n_seq == 1 else (0, t_real - 1)
    mu8 = jnp.concatenate([mu, jnp.zeros((SUBLANE - mu.shape[0], d), F32)], axis=0)
    proj, last = pl.pallas_call(
        functools.partial(_rwkv_in_kernel, tm=tm, t_seq=t, n_seq=n_seq, last_tile=last_tile,
                          last_row=last_row),
        grid=(bg, tg // tm, n_steps),
        in_specs=[pl.BlockSpec((1, tm, d), lambda bi, ti, j: (bi, ti, 0)),
                  pl.BlockSpec((1, n_seq, d), lambda bi, ti, j: (bi, 0, 0)),
                  pl.BlockSpec((1, d), lambda bi, ti, j: (0, 0)),
                  pl.BlockSpec((SUBLANE, d), lambda bi, ti, j: (0, 0)),
                  pl.BlockSpec((d, tn), lambda bi, ti, j: (0, j))],
        out_specs=[pl.BlockSpec((1, tm, tn), lambda bi, ti, j: (bi, ti, j)),
                   pl.BlockSpec((1, n_seq, d), lambda bi, ti, j: (bi, 0, 0))],
        out_shape=[jax.ShapeDtypeStruct((bg, tg, n_steps * tn), F32),
                   jax.ShapeDtypeStruct((bg, n_seq, d), F32)],
        scratch_shapes=[pltpu.VMEM((tm, d), F32), pltpu.VMEM((tm, d), F32),
                        pltpu.VMEM((tm, d), BF16), pltpu.VMEM((n_seq, d), F32)],
        compiler_params=_params("arbitrary", "arbitrary", "arbitrary"),
        name="rwkv_in_proj",
    )(h.reshape(bg, tg, d), shift_prev.reshape(bg, n_seq, d), g.reshape(1, d), mu8, w_cat)
    return proj.reshape(b, t, n_steps * tn), last.reshape(b, d)


def _rwkv_scan_kernel(*refs, chunk, n_chunks, has_vres, t_valid, n_hb):
    if has_vres:
        (r_ref, k_ref, v_ref, tw_ref, ta_ref, tg_ref, tv_ref, vf_ref,
         w2_ref, a2_ref, g2_ref, v2_ref, vec_ref, s0_ref, y_ref, sout_ref, s_sc) = refs
    else:
        (r_ref, k_ref, v_ref, tw_ref, ta_ref, tg_ref,
         w2_ref, a2_ref, g2_ref, vec_ref, s0_ref, y_ref, sout_ref, s_sc) = refs
    ti = pl.program_id(2)
    c_len = chunk
    lanes = MXU_DIM
    rows4 = HEAD_BLOCK * c_len

    @pl.when(ti == 0)
    def _():
        s_sc[...] = s0_ref[0]

    li = lax.broadcasted_iota(jnp.int32, (lanes, lanes), 0) // RWKV_HEAD
    lj = lax.broadcasted_iota(jnp.int32, (lanes, lanes), 1) // RWKV_HEAD
    seg = (li == lj).astype(BF16)
    ci = lax.broadcasted_iota(jnp.int32, (c_len, c_len), 0)
    cj = lax.broadcasted_iota(jnp.int32, (c_len, c_len), 1)
    tri_c = (cj <= ci).astype(BF16)

    split2 = _split_hi_lo

    def seg_sums(xs):
        parts = [p for x in xs for p in split2(x)]
        res = _dot(jnp.concatenate(parts, axis=0).astype(BF16), seg)
        return [res[(2 * n) * c_len:(2 * n + 1) * c_len] + res[(2 * n + 1) * c_len:(2 * n + 2) * c_len]
                for n in range(len(xs))]

    lane_head = lax.broadcasted_iota(jnp.int32, (c_len, lanes), 1) // RWKV_HEAD
    step_head = lax.broadcasted_iota(jnp.int32, (c_len, rows4), 1) // c_len
    t_row = lax.broadcasted_iota(jnp.int32, (c_len, 2 * rows4), 0)
    s_col = lax.broadcasted_iota(jnp.int32, (c_len, 2 * rows4), 1) & (c_len - 1)
    strict = s_col < t_row
    incl = s_col <= t_row
    eye_row = ((lax.broadcasted_iota(jnp.int32, (c_len, rows4), 1) & (c_len - 1))
               == lax.broadcasted_iota(jnp.int32, (c_len, rows4), 0)).astype(F32)
    same_head = li == lj
    n_double = int(math.log2(c_len)) - 1

    def stack(x):
        return jnp.concatenate(
            [jnp.where(lane_head == hh, x, 0.0) for hh in range(HEAD_BLOCK)], axis=0)

    def stack_steps(x):
        return jnp.concatenate(
            [jnp.where(step_head == hh, x, 0.0) for hh in range(HEAD_BLOCK)], axis=0)

    def one_chunk(c, carry):
        sl = pl.ds(pl.multiple_of(c * c_len, c_len), c_len)
        tw_act = jnp.tanh(tw_ref[0, sl, :])
        ta_act = ta_ref[0, sl, :]
        tg_act = _sigmoid(tg_ref[0, sl, :])
        tv_act = tv_ref[0, sl, :] if has_vres else None
        if t_valid is not None:
            t_idx = ti * (n_chunks * c_len) + c * c_len + lax.broadcasted_iota(
                jnp.int32, (c_len, lanes), 0)
            live = t_idx < t_valid
        st = [dict() for _ in range(n_hb)]

        def prep(hb):
            e = st[hb]
            hl = slice(hb * lanes, (hb + 1) * lanes)
            w0, a0, v0 = vec_ref[0:1, hl], vec_ref[1:2, hl], vec_ref[2:3, hl]
            kkw, kaw, rk = vec_ref[3:4, hl], vec_ref[4:5, hl], vec_ref[7:8, hl]
            r = r_ref[0, sl, hl]
            k = k_ref[0, sl, hl]
            v = v_ref[0, sl, hl]
            logw = -_softplus(-(w0 + _dot(tw_act, w2_ref[:, hl]))) - 0.5
            dlog = -jnp.exp(logw)
            rate = _sigmoid(a0 + _dot(ta_act, a2_ref[:, hl]))
            e['gate'] = _dot(tg_act, g2_ref[:, hl])
            if has_vres:
                v = v + (vf_ref[0, sl, hl] - v) * _sigmoid(v0 + _dot(tv_act, v2_ref[:, hl]))
            kk = k * kkw
            k = k * (1.0 + (rate - 1.0) * kaw)
            kk_sq, rk_sum = seg_sums([kk * kk, r * k * rk])
            kk = kk / jnp.maximum(jnp.sqrt(kk_sq), 1e-12)
            if t_valid is not None:
                dlog = jnp.where(live, dlog, 0.0)
                kk = jnp.where(live, kk, 0.0)
                k_live = jnp.where(live, k, 0.0)
            else:
                k_live = k
            d_hi, d_lo = split2(dlog)
            cum2 = _dot(tri_c, jnp.concatenate([d_hi, d_lo], axis=1).astype(BF16))
            cum = cum2[:, :lanes] + cum2[:, lanes:]
            inv = jnp.exp(-cum)
            b_row = kk * rate * inv
            k_row = k_live * inv
            e['ar'] = jnp.concatenate([-kk * jnp.exp(cum - dlog), r * jnp.exp(cum)],
                                      axis=0).astype(BF16)
            e['bk_row'] = jnp.concatenate([b_row, k_row], axis=0).astype(BF16)
            e['bk'] = jnp.concatenate([stack(b_row), stack(k_row)],
                                      axis=0).astype(BF16)
            e['v'] = v
            e['v_s'] = stack(v).astype(BF16)
            e['g_end'] = jnp.exp(cum[c_len - 1:c_len, :])
            e['bonus'] = rk_sum * v

        def products(hb):
            e = st[hb]
            e['s_old'] = s_sc[hb]
            big = _dot_nt(e['ar'], jnp.concatenate([e['bk'], e['s_old'].astype(BF16)], axis=0))
            a_bk = jnp.where(strict, big[:c_len, :2 * rows4], 0.0)
            e['r_bk'] = jnp.where(incl, big[c_len:, :2 * rows4], 0.0)
            e['a_s0'], e['r_s0'] = big[:c_len, 2 * rows4:], big[c_len:, 2 * rows4:]
            e['a_k'] = a_bk[:, rows4:]
            lrow = a_bk[:, :rows4]
            e['tinv'] = eye_row + lrow
            e['lpow'] = lrow

        def mx(x):
            return x.astype(BF16) if x.shape[0] % (2 * SUBLANE) == 0 else x

        def square(hb):
            e = st[hb]
            e['lpow'] = _dot(mx(e['lpow']), mx(stack_steps(e['lpow'])))

        def double(hb):
            e = st[hb]
            both = _dot(mx(jnp.concatenate([e['lpow'], e['tinv']], axis=0)), mx(stack_steps(e['lpow'])))
            e['tinv'] = e['tinv'] + both[c_len:]
            e['lpow'] = both[:c_len]

        def solve(hb):
            e = st[hb]
            tinv = e['tinv'] + _dot(mx(e['tinv']), mx(stack_steps(e['lpow'])))
            rhs = e['a_s0'] + _dot(mx(e['a_k']), e['v_s'])
            e['u'] = _dot(mx(tinv), mx(stack(rhs)))

        def outputs(hb):
            e = st[hb]
            uv_s = jnp.concatenate([stack(e['u']).astype(BF16), e['v_s']], axis=0)
            e['y'] = e['r_s0'] + _dot(mx(e['r_bk']), uv_s)
            uv_row = jnp.concatenate([e['u'], e['v']], axis=0).astype(BF16)
            s_sc[hb] = (e['s_old'] + jnp.where(same_head, _dot_tn(uv_row, e['bk_row']), 0.0)) * e['g_end']

        def group_norm(hb):
            e = st[hb]
            hl = slice(hb * lanes, (hb + 1) * lanes)
            y = e['y']
            mean = seg_sums([y])[0] * (1.0 / RWKV_HEAD)
            yc = y - mean
            var = seg_sums([yc * yc])[0] * (1.0 / RWKV_HEAD)
            yn = yc * lax.rsqrt(var + GN_EPS) * vec_ref[5:6, hl] + vec_ref[6:7, hl]
            y_ref[0, sl, hl] = ((yn + e['bonus']) * e['gate']).astype(y_ref.dtype)

        for stage in [prep, products, square] + [double] * (n_double - 1) + [solve, outputs, group_norm]:
            for hb in range(n_hb):
                stage(hb)
        return carry

    lax.fori_loop(0, n_chunks, one_chunk, 0)

    @pl.when(ti == pl.num_programs(2) - 1)
    def _():
        sout_ref[0] = s_sc[...]


def rwkv_scan(proj, v_first_proj, w2, a2, g2, v2, vec, s0_bd, t_real, chunk, tb, n_hb=6):
    b, t, _ = proj.shape
    lanes = MXU_DIM
    tb = min(tb, t)
    assert t % tb == 0 and tb % chunk == 0 and N_HEAD_BLOCKS % n_hb == 0
    has_vres = v_first_proj is not None
    t_valid = None if t_real == t else t_real

    wide = n_hb * lanes

    def col(block0):
        assert block0 % n_hb == 0
        return pl.BlockSpec((1, tb, wide), lambda bi, hg, ti: (bi, ti, block0 // n_hb + hg))

    def fixed(block):
        return pl.BlockSpec((1, tb, lanes), lambda bi, hg, ti: (bi, ti, block))

    def wcol():
        return pl.BlockSpec((LOWRANK_PAD, wide), lambda bi, hg, ti: (0, hg))

    st = pl.BlockSpec((1, n_hb, lanes, lanes), lambda bi, hg, ti: (bi, hg, 0, 0))
    if has_vres:
        in_specs = [col(RW_R), col(RW_K), col(RW_V), fixed(RW_W1), fixed(RW_A1), fixed(RW_G1),
                    fixed(RW_V1), col(RW_V), wcol(), wcol(), wcol(), wcol()]
        args = [proj, proj, proj, proj, proj, proj, proj, v_first_proj, w2, a2, g2, v2]
    else:
        in_specs = [col(RW_R), col(RW_K), col(RW_V), fixed(RW_W1), fixed(RW_A1), fixed(RW_G1),
                    wcol(), wcol(), wcol()]
        args = [proj, proj, proj, proj, proj, proj, w2, a2, g2]
    in_specs += [pl.BlockSpec((SUBLANE, wide), lambda bi, hg, ti: (0, hg)), st]
    args += [vec, s0_bd]
    return pl.pallas_call(
        functools.partial(_rwkv_scan_kernel, chunk=chunk, n_chunks=tb // chunk,
                          has_vres=has_vres, t_valid=t_valid, n_hb=n_hb),
        grid=(b, N_HEAD_BLOCKS // n_hb, t // tb),
        in_specs=in_specs,
        out_specs=[pl.BlockSpec((1, tb, wide), lambda bi, hg, ti: (bi, ti, hg)), st],
        out_shape=[jax.ShapeDtypeStruct((b, t, MAIN_W), BF16),
                   jax.ShapeDtypeStruct(s0_bd.shape, F32)],
        scratch_shapes=[pltpu.VMEM((n_hb, lanes, lanes), F32)],
        compiler_params=_params("arbitrary", "arbitrary", "arbitrary"),
        name="rwkv_scan",
    )(*args)


def _to_block_diag(s):
    b = s.shape[0]
    s = s.reshape(b, N_HEAD_BLOCKS, HEAD_BLOCK, RWKV_HEAD, RWKV_HEAD)
    eye = jnp.eye(HEAD_BLOCK, dtype=s.dtype)
    bd = s[:, :, :, :, None, :] * eye[None, None, :, None, :, None]
    return bd.reshape(b, N_HEAD_BLOCKS, MXU_DIM, MXU_DIM)


def _from_block_diag(bd):
    b = bd.shape[0]
    n = RWKV_HEAD
    x = jnp.stack([bd[:, :, hh * n:(hh + 1) * n, hh * n:(hh + 1) * n] for hh in range(HEAD_BLOCK)],
                  axis=2)
    return x.reshape(b, RWKV_HEADS, RWKV_HEAD, RWKV_HEAD)


def _mem_attn_kernel(q_ref, k_ref, v_ref, o_ref, *, per_head):
    scale = MEM_HEAD_DIM ** -0.5
    for hh in range(MEM_HEADS):
        sl = slice(hh * MEM_HEAD_DIM, (hh + 1) * MEM_HEAD_DIM)
        k = k_ref[0, 0, :, hh, :] if per_head else k_ref[0, :, sl]
        v = v_ref[0, 0, :, hh, :] if per_head else v_ref[0, :, sl]
        s = _dot_nt(q_ref[0, :, sl], k) * scale
        p = jnp.exp(s - jnp.max(s, axis=-1, keepdims=True))
        p = p / jnp.sum(p, axis=-1, keepdims=True)
        o_ref[0, :, sl] = _dot(p, v).astype(o_ref.dtype)


def mem_attend(qsrc, q_block, ksrc, k_block, vsrc, v_block, tq=512):
    b, t, _ = qsrc.shape
    per_head = ksrc.ndim == 5
    m = ksrc.shape[-3] if per_head else ksrc.shape[1]
    tq = _row_tile(t, tq)
    if per_head:
        kv_spec = lambda blk: pl.BlockSpec((1, 1, m, MEM_HEADS, MEM_HEAD_DIM),
                                           lambda bi, ti: (blk, bi, 0, 0, 0))
    else:
        kv_spec = lambda blk: pl.BlockSpec((1, m, MEM_W), lambda bi, ti: (bi, 0, blk))
    return pl.pallas_call(
        functools.partial(_mem_attn_kernel, per_head=per_head),
        grid=(b, t // tq),
        in_specs=[pl.BlockSpec((1, tq, MEM_W), lambda bi, ti: (bi, ti, q_block)),
                  kv_spec(k_block), kv_spec(v_block)],
        out_specs=pl.BlockSpec((1, tq, MEM_W), lambda bi, ti: (bi, ti, 0)),
        out_shape=jax.ShapeDtypeStruct((b, t, MEM_W), BF16),
        compiler_params=_params("arbitrary", "arbitrary"),
        name="mem_attend",
    )(qsrc, ksrc, vsrc)


CMP_PAGE = 128
CMP_CHUNKS = CMP_PAGE // CMP_STRIDE
KV_LANES = NSA_KV_GROUPS * NSA_HEAD_DIM


def _chunk_proj_kernel(pt_ref, *refs, n_pages):
    del pt_ref
    page_refs = refs[:n_pages]
    w_ref, o0_ref, o1_ref, x_sc = refs[n_pages:]
    for kp in range(n_pages):
        for pos in range(CMP_STRIDE):
            x_sc[kp * CMP_CHUNKS:(kp + 1) * CMP_CHUNKS, pos * LANE:(pos + 1) * LANE] = (
                page_refs[kp][pl.ds(pos, CMP_CHUNKS, stride=CMP_STRIDE), :])
    res = _dot(x_sc[...].astype(BF16), w_ref[...])
    o0_ref[0] = res[:, :LANE]
    o1_ref[0] = res[:, LANE:]


def chunk_proj(rows2d, table, w_flat, n_pages, col_block=0):
    b, n_tab = table.shape
    assert n_tab % n_pages == 0

    def page_spec(kp):
        return pl.BlockSpec(
            (CMP_PAGE, LANE),
            lambda bi, pg, g, pt: (pt[bi, pg * n_pages + kp], col_block * NSA_KV_GROUPS + g))

    out_spec = pl.BlockSpec((1, n_pages * CMP_CHUNKS, LANE), lambda bi, pg, g, pt: (bi, pg, g))
    out_shape = jax.ShapeDtypeStruct((b, n_tab * CMP_CHUNKS, KV_LANES), F32)
    grid_spec = pltpu.PrefetchScalarGridSpec(
        num_scalar_prefetch=1,
        grid=(b, n_tab // n_pages, NSA_KV_GROUPS),
        in_specs=[page_spec(kp) for kp in range(n_pages)]
        + [pl.BlockSpec(w_flat.shape, lambda bi, pg, g, pt: (0, 0))],
        out_specs=[out_spec, out_spec],
        scratch_shapes=[pltpu.VMEM((n_pages * CMP_CHUNKS, CMP_STRIDE * LANE), F32)],
    )
    return pl.pallas_call(
        functools.partial(_chunk_proj_kernel, n_pages=n_pages),
        grid_spec=grid_spec,
        out_shape=[out_shape, out_shape],
        compiler_params=_params("arbitrary", "arbitrary", "arbitrary"),
        name="chunk_proj",
    )(table, *([rows2d] * n_pages), w_flat)


def _chunk_proj_pool_kernel(pt_ref, *refs, n_pages):
    del pt_ref
    page_refs = refs[:n_pages]
    w_ref, o0_ref, o1_ref, x_sc = refs[n_pages:]
    for g in range(NSA_KV_GROUPS):
        for kp in range(n_pages):
            for pos in range(CMP_STRIDE):
                x_sc[kp * CMP_CHUNKS:(kp + 1) * CMP_CHUNKS, pos * LANE:(pos + 1) * LANE] = (
                    page_refs[kp][0, pl.ds(pos, CMP_CHUNKS, stride=CMP_STRIDE), g, :])
        res = _dot(x_sc[...].astype(BF16), w_ref[...])
        o0_ref[0, :, g * LANE:(g + 1) * LANE] = res[:, :LANE]
        o1_ref[0, :, g * LANE:(g + 1) * LANE] = res[:, LANE:]


def chunk_proj_pool(pool, table, w_flat, n_pages):
    b, n_tab = table.shape
    assert n_tab % n_pages == 0

    def page_spec(kp):
        return pl.BlockSpec((1, CMP_PAGE, NSA_KV_GROUPS, LANE),
                            lambda bi, pg, pt: (pt[bi, pg * n_pages + kp], 0, 0, 0))

    out_spec = pl.BlockSpec((1, n_pages * CMP_CHUNKS, KV_LANES), lambda bi, pg, pt: (bi, pg, 0))
    out_shape = jax.ShapeDtypeStruct((b, n_tab * CMP_CHUNKS, KV_LANES), F32)
    grid_spec = pltpu.PrefetchScalarGridSpec(
        num_scalar_prefetch=1,
        grid=(b, n_tab // n_pages),
        in_specs=[page_spec(kp) for kp in range(n_pages)]
        + [pl.BlockSpec(w_flat.shape, lambda bi, pg, pt: (0, 0))],
        out_specs=[out_spec, out_spec],
        scratch_shapes=[pltpu.VMEM((n_pages * CMP_CHUNKS, CMP_STRIDE * LANE), F32)],
    )
    return pl.pallas_call(
        functools.partial(_chunk_proj_pool_kernel, n_pages=n_pages),
        grid_spec=grid_spec,
        out_shape=[out_shape, out_shape],
        compiler_params=_params("arbitrary", "arbitrary"),
        name="chunk_proj_pool",
    )(table, *([pool] * n_pages), w_flat)


def _block_mlp_kernel(p0_ref, p1_ref, pe_ref, w1_ref, b1_ref, w2_ref, o_ref):
    const = _dot(pe_ref[...], w1_ref[...])[0:1, :] + b1_ref[...]
    w2 = w2_ref[...]
    for g in range(NSA_KV_GROUPS):
        sl = slice(g * LANE, (g + 1) * LANE)
        hid = const + p0_ref[0, :, sl] + p1_ref[0, :, sl]
        o_ref[0, :, sl] = _dot(_gelu_tanh(hid), w2)


def block_mlp(p0, p1, pe, w1, b1, w2, tb=1024):
    b, nb, _ = p0.shape
    tb = _row_tile(nb, tb)
    pe8 = jnp.concatenate([pe.reshape(1, -1), jnp.zeros((SUBLANE - 1, pe.size), F32)], axis=0)
    spec = pl.BlockSpec((1, tb, KV_LANES), lambda bi, i: (bi, i, 0))
    full = lambda a: pl.BlockSpec(a.shape, lambda bi, i: (0,) * a.ndim)
    b1r = b1.reshape(1, -1)
    return pl.pallas_call(
        _block_mlp_kernel,
        grid=(b, nb // tb),
        in_specs=[spec, spec, full(pe8), full(w1), full(b1r), full(w2)],
        out_specs=spec,
        out_shape=jax.ShapeDtypeStruct((b, nb, KV_LANES), F32),
        compiler_params=_params("arbitrary", "arbitrary"),
        name="block_mlp",
    )(p0, p1, pe8, w1, b1r, w2)


def _w1_flat(w1):
    r = CMP_BLOCK // CMP_STRIDE
    e = w1.shape[1]
    w = w1.reshape(r, CMP_STRIDE, NSA_HEAD_DIM, e)
    return jnp.transpose(w, (1, 2, 0, 3)).reshape(CMP_STRIDE * NSA_HEAD_DIM, r * e).astype(BF16)


def compress(parts, nb, pe, w1, b1, w2):
    return block_mlp(parts[0][:, :nb], parts[1][:, 1:nb + 1], pe, w1, b1, w2)


def _rel_bucket(dist):
    n = jnp.maximum(dist, 0)
    max_exact = REL_BUCKETS // 2
    nf = jnp.maximum(n, 1).astype(F32)
    large = max_exact + (jnp.log(nf / max_exact) / math.log(REL_MAX_DIST / max_exact)
                         * (REL_BUCKETS - max_exact)).astype(jnp.int32)
    return jnp.where(n < max_exact, n, jnp.minimum(large, REL_BUCKETS - 1))


def _bias_lookup(rel_bias, dist):
    tab = rel_bias.astype(F32).reshape(REL_BUCKETS, NSA_KV_GROUPS, NSA_HPG)
    onehot = jax.nn.one_hot(_rel_bucket(dist), REL_BUCKETS, dtype=F32)
    out = jnp.dot(onehot, tab.reshape(REL_BUCKETS, -1), precision=HI)
    out = out.reshape(dist.shape + (NSA_KV_GROUPS, NSA_HPG))
    return jnp.moveaxis(out, (-2, -1), (0, 1))


def _cmp_to_sel(nb_pad, nb, nsb_pad, nsb):
    i = jnp.arange(nb_pad)[:, None]
    j = jnp.arange(nsb_pad)[None, :]
    start = i * CMP_STRIDE
    hit = (start < (j + 1) * SEL_BLOCK) & (start + CMP_BLOCK > j * SEL_BLOCK) & (i < nb) & (j < nsb)
    return hit.astype(F32)


def _nsa_prompt_kernel(q_ref, gt_ref, kc_ref, vc_ref, bc_ref, band_ref, ks_ref, vs_ref,
                       kw_ref, vw_ref, m_ref, o_ref, s_sc, *, nb, nsb, n_g):
    tq = NSA_TQ
    rows = NSA_HPG * tq
    qw = NSA_HPG * LANE
    i = pl.program_id(2)
    scale = NSA_HEAD_DIM ** -0.5
    row_q = lax.broadcasted_iota(jnp.int32, (rows, LANE), 0) & (tq - 1)
    lane = lax.broadcasted_iota(jnp.int32, (rows, LANE), 1)
    t_pos = i * tq + row_q
    kt_w = NSA_KT
    tiles_per_kt = kt_w // tq
    rel = (i * tq + (lax.broadcasted_iota(jnp.int32, (rows, kt_w), 0) & (tq - 1))
           - lax.broadcasted_iota(jnp.int32, (rows, kt_w), 1))
    m_c = (t_pos - (lane * CMP_STRIDE + (CMP_BLOCK - 1)) >= 0) & (lane < nb)
    jb = lax.broadcasted_iota(jnp.int32, (nsb, tq), 0)
    cur = (i * tq + lax.broadcasted_iota(jnp.int32, (nsb, tq), 1)) // SEL_BLOCK
    valid = jb <= cur
    forced = valid & ((jb == 0) | (jb > cur - SEL_LOCAL))
    blk_row = lax.broadcasted_iota(jnp.int32, (nsb, kt_w), 0)
    blk_lane = lax.broadcasted_iota(jnp.int32, (nsb, kt_w), 1) // SEL_BLOCK
    sel_map = m_ref[...].astype(BF16)
    n_win_tiles = WINDOW // kt_w + 1
    n_kt = i // tiles_per_kt + 1
    st = [dict() for _ in range(n_g)]
    gl = lambda g: slice(g * LANE, (g + 1) * LANE)

    def masked_scores(g, k_ref, kt, mask_of):
        ksl = pl.ds(pl.multiple_of(kt * kt_w, kt_w), kt_w)
        band = band_ref[g, jnp.clip(i - kt * tiles_per_kt, 0, NSA_BANDS - 1)].reshape(rows, kt_w)
        s = _dot_nt(st[g]['q3'], k_ref[0, ksl, gl(g)].astype(BF16)) + band
        return jnp.where(mask_of(kt * kt_w), s, MASK_NEG), ksl

    def finish(m_fin, l_fin, acc):
        return jnp.where(m_fin > MASK_NEG, acc / jnp.where(l_fin > 0, l_fin, 1.0), 0.0)

    def window_scores(g):
        e = st[g]
        e['q3'] = (jnp.concatenate([q_ref[0, :, g * qw + hh * LANE:g * qw + (hh + 1) * LANE]
                                    for hh in range(NSA_HPG)], axis=0) * scale).astype(BF16)
        e['win'] = []
        m_w = jnp.full((rows, 1), MASK_NEG, F32)
        for jw in range(n_win_tiles):
            kt_raw = i // tiles_per_kt - (n_win_tiles - 1) + jw
            reach = jnp.where(kt_raw >= 0, WINDOW, 0)
            s_w, ksl_w = masked_scores(g, kw_ref, jnp.maximum(kt_raw, 0),
                                       lambda base, reach=reach: (rel >= base) & (rel < base + reach))
            e['win'].append((s_w, ksl_w))
            m_w = jnp.maximum(m_w, jnp.max(s_w, axis=-1, keepdims=True))
        e['m_w'] = m_w

    def compressed(g):
        e = st[g]
        s_c = _dot_nt(e['q3'], kc_ref[0, :, gl(g)].astype(BF16)) + bc_ref[g].reshape(rows, LANE)
        e['p_c'] = _masked_softmax(s_c, m_c)
        e['o_c'] = _dot(e['p_c'], vc_ref[0, :, gl(g)])

    def select(g):
        e = st[g]
        p_hi, p_lo = _split_hi_lo(e['p_c'])
        imp3 = _dot_nt(sel_map, p_hi.astype(BF16)) + _dot_nt(sel_map, p_lo.astype(BF16))
        imp = imp3[:, 0:tq]
        for hh in range(1, NSA_HPG):
            imp = imp + imp3[:, hh * tq:(hh + 1) * tq]
        imp = jnp.where(forced, FORCE_SCORE, jnp.where(valid, imp, -FORCE_SCORE))
        rank = jnp.zeros((nsb, tq), jnp.int32)
        for jp in range(nsb):
            other = imp[jp:jp + 1, :]
            rank = rank + ((other > imp) | ((other == imp) & (jb > jp))).astype(jnp.int32)
        sel = (rank < min(SEL_TOPK, nsb)).astype(F32).T
        e['sel3'] = jnp.concatenate([sel] * NSA_HPG, axis=0).astype(BF16)

    def window_values(g):
        e = st[g]
        l_w = jnp.zeros((rows, 1), F32)
        acc_w = jnp.zeros((rows, LANE), F32)
        for s_w, ksl_w in e['win']:
            p_w = jnp.exp(s_w - e['m_w'])
            l_w = l_w + jnp.sum(p_w, axis=-1, keepdims=True)
            acc_w = acc_w + _dot(p_w, vw_ref[0, ksl_w, gl(g)])
        e['o_w'] = finish(e['m_w'], l_w, acc_w)

    for stage in (window_scores, compressed, select, window_values):
        for g in range(n_g):
            stage(g)

    def scores(kt, m_run):
        expand = (blk_row == kt * (kt_w // SEL_BLOCK) + blk_lane).astype(BF16)
        out = []
        for g in range(n_g):
            chosen = _dot(st[g]['sel3'], expand) > 0.5
            s, _ = masked_scores(g, ks_ref, kt, lambda base, chosen=chosen: chosen & (rel >= base))
            s_sc[g, kt] = s
            out.append(jnp.maximum(m_run[g], jnp.max(s, axis=-1, keepdims=True)))
        return tuple(out)

    m_s = lax.fori_loop(0, n_kt, scores, tuple(jnp.full((rows, 1), MASK_NEG, F32) for _ in range(n_g)))

    def accum(kt, carry):
        ksl = pl.ds(pl.multiple_of(kt * kt_w, kt_w), kt_w)
        out = []
        for g in range(n_g):
            l_run, acc = carry[g]
            p = jnp.exp(s_sc[g, kt] - m_s[g])
            out.append((l_run + jnp.sum(p, axis=-1, keepdims=True), acc + _dot(p, vs_ref[0, ksl, gl(g)])))
        return tuple(out)

    fin = lax.fori_loop(0, n_kt, accum, tuple((jnp.zeros((rows, 1), F32), jnp.zeros((rows, LANE), F32))
                                              for _ in range(n_g)))
    for g in range(n_g):
        e = st[g]
        o_s = finish(m_s[g], fin[g][0], fin[g][1])
        gt = _sigmoid(gt_ref[0, g])
        gcol = lambda br, gt=gt: jnp.concatenate(
            [gt[:, br * NSA_HPG + hh:br * NSA_HPG + hh + 1] for hh in range(NSA_HPG)], axis=0)
        o = gcol(0) * e['o_c'] + gcol(1) * o_s + gcol(2) * e['o_w']
        for hh in range(NSA_HPG):
            o_ref[0, :, g * qw + hh * LANE:g * qw + (hh + 1) * LANE] = (
                o[hh * tq:(hh + 1) * tq].astype(o_ref.dtype))


def nsa_prompt(proj, gates_t, kc, vc, bias_c, band, side, sel_map, nb, nsb, n_g=2):
    b, t, _ = proj.shape
    tq = NSA_TQ
    qw = n_g * NSA_HPG * LANE
    kvw = n_g * LANE
    n_gp = NSA_KV_GROUPS // n_g
    kv = lambda off: pl.BlockSpec((1, t, kvw), lambda bi, gp, i: (bi, 0, off * n_gp + gp))
    return pl.pallas_call(
        functools.partial(_nsa_prompt_kernel, nb=nb, nsb=nsb, n_g=n_g),
        grid=(b, n_gp, t // tq),
        in_specs=[pl.BlockSpec((1, tq, qw), lambda bi, gp, i: (bi, i, gp)),
                  pl.BlockSpec((1, n_g, tq, NSA_HPG * 3), lambda bi, gp, i: (bi, gp, i, 0)),
                  pl.BlockSpec((1, LANE, kvw), lambda bi, gp, i: (bi, 0, gp)),
                  pl.BlockSpec((1, LANE, kvw), lambda bi, gp, i: (bi, 0, gp)),
                  pl.BlockSpec((n_g, NSA_HPG, tq, LANE), lambda bi, gp, i: (gp, 0, i, 0)),
                  pl.BlockSpec((n_g, NSA_BANDS, NSA_HPG, tq, NSA_KT), lambda bi, gp, i: (gp, 0, 0, 0, 0)),
                  kv(2), kv(3), kv(4), kv(5),
                  pl.BlockSpec(sel_map.shape, lambda bi, gp, i: (0, 0))],
        out_specs=pl.BlockSpec((1, tq, qw), lambda bi, gp, i: (bi, i, gp)),
        out_shape=jax.ShapeDtypeStruct((b, t, MAIN_W), BF16),
        scratch_shapes=[pltpu.VMEM((n_g, t // NSA_KT, NSA_HPG * tq, NSA_KT), F32)],
        compiler_params=_params("arbitrary", "arbitrary", "arbitrary"),
        name="nsa_prompt",
    )(proj, gates_t, kc, vc, bias_c, band, side, side, side, side, sel_map)


def _nsa_dec_cmp_kernel(q_ref, kc_ref, vc_ref, bc_ref, m_ref, oc_ref, idx_ref, *, nb, nsb, t_pos):
    scale = NSA_HEAD_DIM ** -0.5
    nbp = kc_ref.shape[1]
    nsp = m_ref.shape[1]
    n_sel = min(SEL_TOPK, nsb)
    lane_b = lax.broadcasted_iota(jnp.int32, (SUBLANE, nbp), 1)
    m_c = (t_pos - (lane_b * CMP_STRIDE + (CMP_BLOCK - 1)) >= 0) & (lane_b < nb)
    row_s = lax.broadcasted_iota(jnp.int32, (SUBLANE, nsp), 0)
    jb = lax.broadcasted_iota(jnp.int32, (1, nsp), 1)
    cur = t_pos // SEL_BLOCK
    valid = jb <= cur
    forced = valid & ((jb == 0) | (jb > cur - SEL_LOCAL))
    out_lane = lax.broadcasted_iota(jnp.int32, (1, LANE), 1)
    idx_rows = []
    for g in range(NSA_KV_GROUPS):
        q3 = jnp.concatenate(
            [q_ref[0, 0:1, (g * NSA_HPG + hh) * LANE:(g * NSA_HPG + hh + 1) * LANE]
             for hh in range(NSA_HPG)] + [jnp.zeros((SUBLANE - NSA_HPG, LANE), F32)], axis=0)
        sl = slice(g * LANE, (g + 1) * LANE)
        s_c = _dot_nt(q3, kc_ref[0, :, sl]) * scale + bc_ref[g]
        p_c = _masked_softmax(s_c, m_c)
        oc_ref[0, g] = _dot(p_c, vc_ref[0, :, sl])
        imp8 = jnp.where(row_s < NSA_HPG, _dot(p_c, m_ref[...], HI), 0.0)
        imp = jnp.sum(imp8, axis=0, keepdims=True)
        imp = jnp.where(forced, FORCE_SCORE, jnp.where(valid, imp, -FORCE_SCORE))
        imp = jnp.where(jb < nsb, imp, -jnp.inf)
        jbf = jb.astype(F32)
        picks = jnp.zeros((1, LANE), F32)
        for kk in range(n_sel):
            best = jnp.max(imp, axis=-1, keepdims=True)
            arg = jnp.min(jnp.where(imp == best, jbf, float(nsp)), axis=-1, keepdims=True)
            picks = jnp.where(out_lane == kk, arg, picks)
            imp = jnp.where(jbf == arg, -jnp.inf, imp)
        idx_rows.append(picks.astype(jnp.int32))
    idx_rows.append(jnp.zeros((SUBLANE - NSA_KV_GROUPS, LANE), jnp.int32))
    idx_ref[0] = jnp.concatenate(idx_rows, axis=0)


def nsa_dec_cmp(proj, kc, vc, bias_c, sel_map, nb, nsb, t_pos):
    b, tp, _ = proj.shape
    nbp = kc.shape[1]
    return pl.pallas_call(
        functools.partial(_nsa_dec_cmp_kernel, nb=nb, nsb=nsb, t_pos=t_pos),
        grid=(b,),
        in_specs=[pl.BlockSpec((1, tp, MAIN_W), lambda bi: (bi, 0, 0)),
                  pl.BlockSpec((1, nbp, KV_LANES), lambda bi: (bi, 0, 0)),
                  pl.BlockSpec((1, nbp, KV_LANES), lambda bi: (bi, 0, 0)),
                  pl.BlockSpec(bias_c.shape, lambda bi: (0, 0, 0)),
                  pl.BlockSpec(sel_map.shape, lambda bi: (0, 0))],
        out_specs=[pl.BlockSpec((1, NSA_KV_GROUPS, SUBLANE, LANE), lambda bi: (bi, 0, 0, 0)),
                   pl.BlockSpec((1, SUBLANE, LANE), lambda bi: (bi, 0, 0))],
        out_shape=[jax.ShapeDtypeStruct((b, NSA_KV_GROUPS, SUBLANE, LANE), F32),
                   jax.ShapeDtypeStruct((b, SUBLANE, LANE), jnp.int32)],
        compiler_params=_params("arbitrary"),
        name="nsa_dec_cmp",
    )(proj, kc, vc, bias_c, sel_map)


def _nsa_dec_sel_kernel(phys_ref, isnew_ref, *refs, n_sel, n_win, t_pos, win_start):
    del phys_ref
    kb_refs = refs[:n_sel]
    vb_refs = refs[n_sel:2 * n_sel]
    (q_ref, new_ref, ds_ref, bs_ref, wk_ref, wv_ref, bw_ref, oc_ref, gt_ref, o_ref) = refs[2 * n_sel:]
    bi = pl.program_id(0)
    g = pl.program_id(1)
    scale = NSA_HEAD_DIM ** -0.5
    q3 = q_ref[0, 0]
    new_rows = new_ref[0, 0]
    pad_blk = jnp.zeros((SEL_BLOCK - 1, LANE), F32)
    new_k = jnp.concatenate([new_rows[0:1], pad_blk], axis=0)
    new_v = jnp.concatenate([new_rows[1:2], pad_blk], axis=0)
    ks, vs = [], []
    for kk in range(n_sel):
        fresh = isnew_ref[bi, g, kk] > 0
        ks.append(jnp.where(fresh, new_k, kb_refs[kk][0, :, g, :]))
        vs.append(jnp.where(fresh, new_v, vb_refs[kk][0, :, g, :]))
    ks = jnp.concatenate(ks, axis=0)
    vs = jnp.concatenate(vs, axis=0)
    s_s = _dot_nt(q3, ks) * scale + bs_ref[0, 0]
    p_s = _masked_softmax(s_s, ds_ref[0, 0] >= 0)
    o_s = _dot(p_s, vs)

    pad_w = jnp.zeros((SUBLANE - 1, LANE), F32)
    kw = jnp.concatenate([wk_ref[0, :, g, :], new_rows[2:3], pad_w], axis=0)
    vw = jnp.concatenate([wv_ref[0, :, g, :], new_rows[3:4], pad_w], axis=0)
    nw = kw.shape[0]
    pos = win_start + lax.broadcasted_iota(jnp.int32, (SUBLANE, nw), 1)
    d_w = t_pos - pos
    m_w = (d_w >= 0) & (d_w < WINDOW) & (pos >= 0) & (pos - win_start < n_win)
    s_w = _dot_nt(q3, kw) * scale + bw_ref[0]
    p_w = _masked_softmax(s_w, m_w)
    o_w = _dot(p_w, vw)

    gt = _sigmoid(gt_ref[0, 0])
    o_ref[0, 0] = gt[:, 0:1] * oc_ref[0, 0] + gt[:, 1:2] * o_s + gt[:, 2:3] * o_w


def nsa_dec_sel(phys, isnew, pool_k, pool_v, q8, new_rows, d_s, bias_s, win_k, win_v, bias_w,
                o_c, gates8, n_win, t_pos, win_start):
    b = q8.shape[0]
    n_sel = phys.shape[-1]
    n_keys = n_sel * SEL_BLOCK
    nw = win_k.shape[1]
    nwp = nw + SUBLANE

    def blk_spec(kk):
        return pl.BlockSpec((1, SEL_BLOCK, NSA_KV_GROUPS, LANE),
                            lambda bi, g, ph, nf: (ph[bi, g, kk], 0, 0, 0))

    per_bg = lambda *shape: pl.BlockSpec((1, 1) + shape, lambda bi, g, ph, nf: (bi, g) + (0,) * len(shape))
    grid_spec = pltpu.PrefetchScalarGridSpec(
        num_scalar_prefetch=2,
        grid=(b, NSA_KV_GROUPS),
        in_specs=[blk_spec(kk) for kk in range(n_sel)] + [blk_spec(kk) for kk in range(n_sel)]
        + [per_bg(SUBLANE, LANE), per_bg(SUBLANE, LANE), per_bg(1, n_keys), per_bg(SUBLANE, n_keys),
           pl.BlockSpec((1, nw, NSA_KV_GROUPS, LANE), lambda bi, g, ph, nf: (bi, 0, 0, 0)),
           pl.BlockSpec((1, nw, NSA_KV_GROUPS, LANE), lambda bi, g, ph, nf: (bi, 0, 0, 0)),
           pl.BlockSpec((1, SUBLANE, nwp), lambda bi, g, ph, nf: (g, 0, 0)),
           per_bg(SUBLANE, LANE), per_bg(SUBLANE, SUBLANE)],
        out_specs=per_bg(SUBLANE, LANE),
    )
    return pl.pallas_call(
        functools.partial(_nsa_dec_sel_kernel, n_sel=n_sel, n_win=nw + 1, t_pos=t_pos,
                          win_start=win_start),
        grid_spec=grid_spec,
        out_shape=jax.ShapeDtypeStruct((b, NSA_KV_GROUPS, SUBLANE, LANE), F32),
        compiler_params=_params("arbitrary", "arbitrary"),
        name="nsa_dec_sel",
    )(phys, isnew, *([pool_k] * n_sel), *([pool_v] * n_sel), q8, new_rows, d_s, bias_s,
      win_k, win_v, bias_w, o_c, gates8)


def _pad_cols(w, n):
    return jnp.pad(w, ((0, 0), (0, n - w.shape[1])))


def _pad_rows(w, n):
    return jnp.pad(w, ((0, n - w.shape[0]), (0, 0)))


def _prep_weights(P):
    W = {}
    W['w_gu'] = P['ffn_gu']
    W['w_d'] = P['ffn_d'].astype(BF16)
    W['w_out'] = P['w_out'].astype(BF16)
    W['rw_in'], W['rw_w2'], W['rw_a2'], W['rw_g2'], W['rw_v2'], W['rw_vec'] = [], [], [], [], [], []
    lp = LOWRANK_PAD
    for l in range(N_A):
        w_in = P['w_in_a'][l]
        v1 = P['rw_v1'][l - 1] if l > 0 else jnp.zeros((D_MODEL, lp), F32)
        W['rw_in'].append(jnp.concatenate([
            w_in, _pad_cols(P['rw_w1'][l], lp), _pad_cols(v1, lp), _pad_cols(P['rw_a1'][l], lp),
            _pad_cols(P['rw_g1'][l], lp)], axis=1).astype(BF16))
        W['rw_w2'].append(_pad_rows(P['rw_w2'][l], lp))
        W['rw_a2'].append(_pad_rows(P['rw_a2'][l], lp))
        W['rw_g2'].append(_pad_rows(P['rw_g2'][l], lp))
        W['rw_v2'].append(_pad_rows(P['rw_v2'][l - 1], lp) if l > 0 else None)
        v0 = P['rw_v0'][l - 1] if l > 0 else jnp.zeros((MAIN_W,), F32)
        W['rw_vec'].append(jnp.stack([P['rw_w0'][l], P['rw_a0'][l], v0, P['rw_kk'][l], P['rw_ka'][l],
                                      P['rw_lnw'][l], P['rw_lnb'][l], P['rw_rk'][l].reshape(-1)]))
    W['w_in_b'] = []
    for l in range(DEPTH - N_A):
        w = P['w_in_b'][l]
        W['w_in_b'].append(jnp.concatenate([
            w[:, :MAIN_W], w[:, MAIN_W + GATE_W:], _pad_cols(w[:, MAIN_W:MAIN_W + GATE_W], MXU_DIM)],
            axis=1).astype(BF16))
    W['w_kv'] = P['w_kv'].astype(BF16)
    W['w_mem_kv'] = [P['w_mem_kv'][l].astype(BF16) for l in range(DEPTH)]
    W['cmp_w1_flat'] = [_w1_flat(P['cmp_w1'][c]) for c in range(2)]
    return W


B_QM_BLOCK = MAIN_W // MEM_W
B_GATE_OFF = MAIN_W + MEM_W


def _ffn(h2, norms_l, first, W, l, i):
    hid = norm_swiglu(h2, norms_l[first], W['w_gu'], l, i, tn=512, tm=1024)
    return matmul_norm_res(hid, W['w_d'], l, i, norms_l[first + 1], h2, 0.5, tk=FFN_DOWN_TK)


def _trunk(x, t_real, P, W, shift0, wkv0_bd, mem_src, make_side, attend, chunk, tb):
    b, t, d = x.shape
    h2 = x.reshape(b * t, d)
    shifts, states = [], []
    v_first_proj, ctx, side_state = None, None, None
    for l in range(DEPTH):
        n = P['norms'][l]
        h2 = _ffn(h2, n, 0, W, l, 0)
        if l < N_A:
            proj, last = rwkv_in_proj(h2.reshape(b, t, d), shift0[l], n[2], P['rw_mu'][l],
                                      W['rw_in'][l], t_real)
            main, s_bd = rwkv_scan(proj, v_first_proj if l > 0 else None, W['rw_w2'][l], W['rw_a2'][l],
                                   W['rw_g2'][l], W['rw_v2'][l], W['rw_vec'][l], wkv0_bd[l],
                                   t_real, chunk, tb)
            if l == 0:
                v_first_proj = proj
            shifts.append(last)
            states.append(_from_block_diag(s_bd))
            q_src, q_block = proj, RW_QM * MXU_DIM // MEM_W
        else:
            proj = norm_matmul(h2, n[2], W['w_in_b'][l - N_A], tn=768, tm=1024).reshape(b, t, -1)
            main = attend(proj, ctx)
            q_src, q_block = proj, B_QM_BLOCK
        mk, kb, mv, vb = mem_src(l)
        mo = mem_attend(q_src, q_block, mk, kb, mv, vb)
        h2 = out_proj(main.reshape(b * t, MAIN_W), mo.reshape(b * t, MEM_W), W['w_out'], l, n[3], h2)
        h2 = _ffn(h2, n, 4, W, l, 1)
        if l == N_A - 1:
            side = norm_matmul(h2, P['kv_norm'], W['w_kv'], tn=768, tm=1024).reshape(b, t, -1)
            ctx, side_state = make_side(side)
    return h2.reshape(b, t, d), jnp.stack(shifts), jnp.stack(states), side_state


def kernel(x_prompt, x_sample, mem_prompt, state_wkv, state_shift, cache_mem_k, cache_mem_v,
           cache_cmp_k, cache_cmp_v, cache_slc_k, cache_slc_v, cache_win_k, cache_win_v, page_table,
           norms, ffn_gu, ffn_d, w_in_a, w_in_b, w_out, mem_norm, w_mem_kv, kv_norm, w_kv,
           cmp_pe, cmp_w1, cmp_b1, cmp_w2, rel_bias,
           rw_mu, rw_w0, rw_w1, rw_w2, rw_a0, rw_a1, rw_a2, rw_g1, rw_g2, rw_v0, rw_v1, rw_v2,
           rw_kk, rw_ka, rw_rk, rw_lnw, rw_lnb):
    P = dict(norms=norms, ffn_gu=ffn_gu, ffn_d=ffn_d, w_in_a=w_in_a, w_in_b=w_in_b, w_out=w_out,
             kv_norm=kv_norm, w_kv=w_kv, w_mem_kv=w_mem_kv, cmp_w1=cmp_w1, rw_mu=rw_mu, rw_w0=rw_w0,
             rw_w1=rw_w1, rw_w2=rw_w2, rw_a0=rw_a0, rw_a1=rw_a1, rw_a2=rw_a2, rw_g1=rw_g1,
             rw_g2=rw_g2, rw_v0=rw_v0, rw_v1=rw_v1, rw_v2=rw_v2, rw_kk=rw_kk, rw_ka=rw_ka,
             rw_rk=rw_rk, rw_lnw=rw_lnw, rw_lnb=rw_lnb)
    W = _prep_weights(P)
    G, dh = NSA_KV_GROUPS, NSA_HEAD_DIM
    split_side = lambda side, bx, t: [side[:, :t, c * KV_LANES:(c + 1) * KV_LANES].reshape(bx, t, G, dh)
                                      for c in range(6)]

    def cmp_mlp(parts, nb, c):
        return compress(parts, nb, cmp_pe[c], cmp_w1[c], cmp_b1[c], cmp_w2[c])

    bp, tp, d = x_prompt.shape
    n_mem = mem_prompt.shape[1]
    mem2 = mem_prompt.reshape(bp * n_mem, d)
    p_mkv = [norm_matmul(mem2, mem_norm[l], W['w_mem_kv'][l], tn=512).reshape(bp, n_mem, 2 * MEM_W)
             for l in range(DEPTH)]
    p_mem_k = jnp.stack([m[..., :MEM_W].reshape(bp, n_mem, MEM_HEADS, MEM_HEAD_DIM) for m in p_mkv])
    p_mem_v = jnp.stack([m[..., MEM_W:].reshape(bp, n_mem, MEM_HEADS, MEM_HEAD_DIM) for m in p_mkv])

    nb_p = tp // CMP_STRIDE - 1
    nsb_p = tp // SEL_BLOCK
    nq_tiles = tp // NSA_TQ
    t_all = jnp.arange(tp)
    c_end = jnp.arange(LANE) * CMP_STRIDE + (CMP_BLOCK - 1)
    assert nb_p <= LANE
    bias_c_p = _bias_lookup(rel_bias, t_all[:, None] - c_end[None, :])
    ii = jnp.arange(NSA_TQ)
    cc = jnp.arange(NSA_KT)
    band_p = jnp.stack([_bias_lookup(rel_bias, dd * NSA_TQ + ii[:, None] - cc[None, :])
                        for dd in range(NSA_BANDS)], axis=1)
    assert (NSA_BANDS - 1) * NSA_TQ - (NSA_KT - 1) >= REL_MAX_DIST and tp % NSA_KT == 0
    sel_map_p = _cmp_to_sel(LANE, nb_p, nsb_p, nsb_p).T
    ident = jnp.arange(bp * tp // CMP_PAGE, dtype=jnp.int32).reshape(bp, tp // CMP_PAGE)

    def prompt_side(side):
        rows = side.reshape(bp * tp, -1)
        kc = cmp_mlp(chunk_proj(rows, ident, W['cmp_w1_flat'][0], tp // CMP_PAGE, 0), nb_p, 0)
        vc = cmp_mlp(chunk_proj(rows, ident, W['cmp_w1_flat'][1], tp // CMP_PAGE, 1), nb_p, 1)
        padb = ((0, 0), (0, LANE - nb_p), (0, 0))
        wb = min(WINDOW, tp)
        kc_r, vc_r, ks, vs, kw, vw = split_side(side, bp, tp)
        return ((jnp.pad(kc, padb), jnp.pad(vc, padb), side),
                (kc_r, vc_r, ks, vs, kw[:, tp - wb:], vw[:, tp - wb:]))

    def prompt_attend(proj, ctx):
        kc, vc, side = ctx
        gates = proj[..., B_GATE_OFF:B_GATE_OFF + GATE_W].reshape(bp, tp, 3, G, NSA_HPG)
        gates_t = jnp.transpose(gates, (0, 3, 1, 2, 4)).reshape(bp, G, tp, 3 * NSA_HPG)
        return nsa_prompt(proj, gates_t, kc, vc, bias_c_p, band_p, side, sel_map_p, nb_p, nsb_p)

    zeros_shift = jnp.zeros((N_A, bp, d), F32)
    zeros_state = jnp.zeros((N_A, bp, N_HEAD_BLOCKS, MXU_DIM, MXU_DIM), F32)
    y_prompt, p_shift, p_wkv, p_side = _trunk(
        x_prompt, tp, P, W, zeros_shift, zeros_state,
        lambda l: (p_mkv[l], 0, p_mkv[l], 1), prompt_side, prompt_attend, chunk=64, tb=256)
    p_cmp_k, p_cmp_v, p_slc_k, p_slc_v, p_win_k, p_win_v = p_side

    bd, s_new, _ = x_sample.shape
    assert s_new == 1
    ts = SUBLANE
    xs = jnp.pad(x_sample, ((0, 0), (0, ts - s_new), (0, 0)))
    past_len = page_table.shape[1] * CMP_PAGE
    n_past_blk = past_len // SEL_BLOCK
    blk_per_page = CMP_PAGE // SEL_BLOCK
    nsb_s = n_past_blk + 1
    t_pos = past_len
    nc_s = -(-(past_len + s_new) // CMP_STRIDE)
    nb_s = nc_s - 1
    wb_s = cache_win_k.shape[1]
    win_start = past_len - wb_s
    nsp = -(-nsb_s // LANE) * LANE
    sel_map_s = _cmp_to_sel(nb_s, nb_s, nsp, nsb_s)
    c_end_s = jnp.arange(nb_s) * CMP_STRIDE + (CMP_BLOCK - 1)
    bias_c_s = _bias_lookup(rel_bias, t_pos - c_end_s)
    bias_c_s = jnp.pad(bias_c_s, ((0, 0), (0, SUBLANE - NSA_HPG), (0, 0)))
    nwp = wb_s + SUBLANE
    bias_w_s = _bias_lookup(rel_bias, t_pos - (win_start + jnp.arange(nwp)))
    bias_w_s = jnp.pad(bias_w_s, ((0, 0), (0, SUBLANE - NSA_HPG), (0, 0)))
    ident_s = jnp.arange(bd, dtype=jnp.int32).reshape(bd, 1)
    pages_per_step = math.gcd(16, page_table.shape[1])

    def sample_side(side):
        new = side[:, :s_new]
        parts = []
        for c, pool in enumerate((cache_cmp_k, cache_cmp_v)):
            past = chunk_proj_pool(pool, page_table, W['cmp_w1_flat'][c], pages_per_step)
            fresh_page = jnp.pad(new[:, :, c * KV_LANES:(c + 1) * KV_LANES],
                                 ((0, 0), (0, CMP_PAGE - s_new), (0, 0))).reshape(bd, CMP_PAGE, G, dh)
            fresh = chunk_proj_pool(fresh_page, ident_s, W['cmp_w1_flat'][c], 1)
            n_fresh = nc_s - past[0].shape[1]
            parts.append([jnp.concatenate([p, f[:, :n_fresh]], axis=1) for p, f in zip(past, fresh)])
        kc = cmp_mlp(parts[0], nb_s, 0)
        vc = cmp_mlp(parts[1], nb_s, 1)
        kc_n, vc_n, ks_n, vs_n, kw_n, vw_n = split_side(side, bd, s_new)
        s_win_k = jnp.concatenate([cache_win_k, kw_n], axis=1)[:, s_new:]
        s_win_v = jnp.concatenate([cache_win_v, vw_n], axis=1)[:, s_new:]
        return (kc, vc, new), (kc_n, vc_n, ks_n, vs_n, s_win_k, s_win_v)

    def sample_attend(proj, ctx):
        kc, vc, new = ctx
        o_c, idx8 = nsa_dec_cmp(proj, kc, vc, bias_c_s, sel_map_s, nb_s, nsb_s, t_pos)
        idx = idx8[:, :G, :SEL_TOPK]
        is_new = idx >= n_past_blk
        jp = jnp.minimum(idx, n_past_blk - 1)
        phys = (jnp.take_along_axis(page_table[:, None, :], jp // blk_per_page, axis=2) * blk_per_page
                + jp % blk_per_page)
        k_pos = (idx[..., None] * SEL_BLOCK + jnp.arange(SEL_BLOCK)).reshape(bd, G, 1, -1)
        d_s = t_pos - k_pos
        tab = rel_bias.astype(F32).reshape(REL_BUCKETS, G, NSA_HPG)
        onehot = jax.nn.one_hot(_rel_bucket(d_s[:, :, 0]), REL_BUCKETS, dtype=F32)
        bias_s = jnp.einsum('bgkn,ngh->bghk', onehot, tab, precision=HI)
        bias_s = jnp.pad(bias_s, ((0, 0), (0, 0), (0, SUBLANE - NSA_HPG), (0, 0)))
        q8 = jnp.pad(proj[:, 0, :MAIN_W].reshape(bd, G, NSA_HPG, dh),
                     ((0, 0), (0, 0), (0, SUBLANE - NSA_HPG), (0, 0)))
        new_rows = jnp.pad(jnp.transpose(new[:, 0, 2 * KV_LANES:].reshape(bd, 4, G, dh), (0, 2, 1, 3)),
                           ((0, 0), (0, 0), (0, SUBLANE - 4), (0, 0)))
        gates = proj[:, 0, B_GATE_OFF:B_GATE_OFF + GATE_W].reshape(bd, 3, G, NSA_HPG)
        gates8 = jnp.pad(jnp.transpose(gates, (0, 2, 3, 1)),
                         ((0, 0), (0, 0), (0, SUBLANE - NSA_HPG), (0, SUBLANE - 3)))
        o = nsa_dec_sel(phys.astype(jnp.int32), is_new.astype(jnp.int32),
                        cache_slc_k.reshape(-1, SEL_BLOCK, G, dh),
                        cache_slc_v.reshape(-1, SEL_BLOCK, G, dh),
                        q8, new_rows, d_s.astype(jnp.int32), bias_s,
                        cache_win_k, cache_win_v,
                        bias_w_s, o_c, gates8, wb_s, t_pos, win_start)
        main = o[:, :, :NSA_HPG].reshape(bd, 1, MAIN_W)
        return jnp.pad(main, ((0, 0), (0, ts - 1), (0, 0))).astype(BF16)

    y_s, s_shift, s_wkv, s_side = _trunk(
        xs, s_new, P, W, state_shift, jnp.stack([_to_block_diag(state_wkv[l]) for l in range(N_A)]),
        lambda l: (cache_mem_k, l, cache_mem_v, l), sample_side, sample_attend, chunk=SUBLANE, tb=SUBLANE)
    y_sample = y_s[:, :s_new]
    s_cmp_k, s_cmp_v, s_slc_k, s_slc_v, s_win_k, s_win_v = s_side

    return (y_prompt, y_sample, p_mem_k, p_mem_v, p_wkv, p_shift,
            p_cmp_k, p_cmp_v, p_slc_k, p_slc_v, p_win_k, p_win_v,
            s_wkv, s_shift, s_cmp_k, s_cmp_v, s_slc_k, s_slc_v, s_win_k, s_win_v)
```

```python
import functools
import math

import jax
import jax.numpy as jnp
from jax import lax
from jax.experimental import pallas as pl
from jax.experimental.pallas import tpu as pltpu

F32 = jnp.float32
BF16 = jnp.bfloat16
HI = lax.Precision.HIGHEST

D_MODEL = 2048
DEPTH = 4
N_A = 2
MEM_HEADS = 4
MEM_HEAD_DIM = 128
MEM_W = 512
MAIN_W = 1536
RWKV_HEAD = 64
RWKV_HEADS = 24
GN_EPS = 64e-5
NSA_HEAD_DIM = 128
NSA_Q_HEADS = 12
NSA_KV_GROUPS = 4
NSA_HPG = 3
GATE_W = 36
CMP_BLOCK = 32
CMP_STRIDE = 16
SEL_BLOCK = 64
SEL_TOPK = 16
SEL_LOCAL = 2
WINDOW = 512
REL_BUCKETS = 32
REL_MAX_DIST = 128
D_FF = 5504
NORM_EPS = 1e-6
MASK_NEG = -1e30
FORCE_SCORE = 1e9

LANE = 128
SUBLANE = 8
MXU_DIM = 256
VMEM_LIMIT = 56 * 1024 * 1024

FFN_DOWN_TK = 22 * LANE
HEAD_BLOCK = MXU_DIM // RWKV_HEAD
N_HEAD_BLOCKS = RWKV_HEADS // HEAD_BLOCK
LOWRANK_PAD = MXU_DIM
NSA_TQ = 256
NSA_KT = 256
NSA_BANDS = 3


def _params(*sem):
    return pltpu.CompilerParams(dimension_semantics=sem, vmem_limit_bytes=VMEM_LIMIT)


def _rms(x, g):
    return x * lax.rsqrt(jnp.mean(x * x, axis=-1, keepdims=True) + NORM_EPS) * g


def _sigmoid(x):
    return 1.0 / (1.0 + jnp.exp(-x))


def _softplus(x):
    return jnp.maximum(x, 0.0) + jnp.log(1.0 + jnp.exp(-jnp.abs(x)))


def _gelu_tanh(x):
    return 0.5 * x * (1.0 + jnp.tanh(math.sqrt(2.0 / math.pi) * (x + 0.044715 * x * x * x)))


def _dot(a, b, precision=None):
    return jnp.dot(a, b, preferred_element_type=F32, precision=precision)


def _dot_nt(a, b, precision=None):
    return lax.dot_general(a, b, (((1,), (1,)), ((), ())), preferred_element_type=F32,
                           precision=precision)


def _dot_tn(a, b, precision=None):
    return lax.dot_general(a, b, (((0,), (0,)), ((), ())), preferred_element_type=F32,
                           precision=precision)


BF16_BITS_OF_F32 = 0xFFFF0000


def _split_hi_lo(x):
    bits = lax.bitcast_convert_type(x, jnp.uint32) & jnp.uint32(BF16_BITS_OF_F32)
    hi = lax.bitcast_convert_type(bits, F32)
    return hi, x - hi


def _masked_softmax(s, mask):
    s = jnp.where(mask, s, MASK_NEG)
    p = jnp.exp(s - jnp.max(s, axis=-1, keepdims=True)) * mask.astype(F32)
    den = jnp.sum(p, axis=-1, keepdims=True)
    return p / jnp.where(den > 0, den, 1.0)


def _row_tile(rows, target):
    t = min(rows, target)
    while rows % t:
        t -= SUBLANE
    return t


def _norm_matmul_kernel(x_ref, g_ref, w_ref, o_ref, xn_ref):
    @pl.when(pl.program_id(1) == 0)
    def _():
        xn_ref[...] = _rms(x_ref[...], g_ref[...]).astype(BF16)

    o_ref[...] = _dot(xn_ref[...], w_ref[...]).astype(o_ref.dtype)


def norm_matmul(x, g, w, tn, out_dtype=F32, tm=512):
    rows, d = x.shape
    n = w.shape[1]
    tm = _row_tile(rows, tm)
    assert n % tn == 0
    return pl.pallas_call(
        _norm_matmul_kernel,
        grid=(rows // tm, n // tn),
        in_specs=[pl.BlockSpec((tm, d), lambda i, j: (i, 0)),
                  pl.BlockSpec((1, d), lambda i, j: (0, 0)),
                  pl.BlockSpec((d, tn), lambda i, j: (0, j))],
        out_specs=pl.BlockSpec((tm, tn), lambda i, j: (i, j)),
        out_shape=jax.ShapeDtypeStruct((rows, n), out_dtype),
        scratch_shapes=[pltpu.VMEM((tm, d), BF16)],
        compiler_params=_params("arbitrary", "arbitrary"),
        name="norm_matmul",
    )(x, g.reshape(1, d), w)


def _norm_swiglu_kernel(x_ref, g_ref, wg_ref, wu_ref, o_ref, xn_ref, *, tail):
    j = pl.program_id(1)
    last = pl.num_programs(1) - 1

    @pl.when(j == 0)
    def _():
        xn_ref[...] = _rms(x_ref[...], g_ref[...]).astype(BF16)

    xn = xn_ref[...]
    tn = o_ref.shape[1]
    sub = min(tn, MXU_DIM)

    def tile(up_shift):
        c0 = 0
        while c0 < tn - up_shift:
            w = min(sub, tn - up_shift - c0)
            gate = _dot(xn, wg_ref[0, 0, :, c0:c0 + w].astype(BF16))
            up = _dot(xn, wu_ref[0, 0, :, c0 + up_shift:c0 + up_shift + w].astype(BF16))
            o_ref[:, c0:c0 + w] = (gate * _sigmoid(gate) * up).astype(o_ref.dtype)
            c0 += w
        if up_shift:
            o_ref[:, c0:] = jnp.zeros((o_ref.shape[0], up_shift), o_ref.dtype)

    if tail == tn:
        tile(0)
    else:
        pl.when(j != last)(lambda: tile(0))
        pl.when(j == last)(lambda: tile(tn - tail))


def norm_swiglu(x, g, w_gu, l, i, tn=512, tm=512):
    rows, d = x.shape
    f = w_gu.shape[-1] // 2
    assert f % LANE == 0 and tn % LANE == 0
    tm = _row_tile(rows, tm)
    n_tiles = -(-f // tn)
    tail = f - (n_tiles - 1) * tn
    el = pl.Element

    def up_col(j):
        return pl.multiple_of(jnp.minimum(f + j * tn, 2 * f - tn), LANE)

    return pl.pallas_call(
        functools.partial(_norm_swiglu_kernel, tail=tail),
        grid=(rows // tm, n_tiles),
        in_specs=[pl.BlockSpec((tm, d), lambda r, j: (r, 0)),
                  pl.BlockSpec((1, d), lambda r, j: (0, 0)),
                  pl.BlockSpec((el(1), el(1), el(d), el(tn)), lambda r, j: (l, i, 0, j * tn)),
                  pl.BlockSpec((el(1), el(1), el(d), el(tn)), lambda r, j: (l, i, 0, up_col(j)))],
        out_specs=pl.BlockSpec((tm, tn), lambda r, j: (r, j)),
        out_shape=jax.ShapeDtypeStruct((rows, n_tiles * tn), BF16),
        scratch_shapes=[pltpu.VMEM((tm, d), BF16)],
        compiler_params=_params("arbitrary", "arbitrary"),
        name="norm_swiglu",
    )(x, g.reshape(1, d), w_gu, w_gu)


def _matmul_norm_res_kernel(a_ref, w_ref, g_ref, h_ref, o_ref, acc_ref, *, scale, overlap):
    k = pl.program_id(1)
    last = pl.num_programs(1) - 1

    @pl.when(k == 0)
    def _():
        acc_ref[...] = jnp.zeros_like(acc_ref)

    a = a_ref[...]
    if overlap:
        col = lax.broadcasted_iota(jnp.int32, a.shape, 1)
        a = jnp.where(col < jnp.where(k == last, overlap, 0), jnp.zeros_like(a), a)
    acc_ref[...] += _dot(a, w_ref[0, 0])

    @pl.when(k == last)
    def _():
        o_ref[...] = h_ref[...] + scale * _rms(acc_ref[...], g_ref[...])


def matmul_norm_res(a, w, l, i, g, h, scale, tk=512, tm=512):
    rows = a.shape[0]
    kdim, d = w.shape[-2:]
    assert kdim % LANE == 0 and tk % LANE == 0 and a.shape[1] >= kdim
    tm = _row_tile(rows, tm)
    n_k = -(-kdim // tk)
    overlap = n_k * tk - kdim
    el = pl.Element

    def k_off(k):
        return pl.multiple_of(jnp.minimum(k * tk, kdim - tk), LANE)

    return pl.pallas_call(
        functools.partial(_matmul_norm_res_kernel, scale=scale, overlap=overlap),
        grid=(rows // tm, n_k),
        in_specs=[pl.BlockSpec((el(tm), el(tk)), lambda r, k: (r * tm, k_off(k))),
                  pl.BlockSpec((el(1), el(1), el(tk), el(d)), lambda r, k: (l, i, k_off(k), 0)),
                  pl.BlockSpec((1, d), lambda r, k: (0, 0)),
                  pl.BlockSpec((tm, d), lambda r, k: (r, 0))],
        out_specs=pl.BlockSpec((tm, d), lambda r, k: (r, 0)),
        out_shape=jax.ShapeDtypeStruct((rows, d), F32),
        scratch_shapes=[pltpu.VMEM((tm, d), F32)],
        compiler_params=_params("arbitrary", "arbitrary"),
        name="matmul_norm_res",
    )(a, w, g.reshape(1, d), h)


def _out_proj_kernel(main_ref, mo_ref, w_ref, g_ref, h_ref, o_ref):
    k_main = main_ref.shape[1]
    y = _dot(main_ref[...], w_ref[0, :k_main, :]) + _dot(mo_ref[...], w_ref[0, k_main:, :])
    o_ref[...] = h_ref[...] + _rms(y, g_ref[...])


def out_proj(main, mo, w, l, g, h, tm=512):
    rows, k_main = main.shape
    k_mo = mo.shape[1]
    d = w.shape[-1]
    tm = _row_tile(rows, tm)
    return pl.pallas_call(
        _out_proj_kernel,
        grid=(rows // tm,),
        in_specs=[pl.BlockSpec((tm, k_main), lambda r: (r, 0)),
                  pl.BlockSpec((tm, k_mo), lambda r: (r, 0)),
                  pl.BlockSpec((1, k_main + k_mo, d), lambda r: (l, 0, 0)),
                  pl.BlockSpec((1, d), lambda r: (0, 0)),
                  pl.BlockSpec((tm, d), lambda r: (r, 0))],
        out_specs=pl.BlockSpec((tm, d), lambda r: (r, 0)),
        out_shape=jax.ShapeDtypeStruct((rows, d), F32),
        compiler_params=_params("arbitrary"),
        name="out_proj",
    )(main, mo, w, g.reshape(1, d), h)


RW_R, RW_K, RW_V, RW_QM, RW_W1, RW_V1, RW_A1, RW_G1 = 0, 6, 12, 18, 20, 21, 22, 23
RW_NBLK = 24
RW_STEP_MIXES = ((0,), (0,), (0,), (2,), (2,), (2,), (3,), (3,), (3,), (None,), (1, 3), (4, 5))
RW_FIRST_SPLIT_STEP = 10


def _rwkv_in_kernel(x_ref, sp_ref, g_ref, mu_ref, w_ref, o_ref, last_ref,
                    u_sc, xx_sc, xm_sc, carry_sc, *, tm, t_seq, n_seq, last_tile, last_row):
    ti = pl.program_id(1)
    j = pl.program_id(2)

    @pl.when(j == 0)
    def _():
        @pl.when(ti == 0)
        def _():
            carry_sc[...] = sp_ref[0]

        u = _rms(x_ref[0], g_ref[...])
        rows = lax.broadcasted_iota(jnp.int32, u.shape, 0)
        prev = pltpu.roll(u, 1, axis=0)
        for s in range(n_seq):
            prev = jnp.where(rows == s * t_seq, carry_sc[s:s + 1, :], prev)
        u_sc[...] = u
        xx_sc[...] = prev - u
        if n_seq == 1:
            carry_sc[...] = u[tm - 1:tm, :]

        @pl.when(ti == last_tile)
        def _():
            for s in range(n_seq):
                last_ref[0, s:s + 1, :] = u[s * t_seq + last_row:s * t_seq + last_row + 1, :]

    def set_mix(mix):
        if mix is None:
            xm_sc[...] = u_sc[...].astype(BF16)
        else:
            xm_sc[...] = (u_sc[...] + xx_sc[...] * mu_ref[mix:mix + 1, :]).astype(BF16)

    tn = o_ref.shape[2]
    for step, mixes in enumerate(RW_STEP_MIXES):
        if len(mixes) == 1:
            assert step < RW_FIRST_SPLIT_STEP
            if step == 0 or RW_STEP_MIXES[step - 1] != mixes:
                pl.when(j == step)(functools.partial(set_mix, mixes[0]))
        else:
            assert step >= RW_FIRST_SPLIT_STEP

            @pl.when(j == step)
            def _(mixes=mixes):
                part = tn // len(mixes)
                for q, mix in enumerate(mixes):
                    set_mix(mix)
                    o_ref[0, :, q * part:(q + 1) * part] = _dot(xm_sc[...],
                                                                w_ref[:, q * part:(q + 1) * part])

    @pl.when(j < RW_FIRST_SPLIT_STEP)
    def _():
        o_ref[0] = _dot(xm_sc[...], w_ref[...])


def rwkv_in_proj(h, shift_prev, g, mu, w_cat, t_real, tm=1024):
    b, t, d = h.shape
    n_seq = math.gcd(b, max(1, tm // t))
    bg, tg = b // n_seq, t * n_seq
    tm = _row_tile(tg, tm)
    assert n_seq == 1 or tm == tg
    n_steps = len(RW_STEP_MIXES)
    tn = RW_NBLK * MXU_DIM // n_steps
    last_tile, last_row = ((t_real - 1) // tm, (t_real - 1) % tm) if n_seq == 1 else (0, t_real - 1)
    mu8 = jnp.concatenate([mu, jnp.zeros((SUBLANE - mu.shape[0], d), F32)], axis=0)
    proj, last = pl.pallas_call(
        functools.partial(_rwkv_in_kernel, tm=tm, t_seq=t, n_seq=n_seq, last_tile=last_tile,
                          last_row=last_row),
        grid=(bg, tg // tm, n_steps),
        in_specs=[pl.BlockSpec((1, tm, d), lambda bi, ti, j: (bi, ti, 0)),
                  pl.BlockSpec((1, n_seq, d), lambda bi, ti, j: (bi, 0, 0)),
                  pl.BlockSpec((1, d), lambda bi, ti, j: (0, 0)),
                  pl.BlockSpec((SUBLANE, d), lambda bi, ti, j: (0, 0)),
                  pl.BlockSpec((d, tn), lambda bi, ti, j: (0, j))],
        out_specs=[pl.BlockSpec((1, tm, tn), lambda bi, ti, j: (bi, ti, j)),
                   pl.BlockSpec((1, n_seq, d), lambda bi, ti, j: (bi, 0, 0))],
        out_shape=[jax.ShapeDtypeStruct((bg, tg, n_steps * tn), F32),
                   jax.ShapeDtypeStruct((bg, n_seq, d), F32)],
        scratch_shapes=[pltpu.VMEM((tm, d), F32), pltpu.VMEM((tm, d), F32),
                        pltpu.VMEM((tm, d), BF16), pltpu.VMEM((n_seq, d), F32)],
        compiler_params=_params("arbitrary", "arbitrary", "arbitrary"),
        name="rwkv_in_proj",
    )(h.reshape(bg, tg, d), shift_prev.reshape(bg, n_seq, d), g.reshape(1, d), mu8, w_cat)
    return proj.reshape(b, t, n_steps * tn), last.reshape(b, d)


def _rwkv_scan_kernel(*refs, chunk, n_chunks, has_vres, t_valid, n_hb):
    if has_vres:
        (r_ref, k_ref, v_ref, tw_ref, ta_ref, tg_ref, tv_ref, vf_ref,
         w2_ref, a2_ref, g2_ref, v2_ref, vec_ref, s0_ref, y_ref, sout_ref, s_sc) = refs
    else:
        (r_ref, k_ref, v_ref, tw_ref, ta_ref, tg_ref,
         w2_ref, a2_ref, g2_ref, vec_ref, s0_ref, y_ref, sout_ref, s_sc) = refs
    ti = pl.program_id(2)
    c_len = chunk
    lanes = MXU_DIM
    rows4 = HEAD_BLOCK * c_len

    @pl.when(ti == 0)
    def _():
        s_sc[...] = s0_ref[0]

    li = lax.broadcasted_iota(jnp.int32, (lanes, lanes), 0) // RWKV_HEAD
    lj = lax.broadcasted_iota(jnp.int32, (lanes, lanes), 1) // RWKV_HEAD
    seg = (li == lj).astype(BF16)
    ci = lax.broadcasted_iota(jnp.int32, (c_len, c_len), 0)
    cj = lax.broadcasted_iota(jnp.int32, (c_len, c_len), 1)
    tri_c = (cj <= ci).astype(BF16)

    split2 = _split_hi_lo

    def seg_sums(xs):
        parts = [p for x in xs for p in split2(x)]
        res = _dot(jnp.concatenate(parts, axis=0).astype(BF16), seg)
        return [res[(2 * n) * c_len:(2 * n + 1) * c_len] + res[(2 * n + 1) * c_len:(2 * n + 2) * c_len]
                for n in range(len(xs))]

    lane_head = lax.broadcasted_iota(jnp.int32, (c_len, lanes), 1) // RWKV_HEAD
    step_head = lax.broadcasted_iota(jnp.int32, (c_len, rows4), 1) // c_len
    t_row = lax.broadcasted_iota(jnp.int32, (c_len, 2 * rows4), 0)
    s_col = lax.broadcasted_iota(jnp.int32, (c_len, 2 * rows4), 1) & (c_len - 1)
    strict = s_col < t_row
    incl = s_col <= t_row
    eye_row = ((lax.broadcasted_iota(jnp.int32, (c_len, rows4), 1) & (c_len - 1))
               == lax.broadcasted_iota(jnp.int32, (c_len, rows4), 0)).astype(F32)
    same_head = li == lj
    n_double = int(math.log2(c_len)) - 1

    def stack(x):
        return jnp.concatenate(
            [jnp.where(lane_head == hh, x, 0.0) for hh in range(HEAD_BLOCK)], axis=0)

    def stack_steps(x):
        return jnp.concatenate(
            [jnp.where(step_head == hh, x, 0.0) for hh in range(HEAD_BLOCK)], axis=0)

    def one_chunk(c, carry):
        sl = pl.ds(pl.multiple_of(c * c_len, c_len), c_len)
        tw_act = jnp.tanh(tw_ref[0, sl, :])
        ta_act = ta_ref[0, sl, :]
        tg_act = _sigmoid(tg_ref[0, sl, :])
        tv_act = tv_ref[0, sl, :] if has_vres else None
        if t_valid is not None:
            t_idx = ti * (n_chunks * c_len) + c * c_len + lax.broadcasted_iota(
                jnp.int32, (c_len, lanes), 0)
            live = t_idx < t_valid
        st = [dict() for _ in range(n_hb)]

        def prep(hb):
            e = st[hb]
            hl = slice(hb * lanes, (hb + 1) * lanes)
            w0, a0, v0 = vec_ref[0:1, hl], vec_ref[1:2, hl], vec_ref[2:3, hl]
            kkw, kaw, rk = vec_ref[3:4, hl], vec_ref[4:5, hl], vec_ref[7:8, hl]
            r = r_ref[0, sl, hl]
            k = k_ref[0, sl, hl]
            v = v_ref[0, sl, hl]
            logw = -_softplus(-(w0 + _dot(tw_act, w2_ref[:, hl]))) - 0.5
            dlog = -jnp.exp(logw)
            rate = _sigmoid(a0 + _dot(ta_act, a2_ref[:, hl]))
            e['gate'] = _dot(tg_act, g2_ref[:, hl])
            if has_vres:
                v = v + (vf_ref[0, sl, hl] - v) * _sigmoid(v0 + _dot(tv_act, v2_ref[:, hl]))
            kk = k * kkw
            k = k * (1.0 + (rate - 1.0) * kaw)
            kk_sq, rk_sum = seg_sums([kk * kk, r * k * rk])
            kk = kk / jnp.maximum(jnp.sqrt(kk_sq), 1e-12)
            if t_valid is not None:
                dlog = jnp.where(live, dlog, 0.0)
                kk = jnp.where(live, kk, 0.0)
                k_live = jnp.where(live, k, 0.0)
            else:
                k_live = k
            d_hi, d_lo = split2(dlog)
            cum2 = _dot(tri_c, jnp.concatenate([d_hi, d_lo], axis=1).astype(BF16))
            cum = cum2[:, :lanes] + cum2[:, lanes:]
            inv = jnp.exp(-cum)
            b_row = kk * rate * inv
            k_row = k_live * inv
            e['ar'] = jnp.concatenate([-kk * jnp.exp(cum - dlog), r * jnp.exp(cum)],
                                      axis=0).astype(BF16)
            e['bk_row'] = jnp.concatenate([b_row, k_row], axis=0).astype(BF16)
            e['bk'] = jnp.concatenate([stack(b_row), stack(k_row)],
                                      axis=0).astype(BF16)
            e['v'] = v
            e['v_s'] = stack(v).astype(BF16)
            e['g_end'] = jnp.exp(cum[c_len - 1:c_len, :])
            e['bonus'] = rk_sum * v

        def products(hb):
            e = st[hb]
            e['s_old'] = s_sc[hb]
            big = _dot_nt(e['ar'], jnp.concatenate([e['bk'], e['s_old'].astype(BF16)], axis=0))
            a_bk = jnp.where(strict, big[:c_len, :2 * rows4], 0.0)
            e['r_bk'] = jnp.where(incl, big[c_len:, :2 * rows4], 0.0)
            e['a_s0'], e['r_s0'] = big[:c_len, 2 * rows4:], big[c_len:, 2 * rows4:]
            e['a_k'] = a_bk[:, rows4:]
            lrow = a_bk[:, :rows4]
            e['tinv'] = eye_row + lrow
            e['lpow'] = lrow

        def mx(x):
            return x.astype(BF16) if x.shape[0] % (2 * SUBLANE) == 0 else x

        def square(hb):
            e = st[hb]
            e['lpow'] = _dot(mx(e['lpow']), mx(stack_steps(e['lpow'])))

        def double(hb):
            e = st[hb]
            both = _dot(mx(jnp.concatenate([e['lpow'], e['tinv']], axis=0)), mx(stack_steps(e['lpow'])))
            e['tinv'] = e['tinv'] + both[c_len:]
            e['lpow'] = both[:c_len]

        def solve(hb):
            e = st[hb]
            tinv = e['tinv'] + _dot(mx(e['tinv']), mx(stack_steps(e['lpow'])))
            rhs = e['a_s0'] + _dot(mx(e['a_k']), e['v_s'])
            e['u'] = _dot(mx(tinv), mx(stack(rhs)))

        def outputs(hb):
            e = st[hb]
            uv_s = jnp.concatenate([stack(e['u']).astype(BF16), e['v_s']], axis=0)
            e['y'] = e['r_s0'] + _dot(mx(e['r_bk']), uv_s)
            uv_row = jnp.concatenate([e['u'], e['v']], axis=0).astype(BF16)
            s_sc[hb] = (e['s_old'] + jnp.where(same_head, _dot_tn(uv_row, e['bk_row']), 0.0)) * e['g_end']

        def group_norm(hb):
            e = st[hb]
            hl = slice(hb * lanes, (hb + 1) * lanes)
            y = e['y']
            mean = seg_sums([y])[0] * (1.0 / RWKV_HEAD)
            yc = y - mean
            var = seg_sums([yc * yc])[0] * (1.0 / RWKV_HEAD)
            yn = yc * lax.rsqrt(var + GN_EPS) * vec_ref[5:6, hl] + vec_ref[6:7, hl]
            y_ref[0, sl, hl] = ((yn + e['bonus']) * e['gate']).astype(y_ref.dtype)

        for stage in [prep, products, square] + [double] * (n_double - 1) + [solve, outputs, group_norm]:
            for hb in range(n_hb):
                stage(hb)
        return carry

    lax.fori_loop(0, n_chunks, one_chunk, 0)

    @pl.when(ti == pl.num_programs(2) - 1)
    def _():
        sout_ref[0] = s_sc[...]


def rwkv_scan(proj, v_first_proj, w2, a2, g2, v2, vec, s0_bd, t_real, chunk, tb, n_hb=6):
    b, t, _ = proj.shape
    lanes = MXU_DIM
    tb = min(tb, t)
    while t % tb:
        tb -= chunk
    assert tb % chunk == 0 and N_HEAD_BLOCKS % n_hb == 0
    has_vres = v_first_proj is not None
    t_valid = None if t_real == t else t_real

    wide = n_hb * lanes

    def col(block0):
        assert block0 % n_hb == 0
        return pl.BlockSpec((1, tb, wide), lambda bi, hg, ti: (bi, ti, block0 // n_hb + hg))

    def fixed(block):
        return pl.BlockSpec((1, tb, lanes), lambda bi, hg, ti: (bi, ti, block))

    def wcol():
        return pl.BlockSpec((LOWRANK_PAD, wide), lambda bi, hg, ti: (0, hg))

    st = pl.BlockSpec((1, n_hb, lanes, lanes), lambda bi, hg, ti: (bi, hg, 0, 0))
    if has_vres:
        in_specs = [col(RW_R), col(RW_K), col(RW_V), fixed(RW_W1), fixed(RW_A1), fixed(RW_G1),
                    fixed(RW_V1), col(RW_V), wcol(), wcol(), wcol(), wcol()]
        args = [proj, proj, proj, proj, proj, proj, proj, v_first_proj, w2, a2, g2, v2]
    else:
        in_specs = [col(RW_R), col(RW_K), col(RW_V), fixed(RW_W1), fixed(RW_A1), fixed(RW_G1),
                    wcol(), wcol(), wcol()]
        args = [proj, proj, proj, proj, proj, proj, w2, a2, g2]
    in_specs += [pl.BlockSpec((SUBLANE, wide), lambda bi, hg, ti: (0, hg)), st]
    args += [vec, s0_bd]
    return pl.pallas_call(
        functools.partial(_rwkv_scan_kernel, chunk=chunk, n_chunks=tb // chunk,
                          has_vres=has_vres, t_valid=t_valid, n_hb=n_hb),
        grid=(b, N_HEAD_BLOCKS // n_hb, t // tb),
        in_specs=in_specs,
        out_specs=[pl.BlockSpec((1, tb, wide), lambda bi, hg, ti: (bi, ti, hg)), st],
        out_shape=[jax.ShapeDtypeStruct((b, t, MAIN_W), BF16),
                   jax.ShapeDtypeStruct(s0_bd.shape, F32)],
        scratch_shapes=[pltpu.VMEM((n_hb, lanes, lanes), F32)],
        compiler_params=_params("arbitrary", "arbitrary", "arbitrary"),
        name="rwkv_scan",
    )(*args)


def _to_block_diag(s):
    b = s.shape[0]
    s = s.reshape(b, N_HEAD_BLOCKS, HEAD_BLOCK, RWKV_HEAD, RWKV_HEAD)
    eye = jnp.eye(HEAD_BLOCK, dtype=s.dtype)
    bd = s[:, :, :, :, None, :] * eye[None, None, :, None, :, None]
    return bd.reshape(b, N_HEAD_BLOCKS, MXU_DIM, MXU_DIM)


def _from_block_diag(bd):
    b = bd.shape[0]
    n = RWKV_HEAD
    x = jnp.stack([bd[:, :, hh * n:(hh + 1) * n, hh * n:(hh + 1) * n] for hh in range(HEAD_BLOCK)],
                  axis=2)
    return x.reshape(b, RWKV_HEADS, RWKV_HEAD, RWKV_HEAD)


def _mem_attn_kernel(q_ref, k_ref, v_ref, o_ref, *, per_head):
    scale = MEM_HEAD_DIM ** -0.5
    for hh in range(MEM_HEADS):
        sl = slice(hh * MEM_HEAD_DIM, (hh + 1) * MEM_HEAD_DIM)
        k = k_ref[0, 0, :, hh, :] if per_head else k_ref[0, :, sl]
        v = v_ref[0, 0, :, hh, :] if per_head else v_ref[0, :, sl]
        s = _dot_nt(q_ref[0, :, sl], k) * scale
        p = jnp.exp(s - jnp.max(s, axis=-1, keepdims=True))
        p = p / jnp.sum(p, axis=-1, keepdims=True)
        o_ref[0, :, sl] = _dot(p, v).astype(o_ref.dtype)


def mem_attend(qsrc, q_block, ksrc, k_block, vsrc, v_block, tq=512):
    b, t, _ = qsrc.shape
    per_head = ksrc.ndim == 5
    m = ksrc.shape[-3] if per_head else ksrc.shape[1]
    tq = _row_tile(t, tq)
    if per_head:
        kv_spec = lambda blk: pl.BlockSpec((1, 1, m, MEM_HEADS, MEM_HEAD_DIM),
                                           lambda bi, ti: (blk, bi, 0, 0, 0))
    else:
        kv_spec = lambda blk: pl.BlockSpec((1, m, MEM_W), lambda bi, ti: (bi, 0, blk))
    return pl.pallas_call(
        functools.partial(_mem_attn_kernel, per_head=per_head),
        grid=(b, t // tq),
        in_specs=[pl.BlockSpec((1, tq, MEM_W), lambda bi, ti: (bi, ti, q_block)),
                  kv_spec(k_block), kv_spec(v_block)],
        out_specs=pl.BlockSpec((1, tq, MEM_W), lambda bi, ti: (bi, ti, 0)),
        out_shape=jax.ShapeDtypeStruct((b, t, MEM_W), BF16),
        compiler_params=_params("arbitrary", "arbitrary"),
        name="mem_attend",
    )(qsrc, ksrc, vsrc)


CMP_PAGE = 128
CMP_CHUNKS = CMP_PAGE // CMP_STRIDE
KV_LANES = NSA_KV_GROUPS * NSA_HEAD_DIM


def _chunk_proj_kernel(pt_ref, *refs, n_pages):
    del pt_ref
    page_refs = refs[:n_pages]
    w_ref, o0_ref, o1_ref, x_sc = refs[n_pages:]
    for kp in range(n_pages):
        for pos in range(CMP_STRIDE):
            x_sc[kp * CMP_CHUNKS:(kp + 1) * CMP_CHUNKS, pos * LANE:(pos + 1) * LANE] = (
                page_refs[kp][pl.ds(pos, CMP_CHUNKS, stride=CMP_STRIDE), :])
    res = _dot(x_sc[...].astype(BF16), w_ref[...])
    o0_ref[0] = res[:, :LANE]
    o1_ref[0] = res[:, LANE:]


def chunk_proj(rows2d, table, w_flat, n_pages, col_block=0):
    b, n_tab = table.shape
    assert n_tab % n_pages == 0

    def page_spec(kp):
        return pl.BlockSpec(
            (CMP_PAGE, LANE),
            lambda bi, pg, g, pt: (pt[bi, pg * n_pages + kp], col_block * NSA_KV_GROUPS + g))

    out_spec = pl.BlockSpec((1, n_pages * CMP_CHUNKS, LANE), lambda bi, pg, g, pt: (bi, pg, g))
    out_shape = jax.ShapeDtypeStruct((b, n_tab * CMP_CHUNKS, KV_LANES), F32)
    grid_spec = pltpu.PrefetchScalarGridSpec(
        num_scalar_prefetch=1,
        grid=(b, n_tab // n_pages, NSA_KV_GROUPS),
        in_specs=[page_spec(kp) for kp in range(n_pages)]
        + [pl.BlockSpec(w_flat.shape, lambda bi, pg, g, pt: (0, 0))],
        out_specs=[out_spec, out_spec],
        scratch_shapes=[pltpu.VMEM((n_pages * CMP_CHUNKS, CMP_STRIDE * LANE), F32)],
    )
    return pl.pallas_call(
        functools.partial(_chunk_proj_kernel, n_pages=n_pages),
        grid_spec=grid_spec,
        out_shape=[out_shape, out_shape],
        compiler_params=_params("arbitrary", "arbitrary", "arbitrary"),
        name="chunk_proj",
    )(table, *([rows2d] * n_pages), w_flat)


def _chunk_proj_pool_kernel(pt_ref, *refs, n_pages):
    del pt_ref
    page_refs = refs[:n_pages]
    w_ref, o0_ref, o1_ref, x_sc = refs[n_pages:]
    for g in range(NSA_KV_GROUPS):
        for kp in range(n_pages):
            for pos in range(CMP_STRIDE):
                x_sc[kp * CMP_CHUNKS:(kp + 1) * CMP_CHUNKS, pos * LANE:(pos + 1) * LANE] = (
                    page_refs[kp][0, pl.ds(pos, CMP_CHUNKS, stride=CMP_STRIDE), g, :])
        res = _dot(x_sc[...].astype(BF16), w_ref[...])
        o0_ref[0, :, g * LANE:(g + 1) * LANE] = res[:, :LANE]
        o1_ref[0, :, g * LANE:(g + 1) * LANE] = res[:, LANE:]


def chunk_proj_pool(pool, table, w_flat, n_pages):
    b, n_tab = table.shape
    assert n_tab % n_pages == 0

    def page_spec(kp):
        return pl.BlockSpec((1, CMP_PAGE, NSA_KV_GROUPS, LANE),
                            lambda bi, pg, pt: (pt[bi, pg * n_pages + kp], 0, 0, 0))

    out_spec = pl.BlockSpec((1, n_pages * CMP_CHUNKS, KV_LANES), lambda bi, pg, pt: (bi, pg, 0))
    out_shape = jax.ShapeDtypeStruct((b, n_tab * CMP_CHUNKS, KV_LANES), F32)
    grid_spec = pltpu.PrefetchScalarGridSpec(
        num_scalar_prefetch=1,
        grid=(b, n_tab // n_pages),
        in_specs=[page_spec(kp) for kp in range(n_pages)]
        + [pl.BlockSpec(w_flat.shape, lambda bi, pg, pt: (0, 0))],
        out_specs=[out_spec, out_spec],
        scratch_shapes=[pltpu.VMEM((n_pages * CMP_CHUNKS, CMP_STRIDE * LANE), F32)],
    )
    return pl.pallas_call(
        functools.partial(_chunk_proj_pool_kernel, n_pages=n_pages),
        grid_spec=grid_spec,
        out_shape=[out_shape, out_shape],
        compiler_params=_params("arbitrary", "arbitrary"),
        name="chunk_proj_pool",
    )(table, *([pool] * n_pages), w_flat)


def _block_mlp_kernel(p0_ref, p1_ref, pe_ref, w1_ref, b1_ref, w2_ref, o_ref):
    const = _dot(pe_ref[...], w1_ref[...])[0:1, :] + b1_ref[...]
    w2 = w2_ref[...]
    for g in range(NSA_KV_GROUPS):
        sl = slice(g * LANE, (g + 1) * LANE)
        hid = const + p0_ref[0, :, sl] + p1_ref[0, :, sl]
        o_ref[0, :, sl] = _dot(_gelu_tanh(hid), w2)


def block_mlp(p0, p1, pe, w1, b1, w2, tb=1024):
    b, nb, _ = p0.shape
    tb = _row_tile(nb, tb)
    pe8 = jnp.concatenate([pe.reshape(1, -1), jnp.zeros((SUBLANE - 1, pe.size), F32)], axis=0)
    spec = pl.BlockSpec((1, tb, KV_LANES), lambda bi, i: (bi, i, 0))
    full = lambda a: pl.BlockSpec(a.shape, lambda bi, i: (0,) * a.ndim)
    b1r = b1.reshape(1, -1)
    return pl.pallas_call(
        _block_mlp_kernel,
        grid=(b, nb // tb),
        in_specs=[spec, spec, full(pe8), full(w1), full(b1r), full(w2)],
        out_specs=spec,
        out_shape=jax.ShapeDtypeStruct((b, nb, KV_LANES), F32),
        compiler_params=_params("arbitrary", "arbitrary"),
        name="block_mlp",
    )(p0, p1, pe8, w1, b1r, w2)


def _w1_flat(w1):
    r = CMP_BLOCK // CMP_STRIDE
    e = w1.shape[1]
    w = w1.reshape(r, CMP_STRIDE, NSA_HEAD_DIM, e)
    return jnp.transpose(w, (1, 2, 0, 3)).reshape(CMP_STRIDE * NSA_HEAD_DIM, r * e).astype(BF16)


def compress(parts, nb, pe, w1, b1, w2):
    return block_mlp(parts[0][:, :nb], parts[1][:, 1:nb + 1], pe, w1, b1, w2)


def _rel_bucket(dist):
    n = jnp.maximum(dist, 0)
    max_exact = REL_BUCKETS // 2
    nf = jnp.maximum(n, 1).astype(F32)
    large = max_exact + (jnp.log(nf / max_exact) / math.log(REL_MAX_DIST / max_exact)
                         * (REL_BUCKETS - max_exact)).astype(jnp.int32)
    return jnp.where(n < max_exact, n, jnp.minimum(large, REL_BUCKETS - 1))


def _bias_lookup(rel_bias, dist):
    tab = rel_bias.astype(F32).reshape(REL_BUCKETS, NSA_KV_GROUPS, NSA_HPG)
    onehot = jax.nn.one_hot(_rel_bucket(dist), REL_BUCKETS, dtype=F32)
    out = jnp.dot(onehot, tab.reshape(REL_BUCKETS, -1), precision=HI)
    out = out.reshape(dist.shape + (NSA_KV_GROUPS, NSA_HPG))
    return jnp.moveaxis(out, (-2, -1), (0, 1))


def _cmp_to_sel(nb_pad, nb, nsb_pad, nsb):
    i = jnp.arange(nb_pad)[:, None]
    j = jnp.arange(nsb_pad)[None, :]
    start = i * CMP_STRIDE
    hit = (start < (j + 1) * SEL_BLOCK) & (start + CMP_BLOCK > j * SEL_BLOCK) & (i < nb) & (j < nsb)
    return hit.astype(F32)


def _nsa_prompt_kernel(q_ref, gt_ref, kc_ref, vc_ref, bc_ref, band_ref, ks_ref, vs_ref,
                       kw_ref, vw_ref, m_ref, o_ref, s_sc, *, nb, nsb, n_g):
    tq = NSA_TQ
    rows = NSA_HPG * tq
    qw = NSA_HPG * LANE
    i = pl.program_id(2)
    scale = NSA_HEAD_DIM ** -0.5
    row_q = lax.broadcasted_iota(jnp.int32, (rows, LANE), 0) & (tq - 1)
    lane = lax.broadcasted_iota(jnp.int32, (rows, LANE), 1)
    t_pos = i * tq + row_q
    kt_w = NSA_KT
    tiles_per_kt = kt_w // tq
    rel = (i * tq + (lax.broadcasted_iota(jnp.int32, (rows, kt_w), 0) & (tq - 1))
           - lax.broadcasted_iota(jnp.int32, (rows, kt_w), 1))
    m_c = (t_pos - (lane * CMP_STRIDE + (CMP_BLOCK - 1)) >= 0) & (lane < nb)
    jb = lax.broadcasted_iota(jnp.int32, (nsb, tq), 0)
    cur = (i * tq + lax.broadcasted_iota(jnp.int32, (nsb, tq), 1)) // SEL_BLOCK
    valid = jb <= cur
    forced = valid & ((jb == 0) | (jb > cur - SEL_LOCAL))
    blk_row = lax.broadcasted_iota(jnp.int32, (nsb, kt_w), 0)
    blk_lane = lax.broadcasted_iota(jnp.int32, (nsb, kt_w), 1) // SEL_BLOCK
    sel_map = m_ref[...].astype(BF16)
    n_win_tiles = WINDOW // kt_w + 1
    n_kt = i // tiles_per_kt + 1
    st = [dict() for _ in range(n_g)]
    gl = lambda g: slice(g * LANE, (g + 1) * LANE)

    def masked_scores(g, k_ref, kt, mask_of):
        ksl = pl.ds(pl.multiple_of(kt * kt_w, kt_w), kt_w)
        band = band_ref[g, jnp.clip(i - kt * tiles_per_kt, 0, NSA_BANDS - 1)].reshape(rows, kt_w)
        s = _dot_nt(st[g]['q3'], k_ref[0, ksl, gl(g)].astype(BF16)) + band
        return jnp.where(mask_of(kt * kt_w), s, MASK_NEG), ksl

    def finish(m_fin, l_fin, acc):
        return jnp.where(m_fin > MASK_NEG, acc / jnp.where(l_fin > 0, l_fin, 1.0), 0.0)

    def window_scores(g):
        e = st[g]
        e['q3'] = (jnp.concatenate([q_ref[0, :, g * qw + hh * LANE:g * qw + (hh + 1) * LANE]
                                    for hh in range(NSA_HPG)], axis=0) * scale).astype(BF16)
        e['win'] = []
        m_w = jnp.full((rows, 1), MASK_NEG, F32)
        for jw in range(n_win_tiles):
            kt_raw = i // tiles_per_kt - (n_win_tiles - 1) + jw
            reach = jnp.where(kt_raw >= 0, WINDOW, 0)
            s_w, ksl_w = masked_scores(g, kw_ref, jnp.maximum(kt_raw, 0),
                                       lambda base, reach=reach: (rel >= base) & (rel < base + reach))
            e['win'].append((s_w, ksl_w))
            m_w = jnp.maximum(m_w, jnp.max(s_w, axis=-1, keepdims=True))
        e['m_w'] = m_w

    def compressed(g):
        e = st[g]
        s_c = _dot_nt(e['q3'], kc_ref[0, :, gl(g)].astype(BF16)) + bc_ref[g].reshape(rows, LANE)
        e['p_c'] = _masked_softmax(s_c, m_c)
        e['o_c'] = _dot(e['p_c'], vc_ref[0, :, gl(g)])

    def select(g):
        e = st[g]
        p_hi, p_lo = _split_hi_lo(e['p_c'])
        imp3 = _dot_nt(sel_map, p_hi.astype(BF16)) + _dot_nt(sel_map, p_lo.astype(BF16))
        imp = imp3[:, 0:tq]
        for hh in range(1, NSA_HPG):
            imp = imp + imp3[:, hh * tq:(hh + 1) * tq]
        imp = jnp.where(forced, FORCE_SCORE, jnp.where(valid, imp, -FORCE_SCORE))
        rank = jnp.zeros((nsb, tq), jnp.int32)
        for jp in range(nsb):
            other = imp[jp:jp + 1, :]
            rank = rank + ((other > imp) | ((other == imp) & (jb > jp))).astype(jnp.int32)
        sel = (rank < min(SEL_TOPK, nsb)).astype(F32).T
        e['sel3'] = jnp.concatenate([sel] * NSA_HPG, axis=0).astype(BF16)

    def window_values(g):
        e = st[g]
        l_w = jnp.zeros((rows, 1), F32)
        acc_w = jnp.zeros((rows, LANE), F32)
        for s_w, ksl_w in e['win']:
            p_w = jnp.exp(s_w - e['m_w'])
            l_w = l_w + jnp.sum(p_w, axis=-1, keepdims=True)
            acc_w = acc_w + _dot(p_w, vw_ref[0, ksl_w, gl(g)])
        e['o_w'] = finish(e['m_w'], l_w, acc_w)

    for stage in (window_scores, compressed, select, window_values):
        for g in range(n_g):
            stage(g)

    def scores(kt, m_run):
        expand = (blk_row == kt * (kt_w // SEL_BLOCK) + blk_lane).astype(BF16)
        out = []
        for g in range(n_g):
            chosen = _dot(st[g]['sel3'], expand) > 0.5
            s, _ = masked_scores(g, ks_ref, kt, lambda base, chosen=chosen: chosen & (rel >= base))
            s_sc[g, kt] = s
            out.append(jnp.maximum(m_run[g], jnp.max(s, axis=-1, keepdims=True)))
        return tuple(out)

    m_s = lax.fori_loop(0, n_kt, scores, tuple(jnp.full((rows, 1), MASK_NEG, F32) for _ in range(n_g)))

    def accum(kt, carry):
        ksl = pl.ds(pl.multiple_of(kt * kt_w, kt_w), kt_w)
        out = []
        for g in range(n_g):
            l_run, acc = carry[g]
            p = jnp.exp(s_sc[g, kt] - m_s[g])
            out.append((l_run + jnp.sum(p, axis=-1, keepdims=True), acc + _dot(p, vs_ref[0, ksl, gl(g)])))
        return tuple(out)

    fin = lax.fori_loop(0, n_kt, accum, tuple((jnp.zeros((rows, 1), F32), jnp.zeros((rows, LANE), F32))
                                              for _ in range(n_g)))
    for g in range(n_g):
        e = st[g]
        o_s = finish(m_s[g], fin[g][0], fin[g][1])
        gt = _sigmoid(gt_ref[0, g])
        gcol = lambda br, gt=gt: jnp.concatenate(
            [gt[:, br * NSA_HPG + hh:br * NSA_HPG + hh + 1] for hh in range(NSA_HPG)], axis=0)
        o = gcol(0) * e['o_c'] + gcol(1) * o_s + gcol(2) * e['o_w']
        for hh in range(NSA_HPG):
            o_ref[0, :, g * qw + hh * LANE:g * qw + (hh + 1) * LANE] = (
                o[hh * tq:(hh + 1) * tq].astype(o_ref.dtype))


def nsa_prompt(proj, gates_t, kc, vc, bias_c, band, side, sel_map, nb, nsb, n_g=2):
    b, t, _ = proj.shape
    tq = NSA_TQ
    qw = n_g * NSA_HPG * LANE
    kvw = n_g * LANE
    n_gp = NSA_KV_GROUPS // n_g
    kv = lambda off: pl.BlockSpec((1, t, kvw), lambda bi, gp, i: (bi, 0, off * n_gp + gp))
    return pl.pallas_call(
        functools.partial(_nsa_prompt_kernel, nb=nb, nsb=nsb, n_g=n_g),
        grid=(b, n_gp, t // tq),
        in_specs=[pl.BlockSpec((1, tq, qw), lambda bi, gp, i: (bi, i, gp)),
                  pl.BlockSpec((1, n_g, tq, NSA_HPG * 3), lambda bi, gp, i: (bi, gp, i, 0)),
                  pl.BlockSpec((1, LANE, kvw), lambda bi, gp, i: (bi, 0, gp)),
                  pl.BlockSpec((1, LANE, kvw), lambda bi, gp, i: (bi, 0, gp)),
                  pl.BlockSpec((n_g, NSA_HPG, tq, LANE), lambda bi, gp, i: (gp, 0, i, 0)),
                  pl.BlockSpec((n_g, NSA_BANDS, NSA_HPG, tq, NSA_KT), lambda bi, gp, i: (gp, 0, 0, 0, 0)),
                  kv(2), kv(3), kv(4), kv(5),
                  pl.BlockSpec(sel_map.shape, lambda bi, gp, i: (0, 0))],
        out_specs=pl.BlockSpec((1, tq, qw), lambda bi, gp, i: (bi, i, gp)),
        out_shape=jax.ShapeDtypeStruct((b, t, MAIN_W), BF16),
        scratch_shapes=[pltpu.VMEM((n_g, t // NSA_KT, NSA_HPG * tq, NSA_KT), F32)],
        compiler_params=_params("arbitrary", "arbitrary", "arbitrary"),
        name="nsa_prompt",
    )(proj, gates_t, kc, vc, bias_c, band, side, side, side, side, sel_map)


def _nsa_dec_cmp_kernel(q_ref, kc_ref, vc_ref, bc_ref, m_ref, oc_ref, idx_ref, *, nb, nsb, t_pos):
    scale = NSA_HEAD_DIM ** -0.5
    nbp = kc_ref.shape[1]
    nsp = m_ref.shape[1]
    n_sel = min(SEL_TOPK, nsb)
    lane_b = lax.broadcasted_iota(jnp.int32, (SUBLANE, nbp), 1)
    m_c = (t_pos - (lane_b * CMP_STRIDE + (CMP_BLOCK - 1)) >= 0) & (lane_b < nb)
    row_s = lax.broadcasted_iota(jnp.int32, (SUBLANE, nsp), 0)
    jb = lax.broadcasted_iota(jnp.int32, (1, nsp), 1)
    cur = t_pos // SEL_BLOCK
    valid = jb <= cur
    forced = valid & ((jb == 0) | (jb > cur - SEL_LOCAL))
    out_lane = lax.broadcasted_iota(jnp.int32, (SUBLANE, LANE), 1)
    imp_rows = []
    for g in range(NSA_KV_GROUPS):
        q3 = jnp.concatenate(
            [q_ref[0, 0:1, (g * NSA_HPG + hh) * LANE:(g * NSA_HPG + hh + 1) * LANE]
             for hh in range(NSA_HPG)] + [jnp.zeros((SUBLANE - NSA_HPG, LANE), F32)], axis=0)
        sl = slice(g * LANE, (g + 1) * LANE)
        s_c = _dot_nt(q3, kc_ref[0, :, sl]) * scale + bc_ref[g]
        p_c = _masked_softmax(s_c, m_c)
        oc_ref[0, g] = _dot(p_c, vc_ref[0, :, sl])
        imp8 = jnp.where(row_s < NSA_HPG, _dot(p_c, m_ref[...], HI), 0.0)
        imp_rows.append(jnp.sum(imp8, axis=0, keepdims=True))
    imp = jnp.concatenate(imp_rows + [jnp.zeros((SUBLANE - NSA_KV_GROUPS, nsp), F32)], axis=0)
    imp = jnp.where(forced, FORCE_SCORE, jnp.where(valid, imp, -FORCE_SCORE))
    imp = jnp.where(jb < nsb, imp, -jnp.inf)
    jbf = jnp.broadcast_to(jb.astype(F32), imp.shape)
    picks = jnp.zeros((SUBLANE, LANE), F32)
    for kk in range(n_sel):
        best = jnp.max(imp, axis=-1, keepdims=True)
        arg = jnp.min(jnp.where(imp == best, jbf, float(nsp)), axis=-1, keepdims=True)
        picks = jnp.where(out_lane == kk, arg, picks)
        imp = jnp.where(jbf == arg, -jnp.inf, imp)
    idx_ref[0] = picks.astype(jnp.int32)


def nsa_dec_cmp(proj, kc, vc, bias_c, sel_map, nb, nsb, t_pos):
    b, tp, _ = proj.shape
    nbp = kc.shape[1]
    return pl.pallas_call(
        functools.partial(_nsa_dec_cmp_kernel, nb=nb, nsb=nsb, t_pos=t_pos),
        grid=(b,),
        in_specs=[pl.BlockSpec((1, tp, MAIN_W), lambda bi: (bi, 0, 0)),
                  pl.BlockSpec((1, nbp, KV_LANES), lambda bi: (bi, 0, 0)),
                  pl.BlockSpec((1, nbp, KV_LANES), lambda bi: (bi, 0, 0)),
                  pl.BlockSpec(bias_c.shape, lambda bi: (0, 0, 0)),
                  pl.BlockSpec(sel_map.shape, lambda bi: (0, 0))],
        out_specs=[pl.BlockSpec((1, NSA_KV_GROUPS, SUBLANE, LANE), lambda bi: (bi, 0, 0, 0)),
                   pl.BlockSpec((1, SUBLANE, LANE), lambda bi: (bi, 0, 0))],
        out_shape=[jax.ShapeDtypeStruct((b, NSA_KV_GROUPS, SUBLANE, LANE), F32),
                   jax.ShapeDtypeStruct((b, SUBLANE, LANE), jnp.int32)],
        compiler_params=_params("arbitrary"),
        name="nsa_dec_cmp",
    )(proj, kc, vc, bias_c, sel_map)


def _nsa_dec_sel_kernel(phys_ref, isnew_ref, *refs, n_sel, n_win, t_pos, win_start):
    del phys_ref
    kb_refs = refs[:n_sel]
    vb_refs = refs[n_sel:2 * n_sel]
    (q_ref, new_ref, ds_ref, bs_ref, wk_ref, wv_ref, bw_ref, oc_ref, gt_ref, o_ref) = refs[2 * n_sel:]
    bi = pl.program_id(0)
    g = pl.program_id(1)
    scale = NSA_HEAD_DIM ** -0.5
    q3 = q_ref[0, 0]
    new_rows = new_ref[0, 0]
    pad_blk = jnp.zeros((SEL_BLOCK - 1, LANE), F32)
    new_k = jnp.concatenate([new_rows[0:1], pad_blk], axis=0)
    new_v = jnp.concatenate([new_rows[1:2], pad_blk], axis=0)
    ks, vs = [], []
    for kk in range(n_sel):
        fresh = isnew_ref[bi, g, kk] > 0
        ks.append(jnp.where(fresh, new_k, kb_refs[kk][0, :, g, :]))
        vs.append(jnp.where(fresh, new_v, vb_refs[kk][0, :, g, :]))
    ks = jnp.concatenate(ks, axis=0)
    vs = jnp.concatenate(vs, axis=0)
    s_s = _dot_nt(q3, ks) * scale + bs_ref[0, 0]
    p_s = _masked_softmax(s_s, ds_ref[0, 0] >= 0)
    o_s = _dot(p_s, vs)

    pad_w = jnp.zeros((SUBLANE - 1, LANE), F32)
    kw = jnp.concatenate([wk_ref[0, :, g, :], new_rows[2:3], pad_w], axis=0)
    vw = jnp.concatenate([wv_ref[0, :, g, :], new_rows[3:4], pad_w], axis=0)
    nw = kw.shape[0]
    pos = win_start + lax.broadcasted_iota(jnp.int32, (SUBLANE, nw), 1)
    d_w = t_pos - pos
    m_w = (d_w >= 0) & (d_w < WINDOW) & (pos >= 0) & (pos - win_start < n_win)
    s_w = _dot_nt(q3, kw) * scale + bw_ref[0]
    p_w = _masked_softmax(s_w, m_w)
    o_w = _dot(p_w, vw)

    gt = _sigmoid(gt_ref[0, 0])
    o_ref[0, 0] = gt[:, 0:1] * oc_ref[0, 0] + gt[:, 1:2] * o_s + gt[:, 2:3] * o_w


def nsa_dec_sel(phys, isnew, pool_k, pool_v, q8, new_rows, d_s, bias_s, win_k, win_v, bias_w,
                o_c, gates8, n_win, t_pos, win_start):
    b = q8.shape[0]
    n_sel = phys.shape[-1]
    n_keys = n_sel * SEL_BLOCK
    nw = win_k.shape[1]
    nwp = nw + SUBLANE

    def blk_spec(kk):
        return pl.BlockSpec((1, SEL_BLOCK, NSA_KV_GROUPS, LANE),
                            lambda bi, g, ph, nf: (ph[bi, g, kk], 0, 0, 0))

    per_bg = lambda *shape: pl.BlockSpec((1, 1) + shape, lambda bi, g, ph, nf: (bi, g) + (0,) * len(shape))
    grid_spec = pltpu.PrefetchScalarGridSpec(
        num_scalar_prefetch=2,
        grid=(b, NSA_KV_GROUPS),
        in_specs=[blk_spec(kk) for kk in range(n_sel)] + [blk_spec(kk) for kk in range(n_sel)]
        + [per_bg(SUBLANE, LANE), per_bg(SUBLANE, LANE), per_bg(1, n_keys), per_bg(SUBLANE, n_keys),
           pl.BlockSpec((1, nw, NSA_KV_GROUPS, LANE), lambda bi, g, ph, nf: (bi, 0, 0, 0)),
           pl.BlockSpec((1, nw, NSA_KV_GROUPS, LANE), lambda bi, g, ph, nf: (bi, 0, 0, 0)),
           pl.BlockSpec((1, SUBLANE, nwp), lambda bi, g, ph, nf: (g, 0, 0)),
           per_bg(SUBLANE, LANE), per_bg(SUBLANE, SUBLANE)],
        out_specs=per_bg(SUBLANE, LANE),
    )
    return pl.pallas_call(
        functools.partial(_nsa_dec_sel_kernel, n_sel=n_sel, n_win=nw + 1, t_pos=t_pos,
                          win_start=win_start),
        grid_spec=grid_spec,
        out_shape=jax.ShapeDtypeStruct((b, NSA_KV_GROUPS, SUBLANE, LANE), F32),
        compiler_params=_params("arbitrary", "arbitrary"),
        name="nsa_dec_sel",
    )(phys, isnew, *([pool_k] * n_sel), *([pool_v] * n_sel), q8, new_rows, d_s, bias_s,
      win_k, win_v, bias_w, o_c, gates8)


def _pad_cols(w, n):
    return jnp.pad(w, ((0, 0), (0, n - w.shape[1])))


def _pad_rows(w, n):
    return jnp.pad(w, ((0, n - w.shape[0]), (0, 0)))


def _prep_weights(P):
    W = {}
    W['w_gu'] = P['ffn_gu']
    W['w_d'] = P['ffn_d'].astype(BF16)
    W['w_out'] = P['w_out'].astype(BF16)
    W['rw_in'], W['rw_w2'], W['rw_a2'], W['rw_g2'], W['rw_v2'], W['rw_vec'] = [], [], [], [], [], []
    lp = LOWRANK_PAD
    for l in range(N_A):
        w_in = P['w_in_a'][l]
        v1 = P['rw_v1'][l - 1] if l > 0 else jnp.zeros((D_MODEL, lp), F32)
        W['rw_in'].append(jnp.concatenate([
            w_in, _pad_cols(P['rw_w1'][l], lp), _pad_cols(v1, lp), _pad_cols(P['rw_a1'][l], lp),
            _pad_cols(P['rw_g1'][l], lp)], axis=1).astype(BF16))
        W['rw_w2'].append(_pad_rows(P['rw_w2'][l], lp))
        W['rw_a2'].append(_pad_rows(P['rw_a2'][l], lp))
        W['rw_g2'].append(_pad_rows(P['rw_g2'][l], lp))
        W['rw_v2'].append(_pad_rows(P['rw_v2'][l - 1], lp) if l > 0 else None)
        v0 = P['rw_v0'][l - 1] if l > 0 else jnp.zeros((MAIN_W,), F32)
        W['rw_vec'].append(jnp.stack([P['rw_w0'][l], P['rw_a0'][l], v0, P['rw_kk'][l], P['rw_ka'][l],
                                      P['rw_lnw'][l], P['rw_lnb'][l], P['rw_rk'][l].reshape(-1)]))
    W['w_in_b'] = []
    for l in range(DEPTH - N_A):
        w = P['w_in_b'][l]
        W['w_in_b'].append(jnp.concatenate([
            w[:, :MAIN_W], w[:, MAIN_W + GATE_W:], _pad_cols(w[:, MAIN_W:MAIN_W + GATE_W], MXU_DIM)],
            axis=1).astype(BF16))
    W['w_kv'] = P['w_kv'].astype(BF16)
    W['w_mem_kv'] = [P['w_mem_kv'][l].astype(BF16) for l in range(DEPTH)]
    W['cmp_w1_flat'] = [_w1_flat(P['cmp_w1'][c]) for c in range(2)]
    return W


B_QM_BLOCK = MAIN_W // MEM_W
B_GATE_OFF = MAIN_W + MEM_W


def _ffn(h2, norms_l, first, W, l, i):
    hid = norm_swiglu(h2, norms_l[first], W['w_gu'], l, i, tn=512, tm=1024)
    return matmul_norm_res(hid, W['w_d'], l, i, norms_l[first + 1], h2, 0.5, tk=FFN_DOWN_TK)


def _trunk(x, t_real, P, W, shift0, wkv0_bd, mem_src, make_side, attend, chunk, tb):
    b, t, d = x.shape
    h2 = x.reshape(b * t, d)
    shifts, states = [], []
    v_first_proj, ctx, side_state = None, None, None
    for l in range(DEPTH):
        n = P['norms'][l]
        h2 = _ffn(h2, n, 0, W, l, 0)
        if l < N_A:
            proj, last = rwkv_in_proj(h2.reshape(b, t, d), shift0[l], n[2], P['rw_mu'][l],
                                      W['rw_in'][l], t_real)
            main, s_bd = rwkv_scan(proj, v_first_proj if l > 0 else None, W['rw_w2'][l], W['rw_a2'][l],
                                   W['rw_g2'][l], W['rw_v2'][l], W['rw_vec'][l], wkv0_bd[l],
                                   t_real, chunk, tb)
            if l == 0:
                v_first_proj = proj
            shifts.append(last)
            states.append(_from_block_diag(s_bd))
            q_src, q_block = proj, RW_QM * MXU_DIM // MEM_W
        else:
            proj = norm_matmul(h2, n[2], W['w_in_b'][l - N_A], tn=768, tm=1024).reshape(b, t, -1)
            main = attend(proj, ctx)
            q_src, q_block = proj, B_QM_BLOCK
        mk, kb, mv, vb = mem_src(l)
        mo = mem_attend(q_src, q_block, mk, kb, mv, vb)
        h2 = out_proj(main.reshape(b * t, MAIN_W), mo.reshape(b * t, MEM_W), W['w_out'], l, n[3], h2)
        h2 = _ffn(h2, n, 4, W, l, 1)
        if l == N_A - 1:
            side = norm_matmul(h2, P['kv_norm'], W['w_kv'], tn=768, tm=1024).reshape(b, t, -1)
            ctx, side_state = make_side(side)
    return h2.reshape(b, t, d), jnp.stack(shifts), jnp.stack(states), side_state


def kernel(x_prompt, x_sample, mem_prompt, state_wkv, state_shift, cache_mem_k, cache_mem_v,
           cache_cmp_k, cache_cmp_v, cache_slc_k, cache_slc_v, cache_win_k, cache_win_v, page_table,
           norms, ffn_gu, ffn_d, w_in_a, w_in_b, w_out, mem_norm, w_mem_kv, kv_norm, w_kv,
           cmp_pe, cmp_w1, cmp_b1, cmp_w2, rel_bias,
           rw_mu, rw_w0, rw_w1, rw_w2, rw_a0, rw_a1, rw_a2, rw_g1, rw_g2, rw_v0, rw_v1, rw_v2,
           rw_kk, rw_ka, rw_rk, rw_lnw, rw_lnb):
    P = dict(norms=norms, ffn_gu=ffn_gu, ffn_d=ffn_d, w_in_a=w_in_a, w_in_b=w_in_b, w_out=w_out,
             kv_norm=kv_norm, w_kv=w_kv, w_mem_kv=w_mem_kv, cmp_w1=cmp_w1, rw_mu=rw_mu, rw_w0=rw_w0,
             rw_w1=rw_w1, rw_w2=rw_w2, rw_a0=rw_a0, rw_a1=rw_a1, rw_a2=rw_a2, rw_g1=rw_g1,
             rw_g2=rw_g2, rw_v0=rw_v0, rw_v1=rw_v1, rw_v2=rw_v2, rw_kk=rw_kk, rw_ka=rw_ka,
             rw_rk=rw_rk, rw_lnw=rw_lnw, rw_lnb=rw_lnb)
    W = _prep_weights(P)
    G, dh = NSA_KV_GROUPS, NSA_HEAD_DIM
    split_side = lambda side, bx, t: [side[:, :t, c * KV_LANES:(c + 1) * KV_LANES].reshape(bx, t, G, dh)
                                      for c in range(6)]

    def cmp_mlp(parts, nb, c):
        return compress(parts, nb, cmp_pe[c], cmp_w1[c], cmp_b1[c], cmp_w2[c])

    bp, tp, d = x_prompt.shape
    n_mem = mem_prompt.shape[1]
    mem2 = mem_prompt.reshape(bp * n_mem, d)
    p_mkv = [norm_matmul(mem2, mem_norm[l], W['w_mem_kv'][l], tn=512).reshape(bp, n_mem, 2 * MEM_W)
             for l in range(DEPTH)]
    p_mem_k = jnp.stack([m[..., :MEM_W].reshape(bp, n_mem, MEM_HEADS, MEM_HEAD_DIM) for m in p_mkv])
    p_mem_v = jnp.stack([m[..., MEM_W:].reshape(bp, n_mem, MEM_HEADS, MEM_HEAD_DIM) for m in p_mkv])

    nb_p = tp // CMP_STRIDE - 1
    nsb_p = tp // SEL_BLOCK
    t_all = jnp.arange(tp)
    c_end = jnp.arange(LANE) * CMP_STRIDE + (CMP_BLOCK - 1)
    assert nb_p <= LANE
    bias_c_p = _bias_lookup(rel_bias, t_all[:, None] - c_end[None, :])
    ii = jnp.arange(NSA_TQ)
    cc = jnp.arange(NSA_KT)
    band_p = jnp.stack([_bias_lookup(rel_bias, dd * NSA_TQ + ii[:, None] - cc[None, :])
                        for dd in range(NSA_BANDS)], axis=1)
    assert (NSA_BANDS - 1) * NSA_TQ - (NSA_KT - 1) >= REL_MAX_DIST and tp % NSA_KT == 0
    sel_map_p = _cmp_to_sel(LANE, nb_p, nsb_p, nsb_p).T
    ident = jnp.arange(bp * tp // CMP_PAGE, dtype=jnp.int32).reshape(bp, tp // CMP_PAGE)

    def prompt_side(side):
        rows = side.reshape(bp * tp, -1)
        kc = cmp_mlp(chunk_proj(rows, ident, W['cmp_w1_flat'][0], tp // CMP_PAGE, 0), nb_p, 0)
        vc = cmp_mlp(chunk_proj(rows, ident, W['cmp_w1_flat'][1], tp // CMP_PAGE, 1), nb_p, 1)
        padb = ((0, 0), (0, LANE - nb_p), (0, 0))
        wb = min(WINDOW, tp)
        kc_r, vc_r, ks, vs, kw, vw = split_side(side, bp, tp)
        return ((jnp.pad(kc, padb), jnp.pad(vc, padb), side),
                (kc_r, vc_r, ks, vs, kw[:, tp - wb:], vw[:, tp - wb:]))

    def prompt_attend(proj, ctx):
        kc, vc, side = ctx
        gates = proj[..., B_GATE_OFF:B_GATE_OFF + GATE_W].reshape(bp, tp, 3, G, NSA_HPG)
        gates_t = jnp.transpose(gates, (0, 3, 1, 2, 4)).reshape(bp, G, tp, 3 * NSA_HPG)
        return nsa_prompt(proj, gates_t, kc, vc, bias_c_p, band_p, side, sel_map_p, nb_p, nsb_p)

    zeros_shift = jnp.zeros((N_A, bp, d), F32)
    zeros_state = jnp.zeros((N_A, bp, N_HEAD_BLOCKS, MXU_DIM, MXU_DIM), F32)
    y_prompt, p_shift, p_wkv, p_side = _trunk(
        x_prompt, tp, P, W, zeros_shift, zeros_state,
        lambda l: (p_mkv[l], 0, p_mkv[l], 1), prompt_side, prompt_attend, chunk=64, tb=512)
    p_cmp_k, p_cmp_v, p_slc_k, p_slc_v, p_win_k, p_win_v = p_side

    bd, s_new, _ = x_sample.shape
    assert s_new == 1
    ts = SUBLANE
    xs = jnp.pad(x_sample, ((0, 0), (0, ts - s_new), (0, 0)))
    past_len = page_table.shape[1] * CMP_PAGE
    n_past_blk = past_len // SEL_BLOCK
    blk_per_page = CMP_PAGE // SEL_BLOCK
    nsb_s = n_past_blk + 1
    t_pos = past_len
    nc_s = -(-(past_len + s_new) // CMP_STRIDE)
    nb_s = nc_s - 1
    wb_s = cache_win_k.shape[1]
    win_start = past_len - wb_s
    nsp = -(-nsb_s // LANE) * LANE
    sel_map_s = _cmp_to_sel(nb_s, nb_s, nsp, nsb_s)
    c_end_s = jnp.arange(nb_s) * CMP_STRIDE + (CMP_BLOCK - 1)
    bias_c_s = _bias_lookup(rel_bias, t_pos - c_end_s)
    bias_c_s = jnp.pad(bias_c_s, ((0, 0), (0, SUBLANE - NSA_HPG), (0, 0)))
    nwp = wb_s + SUBLANE
    bias_w_s = _bias_lookup(rel_bias, t_pos - (win_start + jnp.arange(nwp)))
    bias_w_s = jnp.pad(bias_w_s, ((0, 0), (0, SUBLANE - NSA_HPG), (0, 0)))
    ident_s = jnp.arange(bd, dtype=jnp.int32).reshape(bd, 1)
    pages_per_step = math.gcd(16, page_table.shape[1])

    def sample_side(side):
        new = side[:, :s_new]
        parts = []
        for c, pool in enumerate((cache_cmp_k, cache_cmp_v)):
            past = chunk_proj_pool(pool, page_table, W['cmp_w1_flat'][c], pages_per_step)
            fresh_page = jnp.pad(new[:, :, c * KV_LANES:(c + 1) * KV_LANES],
                                 ((0, 0), (0, CMP_PAGE - s_new), (0, 0))).reshape(bd, CMP_PAGE, G, dh)
            fresh = chunk_proj_pool(fresh_page, ident_s, W['cmp_w1_flat'][c], 1)
            n_fresh = nc_s - past[0].shape[1]
            parts.append([jnp.concatenate([p, f[:, :n_fresh]], axis=1) for p, f in zip(past, fresh)])
        kc = cmp_mlp(parts[0], nb_s, 0)
        vc = cmp_mlp(parts[1], nb_s, 1)
        kc_n, vc_n, ks_n, vs_n, kw_n, vw_n = split_side(side, bd, s_new)
        s_win_k = jnp.concatenate([cache_win_k, kw_n], axis=1)[:, s_new:]
        s_win_v = jnp.concatenate([cache_win_v, vw_n], axis=1)[:, s_new:]
        return (kc, vc, new), (kc_n, vc_n, ks_n, vs_n, s_win_k, s_win_v)

    def sample_attend(proj, ctx):
        kc, vc, new = ctx
        o_c, idx8 = nsa_dec_cmp(proj, kc, vc, bias_c_s, sel_map_s, nb_s, nsb_s, t_pos)
        idx = idx8[:, :G, :SEL_TOPK]
        is_new = idx >= n_past_blk
        jp = jnp.minimum(idx, n_past_blk - 1)
        phys = (jnp.take_along_axis(page_table[:, None, :], jp // blk_per_page, axis=2) * blk_per_page
                + jp % blk_per_page)
        k_pos = (idx[..., None] * SEL_BLOCK + jnp.arange(SEL_BLOCK)).reshape(bd, G, 1, -1)
        d_s = t_pos - k_pos
        tab = rel_bias.astype(F32).reshape(REL_BUCKETS, G, NSA_HPG)
        onehot = jax.nn.one_hot(_rel_bucket(d_s[:, :, 0]), REL_BUCKETS, dtype=F32)
        bias_s = jnp.einsum('bgkn,ngh->bghk', onehot, tab, precision=HI)
        bias_s = jnp.pad(bias_s, ((0, 0), (0, 0), (0, SUBLANE - NSA_HPG), (0, 0)))
        q8 = jnp.pad(proj[:, 0, :MAIN_W].reshape(bd, G, NSA_HPG, dh),
                     ((0, 0), (0, 0), (0, SUBLANE - NSA_HPG), (0, 0)))
        new_rows = jnp.pad(jnp.transpose(new[:, 0, 2 * KV_LANES:].reshape(bd, 4, G, dh), (0, 2, 1, 3)),
                           ((0, 0), (0, 0), (0, SUBLANE - 4), (0, 0)))
        gates = proj[:, 0, B_GATE_OFF:B_GATE_OFF + GATE_W].reshape(bd, 3, G, NSA_HPG)
        gates8 = jnp.pad(jnp.transpose(gates, (0, 2, 3, 1)),
                         ((0, 0), (0, 0), (0, SUBLANE - NSA_HPG), (0, SUBLANE - 3)))
        o = nsa_dec_sel(phys.astype(jnp.int32), is_new.astype(jnp.int32),
                        cache_slc_k.reshape(-1, SEL_BLOCK, G, dh),
                        cache_slc_v.reshape(-1, SEL_BLOCK, G, dh),
                        q8, new_rows, d_s.astype(jnp.int32), bias_s,
                        cache_win_k, cache_win_v,
                        bias_w_s, o_c, gates8, wb_s, t_pos, win_start)
        main = o[:, :, :NSA_HPG].reshape(bd, 1, MAIN_W)
        return jnp.pad(main, ((0, 0), (0, ts - 1), (0, 0))).astype(BF16)

    y_s, s_shift, s_wkv, s_side = _trunk(
        xs, s_new, P, W, state_shift, jnp.stack([_to_block_diag(state_wkv[l]) for l in range(N_A)]),
        lambda l: (cache_mem_k, l, cache_mem_v, l), sample_side, sample_attend, chunk=SUBLANE, tb=SUBLANE)
    y_sample = y_s[:, :s_new]
    s_cmp_k, s_cmp_v, s_slc_k, s_slc_v, s_win_k, s_win_v = s_side

    return (y_prompt, y_sample, p_mem_k, p_mem_v, p_wkv, p_shift,
            p_cmp_k, p_cmp_v, p_slc_k, p_slc_v, p_win_k, p_win_v,
            s_wkv, s_shift, s_cmp_k, s_cmp_v, s_slc_k, s_slc_v, s_win_k, s_win_v)
```

```python
import functools
import math

import jax
import jax.numpy as jnp
from jax import lax
from jax.experimental import pallas as pl
from jax.experimental.pallas import tpu as pltpu

F32 = jnp.float32
BF16 = jnp.bfloat16
HI = lax.Precision.HIGHEST

D_MODEL = 2048
DEPTH = 4
N_A = 2
MEM_HEADS = 4
MEM_HEAD_DIM = 128
MEM_W = 512
MAIN_W = 1536
RWKV_HEAD = 64
RWKV_HEADS = 24
GN_EPS = 64e-5
NSA_HEAD_DIM = 128
NSA_Q_HEADS = 12
NSA_KV_GROUPS = 4
NSA_HPG = 3
GATE_W = 36
CMP_BLOCK = 32
CMP_STRIDE = 16
SEL_BLOCK = 64
SEL_TOPK = 16
SEL_LOCAL = 2
WINDOW = 512
REL_BUCKETS = 32
REL_MAX_DIST = 128
D_FF = 5504
NORM_EPS = 1e-6
MASK_NEG = -1e30
FORCE_SCORE = 1e9

LANE = 128
SUBLANE = 8
MXU_DIM = 256
VMEM_LIMIT = 56 * 1024 * 1024

FFN_DOWN_TK = 22 * LANE
HEAD_BLOCK = MXU_DIM // RWKV_HEAD
N_HEAD_BLOCKS = RWKV_HEADS // HEAD_BLOCK
LOWRANK_PAD = MXU_DIM
NSA_TQ = 256
NSA_KT = 256
NSA_BANDS = 3


def _params(*sem):
    return pltpu.CompilerParams(dimension_semantics=sem, vmem_limit_bytes=VMEM_LIMIT)


def _rms(x, g):
    return x * lax.rsqrt(jnp.mean(x * x, axis=-1, keepdims=True) + NORM_EPS) * g


def _sigmoid(x):
    return 1.0 / (1.0 + jnp.exp(-x))


def _softplus(x):
    return jnp.maximum(x, 0.0) + jnp.log(1.0 + jnp.exp(-jnp.abs(x)))


def _gelu_tanh(x):
    return 0.5 * x * (1.0 + jnp.tanh(math.sqrt(2.0 / math.pi) * (x + 0.044715 * x * x * x)))


def _dot(a, b, precision=None):
    return jnp.dot(a, b, preferred_element_type=F32, precision=precision)


def _dot_nt(a, b, precision=None):
    return lax.dot_general(a, b, (((1,), (1,)), ((), ())), preferred_element_type=F32,
                           precision=precision)


def _dot_tn(a, b, precision=None):
    return lax.dot_general(a, b, (((0,), (0,)), ((), ())), preferred_element_type=F32,
                           precision=precision)


BF16_BITS_OF_F32 = 0xFFFF0000


def _split_hi_lo(x):
    bits = lax.bitcast_convert_type(x, jnp.uint32) & jnp.uint32(BF16_BITS_OF_F32)
    hi = lax.bitcast_convert_type(bits, F32)
    return hi, x - hi


def _masked_softmax(s, mask):
    s = jnp.where(mask, s, MASK_NEG)
    p = jnp.exp(s - jnp.max(s, axis=-1, keepdims=True)) * mask.astype(F32)
    den = jnp.sum(p, axis=-1, keepdims=True)
    return p / jnp.where(den > 0, den, 1.0)


def _row_tile(rows, target):
    t = min(rows, target)
    while rows % t:
        t -= SUBLANE
    return t


def _norm_matmul_kernel(x_ref, g_ref, w_ref, o_ref, xn_ref):
    @pl.when(pl.program_id(1) == 0)
    def _():
        xn_ref[...] = _rms(x_ref[...], g_ref[...]).astype(BF16)

    o_ref[...] = _dot(xn_ref[...], w_ref[...]).astype(o_ref.dtype)


def norm_matmul(x, g, w, tn, out_dtype=F32, tm=512):
    rows, d = x.shape
    n = w.shape[1]
    tm = _row_tile(rows, tm)
    assert n % tn == 0
    return pl.pallas_call(
        _norm_matmul_kernel,
        grid=(rows // tm, n // tn),
        in_specs=[pl.BlockSpec((tm, d), lambda i, j: (i, 0)),
                  pl.BlockSpec((1, d), lambda i, j: (0, 0)),
                  pl.BlockSpec((d, tn), lambda i, j: (0, j))],
        out_specs=pl.BlockSpec((tm, tn), lambda i, j: (i, j)),
        out_shape=jax.ShapeDtypeStruct((rows, n), out_dtype),
        scratch_shapes=[pltpu.VMEM((tm, d), BF16)],
        compiler_params=_params("arbitrary", "arbitrary"),
        name="norm_matmul",
    )(x, g.reshape(1, d), w)


def _norm_swiglu_kernel(x_ref, g_ref, wg_ref, wu_ref, o_ref, xn_ref, *, tail):
    j = pl.program_id(1)
    last = pl.num_programs(1) - 1

    @pl.when(j == 0)
    def _():
        xn_ref[...] = _rms(x_ref[...], g_ref[...]).astype(BF16)

    xn = xn_ref[...]
    tn = o_ref.shape[1]
    sub = min(tn, MXU_DIM)

    def tile(up_shift):
        c0 = 0
        while c0 < tn - up_shift:
            w = min(sub, tn - up_shift - c0)
            gate = _dot(xn, wg_ref[0, 0, :, c0:c0 + w].astype(BF16))
            up = _dot(xn, wu_ref[0, 0, :, c0 + up_shift:c0 + up_shift + w].astype(BF16))
            o_ref[:, c0:c0 + w] = (gate * _sigmoid(gate) * up).astype(o_ref.dtype)
            c0 += w
        if up_shift:
            o_ref[:, c0:] = jnp.zeros((o_ref.shape[0], up_shift), o_ref.dtype)

    if tail == tn:
        tile(0)
    else:
        pl.when(j != last)(lambda: tile(0))
        pl.when(j == last)(lambda: tile(tn - tail))


def norm_swiglu(x, g, w_gu, l, i, tn=512, tm=512):
    rows, d = x.shape
    f = w_gu.shape[-1] // 2
    assert f % LANE == 0 and tn % LANE == 0
    tm = _row_tile(rows, tm)
    n_tiles = -(-f // tn)
    tail = f - (n_tiles - 1) * tn
    el = pl.Element

    def up_col(j):
        return pl.multiple_of(jnp.minimum(f + j * tn, 2 * f - tn), LANE)

    return pl.pallas_call(
        functools.partial(_norm_swiglu_kernel, tail=tail),
        grid=(rows // tm, n_tiles),
        in_specs=[pl.BlockSpec((tm, d), lambda r, j: (r, 0)),
                  pl.BlockSpec((1, d), lambda r, j: (0, 0)),
                  pl.BlockSpec((el(1), el(1), el(d), el(tn)), lambda r, j: (l, i, 0, j * tn)),
                  pl.BlockSpec((el(1), el(1), el(d), el(tn)), lambda r, j: (l, i, 0, up_col(j)))],
        out_specs=pl.BlockSpec((tm, tn), lambda r, j: (r, j)),
        out_shape=jax.ShapeDtypeStruct((rows, n_tiles * tn), BF16),
        scratch_shapes=[pltpu.VMEM((tm, d), BF16)],
        compiler_params=_params("arbitrary", "arbitrary"),
        name="norm_swiglu",
    )(x, g.reshape(1, d), w_gu, w_gu)


def _matmul_norm_res_kernel(a_ref, w_ref, g_ref, h_ref, o_ref, acc_ref, *, scale, overlap):
    k = pl.program_id(1)
    last = pl.num_programs(1) - 1

    @pl.when(k == 0)
    def _():
        acc_ref[...] = jnp.zeros_like(acc_ref)

    a = a_ref[...]
    if overlap:
        col = lax.broadcasted_iota(jnp.int32, a.shape, 1)
        a = jnp.where(col < jnp.where(k == last, overlap, 0), jnp.zeros_like(a), a)
    @pl.when(k < last)
    def _():
        acc_ref[...] += _dot(a, w_ref[0, 0])

    @pl.when(k == last)
    def _():
        n_out = acc_ref.shape[1]
        sub = min(n_out, 2 * MXU_DIM)
        ss = jnp.zeros((acc_ref.shape[0], 1), F32)
        for c0 in range(0, n_out, sub):
            y = acc_ref[:, c0:c0 + sub] + _dot(a, w_ref[0, 0, :, c0:c0 + sub])
            acc_ref[:, c0:c0 + sub] = y
            ss = ss + jnp.sum(y * y, axis=-1, keepdims=True)
        inv = lax.rsqrt(ss * (1.0 / n_out) + NORM_EPS)
        o_ref[...] = h_ref[...] + scale * (acc_ref[...] * inv * g_ref[...])


def matmul_norm_res(a, w, l, i, g, h, scale, tk=512, tm=512):
    rows = a.shape[0]
    kdim, d = w.shape[-2:]
    assert kdim % LANE == 0 and tk % LANE == 0 and a.shape[1] >= kdim
    tm = _row_tile(rows, tm)
    n_k = -(-kdim // tk)
    overlap = n_k * tk - kdim
    el = pl.Element

    def k_off(k):
        return pl.multiple_of(jnp.minimum(k * tk, kdim - tk), LANE)

    return pl.pallas_call(
        functools.partial(_matmul_norm_res_kernel, scale=scale, overlap=overlap),
        grid=(rows // tm, n_k),
        in_specs=[pl.BlockSpec((el(tm), el(tk)), lambda r, k: (r * tm, k_off(k))),
                  pl.BlockSpec((el(1), el(1), el(tk), el(d)), lambda r, k: (l, i, k_off(k), 0)),
                  pl.BlockSpec((1, d), lambda r, k: (0, 0)),
                  pl.BlockSpec((tm, d), lambda r, k: (r, 0))],
        out_specs=pl.BlockSpec((tm, d), lambda r, k: (r, 0)),
        out_shape=jax.ShapeDtypeStruct((rows, d), F32),
        scratch_shapes=[pltpu.VMEM((tm, d), F32)],
        compiler_params=_params("arbitrary", "arbitrary"),
        name="matmul_norm_res",
    )(a, w, g.reshape(1, d), h)


def _out_proj_kernel(main_ref, mo_ref, w_ref, g_ref, h_ref, o_ref):
    k_main = main_ref.shape[1]
    y = _dot(main_ref[...], w_ref[0, :k_main, :]) + _dot(mo_ref[...], w_ref[0, k_main:, :])
    o_ref[...] = h_ref[...] + _rms(y, g_ref[...])


def out_proj(main, mo, w, l, g, h, tm=512):
    rows, k_main = main.shape
    k_mo = mo.shape[1]
    d = w.shape[-1]
    tm = _row_tile(rows, tm)
    return pl.pallas_call(
        _out_proj_kernel,
        grid=(rows // tm,),
        in_specs=[pl.BlockSpec((tm, k_main), lambda r: (r, 0)),
                  pl.BlockSpec((tm, k_mo), lambda r: (r, 0)),
                  pl.BlockSpec((1, k_main + k_mo, d), lambda r: (l, 0, 0)),
                  pl.BlockSpec((1, d), lambda r: (0, 0)),
                  pl.BlockSpec((tm, d), lambda r: (r, 0))],
        out_specs=pl.BlockSpec((tm, d), lambda r: (r, 0)),
        out_shape=jax.ShapeDtypeStruct((rows, d), F32),
        compiler_params=_params("arbitrary"),
        name="out_proj",
    )(main, mo, w, g.reshape(1, d), h)


RW_R, RW_K, RW_V, RW_QM, RW_W1, RW_V1, RW_A1, RW_G1 = 0, 6, 12, 18, 20, 21, 22, 23
RW_NBLK = 24
RW_STEP_MIXES = ((0,), (0,), (0,), (2,), (2,), (2,), (3,), (3,), (3,), (None,), (1, 3), (4, 5))
RW_FIRST_SPLIT_STEP = 10


def _rwkv_in_kernel(x_ref, sp_ref, g_ref, mu_ref, w_ref, o_ref, last_ref,
                    u_sc, xx_sc, xm_sc, carry_sc, *, tm, t_seq, n_seq, last_tile, last_row):
    ti = pl.program_id(1)
    j = pl.program_id(2)

    @pl.when(j == 0)
    def _():
        @pl.when(ti == 0)
        def _():
            carry_sc[...] = sp_ref[0]

        u = _rms(x_ref[0], g_ref[...])
        rows = lax.broadcasted_iota(jnp.int32, u.shape, 0)
        prev = pltpu.roll(u, 1, axis=0)
        for s in range(n_seq):
            prev = jnp.where(rows == s * t_seq, carry_sc[s:s + 1, :], prev)
        u_sc[...] = u
        xx_sc[...] = prev - u
        if n_seq == 1:
            carry_sc[...] = u[tm - 1:tm, :]

        @pl.when(ti == last_tile)
        def _():
            for s in range(n_seq):
                last_ref[0, s:s + 1, :] = u[s * t_seq + last_row:s * t_seq + last_row + 1, :]

    def set_mix(mix):
        if mix is None:
            xm_sc[...] = u_sc[...].astype(BF16)
        else:
            xm_sc[...] = (u_sc[...] + xx_sc[...] * mu_ref[mix:mix + 1, :]).astype(BF16)

    tn = o_ref.shape[2]
    for step, mixes in enumerate(RW_STEP_MIXES):
        if len(mixes) == 1:
            assert step < RW_FIRST_SPLIT_STEP
            if step == 0 or RW_STEP_MIXES[step - 1] != mixes:
                pl.when(j == step)(functools.partial(set_mix, mixes[0]))
        else:
            assert step >= RW_FIRST_SPLIT_STEP

            @pl.when(j == step)
            def _(mixes=mixes):
                part = tn // len(mixes)
                for q, mix in enumerate(mixes):
                    set_mix(mix)
                    o_ref[0, :, q * part:(q + 1) * part] = _dot(xm_sc[...],
                                                                w_ref[:, q * part:(q + 1) * part])

    @pl.when(j < RW_FIRST_SPLIT_STEP)
    def _():
        o_ref[0] = _dot(xm_sc[...], w_ref[...])


def rwkv_in_proj(h, shift_prev, g, mu, w_cat, t_real, tm=1024):
    b, t, d = h.shape
    n_seq = math.gcd(b, max(1, tm // t))
    bg, tg = b // n_seq, t * n_seq
    tm = _row_tile(tg, tm)
    assert n_seq == 1 or tm == tg
    n_steps = len(RW_STEP_MIXES)
    tn = RW_NBLK * MXU_DIM // n_steps
    last_tile, last_row = ((t_real - 1) // tm, (t_real - 1) % tm) if n_seq == 1 else (0, t_real - 1)
    mu8 = jnp.concatenate([mu, jnp.zeros((SUBLANE - mu.shape[0], d), F32)], axis=0)
    proj, last = pl.pallas_call(
        functools.partial(_rwkv_in_kernel, tm=tm, t_seq=t, n_seq=n_seq, last_tile=last_tile,
                          last_row=last_row),
        grid=(bg, tg // tm, n_steps),
        in_specs=[pl.BlockSpec((1, tm, d), lambda bi, ti, j: (bi, ti, 0)),
                  pl.BlockSpec((1, n_seq, d), lambda bi, ti, j: (bi, 0, 0)),
                  pl.BlockSpec((1, d), lambda bi, ti, j: (0, 0)),
                  pl.BlockSpec((SUBLANE, d), lambda bi, ti, j: (0, 0)),
                  pl.BlockSpec((d, tn), lambda bi, ti, j: (0, j))],
        out_specs=[pl.BlockSpec((1, tm, tn), lambda bi, ti, j: (bi, ti, j)),
                   pl.BlockSpec((1, n_seq, d), lambda bi, ti, j: (bi, 0, 0))],
        out_shape=[jax.ShapeDtypeStruct((bg, tg, n_steps * tn), F32),
                   jax.ShapeDtypeStruct((bg, n_seq, d), F32)],
        scratch_shapes=[pltpu.VMEM((tm, d), F32), pltpu.VMEM((tm, d), F32),
                        pltpu.VMEM((tm, d), BF16), pltpu.VMEM((n_seq, d), F32)],
        compiler_params=_params("arbitrary", "arbitrary", "arbitrary"),
        name="rwkv_in_proj",
    )(h.reshape(bg, tg, d), shift_prev.reshape(bg, n_seq, d), g.reshape(1, d), mu8, w_cat)
    return proj.reshape(b, t, n_steps * tn), last.reshape(b, d)


def _rwkv_scan_kernel(*refs, chunk, n_chunks, has_vres, t_valid, n_hb):
    if has_vres:
        (r_ref, k_ref, v_ref, tw_ref, ta_ref, tg_ref, tv_ref, vf_ref,
         w2_ref, a2_ref, g2_ref, v2_ref, vec_ref, s0_ref, y_ref, sout_ref, s_sc) = refs
    else:
        (r_ref, k_ref, v_ref, tw_ref, ta_ref, tg_ref,
         w2_ref, a2_ref, g2_ref, vec_ref, s0_ref, y_ref, sout_ref, s_sc) = refs
    ti = pl.program_id(2)
    c_len = chunk
    lanes = MXU_DIM
    rows4 = HEAD_BLOCK * c_len

    @pl.when(ti == 0)
    def _():
        s_sc[...] = s0_ref[0]

    li = lax.broadcasted_iota(jnp.int32, (lanes, lanes), 0) // RWKV_HEAD
    lj = lax.broadcasted_iota(jnp.int32, (lanes, lanes), 1) // RWKV_HEAD
    seg = (li == lj).astype(BF16)
    ci = lax.broadcasted_iota(jnp.int32, (c_len, c_len), 0)
    cj = lax.broadcasted_iota(jnp.int32, (c_len, c_len), 1)
    tri_c = (cj <= ci).astype(BF16)

    split2 = _split_hi_lo

    def seg_sums(xs):
        parts = [p for x in xs for p in split2(x)]
        res = _dot(jnp.concatenate(parts, axis=0).astype(BF16), seg)
        return [res[(2 * n) * c_len:(2 * n + 1) * c_len] + res[(2 * n + 1) * c_len:(2 * n + 2) * c_len]
                for n in range(len(xs))]

    lane_head = lax.broadcasted_iota(jnp.int32, (c_len, lanes), 1) // RWKV_HEAD
    step_head = lax.broadcasted_iota(jnp.int32, (c_len, rows4), 1) // c_len
    t_row = lax.broadcasted_iota(jnp.int32, (c_len, 2 * rows4), 0)
    s_col = lax.broadcasted_iota(jnp.int32, (c_len, 2 * rows4), 1) & (c_len - 1)
    strict = s_col < t_row
    incl = s_col <= t_row
    eye_row = ((lax.broadcasted_iota(jnp.int32, (c_len, rows4), 1) & (c_len - 1))
               == lax.broadcasted_iota(jnp.int32, (c_len, rows4), 0)).astype(F32)
    same_head = li == lj
    n_double = int(math.log2(c_len)) - 1

    def stack(x):
        return jnp.concatenate(
            [jnp.where(lane_head == hh, x, 0.0) for hh in range(HEAD_BLOCK)], axis=0)

    def stack_steps(x):
        return jnp.concatenate(
            [jnp.where(step_head == hh, x, 0.0) for hh in range(HEAD_BLOCK)], axis=0)

    def one_chunk(c, carry):
        sl = pl.ds(pl.multiple_of(c * c_len, c_len), c_len)
        tw_act = jnp.tanh(tw_ref[0, sl, :])
        ta_act = ta_ref[0, sl, :]
        tg_act = _sigmoid(tg_ref[0, sl, :])
        tv_act = tv_ref[0, sl, :] if has_vres else None
        if t_valid is not None:
            t_idx = ti * (n_chunks * c_len) + c * c_len + lax.broadcasted_iota(
                jnp.int32, (c_len, lanes), 0)
            live = t_idx < t_valid
        st = [dict() for _ in range(n_hb)]

        def prep(hb):
            e = st[hb]
            hl = slice(hb * lanes, (hb + 1) * lanes)
            w0, a0, v0 = vec_ref[0:1, hl], vec_ref[1:2, hl], vec_ref[2:3, hl]
            kkw, kaw, rk = vec_ref[3:4, hl], vec_ref[4:5, hl], vec_ref[7:8, hl]
            r = r_ref[0, sl, hl]
            k = k_ref[0, sl, hl]
            v = v_ref[0, sl, hl]
            logw = -_softplus(-(w0 + _dot(tw_act, w2_ref[:, hl]))) - 0.5
            dlog = -jnp.exp(logw)
            rate = _sigmoid(a0 + _dot(ta_act, a2_ref[:, hl]))
            e['gate'] = _dot(tg_act, g2_ref[:, hl])
            if has_vres:
                v = v + (vf_ref[0, sl, hl] - v) * _sigmoid(v0 + _dot(tv_act, v2_ref[:, hl]))
            kk = k * kkw
            k = k * (1.0 + (rate - 1.0) * kaw)
            kk_sq, rk_sum = seg_sums([kk * kk, r * k * rk])
            kk = kk / jnp.maximum(jnp.sqrt(kk_sq), 1e-12)
            if t_valid is not None:
                dlog = jnp.where(live, dlog, 0.0)
                kk = jnp.where(live, kk, 0.0)
                k_live = jnp.where(live, k, 0.0)
            else:
                k_live = k
            d_hi, d_lo = split2(dlog)
            cum2 = _dot(tri_c, jnp.concatenate([d_hi, d_lo], axis=1).astype(BF16))
            cum = cum2[:, :lanes] + cum2[:, lanes:]
            inv = jnp.exp(-cum)
            b_row = kk * rate * inv
            k_row = k_live * inv
            e['ar'] = jnp.concatenate([-kk * jnp.exp(cum - dlog), r * jnp.exp(cum)],
                                      axis=0).astype(BF16)
            e['bk_row'] = jnp.concatenate([b_row, k_row], axis=0).astype(BF16)
            e['bk'] = jnp.concatenate([stack(b_row), stack(k_row)],
                                      axis=0).astype(BF16)
            e['v'] = v
            e['v_s'] = stack(v).astype(BF16)
            e['g_end'] = jnp.exp(cum[c_len - 1:c_len, :])
            e['bonus'] = rk_sum * v

        def products(hb):
            e = st[hb]
            e['s_old'] = s_sc[hb]
            big = _dot_nt(e['ar'], jnp.concatenate([e['bk'], e['s_old'].astype(BF16)], axis=0))
            a_bk = jnp.where(strict, big[:c_len, :2 * rows4], 0.0)
            e['r_bk'] = jnp.where(incl, big[c_len:, :2 * rows4], 0.0)
            e['a_s0'], e['r_s0'] = big[:c_len, 2 * rows4:], big[c_len:, 2 * rows4:]
            e['a_k'] = a_bk[:, rows4:]
            lrow = a_bk[:, :rows4]
            e['tinv'] = eye_row + lrow
            e['lpow'] = lrow

        def mx(x):
            return x.astype(BF16) if x.shape[0] % (2 * SUBLANE) == 0 else x

        def square(hb):
            e = st[hb]
            e['lpow'] = _dot(mx(e['lpow']), mx(stack_steps(e['lpow'])))

        def double(hb):
            e = st[hb]
            both = _dot(mx(jnp.concatenate([e['lpow'], e['tinv']], axis=0)), mx(stack_steps(e['lpow'])))
            e['tinv'] = e['tinv'] + both[c_len:]
            e['lpow'] = both[:c_len]

        def solve(hb):
            e = st[hb]
            tinv = e['tinv'] + _dot(mx(e['tinv']), mx(stack_steps(e['lpow'])))
            rhs = e['a_s0'] + _dot(mx(e['a_k']), e['v_s'])
            e['u'] = _dot(mx(tinv), mx(stack(rhs)))

        def outputs(hb):
            e = st[hb]
            uv_s = jnp.concatenate([stack(e['u']).astype(BF16), e['v_s']], axis=0)
            e['y'] = e['r_s0'] + _dot(mx(e['r_bk']), uv_s)
            uv_row = jnp.concatenate([e['u'], e['v']], axis=0).astype(BF16)
            s_sc[hb] = (e['s_old'] + jnp.where(same_head, _dot_tn(uv_row, e['bk_row']), 0.0)) * e['g_end']

        def group_norm(hb):
            e = st[hb]
            hl = slice(hb * lanes, (hb + 1) * lanes)
            y = e['y']
            mean = seg_sums([y])[0] * (1.0 / RWKV_HEAD)
            yc = y - mean
            var = seg_sums([yc * yc])[0] * (1.0 / RWKV_HEAD)
            yn = yc * lax.rsqrt(var + GN_EPS) * vec_ref[5:6, hl] + vec_ref[6:7, hl]
            y_ref[0, sl, hl] = ((yn + e['bonus']) * e['gate']).astype(y_ref.dtype)

        for stage in [prep, products, square] + [double] * (n_double - 1) + [solve, outputs, group_norm]:
            for hb in range(n_hb):
                stage(hb)
        return carry

    lax.fori_loop(0, n_chunks, one_chunk, 0)

    @pl.when(ti == pl.num_programs(2) - 1)
    def _():
        sout_ref[0] = s_sc[...]


def rwkv_scan(proj, v_first_proj, w2, a2, g2, v2, vec, s0_bd, t_real, chunk, tb, n_hb=6):
    b, t, _ = proj.shape
    lanes = MXU_DIM
    tb = min(tb, t)
    while t % tb:
        tb -= chunk
    assert tb % chunk == 0 and N_HEAD_BLOCKS % n_hb == 0
    has_vres = v_first_proj is not None
    t_valid = None if t_real == t else t_real

    wide = n_hb * lanes

    def col(block0):
        assert block0 % n_hb == 0
        return pl.BlockSpec((1, tb, wide), lambda bi, hg, ti: (bi, ti, block0 // n_hb + hg))

    def fixed(block):
        return pl.BlockSpec((1, tb, lanes), lambda bi, hg, ti: (bi, ti, block))

    def wcol():
        return pl.BlockSpec((LOWRANK_PAD, wide), lambda bi, hg, ti: (0, hg))

    st = pl.BlockSpec((1, n_hb, lanes, lanes), lambda bi, hg, ti: (bi, hg, 0, 0))
    if has_vres:
        in_specs = [col(RW_R), col(RW_K), col(RW_V), fixed(RW_W1), fixed(RW_A1), fixed(RW_G1),
                    fixed(RW_V1), col(RW_V), wcol(), wcol(), wcol(), wcol()]
        args = [proj, proj, proj, proj, proj, proj, proj, v_first_proj, w2, a2, g2, v2]
    else:
        in_specs = [col(RW_R), col(RW_K), col(RW_V), fixed(RW_W1), fixed(RW_A1), fixed(RW_G1),
                    wcol(), wcol(), wcol()]
        args = [proj, proj, proj, proj, proj, proj, w2, a2, g2]
    in_specs += [pl.BlockSpec((SUBLANE, wide), lambda bi, hg, ti: (0, hg)), st]
    args += [vec, s0_bd]
    return pl.pallas_call(
        functools.partial(_rwkv_scan_kernel, chunk=chunk, n_chunks=tb // chunk,
                          has_vres=has_vres, t_valid=t_valid, n_hb=n_hb),
        grid=(b, N_HEAD_BLOCKS // n_hb, t // tb),
        in_specs=in_specs,
        out_specs=[pl.BlockSpec((1, tb, wide), lambda bi, hg, ti: (bi, ti, hg)), st],
        out_shape=[jax.ShapeDtypeStruct((b, t, MAIN_W), BF16),
                   jax.ShapeDtypeStruct(s0_bd.shape, F32)],
        scratch_shapes=[pltpu.VMEM((n_hb, lanes, lanes), F32)],
        compiler_params=_params("arbitrary", "arbitrary", "arbitrary"),
        name="rwkv_scan",
    )(*args)


def _to_block_diag(s):
    b = s.shape[0]
    s = s.reshape(b, N_HEAD_BLOCKS, HEAD_BLOCK, RWKV_HEAD, RWKV_HEAD)
    eye = jnp.eye(HEAD_BLOCK, dtype=s.dtype)
    bd = s[:, :, :, :, None, :] * eye[None, None, :, None, :, None]
    return bd.reshape(b, N_HEAD_BLOCKS, MXU_DIM, MXU_DIM)


def _from_block_diag(bd):
    b = bd.shape[0]
    n = RWKV_HEAD
    x = jnp.stack([bd[:, :, hh * n:(hh + 1) * n, hh * n:(hh + 1) * n] for hh in range(HEAD_BLOCK)],
                  axis=2)
    return x.reshape(b, RWKV_HEADS, RWKV_HEAD, RWKV_HEAD)


def _mem_attn_kernel(q_ref, k_ref, v_ref, o_ref, *, per_head):
    scale = MEM_HEAD_DIM ** -0.5
    for hh in range(MEM_HEADS):
        sl = slice(hh * MEM_HEAD_DIM, (hh + 1) * MEM_HEAD_DIM)
        k = k_ref[0, 0, :, hh, :] if per_head else k_ref[0, :, sl]
        v = v_ref[0, 0, :, hh, :] if per_head else v_ref[0, :, sl]
        s = _dot_nt(q_ref[0, :, sl], k) * scale
        p = jnp.exp(s - jnp.max(s, axis=-1, keepdims=True))
        p = p / jnp.sum(p, axis=-1, keepdims=True)
        o_ref[0, :, sl] = _dot(p, v).astype(o_ref.dtype)


def mem_attend(qsrc, q_block, ksrc, k_block, vsrc, v_block, tq=512):
    b, t, _ = qsrc.shape
    per_head = ksrc.ndim == 5
    m = ksrc.shape[-3] if per_head else ksrc.shape[1]
    tq = _row_tile(t, tq)
    if per_head:
        kv_spec = lambda blk: pl.BlockSpec((1, 1, m, MEM_HEADS, MEM_HEAD_DIM),
                                           lambda bi, ti: (blk, bi, 0, 0, 0))
    else:
        kv_spec = lambda blk: pl.BlockSpec((1, m, MEM_W), lambda bi, ti: (bi, 0, blk))
    return pl.pallas_call(
        functools.partial(_mem_attn_kernel, per_head=per_head),
        grid=(b, t // tq),
        in_specs=[pl.BlockSpec((1, tq, MEM_W), lambda bi, ti: (bi, ti, q_block)),
                  kv_spec(k_block), kv_spec(v_block)],
        out_specs=pl.BlockSpec((1, tq, MEM_W), lambda bi, ti: (bi, ti, 0)),
        out_shape=jax.ShapeDtypeStruct((b, t, MEM_W), BF16),
        compiler_params=_params("arbitrary", "arbitrary"),
        name="mem_attend",
    )(qsrc, ksrc, vsrc)


CMP_PAGE = 128
CMP_CHUNKS = CMP_PAGE // CMP_STRIDE
KV_LANES = NSA_KV_GROUPS * NSA_HEAD_DIM


def _chunk_proj_kernel(pt_ref, *refs, n_pages):
    del pt_ref
    page_refs = refs[:n_pages]
    w_ref, o0_ref, o1_ref, x_sc = refs[n_pages:]
    for kp in range(n_pages):
        for pos in range(CMP_STRIDE):
            x_sc[kp * CMP_CHUNKS:(kp + 1) * CMP_CHUNKS, pos * LANE:(pos + 1) * LANE] = (
                page_refs[kp][pl.ds(pos, CMP_CHUNKS, stride=CMP_STRIDE), :])
    res = _dot(x_sc[...].astype(BF16), w_ref[...])
    o0_ref[0] = res[:, :LANE]
    o1_ref[0] = res[:, LANE:]


def chunk_proj(rows2d, table, w_flat, n_pages, col_block=0):
    b, n_tab = table.shape
    assert n_tab % n_pages == 0

    def page_spec(kp):
        return pl.BlockSpec(
            (CMP_PAGE, LANE),
            lambda bi, pg, g, pt: (pt[bi, pg * n_pages + kp], col_block * NSA_KV_GROUPS + g))

    out_spec = pl.BlockSpec((1, n_pages * CMP_CHUNKS, LANE), lambda bi, pg, g, pt: (bi, pg, g))
    out_shape = jax.ShapeDtypeStruct((b, n_tab * CMP_CHUNKS, KV_LANES), F32)
    grid_spec = pltpu.PrefetchScalarGridSpec(
        num_scalar_prefetch=1,
        grid=(b, n_tab // n_pages, NSA_KV_GROUPS),
        in_specs=[page_spec(kp) for kp in range(n_pages)]
        + [pl.BlockSpec(w_flat.shape, lambda bi, pg, g, pt: (0, 0))],
        out_specs=[out_spec, out_spec],
        scratch_shapes=[pltpu.VMEM((n_pages * CMP_CHUNKS, CMP_STRIDE * LANE), F32)],
    )
    return pl.pallas_call(
        functools.partial(_chunk_proj_kernel, n_pages=n_pages),
        grid_spec=grid_spec,
        out_shape=[out_shape, out_shape],
        compiler_params=_params("arbitrary", "arbitrary", "arbitrary"),
        name="chunk_proj",
    )(table, *([rows2d] * n_pages), w_flat)


def _chunk_proj_pool_kernel(pt_ref, *refs, n_pages):
    del pt_ref
    page_refs = refs[:n_pages]
    w_ref, o0_ref, o1_ref, x_sc = refs[n_pages:]
    for g in range(NSA_KV_GROUPS):
        for kp in range(n_pages):
            for pos in range(CMP_STRIDE):
                x_sc[kp * CMP_CHUNKS:(kp + 1) * CMP_CHUNKS, pos * LANE:(pos + 1) * LANE] = (
                    page_refs[kp][0, pl.ds(pos, CMP_CHUNKS, stride=CMP_STRIDE), g, :])
        res = _dot(x_sc[...].astype(BF16), w_ref[...])
        o0_ref[0, :, g * LANE:(g + 1) * LANE] = res[:, :LANE]
        o1_ref[0, :, g * LANE:(g + 1) * LANE] = res[:, LANE:]


def chunk_proj_pool(pool, table, w_flat, n_pages):
    b, n_tab = table.shape
    assert n_tab % n_pages == 0

    def page_spec(kp):
        return pl.BlockSpec((1, CMP_PAGE, NSA_KV_GROUPS, LANE),
                            lambda bi, pg, pt: (pt[bi, pg * n_pages + kp], 0, 0, 0))

    out_spec = pl.BlockSpec((1, n_pages * CMP_CHUNKS, KV_LANES), lambda bi, pg, pt: (bi, pg, 0))
    out_shape = jax.ShapeDtypeStruct((b, n_tab * CMP_CHUNKS, KV_LANES), F32)
    grid_spec = pltpu.PrefetchScalarGridSpec(
        num_scalar_prefetch=1,
        grid=(b, n_tab // n_pages),
        in_specs=[page_spec(kp) for kp in range(n_pages)]
        + [pl.BlockSpec(w_flat.shape, lambda bi, pg, pt: (0, 0))],
        out_specs=[out_spec, out_spec],
        scratch_shapes=[pltpu.VMEM((n_pages * CMP_CHUNKS, CMP_STRIDE * LANE), F32)],
    )
    return pl.pallas_call(
        functools.partial(_chunk_proj_pool_kernel, n_pages=n_pages),
        grid_spec=grid_spec,
        out_shape=[out_shape, out_shape],
        compiler_params=_params("arbitrary", "arbitrary"),
        name="chunk_proj_pool",
    )(table, *([pool] * n_pages), w_flat)


def _block_mlp_kernel(p0_ref, p1_ref, pe_ref, w1_ref, b1_ref, w2_ref, o_ref):
    const = _dot(pe_ref[...], w1_ref[...])[0:1, :] + b1_ref[...]
    w2 = w2_ref[...]
    for g in range(NSA_KV_GROUPS):
        sl = slice(g * LANE, (g + 1) * LANE)
        hid = const + p0_ref[0, :, sl] + p1_ref[0, :, sl]
        o_ref[0, :, sl] = _dot(_gelu_tanh(hid), w2)


def block_mlp(p0, p1, pe, w1, b1, w2, tb=1024):
    b, nb, _ = p0.shape
    tb = _row_tile(nb, tb)
    pe8 = jnp.concatenate([pe.reshape(1, -1), jnp.zeros((SUBLANE - 1, pe.size), F32)], axis=0)
    spec = pl.BlockSpec((1, tb, KV_LANES), lambda bi, i: (bi, i, 0))
    full = lambda a: pl.BlockSpec(a.shape, lambda bi, i: (0,) * a.ndim)
    b1r = b1.reshape(1, -1)
    return pl.pallas_call(
        _block_mlp_kernel,
        grid=(b, nb // tb),
        in_specs=[spec, spec, full(pe8), full(w1), full(b1r), full(w2)],
        out_specs=spec,
        out_shape=jax.ShapeDtypeStruct((b, nb, KV_LANES), F32),
        compiler_params=_params("arbitrary", "arbitrary"),
        name="block_mlp",
    )(p0, p1, pe8, w1, b1r, w2)


def _w1_flat(w1):
    r = CMP_BLOCK // CMP_STRIDE
    e = w1.shape[1]
    w = w1.reshape(r, CMP_STRIDE, NSA_HEAD_DIM, e)
    return jnp.transpose(w, (1, 2, 0, 3)).reshape(CMP_STRIDE * NSA_HEAD_DIM, r * e).astype(BF16)


def compress(parts, nb, pe, w1, b1, w2):
    return block_mlp(parts[0][:, :nb], parts[1][:, 1:nb + 1], pe, w1, b1, w2)


def _rel_bucket(dist):
    n = jnp.maximum(dist, 0)
    max_exact = REL_BUCKETS // 2
    nf = jnp.maximum(n, 1).astype(F32)
    large = max_exact + (jnp.log(nf / max_exact) / math.log(REL_MAX_DIST / max_exact)
                         * (REL_BUCKETS - max_exact)).astype(jnp.int32)
    return jnp.where(n < max_exact, n, jnp.minimum(large, REL_BUCKETS - 1))


def _bias_lookup(rel_bias, dist):
    tab = rel_bias.astype(F32).reshape(REL_BUCKETS, NSA_KV_GROUPS, NSA_HPG)
    onehot = jax.nn.one_hot(_rel_bucket(dist), REL_BUCKETS, dtype=F32)
    out = jnp.dot(onehot, tab.reshape(REL_BUCKETS, -1), precision=HI)
    out = out.reshape(dist.shape + (NSA_KV_GROUPS, NSA_HPG))
    return jnp.moveaxis(out, (-2, -1), (0, 1))


def _cmp_to_sel(nb_pad, nb, nsb_pad, nsb):
    i = jnp.arange(nb_pad)[:, None]
    j = jnp.arange(nsb_pad)[None, :]
    start = i * CMP_STRIDE
    hit = (start < (j + 1) * SEL_BLOCK) & (start + CMP_BLOCK > j * SEL_BLOCK) & (i < nb) & (j < nsb)
    return hit.astype(F32)


def _nsa_prompt_kernel(q_ref, gt_ref, kc_ref, vc_ref, bc_ref, band_ref, ks_ref, vs_ref,
                       kw_ref, vw_ref, m_ref, o_ref, s_sc, *, nb, nsb, n_g):
    tq = NSA_TQ
    rows = NSA_HPG * tq
    qw = NSA_HPG * LANE
    i = pl.program_id(2)
    scale = NSA_HEAD_DIM ** -0.5
    row_q = lax.broadcasted_iota(jnp.int32, (rows, LANE), 0) & (tq - 1)
    lane = lax.broadcasted_iota(jnp.int32, (rows, LANE), 1)
    t_pos = i * tq + row_q
    kt_w = NSA_KT
    tiles_per_kt = kt_w // tq
    rel = (i * tq + (lax.broadcasted_iota(jnp.int32, (rows, kt_w), 0) & (tq - 1))
           - lax.broadcasted_iota(jnp.int32, (rows, kt_w), 1))
    m_c = (t_pos - (lane * CMP_STRIDE + (CMP_BLOCK - 1)) >= 0) & (lane < nb)
    jb = lax.broadcasted_iota(jnp.int32, (nsb, tq), 0)
    cur = (i * tq + lax.broadcasted_iota(jnp.int32, (nsb, tq), 1)) // SEL_BLOCK
    valid = jb <= cur
    forced = valid & ((jb == 0) | (jb > cur - SEL_LOCAL))
    blk_row = lax.broadcasted_iota(jnp.int32, (nsb, kt_w), 0)
    blk_lane = lax.broadcasted_iota(jnp.int32, (nsb, kt_w), 1) // SEL_BLOCK
    sel_map = m_ref[...].astype(BF16)
    n_win_tiles = WINDOW // kt_w + 1
    n_kt = i // tiles_per_kt + 1
    st = [dict() for _ in range(n_g)]
    gl = lambda g: slice(g * LANE, (g + 1) * LANE)

    def masked_scores(g, k_ref, kt, mask_of):
        ksl = pl.ds(pl.multiple_of(kt * kt_w, kt_w), kt_w)
        band = band_ref[g, jnp.clip(i - kt * tiles_per_kt, 0, NSA_BANDS - 1)].reshape(rows, kt_w)
        s = _dot_nt(st[g]['q3'], k_ref[0, ksl, gl(g)].astype(BF16)) + band
        return jnp.where(mask_of(kt * kt_w), s, MASK_NEG), ksl

    def finish(m_fin, l_fin, acc):
        return jnp.where(m_fin > MASK_NEG, acc / jnp.where(l_fin > 0, l_fin, 1.0), 0.0)

    def window_scores(g):
        e = st[g]
        e['q3'] = (jnp.concatenate([q_ref[0, :, g * qw + hh * LANE:g * qw + (hh + 1) * LANE]
                                    for hh in range(NSA_HPG)], axis=0) * scale).astype(BF16)
        e['win'] = []
        m_w = jnp.full((rows, 1), MASK_NEG, F32)
        for jw in range(n_win_tiles):
            kt_raw = i // tiles_per_kt - (n_win_tiles - 1) + jw
            reach = jnp.where(kt_raw >= 0, WINDOW, 0)
            s_w, ksl_w = masked_scores(g, kw_ref, jnp.maximum(kt_raw, 0),
                                       lambda base, reach=reach: (rel >= base) & (rel < base + reach))
            e['win'].append((s_w, ksl_w))
            m_w = jnp.maximum(m_w, jnp.max(s_w, axis=-1, keepdims=True))
        e['m_w'] = m_w

    def compressed(g):
        e = st[g]
        s_c = _dot_nt(e['q3'], kc_ref[0, :, gl(g)].astype(BF16)) + bc_ref[g].reshape(rows, LANE)
        e['p_c'] = _masked_softmax(s_c, m_c)
        e['o_c'] = _dot(e['p_c'], vc_ref[0, :, gl(g)])

    def select(g):
        e = st[g]
        p_hi, p_lo = _split_hi_lo(e['p_c'])
        imp3 = _dot_nt(sel_map, p_hi.astype(BF16)) + _dot_nt(sel_map, p_lo.astype(BF16))
        imp = imp3[:, 0:tq]
        for hh in range(1, NSA_HPG):
            imp = imp + imp3[:, hh * tq:(hh + 1) * tq]
        imp = jnp.where(forced, FORCE_SCORE, jnp.where(valid, imp, -FORCE_SCORE))
        rank = jnp.zeros((nsb, tq), jnp.int32)
        for jp in range(nsb):
            other = imp[jp:jp + 1, :]
            rank = rank + ((other > imp) | ((other == imp) & (jb > jp))).astype(jnp.int32)
        sel = (rank < min(SEL_TOPK, nsb)).astype(F32).T
        e['sel3'] = jnp.concatenate([sel] * NSA_HPG, axis=0).astype(BF16)

    def window_values(g):
        e = st[g]
        l_w = jnp.zeros((rows, 1), F32)
        acc_w = jnp.zeros((rows, LANE), F32)
        for s_w, ksl_w in e['win']:
            p_w = jnp.exp(s_w - e['m_w'])
            l_w = l_w + jnp.sum(p_w, axis=-1, keepdims=True)
            acc_w = acc_w + _dot(p_w, vw_ref[0, ksl_w, gl(g)])
        e['o_w'] = finish(e['m_w'], l_w, acc_w)

    for stage in (window_scores, compressed, select, window_values):
        for g in range(n_g):
            stage(g)

    def scores(kt, m_run):
        expand = (blk_row == kt * (kt_w // SEL_BLOCK) + blk_lane).astype(BF16)
        out = []
        for g in range(n_g):
            chosen = _dot(st[g]['sel3'], expand) > 0.5
            s, _ = masked_scores(g, ks_ref, kt, lambda base, chosen=chosen: chosen & (rel >= base))
            s_sc[g, kt] = s
            out.append(jnp.maximum(m_run[g], jnp.max(s, axis=-1, keepdims=True)))
        return tuple(out)

    m_s = lax.fori_loop(0, n_kt, scores, tuple(jnp.full((rows, 1), MASK_NEG, F32) for _ in range(n_g)))

    def accum(kt, carry):
        ksl = pl.ds(pl.multiple_of(kt * kt_w, kt_w), kt_w)
        out = []
        for g in range(n_g):
            l_run, acc = carry[g]
            p = jnp.exp(s_sc[g, kt] - m_s[g])
            out.append((l_run + jnp.sum(p, axis=-1, keepdims=True), acc + _dot(p, vs_ref[0, ksl, gl(g)])))
        return tuple(out)

    fin = lax.fori_loop(0, n_kt, accum, tuple((jnp.zeros((rows, 1), F32), jnp.zeros((rows, LANE), F32))
                                              for _ in range(n_g)))
    for g in range(n_g):
        e = st[g]
        o_s = finish(m_s[g], fin[g][0], fin[g][1])
        gt = _sigmoid(gt_ref[0, g])
        gcol = lambda br, gt=gt: jnp.concatenate(
            [gt[:, br * NSA_HPG + hh:br * NSA_HPG + hh + 1] for hh in range(NSA_HPG)], axis=0)
        o = gcol(0) * e['o_c'] + gcol(1) * o_s + gcol(2) * e['o_w']
        for hh in range(NSA_HPG):
            o_ref[0, :, g * qw + hh * LANE:g * qw + (hh + 1) * LANE] = (
                o[hh * tq:(hh + 1) * tq].astype(o_ref.dtype))


def nsa_prompt(proj, gates_t, kc, vc, bias_c, band, side, sel_map, nb, nsb, n_g=2):
    b, t, _ = proj.shape
    tq = NSA_TQ
    qw = n_g * NSA_HPG * LANE
    kvw = n_g * LANE
    n_gp = NSA_KV_GROUPS // n_g
    kv = lambda off: pl.BlockSpec((1, t, kvw), lambda bi, gp, i: (bi, 0, off * n_gp + gp))
    return pl.pallas_call(
        functools.partial(_nsa_prompt_kernel, nb=nb, nsb=nsb, n_g=n_g),
        grid=(b, n_gp, t // tq),
        in_specs=[pl.BlockSpec((1, tq, qw), lambda bi, gp, i: (bi, i, gp)),
                  pl.BlockSpec((1, n_g, tq, NSA_HPG * 3), lambda bi, gp, i: (bi, gp, i, 0)),
                  pl.BlockSpec((1, LANE, kvw), lambda bi, gp, i: (bi, 0, gp)),
                  pl.BlockSpec((1, LANE, kvw), lambda bi, gp, i: (bi, 0, gp)),
                  pl.BlockSpec((n_g, NSA_HPG, tq, LANE), lambda bi, gp, i: (gp, 0, i, 0)),
                  pl.BlockSpec((n_g, NSA_BANDS, NSA_HPG, tq, NSA_KT), lambda bi, gp, i: (gp, 0, 0, 0, 0)),
                  kv(2), kv(3), kv(4), kv(5),
                  pl.BlockSpec(sel_map.shape, lambda bi, gp, i: (0, 0))],
        out_specs=pl.BlockSpec((1, tq, qw), lambda bi, gp, i: (bi, i, gp)),
        out_shape=jax.ShapeDtypeStruct((b, t, MAIN_W), BF16),
        scratch_shapes=[pltpu.VMEM((n_g, t // NSA_KT, NSA_HPG * tq, NSA_KT), F32)],
        compiler_params=_params("arbitrary", "arbitrary", "arbitrary"),
        name="nsa_prompt",
    )(proj, gates_t, kc, vc, bias_c, band, side, side, side, side, sel_map)


def _nsa_dec_cmp_kernel(q_ref, kc_ref, vc_ref, bc_ref, m_ref, oc_ref, idx_ref, *, nb, nsb, t_pos):
    scale = NSA_HEAD_DIM ** -0.5
    nbp = kc_ref.shape[1]
    nsp = m_ref.shape[1]
    n_sel = min(SEL_TOPK, nsb)
    lane_b = lax.broadcasted_iota(jnp.int32, (SUBLANE, nbp), 1)
    m_c = (t_pos - (lane_b * CMP_STRIDE + (CMP_BLOCK - 1)) >= 0) & (lane_b < nb)
    row_s = lax.broadcasted_iota(jnp.int32, (SUBLANE, nsp), 0)
    jb = lax.broadcasted_iota(jnp.int32, (1, nsp), 1)
    cur = t_pos // SEL_BLOCK
    valid = jb <= cur
    forced = valid & ((jb == 0) | (jb > cur - SEL_LOCAL))
    out_lane = lax.broadcasted_iota(jnp.int32, (SUBLANE, LANE), 1)
    imp_rows = []
    for g in range(NSA_KV_GROUPS):
        q3 = jnp.concatenate(
            [q_ref[0, 0:1, (g * NSA_HPG + hh) * LANE:(g * NSA_HPG + hh + 1) * LANE]
             for hh in range(NSA_HPG)] + [jnp.zeros((SUBLANE - NSA_HPG, LANE), F32)], axis=0)
        sl = slice(g * LANE, (g + 1) * LANE)
        s_c = _dot_nt(q3, kc_ref[0, :, sl]) * scale + bc_ref[g]
        p_c = _masked_softmax(s_c, m_c)
        oc_ref[0, g] = _dot(p_c, vc_ref[0, :, sl])
        imp8 = jnp.where(row_s < NSA_HPG, _dot(p_c, m_ref[...], HI), 0.0)
        imp_rows.append(jnp.sum(imp8, axis=0, keepdims=True))
    imp = jnp.concatenate(imp_rows + [jnp.zeros((SUBLANE - NSA_KV_GROUPS, nsp), F32)], axis=0)
    imp = jnp.where(forced, FORCE_SCORE, jnp.where(valid, imp, -FORCE_SCORE))
    imp = jnp.where(jb < nsb, imp, -jnp.inf)
    jbf = jnp.broadcast_to(jb.astype(F32), imp.shape)
    picks = jnp.zeros((SUBLANE, LANE), F32)
    for kk in range(n_sel):
        best = jnp.max(imp, axis=-1, keepdims=True)
        arg = jnp.min(jnp.where(imp == best, jbf, float(nsp)), axis=-1, keepdims=True)
        picks = jnp.where(out_lane == kk, arg, picks)
        imp = jnp.where(jbf == arg, -jnp.inf, imp)
    idx_ref[0] = picks.astype(jnp.int32)


def nsa_dec_cmp(proj, kc, vc, bias_c, sel_map, nb, nsb, t_pos):
    b, tp, _ = proj.shape
    nbp = kc.shape[1]
    return pl.pallas_call(
        functools.partial(_nsa_dec_cmp_kernel, nb=nb, nsb=nsb, t_pos=t_pos),
        grid=(b,),
        in_specs=[pl.BlockSpec((1, tp, MAIN_W), lambda bi: (bi, 0, 0)),
                  pl.BlockSpec((1, nbp, KV_LANES), lambda bi: (bi, 0, 0)),
                  pl.BlockSpec((1, nbp, KV_LANES), lambda bi: (bi, 0, 0)),
                  pl.BlockSpec(bias_c.shape, lambda bi: (0, 0, 0)),
                  pl.BlockSpec(sel_map.shape, lambda bi: (0, 0))],
        out_specs=[pl.BlockSpec((1, NSA_KV_GROUPS, SUBLANE, LANE), lambda bi: (bi, 0, 0, 0)),
                   pl.BlockSpec((1, SUBLANE, LANE), lambda bi: (bi, 0, 0))],
        out_shape=[jax.ShapeDtypeStruct((b, NSA_KV_GROUPS, SUBLANE, LANE), F32),
                   jax.ShapeDtypeStruct((b, SUBLANE, LANE), jnp.int32)],
        compiler_params=_params("arbitrary"),
        name="nsa_dec_cmp",
    )(proj, kc, vc, bias_c, sel_map)


def _nsa_dec_sel_kernel(phys_ref, isnew_ref, *refs, n_sel, n_win, t_pos, win_start):
    del phys_ref
    kb_refs = refs[:n_sel]
    vb_refs = refs[n_sel:2 * n_sel]
    (q_ref, new_ref, ds_ref, bs_ref, wk_ref, wv_ref, bw_ref, oc_ref, gt_ref, o_ref) = refs[2 * n_sel:]
    bi = pl.program_id(0)
    g = pl.program_id(1)
    scale = NSA_HEAD_DIM ** -0.5
    q3 = q_ref[0, 0]
    new_rows = new_ref[0, 0]
    pad_blk = jnp.zeros((SEL_BLOCK - 1, LANE), F32)
    new_k = jnp.concatenate([new_rows[0:1], pad_blk], axis=0)
    new_v = jnp.concatenate([new_rows[1:2], pad_blk], axis=0)
    ks, vs = [], []
    for kk in range(n_sel):
        fresh = isnew_ref[bi, g, kk] > 0
        ks.append(jnp.where(fresh, new_k, kb_refs[kk][0, :, g, :]))
        vs.append(jnp.where(fresh, new_v, vb_refs[kk][0, :, g, :]))
    ks = jnp.concatenate(ks, axis=0)
    vs = jnp.concatenate(vs, axis=0)
    s_s = _dot_nt(q3, ks) * scale + bs_ref[0, 0]
    p_s = _masked_softmax(s_s, ds_ref[0, 0] >= 0)
    o_s = _dot(p_s, vs)

    pad_w = jnp.zeros((SUBLANE - 1, LANE), F32)
    kw = jnp.concatenate([wk_ref[0, :, g, :], new_rows[2:3], pad_w], axis=0)
    vw = jnp.concatenate([wv_ref[0, :, g, :], new_rows[3:4], pad_w], axis=0)
    nw = kw.shape[0]
    pos = win_start + lax.broadcasted_iota(jnp.int32, (SUBLANE, nw), 1)
    d_w = t_pos - pos
    m_w = (d_w >= 0) & (d_w < WINDOW) & (pos >= 0) & (pos - win_start < n_win)
    s_w = _dot_nt(q3, kw) * scale + bw_ref[0]
    p_w = _masked_softmax(s_w, m_w)
    o_w = _dot(p_w, vw)

    gt = _sigmoid(gt_ref[0, 0])
    o_ref[0, 0] = gt[:, 0:1] * oc_ref[0, 0] + gt[:, 1:2] * o_s + gt[:, 2:3] * o_w


def nsa_dec_sel(phys, isnew, pool_k, pool_v, q8, new_rows, d_s, bias_s, win_k, win_v, bias_w,
                o_c, gates8, n_win, t_pos, win_start):
    b = q8.shape[0]
    n_sel = phys.shape[-1]
    n_keys = n_sel * SEL_BLOCK
    nw = win_k.shape[1]
    nwp = nw + SUBLANE

    def blk_spec(kk):
        return pl.BlockSpec((1, SEL_BLOCK, NSA_KV_GROUPS, LANE),
                            lambda bi, g, ph, nf: (ph[bi, g, kk], 0, 0, 0))

    per_bg = lambda *shape: pl.BlockSpec((1, 1) + shape, lambda bi, g, ph, nf: (bi, g) + (0,) * len(shape))
    grid_spec = pltpu.PrefetchScalarGridSpec(
        num_scalar_prefetch=2,
        grid=(b, NSA_KV_GROUPS),
        in_specs=[blk_spec(kk) for kk in range(n_sel)] + [blk_spec(kk) for kk in range(n_sel)]
        + [per_bg(SUBLANE, LANE), per_bg(SUBLANE, LANE), per_bg(1, n_keys), per_bg(SUBLANE, n_keys),
           pl.BlockSpec((1, nw, NSA_KV_GROUPS, LANE), lambda bi, g, ph, nf: (bi, 0, 0, 0)),
           pl.BlockSpec((1, nw, NSA_KV_GROUPS, LANE), lambda bi, g, ph, nf: (bi, 0, 0, 0)),
           pl.BlockSpec((1, SUBLANE, nwp), lambda bi, g, ph, nf: (g, 0, 0)),
           per_bg(SUBLANE, LANE), per_bg(SUBLANE, SUBLANE)],
        out_specs=per_bg(SUBLANE, LANE),
    )
    return pl.pallas_call(
        functools.partial(_nsa_dec_sel_kernel, n_sel=n_sel, n_win=nw + 1, t_pos=t_pos,
                          win_start=win_start),
        grid_spec=grid_spec,
        out_shape=jax.ShapeDtypeStruct((b, NSA_KV_GROUPS, SUBLANE, LANE), F32),
        compiler_params=_params("arbitrary", "arbitrary"),
        name="nsa_dec_sel",
    )(phys, isnew, *([pool_k] * n_sel), *([pool_v] * n_sel), q8, new_rows, d_s, bias_s,
      win_k, win_v, bias_w, o_c, gates8)


def _pad_cols(w, n):
    return jnp.pad(w, ((0, 0), (0, n - w.shape[1])))


def _pad_rows(w, n):
    return jnp.pad(w, ((0, n - w.shape[0]), (0, 0)))


def _prep_weights(P):
    W = {}
    W['w_gu'] = P['ffn_gu']
    W['w_d'] = P['ffn_d'].astype(BF16)
    W['w_out'] = P['w_out'].astype(BF16)
    W['rw_in'], W['rw_w2'], W['rw_a2'], W['rw_g2'], W['rw_v2'], W['rw_vec'] = [], [], [], [], [], []
    lp = LOWRANK_PAD
    for l in range(N_A):
        w_in = P['w_in_a'][l]
        v1 = P['rw_v1'][l - 1] if l > 0 else jnp.zeros((D_MODEL, lp), F32)
        W['rw_in'].append(jnp.concatenate([
            w_in, _pad_cols(P['rw_w1'][l], lp), _pad_cols(v1, lp), _pad_cols(P['rw_a1'][l], lp),
            _pad_cols(P['rw_g1'][l], lp)], axis=1).astype(BF16))
        W['rw_w2'].append(_pad_rows(P['rw_w2'][l], lp))
        W['rw_a2'].append(_pad_rows(P['rw_a2'][l], lp))
        W['rw_g2'].append(_pad_rows(P['rw_g2'][l], lp))
        W['rw_v2'].append(_pad_rows(P['rw_v2'][l - 1], lp) if l > 0 else None)
        v0 = P['rw_v0'][l - 1] if l > 0 else jnp.zeros((MAIN_W,), F32)
        W['rw_vec'].append(jnp.stack([P['rw_w0'][l], P['rw_a0'][l], v0, P['rw_kk'][l], P['rw_ka'][l],
                                      P['rw_lnw'][l], P['rw_lnb'][l], P['rw_rk'][l].reshape(-1)]))
    W['w_in_b'] = []
    for l in range(DEPTH - N_A):
        w = P['w_in_b'][l]
        W['w_in_b'].append(jnp.concatenate([
            w[:, :MAIN_W], w[:, MAIN_W + GATE_W:], _pad_cols(w[:, MAIN_W:MAIN_W + GATE_W], MXU_DIM)],
            axis=1).astype(BF16))
    W['w_kv'] = P['w_kv'].astype(BF16)
    W['w_mem_kv'] = [P['w_mem_kv'][l].astype(BF16) for l in range(DEPTH)]
    W['cmp_w1_flat'] = [_w1_flat(P['cmp_w1'][c]) for c in range(2)]
    return W


B_QM_BLOCK = MAIN_W // MEM_W
B_GATE_OFF = MAIN_W + MEM_W


def _ffn(h2, norms_l, first, W, l, i):
    hid = norm_swiglu(h2, norms_l[first], W['w_gu'], l, i, tn=512, tm=1024)
    return matmul_norm_res(hid, W['w_d'], l, i, norms_l[first + 1], h2, 0.5, tk=FFN_DOWN_TK)


def _trunk(x, t_real, P, W, shift0, wkv0_bd, mem_src, make_side, attend, chunk, tb):
    b, t, d = x.shape
    h2 = x.reshape(b * t, d)
    shifts, states = [], []
    v_first_proj, ctx, side_state = None, None, None
    for l in range(DEPTH):
        n = P['norms'][l]
        h2 = _ffn(h2, n, 0, W, l, 0)
        if l < N_A:
            proj, last = rwkv_in_proj(h2.reshape(b, t, d), shift0[l], n[2], P['rw_mu'][l],
                                      W['rw_in'][l], t_real)
            main, s_bd = rwkv_scan(proj, v_first_proj if l > 0 else None, W['rw_w2'][l], W['rw_a2'][l],
                                   W['rw_g2'][l], W['rw_v2'][l], W['rw_vec'][l], wkv0_bd[l],
                                   t_real, chunk, tb)
            if l == 0:
                v_first_proj = proj
            shifts.append(last)
            states.append(_from_block_diag(s_bd))
            q_src, q_block = proj, RW_QM * MXU_DIM // MEM_W
        else:
            proj = norm_matmul(h2, n[2], W['w_in_b'][l - N_A], tn=768, tm=1024).reshape(b, t, -1)
            main = attend(proj, ctx)
            q_src, q_block = proj, B_QM_BLOCK
        mk, kb, mv, vb = mem_src(l)
        mo = mem_attend(q_src, q_block, mk, kb, mv, vb)
        h2 = out_proj(main.reshape(b * t, MAIN_W), mo.reshape(b * t, MEM_W), W['w_out'], l, n[3], h2)
        h2 = _ffn(h2, n, 4, W, l, 1)
        if l == N_A - 1:
            side = norm_matmul(h2, P['kv_norm'], W['w_kv'], tn=768, tm=1024).reshape(b, t, -1)
            ctx, side_state = make_side(side)
    return h2.reshape(b, t, d), jnp.stack(shifts), jnp.stack(states), side_state


def kernel(x_prompt, x_sample, mem_prompt, state_wkv, state_shift, cache_mem_k, cache_mem_v,
           cache_cmp_k, cache_cmp_v, cache_slc_k, cache_slc_v, cache_win_k, cache_win_v, page_table,
           norms, ffn_gu, ffn_d, w_in_a, w_in_b, w_out, mem_norm, w_mem_kv, kv_norm, w_kv,
           cmp_pe, cmp_w1, cmp_b1, cmp_w2, rel_bias,
           rw_mu, rw_w0, rw_w1, rw_w2, rw_a0, rw_a1, rw_a2, rw_g1, rw_g2, rw_v0, rw_v1, rw_v2,
           rw_kk, rw_ka, rw_rk, rw_lnw, rw_lnb):
    P = dict(norms=norms, ffn_gu=ffn_gu, ffn_d=ffn_d, w_in_a=w_in_a, w_in_b=w_in_b, w_out=w_out,
             kv_norm=kv_norm, w_kv=w_kv, w_mem_kv=w_mem_kv, cmp_w1=cmp_w1, rw_mu=rw_mu, rw_w0=rw_w0,
             rw_w1=rw_w1, rw_w2=rw_w2, rw_a0=rw_a0, rw_a1=rw_a1, rw_a2=rw_a2, rw_g1=rw_g1,
             rw_g2=rw_g2, rw_v0=rw_v0, rw_v1=rw_v1, rw_v2=rw_v2, rw_kk=rw_kk, rw_ka=rw_ka,
             rw_rk=rw_rk, rw_lnw=rw_lnw, rw_lnb=rw_lnb)
    W = _prep_weights(P)
    G, dh = NSA_KV_GROUPS, NSA_HEAD_DIM
    split_side = lambda side, bx, t: [side[:, :t, c * KV_LANES:(c + 1) * KV_LANES].reshape(bx, t, G, dh)
                                      for c in range(6)]

    def cmp_mlp(parts, nb, c):
        return compress(parts, nb, cmp_pe[c], cmp_w1[c], cmp_b1[c], cmp_w2[c])

    bp, tp, d = x_prompt.shape
    n_mem = mem_prompt.shape[1]
    mem2 = mem_prompt.reshape(bp * n_mem, d)
    p_mkv = [norm_matmul(mem2, mem_norm[l], W['w_mem_kv'][l], tn=512).reshape(bp, n_mem, 2 * MEM_W)
             for l in range(DEPTH)]
    p_mem_k = jnp.stack([m[..., :MEM_W].reshape(bp, n_mem, MEM_HEADS, MEM_HEAD_DIM) for m in p_mkv])
    p_mem_v = jnp.stack([m[..., MEM_W:].reshape(bp, n_mem, MEM_HEADS, MEM_HEAD_DIM) for m in p_mkv])

    nb_p = tp // CMP_STRIDE - 1
    nsb_p = tp // SEL_BLOCK
    t_all = jnp.arange(tp)
    c_end = jnp.arange(LANE) * CMP_STRIDE + (CMP_BLOCK - 1)
    assert nb_p <= LANE
    bias_c_p = _bias_lookup(rel_bias, t_all[:, None] - c_end[None, :])
    ii = jnp.arange(NSA_TQ)
    cc = jnp.arange(NSA_KT)
    band_p = jnp.stack([_bias_lookup(rel_bias, dd * NSA_TQ + ii[:, None] - cc[None, :])
                        for dd in range(NSA_BANDS)], axis=1)
    assert (NSA_BANDS - 1) * NSA_TQ - (NSA_KT - 1) >= REL_MAX_DIST and tp % NSA_KT == 0
    sel_map_p = _cmp_to_sel(LANE, nb_p, nsb_p, nsb_p).T
    ident = jnp.arange(bp * tp // CMP_PAGE, dtype=jnp.int32).reshape(bp, tp // CMP_PAGE)

    def prompt_side(side):
        rows = side.reshape(bp * tp, -1)
        kc = cmp_mlp(chunk_proj(rows, ident, W['cmp_w1_flat'][0], tp // CMP_PAGE, 0), nb_p, 0)
        vc = cmp_mlp(chunk_proj(rows, ident, W['cmp_w1_flat'][1], tp // CMP_PAGE, 1), nb_p, 1)
        padb = ((0, 0), (0, LANE - nb_p), (0, 0))
        wb = min(WINDOW, tp)
        kc_r, vc_r, ks, vs, kw, vw = split_side(side, bp, tp)
        return ((jnp.pad(kc, padb), jnp.pad(vc, padb), side),
                (kc_r, vc_r, ks, vs, kw[:, tp - wb:], vw[:, tp - wb:]))

    def prompt_attend(proj, ctx):
        kc, vc, side = ctx
        gates = proj[..., B_GATE_OFF:B_GATE_OFF + GATE_W].reshape(bp, tp, 3, G, NSA_HPG)
        gates_t = jnp.transpose(gates, (0, 3, 1, 2, 4)).reshape(bp, G, tp, 3 * NSA_HPG)
        return nsa_prompt(proj, gates_t, kc, vc, bias_c_p, band_p, side, sel_map_p, nb_p, nsb_p)

    zeros_shift = jnp.zeros((N_A, bp, d), F32)
    zeros_state = jnp.zeros((N_A, bp, N_HEAD_BLOCKS, MXU_DIM, MXU_DIM), F32)
    y_prompt, p_shift, p_wkv, p_side = _trunk(
        x_prompt, tp, P, W, zeros_shift, zeros_state,
        lambda l: (p_mkv[l], 0, p_mkv[l], 1), prompt_side, prompt_attend, chunk=64, tb=512)
    p_cmp_k, p_cmp_v, p_slc_k, p_slc_v, p_win_k, p_win_v = p_side

    bd, s_new, _ = x_sample.shape
    assert s_new == 1
    ts = SUBLANE
    xs = jnp.pad(x_sample, ((0, 0), (0, ts - s_new), (0, 0)))
    past_len = page_table.shape[1] * CMP_PAGE
    n_past_blk = past_len // SEL_BLOCK
    blk_per_page = CMP_PAGE // SEL_BLOCK
    nsb_s = n_past_blk + 1
    t_pos = past_len
    nc_s = -(-(past_len + s_new) // CMP_STRIDE)
    nb_s = nc_s - 1
    wb_s = cache_win_k.shape[1]
    win_start = past_len - wb_s
    nsp = -(-nsb_s // LANE) * LANE
    sel_map_s = _cmp_to_sel(nb_s, nb_s, nsp, nsb_s)
    c_end_s = jnp.arange(nb_s) * CMP_STRIDE + (CMP_BLOCK - 1)
    bias_c_s = _bias_lookup(rel_bias, t_pos - c_end_s)
    bias_c_s = jnp.pad(bias_c_s, ((0, 0), (0, SUBLANE - NSA_HPG), (0, 0)))
    nwp = wb_s + SUBLANE
    bias_w_s = _bias_lookup(rel_bias, t_pos - (win_start + jnp.arange(nwp)))
    bias_w_s = jnp.pad(bias_w_s, ((0, 0), (0, SUBLANE - NSA_HPG), (0, 0)))
    ident_s = jnp.arange(bd, dtype=jnp.int32).reshape(bd, 1)
    pages_per_step = math.gcd(16, page_table.shape[1])

    def sample_side(side):
        new = side[:, :s_new]
        parts = []
        for c, pool in enumerate((cache_cmp_k, cache_cmp_v)):
            past = chunk_proj_pool(pool, page_table, W['cmp_w1_flat'][c], pages_per_step)
            fresh_page = jnp.pad(new[:, :, c * KV_LANES:(c + 1) * KV_LANES],
                                 ((0, 0), (0, CMP_PAGE - s_new), (0, 0))).reshape(bd, CMP_PAGE, G, dh)
            fresh = chunk_proj_pool(fresh_page, ident_s, W['cmp_w1_flat'][c], 1)
            n_fresh = nc_s - past[0].shape[1]
            parts.append([jnp.concatenate([p, f[:, :n_fresh]], axis=1) for p, f in zip(past, fresh)])
        kc = cmp_mlp(parts[0], nb_s, 0)
        vc = cmp_mlp(parts[1], nb_s, 1)
        kc_n, vc_n, ks_n, vs_n, kw_n, vw_n = split_side(side, bd, s_new)
        s_win_k = jnp.concatenate([cache_win_k, kw_n], axis=1)[:, s_new:]
        s_win_v = jnp.concatenate([cache_win_v, vw_n], axis=1)[:, s_new:]
        return (kc, vc, new), (kc_n, vc_n, ks_n, vs_n, s_win_k, s_win_v)

    def sample_attend(proj, ctx):
        kc, vc, new = ctx
        o_c, idx8 = nsa_dec_cmp(proj, kc, vc, bias_c_s, sel_map_s, nb_s, nsb_s, t_pos)
        idx = idx8[:, :G, :SEL_TOPK]
        is_new = idx >= n_past_blk
        jp = jnp.minimum(idx, n_past_blk - 1)
        phys = (jnp.take_along_axis(page_table[:, None, :], jp // blk_per_page, axis=2) * blk_per_page
                + jp % blk_per_page)
        k_pos = (idx[..., None] * SEL_BLOCK + jnp.arange(SEL_BLOCK)).reshape(bd, G, 1, -1)
        d_s = t_pos - k_pos
        tab = rel_bias.astype(F32).reshape(REL_BUCKETS, G, NSA_HPG)
        onehot = jax.nn.one_hot(_rel_bucket(d_s[:, :, 0]), REL_BUCKETS, dtype=F32)
        bias_s = jnp.einsum('bgkn,ngh->bghk', onehot, tab, precision=HI)
        bias_s = jnp.pad(bias_s, ((0, 0), (0, 0), (0, SUBLANE - NSA_HPG), (0, 0)))
        q8 = jnp.pad(proj[:, 0, :MAIN_W].reshape(bd, G, NSA_HPG, dh),
                     ((0, 0), (0, 0), (0, SUBLANE - NSA_HPG), (0, 0)))
        new_rows = jnp.pad(jnp.transpose(new[:, 0, 2 * KV_LANES:].reshape(bd, 4, G, dh), (0, 2, 1, 3)),
                           ((0, 0), (0, 0), (0, SUBLANE - 4), (0, 0)))
        gates = proj[:, 0, B_GATE_OFF:B_GATE_OFF + GATE_W].reshape(bd, 3, G, NSA_HPG)
        gates8 = jnp.pad(jnp.transpose(gates, (0, 2, 3, 1)),
                         ((0, 0), (0, 0), (0, SUBLANE - NSA_HPG), (0, SUBLANE - 3)))
        o = nsa_dec_sel(phys.astype(jnp.int32), is_new.astype(jnp.int32),
                        cache_slc_k.reshape(-1, SEL_BLOCK, G, dh),
                        cache_slc_v.reshape(-1, SEL_BLOCK, G, dh),
                        q8, new_rows, d_s.astype(jnp.int32), bias_s,
                        cache_win_k, cache_win_v,
                        bias_w_s, o_c, gates8, wb_s, t_pos, win_start)
        main = o[:, :, :NSA_HPG].reshape(bd, 1, MAIN_W)
        return jnp.pad(main, ((0, 0), (0, ts - 1), (0, 0))).astype(BF16)

    y_s, s_shift, s_wkv, s_side = _trunk(
        xs, s_new, P, W, state_shift, jnp.stack([_to_block_diag(state_wkv[l]) for l in range(N_A)]),
        lambda l: (cache_mem_k, l, cache_mem_v, l), sample_side, sample_attend, chunk=SUBLANE, tb=SUBLANE)
    y_sample = y_s[:, :s_new]
    s_cmp_k, s_cmp_v, s_slc_k, s_slc_v, s_win_k, s_win_v = s_side

    return (y_prompt, y_sample, p_mem_k, p_mem_v, p_wkv, p_shift,
            p_cmp_k, p_cmp_v, p_slc_k, p_slc_v, p_win_k, p_win_v,
            s_wkv, s_shift, s_cmp_k, s_cmp_v, s_slc_k, s_slc_v, s_win_k, s_win_v)
```
